```python
import jax
import jax.numpy as jnp
from jax import lax
import numpy as np

D_MODEL = 2048
BATCH = 4
SEQ = 2048
DEPTH = 1
DEC_BATCH = 8
DEC_SEQ = 2048
PAST_LEN = 128

RMS_EPS = 1e-6
M_HEADS = 8
M_QK_DIM = 128
M_V_DIM = 256
M_CHUNK = 128
M_CONV = 3
A_GROUPS = ((128, 1), (512, 4), (2048, 16))
A_SLOTS = 4
A_HEADS = A_SLOTS * len(A_GROUPS)
A_HEAD_DIM = 128
N_GROUPS = 4
EXPERTS_PER_GROUP = 8
N_EXPERTS = N_GROUPS * EXPERTS_PER_GROUP
TOP_K = 2
D_EXPERT = 1024
MOE_BLOCK = 128
M_QK_W = M_HEADS * M_QK_DIM
M_V_W = M_HEADS * M_V_DIM
M_GATE_W = 4 * M_HEADS
A_W = A_HEADS * A_HEAD_DIM
A_OUT_W = A_SLOTS * A_HEAD_DIM
PROJ_SIZES = (M_QK_W, M_QK_W, M_V_W, M_V_W, M_GATE_W, A_W, A_W, A_W, D_MODEL, D_MODEL)
PROJ_W = sum(PROJ_SIZES)

kernel_name = 'hybrid_mlstm_dilated_hmoe_encoder'


def proj_splits():
    out, acc = [], 0
    for s in PROJ_SIZES[:-1]:
        acc += s
        out.append(acc)
    return out


def rmsnorm(x, w):
    xf = x.astype(jnp.float32)
    y = xf * lax.rsqrt(jnp.mean(xf * xf, axis=-1, keepdims=True) + RMS_EPS)
    return (y * w.astype(jnp.float32)).astype(x.dtype)


def centred_dwconv(x, w, b):
    K = w.shape[0]
    half = K // 2
    S = x.shape[1]
    xp = jnp.pad(x, ((0, 0), (half, K - 1 - half), (0, 0)))
    y = b
    for j in range(K):
        y = y + xp[:, j:j + S] * w[j]
    return y


def alibi_slopes(n):
    return 2.0 ** (-8.0 * jnp.arange(1, n + 1, dtype=jnp.float32) / n)


def mlstm_chunkwise(q, k, v, log_i, log_f):
    B, H, S, dk = q.shape
    dv = v.shape[-1]
    L = M_CHUNK
    nc = S // L
    f32 = jnp.float32

    def chunks(a):
        return jnp.moveaxis(a.astype(f32).reshape(B, H, nc, L, *a.shape[3:]), 2, 0)

    tril = jnp.tril(jnp.ones((L, L), dtype=bool))

    def step(carry, xs):
        C, n, m = carry
        qc, kc, vc, ic, fc = xs
        b = jnp.cumsum(fc, axis=-1)
        a = b[..., -1]
        d_intra = jnp.where(tril, b[..., :, None] - b[..., None, :] + ic[..., None, :], -jnp.inf)
        d_inter = b + m[..., None]
        m_t = jnp.maximum(d_inter, jnp.max(d_intra, axis=-1))
        w_intra = jnp.exp(d_intra - m_t[..., None]) * jnp.einsum('bhtd,bhsd->bhts', qc, kc)
        w_inter = jnp.exp(d_inter - m_t)
        num = w_inter[..., None] * jnp.einsum('bhtd,bhde->bhte', qc, C) + jnp.einsum('bhts,bhse->bhte', w_intra, vc)
        den = w_inter * jnp.einsum('bhtd,bhd->bht', qc, n) + jnp.sum(w_intra, axis=-1)
        h = num / jnp.maximum(jnp.abs(den), jnp.exp(-m_t))[..., None]
        g = a[..., None] - b + ic
        m_new = jnp.maximum(a + m, jnp.max(g, axis=-1))
        decay = jnp.exp(a + m - m_new)
        wk = jnp.exp(g - m_new[..., None])
        C_new = decay[..., None, None] * C + jnp.einsum('bhsd,bhse->bhde', kc * wk[..., None], vc)
        n_new = decay[..., None] * n + jnp.einsum('bhs,bhsd->bhd', wk, kc)
        return (C_new, n_new, m_new), h

    init = (jnp.zeros((B, H, dk, dv), f32), jnp.zeros((B, H, dk), f32), jnp.zeros((B, H), f32))
    _, hs = lax.scan(step, init, (chunks(q), chunks(k), chunks(v), chunks(log_i), chunks(log_f)))
    return jnp.moveaxis(hs, 0, 2).reshape(B, H, S, dv)


def mlstm_bidirectional(q, k, v, gate_pre):
    gp = jnp.moveaxis(gate_pre.astype(jnp.float32), 1, -1)
    i_f, f_f, i_b, f_b = gp[:, 0], gp[:, 1], gp[:, 2], gp[:, 3]
    h_fwd = mlstm_chunkwise(q, k, v, i_f, jax.nn.log_sigmoid(f_f))
    rev = lambda a: jnp.flip(a, axis=2)
    h_bwd = rev(mlstm_chunkwise(rev(q), rev(k), rev(v), rev(i_b), rev(jax.nn.log_sigmoid(f_b))))
    return h_fwd + h_bwd


def dilated_window_attention(q, k, v, slopes, window, dil):
    B, S, Hg, dh = q.shape
    side = window // (2 * dil)
    unit = side * dil
    S_pad = -(-S // unit) * unit
    U = S_pad // dil
    nb = U // side
    f32 = jnp.float32

    def to_sub(a):
        a = jnp.pad(a, ((0, 0), (0, S_pad - S), (0, 0), (0, 0)))
        return a.reshape(B, U, dil, Hg, dh).transpose(0, 2, 1, 3, 4)

    def band(a):
        ap = jnp.pad(a, ((0, 0), (0, 0), (side, side), (0, 0), (0, 0)))
        parts = [ap[:, :, j * side: j * side + U].reshape(B, dil, nb, side, Hg, dh) for j in range(3)]
        return jnp.concatenate(parts, axis=3)

    qs = to_sub(q * dh ** -0.5).reshape(B, dil, nb, side, Hg, dh)
    kb = band(to_sub(k))
    vb = band(to_sub(v))
    s = jnp.einsum('bdnqhe,bdnkhe->bdnqhk', qs, kb).astype(f32)
    rel = jnp.arange(3 * side)[None, :] - side - jnp.arange(side)[:, None]
    u_k = jnp.arange(nb)[:, None] * side - side + jnp.arange(3 * side)[None, :]
    kpos = u_k[None] * dil + jnp.arange(dil)[:, None, None]
    valid_k = (u_k[None] >= 0) & (kpos < S)
    in_win = jnp.abs(rel) <= side
    mask = valid_k[:, :, None, None, :] & in_win[None, None, :, None, :]
    dist = (dil * jnp.abs(rel)).astype(f32)
    bias = -slopes[None, :, None] * dist[:, None, :]
    s = jnp.where(mask[None], s + bias, -1e30)
    m = jnp.max(s, axis=-1, keepdims=True)
    p = jnp.exp(s - m)
    den = jnp.sum(p, axis=-1, keepdims=True)
    o = jnp.einsum('bdnqhk,bdnkhe->bdnqhe', (p / den).astype(v.dtype), vb)
    lse = (m + jnp.log(den))[..., 0]
    o = o.reshape(B, dil, U, Hg, dh).transpose(0, 2, 1, 3, 4).reshape(B, S_pad, Hg, dh)[:, :S]
    lse = lse.reshape(B, dil, U, Hg).transpose(0, 2, 1, 3).reshape(B, S_pad, Hg)[:, :S]
    return o, lse


def dilated_mixture_attention(q, k, v):
    B, S, _ = q.shape
    G = len(A_GROUPS)
    q5 = q.reshape(B, S, G, A_SLOTS, A_HEAD_DIM)
    k5 = k.reshape(B, S, G, A_SLOTS, A_HEAD_DIM)
    v5 = v.reshape(B, S, G, A_SLOTS, A_HEAD_DIM)
    slopes = alibi_slopes(A_HEADS)
    outs, lses = [], []
    for g, (window, dil) in enumerate(A_GROUPS):
        o, lse = dilated_window_attention(q5[:, :, g], k5[:, :, g], v5[:, :, g],
                                          slopes[g * A_SLOTS:(g + 1) * A_SLOTS], window, dil)
        outs.append(o)
        lses.append(lse)
    alpha = jax.nn.softmax(jnp.stack(lses, axis=0), axis=0)
    o = jnp.sum(alpha[..., None] * jnp.stack(outs, axis=0).astype(jnp.float32), axis=0)
    return o.astype(q.dtype).reshape(B, S, A_OUT_W)


def hierarchical_moe(h, w_rg, b_rg, w_re, b_re, w_eg, w_eu, w_ed):
    B, S, D = h.shape
    N = B * S
    t = h.reshape(N, D)
    g_logits = (t @ w_rg + b_rg).astype(jnp.float32)
    g_idx = jnp.argmax(g_logits, axis=-1)
    g_w = jnp.take_along_axis(jax.nn.softmax(g_logits, axis=-1), g_idx[:, None], axis=-1)
    e_logits = (t @ w_re + b_re).astype(jnp.float32).reshape(N, N_GROUPS, EXPERTS_PER_GROUP)
    e_sel = jnp.take_along_axis(e_logits, g_idx[:, None, None], axis=1)[:, 0]
    top_v, top_i = lax.top_k(e_sel, TOP_K)
    weights = g_w * jax.nn.softmax(top_v, axis=-1)
    expert = (g_idx[:, None] * EXPERTS_PER_GROUP + top_i).astype(jnp.int32)
    A = N * TOP_K
    flat_e = expert.reshape(A)
    order = jnp.argsort(flat_e)
    e_sorted = flat_e[order]
    tok_sorted = (order // TOP_K).astype(jnp.int32)
    w_sorted = weights.reshape(A)[order].astype(h.dtype)
    counts = jnp.bincount(flat_e, length=N_EXPERTS)
    padded = (counts + MOE_BLOCK - 1) // MOE_BLOCK * MOE_BLOCK
    start = jnp.cumsum(counts) - counts
    pad_end = jnp.cumsum(padded)
    pad_start = pad_end - padded
    dest = pad_start[e_sorted] + jnp.arange(A, dtype=jnp.int32) - start[e_sorted]
    R = A + N_EXPERTS * MOE_BLOCK
    n_blocks = R // MOE_BLOCK
    row_tok = jnp.full((R,), N, dtype=jnp.int32).at[dest].set(tok_sorted)
    t_pad = jnp.concatenate([t, jnp.zeros((1, D), t.dtype)], axis=0)
    xs = t_pad[row_tok].reshape(n_blocks, MOE_BLOCK, D)
    block_expert = jnp.minimum(jnp.searchsorted(pad_end, jnp.arange(n_blocks, dtype=jnp.int32) * MOE_BLOCK, side='right'), N_EXPERTS - 1)

    def expert_block(args):
        xb, e = args
        return (jax.nn.silu(xb @ w_eg[e]) * (xb @ w_eu[e])) @ w_ed[e]

    ys = lax.map(expert_block, (xs, block_expert)).reshape(R, D)
    y = jax.ops.segment_sum(ys[dest] * w_sorted[:, None], tok_sorted, num_segments=N)
    return y.reshape(B, S, D)


def encoder_layer(x, c, w_ada, b_ada, norm1_w, w_in, b_in, mlstm_gate_b, conv_w, conv_b, mlstm_norm_w,
                  p_a, p_b, w_out, norm2_w, w_rg, b_rg, w_re, b_re, w_eg, w_eu, w_ed):
    B, S, D = x.shape
    mod = jax.nn.silu(c) @ w_ada + b_ada
    sh1, sc1, g1, sh2, sc2, g2 = jnp.split(mod[:, None, :], 6, axis=-1)
    h = rmsnorm(x, norm1_w) * (1 + sc1) + sh1
    proj = h @ w_in + b_in
    q_m, k_m, v_m, o_m, gate_m, q_a, k_a, v_a, gate_A, gate_B = jnp.split(proj, proj_splits(), axis=-1)
    qk = jax.nn.silu(centred_dwconv(jnp.concatenate([q_m, k_m], axis=-1), conv_w, conv_b))
    q_m, k_m = jnp.split(qk, 2, axis=-1)
    heads = lambda a, n: a.reshape(B, S, n, -1).transpose(0, 2, 1, 3)
    h_m = mlstm_bidirectional(heads(q_m, M_HEADS) * M_QK_DIM ** -0.5, heads(k_m, M_HEADS), heads(v_m, M_HEADS),
                              gate_m.reshape(B, S, 4, M_HEADS) + mlstm_gate_b.reshape(4, M_HEADS))
    h_m = h_m.astype(x.dtype).transpose(0, 2, 1, 3)
    h_m = rmsnorm(h_m, mlstm_norm_w.reshape(M_HEADS, M_V_DIM)).reshape(B, S, M_V_W) * jax.nn.sigmoid(o_m)
    y_a = h_m @ p_a
    y_b = dilated_mixture_attention(q_a, k_a, v_a) @ p_b
    mix = (jax.nn.sigmoid(gate_A) * y_a + jax.nn.sigmoid(gate_B) * y_b) @ w_out
    x = x + g1 * mix
    h2 = rmsnorm(x, norm2_w) * (1 + sc2) + sh2
    x = x + g2 * hierarchical_moe(h2, w_rg, b_rg, w_re, b_re, w_eg, w_eu, w_ed)
    return x


def trunk(x, c, w_ada, b_ada, norm1_w, w_in, b_in, mlstm_gate_b, conv_w, conv_b, mlstm_norm_w, p_a, p_b,
          w_out, norm2_w, w_router_group, b_router_group, w_router_expert, b_router_expert,
          w_expert_gate, w_expert_up, w_expert_down, final_norm_w):
    for l in range(DEPTH):
        x = encoder_layer(x, c, w_ada[l], b_ada[l], norm1_w[l], w_in[l], b_in[l], mlstm_gate_b[l], conv_w[l],
                          conv_b[l], mlstm_norm_w[l], p_a[l], p_b[l], w_out[l], norm2_w[l], w_router_group[l],
                          b_router_group[l], w_router_expert[l], b_router_expert[l], w_expert_gate[l],
                          w_expert_up[l], w_expert_down[l])
    return rmsnorm(x, final_norm_w)


def setup_inputs(seed: int = 0) -> dict:
    key = jax.random.key(seed)
    ks = jax.random.split(key, 32)
    f32 = jnp.float32
    L = DEPTH

    def nrm(k, shape, scale):
        return scale * jax.random.normal(k, shape, f32)

    gate_i = -1.0 + nrm(ks[8], (L, 2, M_HEADS), 0.1)
    gate_f = jnp.linspace(3.0, 6.0, M_HEADS, dtype=f32) + nrm(ks[9], (L, 2, M_HEADS), 0.1)
    mlstm_gate_b = jnp.stack([gate_i[:, 0], gate_f[:, 0], gate_i[:, 1], gate_f[:, 1]], axis=1).reshape(L, M_GATE_W)
    conv_w = jnp.array([0.25, 0.5, 0.25], f32)[None, :, None] + nrm(ks[10], (L, M_CONV, 2 * M_QK_W), 0.1)
    return {
        'x_prompt': nrm(ks[0], (BATCH, SEQ, D_MODEL), 1.0),
        'x_sample': nrm(ks[1], (DEC_BATCH, DEC_SEQ, D_MODEL), 1.0),
        'c_prompt': nrm(ks[2], (BATCH, D_MODEL), 1.0),
        'c_sample': nrm(ks[3], (DEC_BATCH, D_MODEL), 1.0),
        'w_ada': nrm(ks[4], (L, D_MODEL, 6 * D_MODEL), 0.5 * D_MODEL ** -0.5),
        'b_ada': nrm(ks[5], (L, 6 * D_MODEL), 0.02),
        'norm1_w': 1.0 + nrm(ks[6], (L, D_MODEL), 0.02),
        'w_in': nrm(ks[7], (L, D_MODEL, PROJ_W), D_MODEL ** -0.5),
        'b_in': nrm(ks[11], (L, PROJ_W), 0.02),
        'mlstm_gate_b': mlstm_gate_b,
        'conv_w': conv_w,
        'conv_b': nrm(ks[12], (L, 2 * M_QK_W), 0.02),
        'mlstm_norm_w': 1.0 + nrm(ks[13], (L, M_V_W), 0.02),
        'p_a': nrm(ks[14], (L, M_V_W, D_MODEL), M_V_W ** -0.5),
        'p_b': nrm(ks[15], (L, A_OUT_W, D_MODEL), A_OUT_W ** -0.5),
        'w_out': nrm(ks[16], (L, D_MODEL, D_MODEL), D_MODEL ** -0.5),
        'norm2_w': 1.0 + nrm(ks[17], (L, D_MODEL), 0.02),
        'w_router_group': nrm(ks[18], (L, D_MODEL, N_GROUPS), D_MODEL ** -0.5),
        'b_router_group': nrm(ks[19], (L, N_GROUPS), 0.01),
        'w_router_expert': nrm(ks[20], (L, D_MODEL, N_EXPERTS), D_MODEL ** -0.5),
        'b_router_expert': nrm(ks[21], (L, N_EXPERTS), 0.01),
        'w_expert_gate': nrm(ks[22], (L, N_EXPERTS, D_MODEL, D_EXPERT), D_MODEL ** -0.5),
        'w_expert_up': nrm(ks[23], (L, N_EXPERTS, D_MODEL, D_EXPERT), D_MODEL ** -0.5),
        'w_expert_down': nrm(ks[24], (L, N_EXPERTS, D_EXPERT, D_MODEL), D_EXPERT ** -0.5),
        'final_norm_w': 1.0 + nrm(ks[25], (D_MODEL,), 0.02),
    }


def reference(x_prompt, x_sample, c_prompt, c_sample, w_ada, b_ada, norm1_w, w_in, b_in, mlstm_gate_b, conv_w,
              conv_b, mlstm_norm_w, p_a, p_b, w_out, norm2_w, w_router_group, b_router_group, w_router_expert,
              b_router_expert, w_expert_gate, w_expert_up, w_expert_down, final_norm_w):
    y_prompt = trunk(x_prompt, c_prompt, w_ada, b_ada, norm1_w, w_in, b_in, mlstm_gate_b, conv_w, conv_b,
                     mlstm_norm_w, p_a, p_b, w_out, norm2_w, w_router_group, b_router_group, w_router_expert,
                     b_router_expert, w_expert_gate, w_expert_up, w_expert_down, final_norm_w)
    y_sample = trunk(x_sample, c_sample, w_ada, b_ada, norm1_w, w_in, b_in, mlstm_gate_b, conv_w, conv_b,
                     mlstm_norm_w, p_a, p_b, w_out, norm2_w, w_router_group, b_router_group, w_router_expert,
                     b_router_expert, w_expert_gate, w_expert_up, w_expert_down, final_norm_w)
    return (y_prompt, y_sample)
```

```python
import functools

import jax
import jax.numpy as jnp
from jax import lax
from jax.experimental import pallas as pl
from jax.experimental.pallas import tpu as pltpu

F32 = jnp.float32
BF16 = jnp.bfloat16

D_MODEL = 2048
RMS_EPS = 1e-6
M_HEADS = 8
M_QK_DIM = 128
M_V_DIM = 256
M_CHUNK = 128
A_GROUPS = ((128, 1), (512, 4), (2048, 16))
A_SLOTS = 4
A_HEADS = A_SLOTS * len(A_GROUPS)
A_HEAD_DIM = 128
N_GROUPS = 4
EXPERTS_PER_GROUP = 8
N_EXPERTS = N_GROUPS * EXPERTS_PER_GROUP
TOP_K = 2
D_EXPERT = 1024

COL_GATE_A = 0
COL_GATE_B = 2048
COL_V_M = 4096
COL_O_M = 6144
COL_Q_M = 8192
COL_K_M = 9216
COL_Q_A = 10240
COL_K_A = 11776
COL_V_A = 13312
COL_GATE_M = 14848
PROJ_COLS = 15360
_SRC_RANGES = ((10784, 14880), (2048, 6144), (0, 2048), (6176, 10784), (6144, 6176))

LANES = 128
MOE_ROWS = 256
ROUTER_COLS = 128
VMEM_LIMIT = 56 * 1024 * 1024


def _sigmoid(x):
    return 1.0 / (1.0 + jnp.exp(-x))


def _cparams(sem, vmem=VMEM_LIMIT):
    return pltpu.CompilerParams(dimension_semantics=sem, vmem_limit_bytes=vmem)


def _ada_kernel(c_ref, w_ref, b_ref, o_ref):
    c = c_ref[...]
    a = (c * _sigmoid(c)).astype(BF16)
    o_ref[...] = jnp.dot(a, w_ref[...].astype(BF16), preferred_element_type=F32) + b_ref[...]


def ada_modulation(c, w_ada, b_ada):
    rows, d = c.shape
    n = w_ada.shape[1]
    tn = 1024
    return pl.pallas_call(
        _ada_kernel,
        grid=(n // tn,),
        in_specs=[pl.BlockSpec((rows, d), lambda j: (0, 0)),
                  pl.BlockSpec((d, tn), lambda j: (0, j)),
                  pl.BlockSpec((1, tn), lambda j: (0, j))],
        out_specs=pl.BlockSpec((rows, tn), lambda j: (0, j)),
        out_shape=jax.ShapeDtypeStruct((rows, n), F32),
        compiler_params=_cparams(("parallel",)),
        name="ada_modulation",
    )(c, w_ada, b_ada.reshape(1, n))


def _norm_mod_kernel(nb0, xp_ref, xs_ref, mod_ref, w_ref, o_ref):
    b = pl.program_id(0)
    x = jnp.where(b < nb0, xp_ref[0], xs_ref[0])
    y = x * lax.rsqrt(jnp.mean(x * x, axis=-1, keepdims=True) + RMS_EPS) * w_ref[...]
    o_ref[0] = (y * (1.0 + mod_ref[0, 1:2, :]) + mod_ref[0, 0:1, :]).astype(o_ref.dtype)


def norm_modulate(xp, xs, mod3, norm_w, ts=512):
    nb0, s, d = xp.shape
    nb1 = xs.shape[0]
    return pl.pallas_call(
        functools.partial(_norm_mod_kernel, nb0),
        grid=(nb0 + nb1, s // ts),
        in_specs=[pl.BlockSpec((1, ts, d), lambda b, t: (jnp.minimum(b, nb0 - 1), jnp.where(b < nb0, t, s // ts - 1), 0)),
                  pl.BlockSpec((1, ts, d), lambda b, t: (jnp.maximum(b - nb0, 0), jnp.where(b < nb0, 0, t), 0)),
                  pl.BlockSpec((1, 6, d), lambda b, t: (b, 0, 0)),
                  pl.BlockSpec((1, d), lambda b, t: (0, 0))],
        out_specs=pl.BlockSpec((1, ts, d), lambda b, t: (b, t, 0)),
        out_shape=jax.ShapeDtypeStruct((nb0 + nb1, s, d), BF16),
        compiler_params=_cparams(("parallel", "parallel")),
        name="norm1_modulate",
    )(xp, xs, mod3, norm_w.reshape(1, d))


def _mm_bias_kernel(a_ref, w_ref, b_ref, o_ref):
    o_ref[...] = jnp.dot(a_ref[...], w_ref[...], preferred_element_type=F32) + b_ref[...]


def matmul_bias(a, w, b, tm=1024, tn=1024):
    m, k = a.shape
    n = w.shape[1]
    return pl.pallas_call(
        _mm_bias_kernel,
        grid=(m // tm, n // tn),
        in_specs=[pl.BlockSpec((tm, k), lambda i, j: (i, 0)),
                  pl.BlockSpec((k, tn), lambda i, j: (0, j)),
                  pl.BlockSpec((1, tn), lambda i, j: (0, j))],
        out_specs=pl.BlockSpec((tm, tn), lambda i, j: (i, j)),
        out_shape=jax.ShapeDtypeStruct((m, n), F32),
        compiler_params=_cparams(("parallel", "parallel")),
        name="in_projection",
    )(a, w, b.reshape(1, n))


def _lane_cumsum(x, reverse):
    n = x.shape[-1]
    lane = lax.broadcasted_iota(jnp.int32, x.shape, x.ndim - 1)
    k = 1
    while k < n:
        if reverse:
            x = x + jnp.where(lane < n - k, pltpu.roll(x, n - k, x.ndim - 1), 0.0)
        else:
            x = x + jnp.where(lane >= k, pltpu.roll(x, k, x.ndim - 1), 0.0)
        k *= 2
    return x


def _conv_silu_chunk(x_ref, w_ref, b_ref, c, n_chunks):
    L = M_CHUNK
    s = n_chunks * L
    r0 = pl.multiple_of(c * L, L)
    x = x_ref[0, pl.ds(r0, L), :]
    prev_row = x_ref[0, pl.ds(jnp.maximum(r0 - 1, 0), 1), :]
    next_row = x_ref[0, pl.ds(jnp.minimum(r0 + L, s - 1), 1), :]
    prev_row = jnp.where(c > 0, prev_row, 0.0)
    next_row = jnp.where(c < n_chunks - 1, next_row, 0.0)
    rows = lax.broadcasted_iota(jnp.int32, x.shape, 0)
    x_prev = jnp.where(rows == 0, prev_row, pltpu.roll(x, 1, 0))
    x_next = jnp.where(rows == L - 1, next_row, pltpu.roll(x, L - 1, 0))
    y = b_ref[...] + x_prev * w_ref[0:1, :] + x * w_ref[1:2, :] + x_next * w_ref[2:3, :]
    return y * _sigmoid(y)


def _mlstm_kernel(q_ref, k_ref, v_ref, o_ref, g_ref, cwq_ref, cwk_ref, cbq_ref, cbk_ref, nw_ref, out_ref,
                  qs_ref, kt_ref, hacc_ref, c_ref, brow_ref, irow_ref, grow_ref, a_ref, gmax_ref):
    L = M_CHUNK
    dv = M_V_DIM
    nc = qs_ref.shape[0]

    def prep(c, carry):
        q = _conv_silu_chunk(q_ref, cwq_ref, cbq_ref, c, nc) * (M_QK_DIM ** -0.5)
        qs_ref[c] = q.astype(BF16)
        kt_ref[c] = _conv_silu_chunk(k_ref, cwk_ref, cbk_ref, c, nc).T
        return carry

    lax.fori_loop(0, nc, prep, 0)

    for d in range(2):
        i_pre = g_ref[0, 0, 2 * d]
        f_pre = g_ref[0, 0, 2 * d + 1]
        log_f = -(jnp.maximum(-f_pre, 0.0) + jnp.log1p(jnp.exp(-jnp.abs(f_pre))))
        b = _lane_cumsum(log_f, reverse=(d == 1))
        a = b[:, L - 1:L] if d == 0 else b[:, 0:1]
        g = a - b + i_pre
        brow_ref[d] = b
        irow_ref[d] = i_pre
        grow_ref[d] = g
        a_ref[d] = jnp.broadcast_to(a, (nc, L))
        gmax_ref[d] = jnp.broadcast_to(jnp.max(g, axis=1, keepdims=True), (nc, L))

    t_idx = lax.broadcasted_iota(jnp.int32, (L, L), 0)
    s_idx = lax.broadcasted_iota(jnp.int32, (L, L), 1)
    ones_ext = jnp.ones((L, LANES), BF16)

    def chunk_step(d, c, m_b):
        r0 = pl.multiple_of(c * L, L)
        q = qs_ref[c]
        kt = kt_ref[c]
        v_ext = jnp.concatenate([v_ref[0, pl.ds(r0, L), :].astype(BF16), ones_ext], axis=1)
        brow = brow_ref[d, pl.ds(c, 1), :]
        irow = irow_ref[d, pl.ds(c, 1), :]
        grow = grow_ref[d, pl.ds(c, 1), :]
        a_b = a_ref[d, pl.ds(c, 1), :]
        gmax_b = gmax_ref[d, pl.ds(c, 1), :]

        bmat = jnp.broadcast_to(brow, (L, L)).T
        causal = (s_idx <= t_idx) if d == 0 else (s_idx >= t_idx)
        d_intra = jnp.where(causal, bmat - brow + irow, -jnp.inf)
        m11 = m_b[:, 0:1]
        d_inter = bmat[:, 0:1] + m11
        m_t = jnp.maximum(d_inter, jnp.max(d_intra, axis=1, keepdims=True))
        s_qk = jnp.dot(q, kt.astype(BF16), preferred_element_type=F32)
        w_intra = jnp.exp(d_intra - m_t) * s_qk
        w_inter = jnp.exp(d_inter - m_t)
        c_ext = c_ref[...]
        num = (w_inter * jnp.dot(q, c_ext.astype(BF16), preferred_element_type=F32)
               + jnp.dot(w_intra.astype(BF16), v_ext, preferred_element_type=F32))
        den = num[:, dv:dv + 1]
        h = num[:, :dv] * (1.0 / jnp.maximum(jnp.abs(den), jnp.exp(-m_t)))

        m_new = jnp.maximum(a_b + m_b, gmax_b)
        decay = jnp.exp(a_b + m_b - m_new)
        wk = jnp.exp(grow - m_new)
        upd = jnp.dot((kt * wk).astype(BF16), v_ext, preferred_element_type=F32)
        c_ref[...] = decay[:, 0:1] * c_ext + upd
        return h, m_new

    def fwd_body(c, m_b):
        h, m_new = chunk_step(0, c, m_b)
        hacc_ref[pl.ds(pl.multiple_of(c * L, L), L), :] = h
        return m_new

    def bwd_body(j, m_b):
        c = nc - 1 - j
        r0 = pl.multiple_of(c * L, L)
        h, m_new = chunk_step(1, c, m_b)
        hs = hacc_ref[pl.ds(r0, L), :] + h
        y = hs * lax.rsqrt(jnp.mean(hs * hs, axis=-1, keepdims=True) + RMS_EPS) * nw_ref[...]
        out_ref[0, pl.ds(r0, L), :] = (y * _sigmoid(o_ref[0, pl.ds(r0, L), :])).astype(out_ref.dtype)
        return m_new

    m0 = jnp.zeros((1, L), F32)
    c_ref[...] = jnp.zeros_like(c_ref)
    lax.fori_loop(0, nc, fwd_body, m0)
    c_ref[...] = jnp.zeros_like(c_ref)
    lax.fori_loop(0, nc, bwd_body, m0)


def mlstm_branch(proj3, gates, conv_w, conv_b, norm_w):
    bsz, s, _ = proj3.shape
    L = M_CHUNK
    nc = s // L
    dk, dv = M_QK_DIM, M_V_DIM
    qk_w = M_HEADS * dk
    return pl.pallas_call(
        _mlstm_kernel,
        grid=(bsz, M_HEADS),
        in_specs=[pl.BlockSpec((1, s, dk), lambda b, h: (b, 0, COL_Q_M // dk + h)),
                  pl.BlockSpec((1, s, dk), lambda b, h: (b, 0, COL_K_M // dk + h)),
                  pl.BlockSpec((1, s, dv), lambda b, h: (b, 0, COL_V_M // dv + h)),
                  pl.BlockSpec((1, s, dv), lambda b, h: (b, 0, COL_O_M // dv + h)),
                  pl.BlockSpec((1, 1, 4, nc, L), lambda b, h: (b, h, 0, 0, 0)),
                  pl.BlockSpec((3, dk), lambda b, h: (0, h)),
                  pl.BlockSpec((3, dk), lambda b, h: (0, qk_w // dk + h)),
                  pl.BlockSpec((1, dk), lambda b, h: (0, h)),
                  pl.BlockSpec((1, dk), lambda b, h: (0, qk_w // dk + h)),
                  pl.BlockSpec((1, dv), lambda b, h: (0, h))],
        out_specs=pl.BlockSpec((1, s, dv), lambda b, h: (b, 0, h)),
        out_shape=jax.ShapeDtypeStruct((bsz, s, M_HEADS * dv), BF16),
        scratch_shapes=[pltpu.VMEM((nc, L, dk), BF16),
                        pltpu.VMEM((nc, dk, L), F32),
                        pltpu.VMEM((s, dv), F32),
                        pltpu.VMEM((dk, dv + LANES), F32),
                        pltpu.VMEM((2, nc, L), F32), pltpu.VMEM((2, nc, L), F32), pltpu.VMEM((2, nc, L), F32),
                        pltpu.VMEM((2, nc, L), F32), pltpu.VMEM((2, nc, L), F32)],
        compiler_params=_cparams(("parallel", "parallel")),
        name="mlstm_branch",
    )(proj3, proj3, proj3, proj3, gates, conv_w, conv_w, conv_b.reshape(1, -1), conv_b.reshape(1, -1),
      norm_w.reshape(1, -1))


def _attn_kernel(slopes_ref, q0, q1, q2, k0, k1, k2, v0, v1, v2, out_ref, og_ref, lse_ref):
    s_len = out_ref.shape[1]
    dh = A_HEAD_DIM
    T = 128
    slot = pl.program_id(1)
    qs, ks, vs = (q0, q1, q2), (k0, k1, k2), (v0, v1, v2)

    for g, (window, dil) in enumerate(A_GROUPS):
        side = window // (2 * dil)
        u_len = s_len // dil
        nqb = u_len // T
        kw = min(3 * T, u_len)
        slope = slopes_ref[g * A_SLOTS + slot] * float(dil)
        q_ref, k_ref, v_ref = qs[g], ks[g], vs[g]

        def block(idx, carry, g=g, dil=dil, side=side, u_len=u_len, nqb=nqb, kw=kw, slope=slope,
                  q_ref=q_ref, k_ref=k_ref, v_ref=v_ref):
            r = idx // nqb
            j = idx % nqb
            u0 = j * T
            ku0 = jnp.clip(u0 - T, 0, u_len - kw)
            q_rows = pl.ds(r + u0 * dil, T, stride=dil) if dil > 1 else pl.ds(pl.multiple_of(u0, T), T)
            k_rows = pl.ds(r + ku0 * dil, kw, stride=dil) if dil > 1 else pl.ds(pl.multiple_of(ku0, T), kw)
            q = (q_ref[0, q_rows, :] * (dh ** -0.5)).astype(BF16)
            kk = k_ref[0, k_rows, :].astype(BF16)
            vv = v_ref[0, k_rows, :].astype(BF16)
            s = lax.dot_general(q, kk, (((1,), (1,)), ((), ())), preferred_element_type=F32)
            uq = u0 + lax.broadcasted_iota(jnp.int32, (T, kw), 0)
            uk = ku0 + lax.broadcasted_iota(jnp.int32, (T, kw), 1)
            rel = jnp.abs(uq - uk)
            s = jnp.where(rel <= side, s - slope * rel.astype(F32), -1e30)
            m = jnp.max(s, axis=1, keepdims=True)
            p = jnp.exp(s - m)
            den = jnp.sum(p, axis=1, keepdims=True)
            o = jnp.dot(p.astype(BF16), vv, preferred_element_type=F32) * (1.0 / den)
            og_ref[g, q_rows, :] = o
            lse_ref[g, q_rows, :] = jnp.broadcast_to(m + jnp.log(den), (T, LANES))
            return carry

        lax.fori_loop(0, dil * nqb, block, 0)

    rows_per = 256

    def merge(i, carry):
        rows = pl.ds(pl.multiple_of(i * rows_per, rows_per), rows_per)
        l0, l1, l2 = lse_ref[0, rows, :], lse_ref[1, rows, :], lse_ref[2, rows, :]
        mx = jnp.maximum(jnp.maximum(l0, l1), l2)
        e0, e1, e2 = jnp.exp(l0 - mx), jnp.exp(l1 - mx), jnp.exp(l2 - mx)
        inv = 1.0 / (e0 + e1 + e2)
        o = (e0 * inv) * og_ref[0, rows, :] + (e1 * inv) * og_ref[1, rows, :] + (e2 * inv) * og_ref[2, rows, :]
        out_ref[0, rows, :] = o.astype(out_ref.dtype)
        return carry

    lax.fori_loop(0, s_len // rows_per, merge, 0)


def attention_branch(proj3, slopes):
    bsz, s, _ = proj3.shape
    dh = A_HEAD_DIM

    def col(base, g):
        return lambda b, t, sl: (b, 0, base // dh + g * A_SLOTS + t)

    grid_spec = pltpu.PrefetchScalarGridSpec(
        num_scalar_prefetch=1,
        grid=(bsz, A_SLOTS),
        in_specs=[pl.BlockSpec((1, s, dh), col(base, g))
                  for base in (COL_Q_A, COL_K_A, COL_V_A) for g in range(len(A_GROUPS))],
        out_specs=pl.BlockSpec((1, s, dh), lambda b, t, sl: (b, 0, t)),
        scratch_shapes=[pltpu.VMEM((3, s, dh), F32), pltpu.VMEM((3, s, LANES), F32)],
    )
    return pl.pallas_call(
        _attn_kernel,
        grid_spec=grid_spec,
        out_shape=jax.ShapeDtypeStruct((bsz, s, A_SLOTS * dh), BF16),
        compiler_params=_cparams(("parallel", "parallel")),
        name="dilated_attention",
    )(slopes, *([proj3] * 9))


def _merge_kernel(n0, xp_ref, xs_ref, hm_ref, at_ref, ga_ref, gb_ref, mod_ref, pa_ref, pb_ref, wo_ref, n2_ref,
                  wrh_ref, wrl_ref, br_ref, x1_ref, h2_ref, lg_ref):
    i = pl.program_id(0)
    x = jnp.where(i < n0, xp_ref[...], xs_ref[...])
    y_a = jnp.dot(hm_ref[...], pa_ref[...], preferred_element_type=F32)
    y_b = jnp.dot(at_ref[...], pb_ref[...], preferred_element_type=F32)
    mixin = _sigmoid(ga_ref[...]) * y_a + _sigmoid(gb_ref[...]) * y_b
    mix = jnp.dot(mixin.astype(BF16), wo_ref[...], preferred_element_type=F32)
    x1 = x + mod_ref[0, 2:3, :] * mix
    x1_ref[...] = x1
    y = x1 * lax.rsqrt(jnp.mean(x1 * x1, axis=-1, keepdims=True) + RMS_EPS) * n2_ref[...]
    h2 = y * (1.0 + mod_ref[0, 4:5, :]) + mod_ref[0, 3:4, :]
    h2_ref[...] = h2
    hi = h2.astype(BF16)
    lo = (h2 - hi.astype(F32)).astype(BF16)
    lg_ref[...] = (jnp.dot(hi, wrh_ref[...], preferred_element_type=F32)
                   + (jnp.dot(hi, wrl_ref[...], preferred_element_type=F32)
                      + jnp.dot(lo, wrh_ref[...], preferred_element_type=F32))
                   + br_ref[...])


def merge_project(xp2, xs2, hm, at, proj, mod3, p_a, p_b, w_out, norm2_w, wr_hi, wr_lo, br, seq, tm=256):
    n, d = hm.shape
    n0 = xp2.shape[0] // tm
    n1 = xs2.shape[0] // tm
    per_seq = seq // tm
    const = dict(pipeline_mode=pl.Buffered(1))
    return pl.pallas_call(
        functools.partial(_merge_kernel, n0),
        grid=(n0 + n1,),
        in_specs=[pl.BlockSpec((tm, d), lambda i: (jnp.minimum(i, n0 - 1), 0)),
                  pl.BlockSpec((tm, d), lambda i: (jnp.maximum(i - n0, 0), 0)),
                  pl.BlockSpec((tm, d), lambda i: (i, 0)),
                  pl.BlockSpec((tm, at.shape[1]), lambda i: (i, 0)),
                  pl.BlockSpec((tm, d), lambda i: (i, COL_GATE_A // d)),
                  pl.BlockSpec((tm, d), lambda i: (i, COL_GATE_B // d)),
                  pl.BlockSpec((1, 6, d), lambda i: (i // per_seq, 0, 0)),
                  pl.BlockSpec(p_a.shape, lambda i: (0, 0), **const),
                  pl.BlockSpec(p_b.shape, lambda i: (0, 0), **const),
                  pl.BlockSpec(w_out.shape, lambda i: (0, 0), **const),
                  pl.BlockSpec((1, d), lambda i: (0, 0)),
                  pl.BlockSpec(wr_hi.shape, lambda i: (0, 0), **const),
                  pl.BlockSpec(wr_lo.shape, lambda i: (0, 0), **const),
                  pl.BlockSpec((1, ROUTER_COLS), lambda i: (0, 0))],
        out_specs=[pl.BlockSpec((tm, d), lambda i: (i, 0)),
                   pl.BlockSpec((tm, d), lambda i: (i, 0)),
                   pl.BlockSpec((tm, ROUTER_COLS), lambda i: (i, 0))],
        out_shape=[jax.ShapeDtypeStruct((n, d), F32),
                   jax.ShapeDtypeStruct((n, d), F32),
                   jax.ShapeDtypeStruct((n, ROUTER_COLS), F32)],
        compiler_params=_cparams(("parallel",)),
        name="merge_project",
    )(xp2, xs2, hm, at, proj, proj, mod3, p_a, p_b, w_out, norm2_w.reshape(1, d), wr_hi, wr_lo, br)


def route(logits, tb):
    n = logits.shape[0]
    g_logits = logits[:, :N_GROUPS]
    e_logits = logits[:, N_GROUPS:N_GROUPS + N_EXPERTS].reshape(n, N_GROUPS, EXPERTS_PER_GROUP)
    g_idx = jnp.argmax(g_logits, axis=-1)
    g_w = jnp.take_along_axis(jax.nn.softmax(g_logits, axis=-1), g_idx[:, None], axis=-1)
    e_sel = jnp.take_along_axis(e_logits, g_idx[:, None, None], axis=1)[:, 0]
    top_v, top_i = lax.top_k(e_sel, TOP_K)
    weights = g_w * jax.nn.softmax(top_v, axis=-1)
    expert = (g_idx[:, None] * EXPERTS_PER_GROUP + top_i).astype(jnp.int32)
    a = n * TOP_K
    flat_e = expert.reshape(a)
    order = jnp.argsort(flat_e).astype(jnp.int32)
    e_sorted = flat_e[order]
    tok_sorted = order // TOP_K
    k_sorted = order % TOP_K
    w_sorted = weights.reshape(a)[order]
    counts = jnp.sum(flat_e[:, None] == jnp.arange(N_EXPERTS, dtype=jnp.int32)[None, :], axis=0, dtype=jnp.int32)
    padded = (counts + tb - 1) // tb * tb
    start = jnp.cumsum(counts) - counts
    pad_end = jnp.cumsum(padded)
    pad_start = pad_end - padded
    dest = pad_start[e_sorted] + jnp.arange(a, dtype=jnp.int32) - start[e_sorted]
    r = a + N_EXPERTS * tb
    n_blocks = r // tb
    row_tok = jnp.zeros((r,), jnp.int32).at[dest].set(tok_sorted)
    row_out = jnp.zeros((r,), jnp.int32).at[dest].set(k_sorted * n + tok_sorted)
    row_w = jnp.zeros((r,), F32).at[dest].set(w_sorted)
    block_start = jnp.arange(n_blocks, dtype=jnp.int32) * tb
    block_expert = jnp.minimum(jnp.searchsorted(pad_end, block_start, side='right'), N_EXPERTS - 1).astype(jnp.int32)
    n_valid = jnp.clip(pad_start[block_expert] + counts[block_expert] - block_start, 0, tb).astype(jnp.int32)
    n_used = (pad_end[-1] // tb).astype(jnp.int32).reshape(1)
    return block_expert, n_used, n_valid, row_tok, row_out, row_w.reshape(n_blocks, 1, tb)


def _expert_kernel(be_ref, nused_ref, nvalid_ref, rtok_ref, rout_ref, h2_hbm, w_ref, wg_ref, wu_ref, wd_ref, out_hbm,
                   xbuf, ybuf, gsem, ssem):
    tb = xbuf.shape[1]
    unroll = 8
    i = pl.program_id(0)
    nb = pl.num_programs(0)
    nused = nused_ref[0]
    slot = i % 2

    def gather_copy(row, s, j):
        return pltpu.make_async_copy(h2_hbm.at[pl.ds(row, 1), :], xbuf.at[s, pl.ds(j, 1), :], gsem.at[s])

    def scatter_copy(row, s, j):
        return pltpu.make_async_copy(ybuf.at[s, pl.ds(j, 1), :], out_hbm.at[pl.ds(row, 1), :], ssem.at[s])

    def issue_gather(blk, s):
        def body(j, carry):
            gather_copy(rtok_ref[blk * tb + j], s, j).start()
            return carry
        lax.fori_loop(0, tb, body, 0, unroll=unroll)

    def wait_gather(s):
        pltpu.make_async_copy(h2_hbm.at[pl.ds(0, tb), :], xbuf.at[s], gsem.at[s]).wait()

    def issue_scatter(blk, s):
        nv = nvalid_ref[blk]

        def group(gi, carry):
            for t in range(unroll):
                j = gi * unroll + t
                scatter_copy(rout_ref[blk * tb + j], s, j).start()
            return carry

        def single(j, carry):
            scatter_copy(rout_ref[blk * tb + j], s, j).start()
            return carry

        lax.fori_loop(0, nv // unroll, group, 0)
        lax.fori_loop(nv // unroll * unroll, nv, single, 0)

    def wait_scatter(blk, s):
        nv = nvalid_ref[blk]
        n8 = pl.multiple_of(nv // 8 * 8, 8)

        @pl.when(n8 > 0)
        def _():
            pltpu.make_async_copy(ybuf.at[s, pl.ds(0, n8), :], out_hbm.at[pl.ds(0, n8), :], ssem.at[s]).wait()

        def single(j, carry):
            scatter_copy(0, s, 0).wait()
            return carry

        lax.fori_loop(n8, nv, single, 0)

    @pl.when(jnp.logical_and(i == 0, nused > 0))
    def _():
        issue_gather(0, 0)

    @pl.when(i + 1 < nused)
    def _():
        issue_gather(i + 1, 1 - slot)

    @pl.when(jnp.logical_and(i >= 2, i - 2 < nused))
    def _():
        wait_scatter(i - 2, slot)

    @pl.when(i < nused)
    def _():
        wait_gather(slot)
        x = xbuf[slot].astype(BF16)
        g = jnp.dot(x, wg_ref[0], preferred_element_type=F32)
        u = jnp.dot(x, wu_ref[0], preferred_element_type=F32)
        hdn = (g * _sigmoid(g) * u).astype(BF16)
        y = jnp.dot(hdn, wd_ref[0], preferred_element_type=F32)
        w_row = w_ref[0]
        for t in range(tb // LANES):
            w_mat = jnp.broadcast_to(w_row[:, t * LANES:(t + 1) * LANES], (LANES, LANES)).T
            w_col = w_mat[:, 0:1]
            ybuf[slot, t * LANES:(t + 1) * LANES, :] = y[t * LANES:(t + 1) * LANES, :] * w_col

        issue_scatter(i, slot)

    @pl.when(i == nb - 1)
    def _():
        @pl.when(jnp.logical_and(nb >= 2, nb - 2 < nused))
        def _():
            wait_scatter(nb - 2, 1 - slot)

        @pl.when(nb - 1 < nused)
        def _():
            wait_scatter(nb - 1, slot)


def expert_ffn(h2, tables, wg, wu, wd, tb=MOE_ROWS):
    block_expert, n_used, n_valid, row_tok, row_out, row_w = tables
    n, d = h2.shape
    nb = block_expert.shape[0]
    de = wg.shape[2]
    grid_spec = pltpu.PrefetchScalarGridSpec(
        num_scalar_prefetch=5,
        grid=(nb,),
        in_specs=[pl.BlockSpec(memory_space=pl.ANY),
                  pl.BlockSpec((1, 1, tb), lambda i, be, *_: (i, 0, 0)),
                  pl.BlockSpec((1, d, de), lambda i, be, *_: (be[i], 0, 0)),
                  pl.BlockSpec((1, d, de), lambda i, be, *_: (be[i], 0, 0)),
                  pl.BlockSpec((1, de, d), lambda i, be, *_: (be[i], 0, 0))],
        out_specs=pl.BlockSpec(memory_space=pl.ANY),
        scratch_shapes=[pltpu.VMEM((2, tb, d), F32), pltpu.VMEM((2, tb, d), F32),
                        pltpu.SemaphoreType.DMA((2,)), pltpu.SemaphoreType.DMA((2,))],
    )
    return pl.pallas_call(
        _expert_kernel,
        grid_spec=grid_spec,
        out_shape=jax.ShapeDtypeStruct((TOP_K * n, d), F32),
        compiler_params=_cparams(("arbitrary",)),
        name="expert_ffn",
    )(block_expert, n_used, n_valid, row_tok, row_out, h2, row_w, wg, wu, wd)


def _final_kernel(x1_ref, y0_ref, y1_ref, mod_ref, w_ref, o_ref):
    x = x1_ref[...] + mod_ref[0, 5:6, :] * (y0_ref[...] + y1_ref[...])
    o_ref[...] = x * lax.rsqrt(jnp.mean(x * x, axis=-1, keepdims=True) + RMS_EPS) * w_ref[...]


def final_norm(x1, moe, mod3, final_w, n_tok, row0, rows, seq, tm=512):
    d = x1.shape[1]
    off = row0 // tm
    off1 = (n_tok + row0) // tm
    per_seq = seq // tm
    return pl.pallas_call(
        _final_kernel,
        grid=(rows // tm,),
        in_specs=[pl.BlockSpec((tm, d), lambda i: (off + i, 0)),
                  pl.BlockSpec((tm, d), lambda i: (off + i, 0)),
                  pl.BlockSpec((tm, d), lambda i: (off1 + i, 0)),
                  pl.BlockSpec((1, 6, d), lambda i: ((off + i) // per_seq, 0, 0)),
                  pl.BlockSpec((1, d), lambda i: (0, 0))],
        out_specs=pl.BlockSpec((tm, d), lambda i: (i, 0)),
        out_shape=jax.ShapeDtypeStruct((rows, d), F32),
        compiler_params=_cparams(("parallel",)),
        name="final_norm",
    )(x1, moe, moe, mod3, final_w.reshape(1, d))


def _permute_cols(w):
    return jnp.concatenate([w[..., lo:hi] for lo, hi in _SRC_RANGES], axis=-1)


def kernel(x_prompt, x_sample, c_prompt, c_sample, w_ada, b_ada, norm1_w, w_in, b_in, mlstm_gate_b, conv_w, conv_b, mlstm_norm_w, p_a, p_b, w_out, norm2_w, w_router_group, b_router_group, w_router_expert, b_router_expert, w_expert_gate, w_expert_up, w_expert_down, final_norm_w):
    bp, seq, d = x_prompt.shape
    bs = x_sample.shape[0]
    bt = bp + bs
    n = bt * seq
    layer = 0

    pad = PROJ_COLS - (COL_GATE_M + 4 * M_HEADS)
    w_in_p = jnp.concatenate([_permute_cols(w_in[layer]), jnp.zeros((d, pad), F32)], axis=1).astype(BF16)
    b_gate = b_in[layer, 6144:6176] + mlstm_gate_b[layer]
    b_in_p = jnp.concatenate([_permute_cols(b_in[layer])[:COL_GATE_M], b_gate, jnp.zeros((pad,), F32)])
    wr = jnp.concatenate([w_router_group[layer], w_router_expert[layer],
                          jnp.zeros((d, ROUTER_COLS - N_GROUPS - N_EXPERTS), F32)], axis=1)
    br = jnp.concatenate([b_router_group[layer], b_router_expert[layer],
                          jnp.zeros((ROUTER_COLS - N_GROUPS - N_EXPERTS,), F32)]).reshape(1, ROUTER_COLS)
    wr_hi = wr.astype(BF16)
    wr_lo = (wr - wr_hi.astype(F32)).astype(BF16)
    slopes = 2.0 ** (-8.0 * jnp.arange(1, A_HEADS + 1, dtype=F32) / A_HEADS)

    c_all = jnp.concatenate([c_prompt, c_sample, jnp.zeros((16 - bt, d), F32)], axis=0)
    mod3 = ada_modulation(c_all, w_ada[layer], b_ada[layer])[:bt].reshape(bt, 6, d)

    h = norm_modulate(x_prompt, x_sample, mod3, norm1_w[layer])
    proj = matmul_bias(h.reshape(n, d), w_in_p, b_in_p)
    proj3 = proj.reshape(bt, seq, PROJ_COLS)

    nc = seq // M_CHUNK
    gates = proj3[:, :, COL_GATE_M:COL_GATE_M + 4 * M_HEADS].reshape(bt, nc, M_CHUNK, 4, M_HEADS)
    gates = gates.transpose(0, 4, 3, 1, 2)
    hm = mlstm_branch(proj3, gates, conv_w[layer], conv_b[layer], mlstm_norm_w[layer])
    at = attention_branch(proj3, slopes)

    x1, h2, logits = merge_project(
        x_prompt.reshape(bp * seq, d), x_sample.reshape(bs * seq, d), hm.reshape(n, -1), at.reshape(n, -1), proj,
        mod3, p_a[layer].astype(BF16), p_b[layer].astype(BF16), w_out[layer].astype(BF16), norm2_w[layer],
        wr_hi, wr_lo, br, seq)

    tables = route(logits, MOE_ROWS)
    moe = expert_ffn(h2, tables, w_expert_gate[layer].astype(BF16), w_expert_up[layer].astype(BF16),
                     w_expert_down[layer].astype(BF16))

    y_p = final_norm(x1, moe, mod3, final_norm_w, n, 0, bp * seq, seq)
    y_s = final_norm(x1, moe, mod3, final_norm_w, n, bp * seq, bs * seq, seq)
    return (y_p.reshape(bp, seq, d), y_s.reshape(bs, seq, d))
```

```python
import functools

import jax
import jax.numpy as jnp
from jax import lax
from jax.experimental import pallas as pl
from jax.experimental.pallas import tpu as pltpu

F32 = jnp.float32
BF16 = jnp.bfloat16

D_MODEL = 2048
RMS_EPS = 1e-6
M_HEADS = 8
M_QK_DIM = 128
M_V_DIM = 256
M_CHUNK = 128
A_GROUPS = ((128, 1), (512, 4), (2048, 16))
A_SLOTS = 4
A_HEADS = A_SLOTS * len(A_GROUPS)
A_HEAD_DIM = 128
N_GROUPS = 4
EXPERTS_PER_GROUP = 8
N_EXPERTS = N_GROUPS * EXPERTS_PER_GROUP
TOP_K = 2
D_EXPERT = 1024

COL_GATE_A = 0
COL_GATE_B = 2048
COL_V_M = 4096
COL_O_M = 6144
COL_Q_M = 8192
COL_K_M = 9216
COL_Q_A = 10240
COL_K_A = 11776
COL_V_A = 13312
COL_GATE_M = 14848
PROJ_COLS = 15360
_SRC_RANGES = ((10784, 14880), (2048, 6144), (0, 2048), (6176, 10784), (6144, 6176))

LANES = 128
MOE_ROWS = 256
ROUTER_COLS = 128
VMEM_LIMIT = 56 * 1024 * 1024


def _sigmoid(x):
    return 1.0 / (1.0 + jnp.exp(-x))


def _cparams(sem, vmem=VMEM_LIMIT):
    return pltpu.CompilerParams(dimension_semantics=sem, vmem_limit_bytes=vmem)


def _ada_kernel(c_ref, w_ref, b_ref, o_ref):
    c = c_ref[...]
    a = (c * _sigmoid(c)).astype(BF16)
    o_ref[...] = jnp.dot(a, w_ref[...].astype(BF16), preferred_element_type=F32) + b_ref[...]


def ada_modulation(c, w_ada, b_ada):
    rows, d = c.shape
    n = w_ada.shape[1]
    tn = 1024
    return pl.pallas_call(
        _ada_kernel,
        grid=(n // tn,),
        in_specs=[pl.BlockSpec((rows, d), lambda j: (0, 0)),
                  pl.BlockSpec((d, tn), lambda j: (0, j)),
                  pl.BlockSpec((1, tn), lambda j: (0, j))],
        out_specs=pl.BlockSpec((rows, tn), lambda j: (0, j)),
        out_shape=jax.ShapeDtypeStruct((rows, n), F32),
        compiler_params=_cparams(("parallel",)),
        name="ada_modulation",
    )(c, w_ada, b_ada.reshape(1, n))


def _norm_mod_kernel(nb0, xp_ref, xs_ref, mod_ref, w_ref, o_ref):
    b = pl.program_id(0)
    x = jnp.where(b < nb0, xp_ref[0], xs_ref[0])
    y = x * lax.rsqrt(jnp.mean(x * x, axis=-1, keepdims=True) + RMS_EPS) * w_ref[...]
    o_ref[0] = (y * (1.0 + mod_ref[0, 1:2, :]) + mod_ref[0, 0:1, :]).astype(o_ref.dtype)


def norm_modulate(xp, xs, mod3, norm_w, ts=512):
    nb0, s, d = xp.shape
    nb1 = xs.shape[0]
    return pl.pallas_call(
        functools.partial(_norm_mod_kernel, nb0),
        grid=(nb0 + nb1, s // ts),
        in_specs=[pl.BlockSpec((1, ts, d), lambda b, t: (jnp.minimum(b, nb0 - 1), jnp.where(b < nb0, t, s // ts - 1), 0)),
                  pl.BlockSpec((1, ts, d), lambda b, t: (jnp.maximum(b - nb0, 0), jnp.where(b < nb0, 0, t), 0)),
                  pl.BlockSpec((1, 6, d), lambda b, t: (b, 0, 0)),
                  pl.BlockSpec((1, d), lambda b, t: (0, 0))],
        out_specs=pl.BlockSpec((1, ts, d), lambda b, t: (b, t, 0)),
        out_shape=jax.ShapeDtypeStruct((nb0 + nb1, s, d), BF16),
        compiler_params=_cparams(("parallel", "parallel")),
        name="norm1_modulate",
    )(xp, xs, mod3, norm_w.reshape(1, d))


def _mm_bias_kernel(a_ref, w_ref, b_ref, o_ref):
    o_ref[...] = jnp.dot(a_ref[...], w_ref[...], preferred_element_type=F32) + b_ref[...]


def matmul_bias(a, w, b, tm=1024, tn=1024):
    m, k = a.shape
    n = w.shape[1]
    return pl.pallas_call(
        _mm_bias_kernel,
        grid=(m // tm, n // tn),
        in_specs=[pl.BlockSpec((tm, k), lambda i, j: (i, 0)),
                  pl.BlockSpec((k, tn), lambda i, j: (0, j)),
                  pl.BlockSpec((1, tn), lambda i, j: (0, j))],
        out_specs=pl.BlockSpec((tm, tn), lambda i, j: (i, j)),
        out_shape=jax.ShapeDtypeStruct((m, n), F32),
        compiler_params=_cparams(("parallel", "parallel")),
        name="in_projection",
    )(a, w, b.reshape(1, n))


def _lane_cumsum(x, reverse):
    n = x.shape[-1]
    lane = lax.broadcasted_iota(jnp.int32, x.shape, x.ndim - 1)
    k = 1
    while k < n:
        if reverse:
            x = x + jnp.where(lane < n - k, pltpu.roll(x, n - k, x.ndim - 1), 0.0)
        else:
            x = x + jnp.where(lane >= k, pltpu.roll(x, k, x.ndim - 1), 0.0)
        k *= 2
    return x


def _conv_silu_chunk(x_ref, w_ref, b_ref, c, n_chunks):
    L = M_CHUNK
    s = n_chunks * L
    r0 = pl.multiple_of(c * L, L)
    x = x_ref[0, pl.ds(r0, L), :]
    prev_row = x_ref[0, pl.ds(jnp.maximum(r0 - 1, 0), 1), :]
    next_row = x_ref[0, pl.ds(jnp.minimum(r0 + L, s - 1), 1), :]
    prev_row = jnp.where(c > 0, prev_row, 0.0)
    next_row = jnp.where(c < n_chunks - 1, next_row, 0.0)
    rows = lax.broadcasted_iota(jnp.int32, x.shape, 0)
    x_prev = jnp.where(rows == 0, prev_row, pltpu.roll(x, 1, 0))
    x_next = jnp.where(rows == L - 1, next_row, pltpu.roll(x, L - 1, 0))
    y = b_ref[...] + x_prev * w_ref[0:1, :] + x * w_ref[1:2, :] + x_next * w_ref[2:3, :]
    return y * _sigmoid(y)


def _mlstm_kernel(q_ref, k_ref, v_ref, o_ref, g_ref, cwq_ref, cwk_ref, cbq_ref, cbk_ref, nw_ref, out_ref,
                  qs_ref, kt_ref, hacc_ref, c_ref, brow_ref, irow_ref, grow_ref, a_ref, gmax_ref):
    L = M_CHUNK
    dv = M_V_DIM
    nc = qs_ref.shape[0]

    def prep(c, carry):
        q = _conv_silu_chunk(q_ref, cwq_ref, cbq_ref, c, nc) * (M_QK_DIM ** -0.5)
        qs_ref[c] = q.astype(BF16)
        kt_ref[c] = _conv_silu_chunk(k_ref, cwk_ref, cbk_ref, c, nc).T
        return carry

    lax.fori_loop(0, nc, prep, 0)

    for d in range(2):
        i_pre = g_ref[0, 0, 2 * d]
        f_pre = g_ref[0, 0, 2 * d + 1]
        log_f = -(jnp.maximum(-f_pre, 0.0) + jnp.log1p(jnp.exp(-jnp.abs(f_pre))))
        b = _lane_cumsum(log_f, reverse=(d == 1))
        a = b[:, L - 1:L] if d == 0 else b[:, 0:1]
        g = a - b + i_pre
        brow_ref[d] = b
        irow_ref[d] = i_pre
        grow_ref[d] = g
        a_ref[d] = jnp.broadcast_to(a, (nc, L))
        gmax_ref[d] = jnp.broadcast_to(jnp.max(g, axis=1, keepdims=True), (nc, L))

    t_idx = lax.broadcasted_iota(jnp.int32, (L, L), 0)
    s_idx = lax.broadcasted_iota(jnp.int32, (L, L), 1)
    ones_ext = jnp.ones((L, LANES), BF16)

    def chunk_step(d, c, m_b):
        r0 = pl.multiple_of(c * L, L)
        q = qs_ref[c]
        kt = kt_ref[c]
        v_ext = jnp.concatenate([v_ref[0, pl.ds(r0, L), :].astype(BF16), ones_ext], axis=1)
        brow = brow_ref[d, pl.ds(c, 1), :]
        irow = irow_ref[d, pl.ds(c, 1), :]
        grow = grow_ref[d, pl.ds(c, 1), :]
        a_b = a_ref[d, pl.ds(c, 1), :]
        gmax_b = gmax_ref[d, pl.ds(c, 1), :]

        bmat = jnp.broadcast_to(brow, (L, L)).T
        causal = (s_idx <= t_idx) if d == 0 else (s_idx >= t_idx)
        d_intra = jnp.where(causal, bmat - brow + irow, -jnp.inf)
        m11 = m_b[:, 0:1]
        d_inter = bmat[:, 0:1] + m11
        m_t = jnp.maximum(d_inter, jnp.max(d_intra, axis=1, keepdims=True))
        s_qk = jnp.dot(q, kt.astype(BF16), preferred_element_type=F32)
        w_intra = jnp.exp(d_intra - m_t) * s_qk
        w_inter = jnp.exp(d_inter - m_t)
        c_ext = c_ref[d]
        num = (w_inter * jnp.dot(q, c_ext.astype(BF16), preferred_element_type=F32)
               + jnp.dot(w_intra.astype(BF16), v_ext, preferred_element_type=F32))
        den = num[:, dv:dv + 1]
        h = num[:, :dv] * (1.0 / jnp.maximum(jnp.abs(den), jnp.exp(-m_t)))

        m_new = jnp.maximum(a_b + m_b, gmax_b)
        decay = jnp.exp(a_b + m_b - m_new)
        wk = jnp.exp(grow - m_new)
        upd = jnp.dot((kt * wk).astype(BF16), v_ext, preferred_element_type=F32)
        c_ref[d] = decay[:, 0:1] * c_ext + upd
        return h, m_new

    def finish(c, hs):
        r0 = pl.multiple_of(c * L, L)
        y = hs * lax.rsqrt(jnp.mean(hs * hs, axis=-1, keepdims=True) + RMS_EPS) * nw_ref[...]
        out_ref[0, pl.ds(r0, L), :] = (y * _sigmoid(o_ref[0, pl.ds(r0, L), :])).astype(out_ref.dtype)

    def first_half(j, carry):
        m_f, m_r = carry
        cf, cr = j, nc - 1 - j
        h_f, m_f = chunk_step(0, cf, m_f)
        h_r, m_r = chunk_step(1, cr, m_r)
        hacc_ref[pl.ds(pl.multiple_of(cf * L, L), L), :] = h_f
        hacc_ref[pl.ds(pl.multiple_of(cr * L, L), L), :] = h_r
        return m_f, m_r

    def second_half(j, carry):
        m_f, m_r = carry
        cf, cr = j, nc - 1 - j
        h_f, m_f = chunk_step(0, cf, m_f)
        h_r, m_r = chunk_step(1, cr, m_r)
        finish(cf, h_f + hacc_ref[pl.ds(pl.multiple_of(cf * L, L), L), :])
        finish(cr, hacc_ref[pl.ds(pl.multiple_of(cr * L, L), L), :] + h_r)
        return m_f, m_r

    m0 = jnp.zeros((1, L), F32)
    c_ref[...] = jnp.zeros_like(c_ref)
    carry = lax.fori_loop(0, nc // 2, first_half, (m0, m0))
    lax.fori_loop(nc // 2, nc, second_half, carry)


def mlstm_branch(proj3, gates, conv_w, conv_b, norm_w):
    bsz, s, _ = proj3.shape
    L = M_CHUNK
    nc = s // L
    dk, dv = M_QK_DIM, M_V_DIM
    qk_w = M_HEADS * dk
    return pl.pallas_call(
        _mlstm_kernel,
        grid=(bsz, M_HEADS),
        in_specs=[pl.BlockSpec((1, s, dk), lambda b, h: (b, 0, COL_Q_M // dk + h)),
                  pl.BlockSpec((1, s, dk), lambda b, h: (b, 0, COL_K_M // dk + h)),
                  pl.BlockSpec((1, s, dv), lambda b, h: (b, 0, COL_V_M // dv + h)),
                  pl.BlockSpec((1, s, dv), lambda b, h: (b, 0, COL_O_M // dv + h)),
                  pl.BlockSpec((1, 1, 4, nc, L), lambda b, h: (b, h, 0, 0, 0)),
                  pl.BlockSpec((3, dk), lambda b, h: (0, h)),
                  pl.BlockSpec((3, dk), lambda b, h: (0, qk_w // dk + h)),
                  pl.BlockSpec((1, dk), lambda b, h: (0, h)),
                  pl.BlockSpec((1, dk), lambda b, h: (0, qk_w // dk + h)),
                  pl.BlockSpec((1, dv), lambda b, h: (0, h))],
        out_specs=pl.BlockSpec((1, s, dv), lambda b, h: (b, 0, h)),
        out_shape=jax.ShapeDtypeStruct((bsz, s, M_HEADS * dv), BF16),
        scratch_shapes=[pltpu.VMEM((nc, L, dk), BF16),
                        pltpu.VMEM((nc, dk, L), F32),
                        pltpu.VMEM((s, dv), F32),
                        pltpu.VMEM((2, dk, dv + LANES), F32),
                        pltpu.VMEM((2, nc, L), F32), pltpu.VMEM((2, nc, L), F32), pltpu.VMEM((2, nc, L), F32),
                        pltpu.VMEM((2, nc, L), F32), pltpu.VMEM((2, nc, L), F32)],
        compiler_params=_cparams(("parallel", "parallel")),
        name="mlstm_branch",
    )(proj3, proj3, proj3, proj3, gates, conv_w, conv_w, conv_b.reshape(1, -1), conv_b.reshape(1, -1),
      norm_w.reshape(1, -1))


def _attn_kernel(slopes_ref, q0, q1, q2, k0, k1, k2, v0, v1, v2, out_ref, og_ref, lse_ref):
    s_len = out_ref.shape[1]
    dh = A_HEAD_DIM
    T = 128
    slot = pl.program_id(1)
    qs, ks, vs = (q0, q1, q2), (k0, k1, k2), (v0, v1, v2)

    for g, (window, dil) in enumerate(A_GROUPS):
        side = window // (2 * dil)
        u_len = s_len // dil
        nqb = u_len // T
        kw = min(3 * T, u_len)
        slope = slopes_ref[g * A_SLOTS + slot] * float(dil)
        q_ref, k_ref, v_ref = qs[g], ks[g], vs[g]

        def block(idx, carry, g=g, dil=dil, side=side, u_len=u_len, nqb=nqb, kw=kw, slope=slope,
                  q_ref=q_ref, k_ref=k_ref, v_ref=v_ref):
            r = idx // nqb
            j = idx % nqb
            u0 = j * T
            ku0 = jnp.clip(u0 - T, 0, u_len - kw)
            q_rows = pl.ds(r + u0 * dil, T, stride=dil) if dil > 1 else pl.ds(pl.multiple_of(u0, T), T)
            k_rows = pl.ds(r + ku0 * dil, kw, stride=dil) if dil > 1 else pl.ds(pl.multiple_of(ku0, T), kw)
            q = (q_ref[0, q_rows, :] * (dh ** -0.5)).astype(BF16)
            kk = k_ref[0, k_rows, :].astype(BF16)
            vv = v_ref[0, k_rows, :].astype(BF16)
            s = lax.dot_general(q, kk, (((1,), (1,)), ((), ())), preferred_element_type=F32)
            uq = u0 + lax.broadcasted_iota(jnp.int32, (T, kw), 0)
            uk = ku0 + lax.broadcasted_iota(jnp.int32, (T, kw), 1)
            rel = jnp.abs(uq - uk)
            s = jnp.where(rel <= side, s - slope * rel.astype(F32), -1e30)
            m = jnp.max(s, axis=1, keepdims=True)
            p = jnp.exp(s - m)
            den = jnp.sum(p, axis=1, keepdims=True)
            o = jnp.dot(p.astype(BF16), vv, preferred_element_type=F32) * (1.0 / den)
            og_ref[g, q_rows, :] = o
            lse_ref[g, q_rows, :] = jnp.broadcast_to(m + jnp.log(den), (T, LANES))
            return carry

        lax.fori_loop(0, dil * nqb, block, 0, unroll=4)

    rows_per = 256

    def merge(i, carry):
        rows = pl.ds(pl.multiple_of(i * rows_per, rows_per), rows_per)
        l0, l1, l2 = lse_ref[0, rows, :], lse_ref[1, rows, :], lse_ref[2, rows, :]
        mx = jnp.maximum(jnp.maximum(l0, l1), l2)
        e0, e1, e2 = jnp.exp(l0 - mx), jnp.exp(l1 - mx), jnp.exp(l2 - mx)
        inv = 1.0 / (e0 + e1 + e2)
        o = (e0 * inv) * og_ref[0, rows, :] + (e1 * inv) * og_ref[1, rows, :] + (e2 * inv) * og_ref[2, rows, :]
        out_ref[0, rows, :] = o.astype(out_ref.dtype)
        return carry

    lax.fori_loop(0, s_len // rows_per, merge, 0)


def attention_branch(proj3, slopes):
    bsz, s, _ = proj3.shape
    dh = A_HEAD_DIM

    def col(base, g):
        return lambda b, t, sl: (b, 0, base // dh + g * A_SLOTS + t)

    grid_spec = pltpu.PrefetchScalarGridSpec(
        num_scalar_prefetch=1,
        grid=(bsz, A_SLOTS),
        in_specs=[pl.BlockSpec((1, s, dh), col(base, g))
                  for base in (COL_Q_A, COL_K_A, COL_V_A) for g in range(len(A_GROUPS))],
        out_specs=pl.BlockSpec((1, s, dh), lambda b, t, sl: (b, 0, t)),
        scratch_shapes=[pltpu.VMEM((3, s, dh), F32), pltpu.VMEM((3, s, LANES), F32)],
    )
    return pl.pallas_call(
        _attn_kernel,
        grid_spec=grid_spec,
        out_shape=jax.ShapeDtypeStruct((bsz, s, A_SLOTS * dh), BF16),
        compiler_params=_cparams(("parallel", "parallel")),
        name="dilated_attention",
    )(slopes, *([proj3] * 9))


def _merge_kernel(n0, xp_ref, xs_ref, hm_ref, at_ref, ga_ref, gb_ref, mod_ref, pa_ref, pb_ref, wo_ref, n2_ref,
                  wrh_ref, wrl_ref, br_ref, x1_ref, h2_ref, lg_ref):
    i = pl.program_id(0)
    x = jnp.where(i < n0, xp_ref[...], xs_ref[...])
    y_a = jnp.dot(hm_ref[...], pa_ref[...], preferred_element_type=F32)
    y_b = jnp.dot(at_ref[...], pb_ref[...], preferred_element_type=F32)
    mixin = _sigmoid(ga_ref[...]) * y_a + _sigmoid(gb_ref[...]) * y_b
    mix = jnp.dot(mixin.astype(BF16), wo_ref[...], preferred_element_type=F32)
    x1 = x + mod_ref[0, 2:3, :] * mix
    x1_ref[...] = x1
    y = x1 * lax.rsqrt(jnp.mean(x1 * x1, axis=-1, keepdims=True) + RMS_EPS) * n2_ref[...]
    h2 = y * (1.0 + mod_ref[0, 4:5, :]) + mod_ref[0, 3:4, :]
    h2_ref[...] = h2
    hi = h2.astype(BF16)
    lo = (h2 - hi.astype(F32)).astype(BF16)
    lg_ref[...] = (jnp.dot(hi, wrh_ref[...], preferred_element_type=F32)
                   + (jnp.dot(hi, wrl_ref[...], preferred_element_type=F32)
                      + jnp.dot(lo, wrh_ref[...], preferred_element_type=F32))
                   + br_ref[...])


def merge_project(xp2, xs2, hm, at, proj, mod3, p_a, p_b, w_out, norm2_w, wr_hi, wr_lo, br, seq, tm=256):
    n, d = hm.shape
    n0 = xp2.shape[0] // tm
    n1 = xs2.shape[0] // tm
    per_seq = seq // tm
    const = dict(pipeline_mode=pl.Buffered(1))
    return pl.pallas_call(
        functools.partial(_merge_kernel, n0),
        grid=(n0 + n1,),
        in_specs=[pl.BlockSpec((tm, d), lambda i: (jnp.minimum(i, n0 - 1), 0)),
                  pl.BlockSpec((tm, d), lambda i: (jnp.maximum(i - n0, 0), 0)),
                  pl.BlockSpec((tm, d), lambda i: (i, 0)),
                  pl.BlockSpec((tm, at.shape[1]), lambda i: (i, 0)),
                  pl.BlockSpec((tm, d), lambda i: (i, COL_GATE_A // d)),
                  pl.BlockSpec((tm, d), lambda i: (i, COL_GATE_B // d)),
                  pl.BlockSpec((1, 6, d), lambda i: (i // per_seq, 0, 0)),
                  pl.BlockSpec(p_a.shape, lambda i: (0, 0), **const),
                  pl.BlockSpec(p_b.shape, lambda i: (0, 0), **const),
                  pl.BlockSpec(w_out.shape, lambda i: (0, 0), **const),
                  pl.BlockSpec((1, d), lambda i: (0, 0)),
                  pl.BlockSpec(wr_hi.shape, lambda i: (0, 0), **const),
                  pl.BlockSpec(wr_lo.shape, lambda i: (0, 0), **const),
                  pl.BlockSpec((1, ROUTER_COLS), lambda i: (0, 0))],
        out_specs=[pl.BlockSpec((tm, d), lambda i: (i, 0)),
                   pl.BlockSpec((tm, d), lambda i: (i, 0)),
                   pl.BlockSpec((tm, ROUTER_COLS), lambda i: (i, 0))],
        out_shape=[jax.ShapeDtypeStruct((n, d), F32),
                   jax.ShapeDtypeStruct((n, d), F32),
                   jax.ShapeDtypeStruct((n, ROUTER_COLS), F32)],
        compiler_params=_cparams(("parallel",)),
        name="merge_project",
    )(xp2, xs2, hm, at, proj, proj, mod3, p_a, p_b, w_out, norm2_w.reshape(1, d), wr_hi, wr_lo, br)


def route(logits, tb):
    n = logits.shape[0]
    g_logits = logits[:, :N_GROUPS]
    e_logits = logits[:, N_GROUPS:N_GROUPS + N_EXPERTS].reshape(n, N_GROUPS, EXPERTS_PER_GROUP)
    g_idx = jnp.argmax(g_logits, axis=-1)
    g_w = jnp.take_along_axis(jax.nn.softmax(g_logits, axis=-1), g_idx[:, None], axis=-1)
    e_sel = jnp.take_along_axis(e_logits, g_idx[:, None, None], axis=1)[:, 0]
    top_v, top_i = lax.top_k(e_sel, TOP_K)
    weights = g_w * jax.nn.softmax(top_v, axis=-1)
    expert = (g_idx[:, None] * EXPERTS_PER_GROUP + top_i).astype(jnp.int32)
    a = n * TOP_K
    flat_e = expert.reshape(a)
    e_ids = jnp.arange(N_EXPERTS, dtype=jnp.int32)
    counts = jnp.sum(flat_e[:, None] == e_ids[None, :], axis=0, dtype=jnp.int32)
    padded = (counts + tb - 1) // tb * tb
    pad_end = jnp.cumsum(padded)
    pad_start = pad_end - padded
    filler_e = jnp.repeat(e_ids, tb)
    filler_j = jnp.tile(jnp.arange(tb, dtype=jnp.int32), N_EXPERTS)
    filler_key = jnp.where(filler_j < (padded - counts)[filler_e], 2 * filler_e + 1, 2 * N_EXPERTS + 1)
    keys = jnp.concatenate([2 * flat_e, filler_key])
    ids = jnp.arange(a, dtype=jnp.int32)
    filler0 = jnp.zeros((N_EXPERTS * tb,), jnp.int32)
    tok_src = jnp.concatenate([ids // TOP_K, filler0])
    out_src = jnp.concatenate([(ids % TOP_K) * n + ids // TOP_K, filler0])
    _, row_tok, row_out = lax.sort((keys, tok_src, out_src), num_keys=1)
    n_blocks = (a + N_EXPERTS * tb) // tb
    block_start = jnp.arange(n_blocks, dtype=jnp.int32) * tb
    block_expert = jnp.minimum(jnp.sum(block_start[:, None] >= pad_end[None, :], axis=1), N_EXPERTS - 1).astype(jnp.int32)
    n_valid = jnp.clip(pad_start[block_expert] + counts[block_expert] - block_start, 0, tb).astype(jnp.int32)
    n_used = (pad_end[-1] // tb).astype(jnp.int32).reshape(1)
    return (block_expert, n_used, n_valid, row_tok, row_out), weights


def _expert_kernel(be_ref, nused_ref, nvalid_ref, rtok_ref, rout_ref, h2_hbm, wg_ref, wu_ref, wd_ref, out_hbm,
                   xbuf, ybuf, gsem, ssem):
    tb = xbuf.shape[1]
    unroll = 8
    i = pl.program_id(0)
    nb = pl.num_programs(0)
    nused = nused_ref[0]
    slot = i % 2

    def gather_copy(row, s, j):
        return pltpu.make_async_copy(h2_hbm.at[pl.ds(row, 1), :], xbuf.at[s, pl.ds(j, 1), :], gsem.at[s])

    def scatter_copy(row, s, j):
        return pltpu.make_async_copy(ybuf.at[s, pl.ds(j, 1), :], out_hbm.at[pl.ds(row, 1), :], ssem.at[s])

    def issue_gather(blk, s):
        def body(j, carry):
            gather_copy(rtok_ref[blk * tb + j], s, j).start()
            return carry
        lax.fori_loop(0, tb, body, 0, unroll=unroll)

    def wait_gather(s):
        pltpu.make_async_copy(h2_hbm.at[pl.ds(0, tb), :], xbuf.at[s], gsem.at[s]).wait()

    def issue_scatter(blk, s):
        nv = nvalid_ref[blk]

        def group(gi, carry):
            for t in range(unroll):
                j = gi * unroll + t
                scatter_copy(rout_ref[blk * tb + j], s, j).start()
            return carry

        def single(j, carry):
            scatter_copy(rout_ref[blk * tb + j], s, j).start()
            return carry

        lax.fori_loop(0, nv // unroll, group, 0)
        lax.fori_loop(nv // unroll * unroll, nv, single, 0)

    def wait_scatter(blk, s):
        nv = nvalid_ref[blk]
        n8 = pl.multiple_of(nv // 8 * 8, 8)

        @pl.when(n8 > 0)
        def _():
            pltpu.make_async_copy(ybuf.at[s, pl.ds(0, n8), :], out_hbm.at[pl.ds(0, n8), :], ssem.at[s]).wait()

        def single(j, carry):
            scatter_copy(0, s, 0).wait()
            return carry

        lax.fori_loop(n8, nv, single, 0)

    @pl.when(jnp.logical_and(i == 0, nused > 0))
    def _():
        issue_gather(0, 0)

    @pl.when(i + 1 < nused)
    def _():
        issue_gather(i + 1, 1 - slot)

    @pl.when(jnp.logical_and(i >= 2, i - 2 < nused))
    def _():
        wait_scatter(i - 2, slot)

    @pl.when(i < nused)
    def _():
        wait_gather(slot)
        x = xbuf[slot].astype(BF16)
        g = jnp.dot(x, wg_ref[0], preferred_element_type=F32)
        u = jnp.dot(x, wu_ref[0], preferred_element_type=F32)
        hdn = (g * _sigmoid(g) * u).astype(BF16)
        ybuf[slot] = jnp.dot(hdn, wd_ref[0], preferred_element_type=F32)
        issue_scatter(i, slot)

    @pl.when(i == nb - 1)
    def _():
        @pl.when(jnp.logical_and(nb >= 2, nb - 2 < nused))
        def _():
            wait_scatter(nb - 2, 1 - slot)

        @pl.when(nb - 1 < nused)
        def _():
            wait_scatter(nb - 1, slot)


def expert_ffn(h2, tables, wg, wu, wd, tb=MOE_ROWS):
    block_expert, n_used, n_valid, row_tok, row_out = tables
    n, d = h2.shape
    nb = block_expert.shape[0]
    de = wg.shape[2]
    grid_spec = pltpu.PrefetchScalarGridSpec(
        num_scalar_prefetch=5,
        grid=(nb,),
        in_specs=[pl.BlockSpec(memory_space=pl.ANY),
                  pl.BlockSpec((1, d, de), lambda i, be, *_: (be[i], 0, 0)),
                  pl.BlockSpec((1, d, de), lambda i, be, *_: (be[i], 0, 0)),
                  pl.BlockSpec((1, de, d), lambda i, be, *_: (be[i], 0, 0))],
        out_specs=pl.BlockSpec(memory_space=pl.ANY),
        scratch_shapes=[pltpu.VMEM((2, tb, d), F32), pltpu.VMEM((2, tb, d), F32),
                        pltpu.SemaphoreType.DMA((2,)), pltpu.SemaphoreType.DMA((2,))],
    )
    return pl.pallas_call(
        _expert_kernel,
        grid_spec=grid_spec,
        out_shape=jax.ShapeDtypeStruct((TOP_K * n, d), F32),
        compiler_params=_cparams(("arbitrary",)),
        name="expert_ffn",
    )(block_expert, n_used, n_valid, row_tok, row_out, h2, wg, wu, wd)


def _final_kernel(x1_ref, y0_ref, y1_ref, rw_ref, mod_ref, w_ref, o_ref):
    rw = rw_ref[...]
    moe = y0_ref[...] * rw[:, 0:1] + y1_ref[...] * rw[:, 1:2]
    x = x1_ref[...] + mod_ref[0, 5:6, :] * moe
    o_ref[...] = x * lax.rsqrt(jnp.mean(x * x, axis=-1, keepdims=True) + RMS_EPS) * w_ref[...]


def final_norm(x1, moe, route_w, mod3, final_w, n_tok, row0, rows, seq, tm=512):
    d = x1.shape[1]
    off = row0 // tm
    off1 = (n_tok + row0) // tm
    per_seq = seq // tm
    return pl.pallas_call(
        _final_kernel,
        grid=(rows // tm,),
        in_specs=[pl.BlockSpec((tm, d), lambda i: (off + i, 0)),
                  pl.BlockSpec((tm, d), lambda i: (off + i, 0)),
                  pl.BlockSpec((tm, d), lambda i: (off1 + i, 0)),
                  pl.BlockSpec((tm, TOP_K), lambda i: (off + i, 0)),
                  pl.BlockSpec((1, 6, d), lambda i: ((off + i) // per_seq, 0, 0)),
                  pl.BlockSpec((1, d), lambda i: (0, 0))],
        out_specs=pl.BlockSpec((tm, d), lambda i: (i, 0)),
        out_shape=jax.ShapeDtypeStruct((rows, d), F32),
        compiler_params=_cparams(("parallel",)),
        name="final_norm",
    )(x1, moe, moe, route_w, mod3, final_w.reshape(1, d))


def _permute_cols(w):
    return jnp.concatenate([w[..., lo:hi] for lo, hi in _SRC_RANGES], axis=-1)


def kernel(x_prompt, x_sample, c_prompt, c_sample, w_ada, b_ada, norm1_w, w_in, b_in, mlstm_gate_b, conv_w, conv_b, mlstm_norm_w, p_a, p_b, w_out, norm2_w, w_router_group, b_router_group, w_router_expert, b_router_expert, w_expert_gate, w_expert_up, w_expert_down, final_norm_w):
    bp, seq, d = x_prompt.shape
    bs = x_sample.shape[0]
    bt = bp + bs
    n = bt * seq
    layer = 0

    pad = PROJ_COLS - (COL_GATE_M + 4 * M_HEADS)
    w_in_p = jnp.concatenate([_permute_cols(w_in[layer]), jnp.zeros((d, pad), F32)], axis=1).astype(BF16)
    b_gate = b_in[layer, 6144:6176] + mlstm_gate_b[layer]
    b_in_p = jnp.concatenate([_permute_cols(b_in[layer])[:COL_GATE_M], b_gate, jnp.zeros((pad,), F32)])
    wr = jnp.concatenate([w_router_group[layer], w_router_expert[layer],
                          jnp.zeros((d, ROUTER_COLS - N_GROUPS - N_EXPERTS), F32)], axis=1)
    br = jnp.concatenate([b_router_group[layer], b_router_expert[layer],
                          jnp.zeros((ROUTER_COLS - N_GROUPS - N_EXPERTS,), F32)]).reshape(1, ROUTER_COLS)
    wr_hi = wr.astype(BF16)
    wr_lo = (wr - wr_hi.astype(F32)).astype(BF16)
    slopes = 2.0 ** (-8.0 * jnp.arange(1, A_HEADS + 1, dtype=F32) / A_HEADS)

    c_all = jnp.concatenate([c_prompt, c_sample, jnp.zeros((16 - bt, d), F32)], axis=0)
    mod3 = ada_modulation(c_all, w_ada[layer], b_ada[layer])[:bt].reshape(bt, 6, d)

    h = norm_modulate(x_prompt, x_sample, mod3, norm1_w[layer])
    proj = matmul_bias(h.reshape(n, d), w_in_p, b_in_p)
    proj3 = proj.reshape(bt, seq, PROJ_COLS)

    nc = seq // M_CHUNK
    gates = proj3[:, :, COL_GATE_M:COL_GATE_M + 4 * M_HEADS].reshape(bt, nc, M_CHUNK, 4, M_HEADS)
    gates = gates.transpose(0, 4, 3, 1, 2)
    hm = mlstm_branch(proj3, gates, conv_w[layer], conv_b[layer], mlstm_norm_w[layer])
    at = attention_branch(proj3, slopes)

    x1, h2, logits = merge_project(
        x_prompt.reshape(bp * seq, d), x_sample.reshape(bs * seq, d), hm.reshape(n, -1), at.reshape(n, -1), proj,
        mod3, p_a[layer].astype(BF16), p_b[layer].astype(BF16), w_out[layer].astype(BF16), norm2_w[layer],
        wr_hi, wr_lo, br, seq)

    tables, route_w = route(logits, MOE_ROWS)
    moe = expert_ffn(h2, tables, w_expert_gate[layer].astype(BF16), w_expert_up[layer].astype(BF16),
                     w_expert_down[layer].astype(BF16))

    y_p = final_norm(x1, moe, route_w, mod3, final_norm_w, n, 0, bp * seq, seq)
    y_s = final_norm(x1, moe, route_w, mod3, final_norm_w, n, bp * seq, bs * seq, seq)
    return (y_p.reshape(bp, seq, d), y_s.reshape(bs, seq, d))
```

```python
import functools

import jax
import jax.numpy as jnp
from jax import lax
from jax.experimental import pallas as pl
from jax.experimental.pallas import tpu as pltpu

F32 = jnp.float32
BF16 = jnp.bfloat16

D_MODEL = 2048
RMS_EPS = 1e-6
M_HEADS = 8
M_QK_DIM = 128
M_V_DIM = 256
M_CHUNK = 128
A_GROUPS = ((128, 1), (512, 4), (2048, 16))
A_SLOTS = 4
A_HEADS = A_SLOTS * len(A_GROUPS)
A_HEAD_DIM = 128
N_GROUPS = 4
EXPERTS_PER_GROUP = 8
N_EXPERTS = N_GROUPS * EXPERTS_PER_GROUP
TOP_K = 2
D_EXPERT = 1024

COL_GATE_A = 0
COL_GATE_B = 2048
COL_V_M = 4096
COL_O_M = 6144
COL_Q_M = 8192
COL_K_M = 9216
COL_Q_A = 10240
COL_K_A = 11776
COL_V_A = 13312
COL_GATE_M = 14848
PROJ_COLS = 15360
_SRC_RANGES = ((10784, 14880), (2048, 6144), (0, 2048), (6176, 10784), (6144, 6176))

LANES = 128
MOE_ROWS = 256
ROUTER_COLS = 128
VMEM_LIMIT = 56 * 1024 * 1024


def _sigmoid(x):
    return 1.0 / (1.0 + jnp.exp(-x))


def _cparams(sem, vmem=VMEM_LIMIT):
    return pltpu.CompilerParams(dimension_semantics=sem, vmem_limit_bytes=vmem)


def _ada_kernel(c_ref, w_ref, b_ref, o_ref):
    c = c_ref[...]
    a = (c * _sigmoid(c)).astype(BF16)
    o_ref[...] = jnp.dot(a, w_ref[...].astype(BF16), preferred_element_type=F32) + b_ref[...]


def ada_modulation(c, w_ada, b_ada):
    rows, d = c.shape
    n = w_ada.shape[1]
    tn = 1024
    return pl.pallas_call(
        _ada_kernel,
        grid=(n // tn,),
        in_specs=[pl.BlockSpec((rows, d), lambda j: (0, 0)),
                  pl.BlockSpec((d, tn), lambda j: (0, j)),
                  pl.BlockSpec((1, tn), lambda j: (0, j))],
        out_specs=pl.BlockSpec((rows, tn), lambda j: (0, j)),
        out_shape=jax.ShapeDtypeStruct((rows, n), F32),
        compiler_params=_cparams(("parallel",)),
        name="ada_modulation",
    )(c, w_ada, b_ada.reshape(1, n))


def _norm_mod_kernel(nb0, xp_ref, xs_ref, mod_ref, w_ref, o_ref):
    b = pl.program_id(0)
    x = jnp.where(b < nb0, xp_ref[0], xs_ref[0])
    y = x * lax.rsqrt(jnp.mean(x * x, axis=-1, keepdims=True) + RMS_EPS) * w_ref[...]
    o_ref[0] = (y * (1.0 + mod_ref[0, 1:2, :]) + mod_ref[0, 0:1, :]).astype(o_ref.dtype)


def norm_modulate(xp, xs, mod3, norm_w, ts=512):
    nb0, s, d = xp.shape
    nb1 = xs.shape[0]
    return pl.pallas_call(
        functools.partial(_norm_mod_kernel, nb0),
        grid=(nb0 + nb1, s // ts),
        in_specs=[pl.BlockSpec((1, ts, d), lambda b, t: (jnp.minimum(b, nb0 - 1), jnp.where(b < nb0, t, s // ts - 1), 0)),
                  pl.BlockSpec((1, ts, d), lambda b, t: (jnp.maximum(b - nb0, 0), jnp.where(b < nb0, 0, t), 0)),
                  pl.BlockSpec((1, 6, d), lambda b, t: (b, 0, 0)),
                  pl.BlockSpec((1, d), lambda b, t: (0, 0))],
        out_specs=pl.BlockSpec((1, ts, d), lambda b, t: (b, t, 0)),
        out_shape=jax.ShapeDtypeStruct((nb0 + nb1, s, d), BF16),
        compiler_params=_cparams(("parallel", "parallel")),
        name="norm1_modulate",
    )(xp, xs, mod3, norm_w.reshape(1, d))


def _mm_bias_kernel(a_ref, w_ref, b_ref, o_ref):
    o_ref[...] = jnp.dot(a_ref[...], w_ref[...], preferred_element_type=F32) + b_ref[...]


def matmul_bias(a, w, b, tm=1024, tn=1024):
    m, k = a.shape
    n = w.shape[1]
    return pl.pallas_call(
        _mm_bias_kernel,
        grid=(m // tm, n // tn),
        in_specs=[pl.BlockSpec((tm, k), lambda i, j: (i, 0)),
                  pl.BlockSpec((k, tn), lambda i, j: (0, j)),
                  pl.BlockSpec((1, tn), lambda i, j: (0, j))],
        out_specs=pl.BlockSpec((tm, tn), lambda i, j: (i, j)),
        out_shape=jax.ShapeDtypeStruct((m, n), F32),
        compiler_params=_cparams(("parallel", "parallel")),
        name="in_projection",
    )(a, w, b.reshape(1, n))


def _lane_scan(x, op, fill, reverse):
    n = x.shape[-1]
    axis = x.ndim - 1
    lane = lax.broadcasted_iota(jnp.int32, x.shape, axis)
    k = 1
    while k < n:
        if reverse:
            x = op(x, jnp.where(lane < n - k, pltpu.roll(x, n - k, axis), fill))
        else:
            x = op(x, jnp.where(lane >= k, pltpu.roll(x, k, axis), fill))
        k *= 2
    return x


def _conv_silu_chunk(x_ref, w_ref, b_ref, c, n_chunks):
    L = M_CHUNK
    s = n_chunks * L
    r0 = pl.multiple_of(c * L, L)
    x = x_ref[0, pl.ds(r0, L), :]
    prev_row = x_ref[0, pl.ds(jnp.maximum(r0 - 1, 0), 1), :]
    next_row = x_ref[0, pl.ds(jnp.minimum(r0 + L, s - 1), 1), :]
    prev_row = jnp.where(c > 0, prev_row, 0.0)
    next_row = jnp.where(c < n_chunks - 1, next_row, 0.0)
    rows = lax.broadcasted_iota(jnp.int32, x.shape, 0)
    x_prev = jnp.where(rows == 0, prev_row, pltpu.roll(x, 1, 0))
    x_next = jnp.where(rows == L - 1, next_row, pltpu.roll(x, L - 1, 0))
    y = b_ref[...] + x_prev * w_ref[0:1, :] + x * w_ref[1:2, :] + x_next * w_ref[2:3, :]
    return y * _sigmoid(y)


def _mlstm_kernel(q_ref, k_ref, v_ref, o_ref, g_ref, cwq_ref, cwk_ref, cbq_ref, cbk_ref, nw_ref, out_ref,
                  qs_ref, kt_ref, hacc_ref, c_ref, u_ref, negm_ref, ib_ref, wk_ref, decay_ref, m0_ref, m1_ref):
    L = M_CHUNK
    dk = M_QK_DIM
    dv = M_V_DIM
    nc = qs_ref.shape[0]

    def prep(c, carry):
        q = _conv_silu_chunk(q_ref, cwq_ref, cbq_ref, c, nc) * (M_QK_DIM ** -0.5)
        qs_ref[c] = q.astype(BF16)
        kt_ref[c] = _conv_silu_chunk(k_ref, cwk_ref, cbk_ref, c, nc).T
        return carry

    lax.fori_loop(0, nc, prep, 0)

    for d in range(2):
        rev = d == 1
        i_pre = g_ref[0, 0, 2 * d]
        f_pre = g_ref[0, 0, 2 * d + 1]
        log_f = -(jnp.maximum(-f_pre, 0.0) + jnp.log1p(jnp.exp(-jnp.abs(f_pre))))
        b = _lane_scan(log_f, jnp.add, 0.0, rev)
        a = jnp.broadcast_to(b[:, 0:1] if rev else b[:, L - 1:L], (nc, L))
        g = a - b + i_pre
        g_max = jnp.broadcast_to(jnp.max(g, axis=1, keepdims=True), (nc, L))
        m = jnp.zeros((1, L), F32)
        for c in (range(nc - 1, -1, -1) if rev else range(nc)):
            m0_ref[d, c:c + 1, :] = m
            m = jnp.maximum(a[c:c + 1, :] + m, g_max[c:c + 1, :])
            m1_ref[d, c:c + 1, :] = m
        m0 = m0_ref[d]
        m1 = m1_ref[d]
        ib = i_pre - b
        m_t = jnp.maximum(b + m0, b + _lane_scan(ib, jnp.maximum, -jnp.inf, rev))
        ib_ref[d] = ib
        u_ref[d] = b - m_t
        negm_ref[d] = -m_t
        wk_ref[d] = jnp.exp(g - m1)
        decay_ref[d] = jnp.exp(a + m0 - m1)

    t_idx = lax.broadcasted_iota(jnp.int32, (L, L), 0)
    s_idx = lax.broadcasted_iota(jnp.int32, (L, L), 1)
    ones_ext = jnp.ones((L, LANES), BF16)

    def chunk_step(d, c):
        r0 = pl.multiple_of(c * L, L)
        q = qs_ref[c]
        kt = kt_ref[c]
        v_ext = jnp.concatenate([v_ref[0, pl.ds(r0, L), :].astype(BF16), ones_ext], axis=1)

        def row(ref):
            return ref[d, pl.ds(c, 1), :]

        umat = jnp.broadcast_to(row(u_ref), (L, L)).T
        nmat = jnp.broadcast_to(row(negm_ref), (L, L)).T
        causal = (s_idx <= t_idx) if d == 0 else (s_idx >= t_idx)
        w_intra = jnp.where(causal, jnp.exp(umat + row(ib_ref)), 0.0)
        w_inter = jnp.exp(umat + row(m0_ref))
        s_qk = jnp.dot(q, kt.astype(BF16), preferred_element_type=F32)
        c_ext = c_ref[d]
        lhs = jnp.concatenate([(w_intra * s_qk).astype(BF16), (q.astype(F32) * w_inter).astype(BF16)], axis=1)
        rhs = jnp.concatenate([v_ext, c_ext.astype(BF16)], axis=0)
        num = jnp.dot(lhs, rhs, preferred_element_type=F32)
        r = 1.0 / jnp.maximum(jnp.abs(num[:, dv:]), jnp.exp(nmat))
        h = num[:, :dv] * jnp.concatenate([r] * (dv // LANES), axis=1)

        upd = jnp.dot((kt * row(wk_ref)).astype(BF16), v_ext, preferred_element_type=F32)
        decay = jnp.broadcast_to(row(decay_ref), (dk, LANES))
        c_ref[d] = jnp.concatenate([decay] * (c_ext.shape[1] // LANES), axis=1) * c_ext + upd
        return h

    def finish(c, hs):
        r0 = pl.multiple_of(c * L, L)
        y = hs * lax.rsqrt(jnp.mean(hs * hs, axis=-1, keepdims=True) + RMS_EPS) * nw_ref[...]
        out_ref[0, pl.ds(r0, L), :] = (y * _sigmoid(o_ref[0, pl.ds(r0, L), :])).astype(out_ref.dtype)

    def first_half(j, carry):
        cf, cr = j, nc - 1 - j
        hacc_ref[pl.ds(pl.multiple_of(cf * L, L), L), :] = chunk_step(0, cf)
        hacc_ref[pl.ds(pl.multiple_of(cr * L, L), L), :] = chunk_step(1, cr)
        return carry

    def second_half(j, carry):
        cf, cr = j, nc - 1 - j
        finish(cf, chunk_step(0, cf) + hacc_ref[pl.ds(pl.multiple_of(cf * L, L), L), :])
        finish(cr, hacc_ref[pl.ds(pl.multiple_of(cr * L, L), L), :] + chunk_step(1, cr))
        return carry

    c_ref[...] = jnp.zeros_like(c_ref)
    lax.fori_loop(0, nc // 2, first_half, 0)
    lax.fori_loop(nc // 2, nc, second_half, 0)


def mlstm_branch(proj3, gates, conv_w, conv_b, norm_w):
    bsz, s, _ = proj3.shape
    L = M_CHUNK
    nc = s // L
    dk, dv = M_QK_DIM, M_V_DIM
    qk_w = M_HEADS * dk
    return pl.pallas_call(
        _mlstm_kernel,
        grid=(bsz, M_HEADS),
        in_specs=[pl.BlockSpec((1, s, dk), lambda b, h: (b, 0, COL_Q_M // dk + h)),
                  pl.BlockSpec((1, s, dk), lambda b, h: (b, 0, COL_K_M // dk + h)),
                  pl.BlockSpec((1, s, dv), lambda b, h: (b, 0, COL_V_M // dv + h)),
                  pl.BlockSpec((1, s, dv), lambda b, h: (b, 0, COL_O_M // dv + h)),
                  pl.BlockSpec((1, 1, 4, nc, L), lambda b, h: (b, h, 0, 0, 0)),
                  pl.BlockSpec((3, dk), lambda b, h: (0, h)),
                  pl.BlockSpec((3, dk), lambda b, h: (0, qk_w // dk + h)),
                  pl.BlockSpec((1, dk), lambda b, h: (0, h)),
                  pl.BlockSpec((1, dk), lambda b, h: (0, qk_w // dk + h)),
                  pl.BlockSpec((1, dv), lambda b, h: (0, h))],
        out_specs=pl.BlockSpec((1, s, dv), lambda b, h: (b, 0, h)),
        out_shape=jax.ShapeDtypeStruct((bsz, s, M_HEADS * dv), BF16),
        scratch_shapes=[pltpu.VMEM((nc, L, dk), BF16),
                        pltpu.VMEM((nc, dk, L), F32),
                        pltpu.VMEM((s, dv), F32),
                        pltpu.VMEM((2, dk, dv + LANES), F32),
                        *([pltpu.VMEM((2, nc, L), F32)] * 7)],
        compiler_params=_cparams(("parallel", "parallel")),
        name="mlstm_branch",
    )(proj3, proj3, proj3, proj3, gates, conv_w, conv_w, conv_b.reshape(1, -1), conv_b.reshape(1, -1),
      norm_w.reshape(1, -1))


def _attn_kernel(slopes_ref, q0, q1, q2, k0, k1, k2, v0, v1, v2, out_ref, og_ref, lse_ref, bias_ref):
    s_len = out_ref.shape[1]
    dh = A_HEAD_DIM
    T = 128
    slot = pl.program_id(1)
    qs, ks, vs = (q0, q1, q2), (k0, k1, k2), (v0, v1, v2)

    for g, (window, dil) in enumerate(A_GROUPS):
        side = window // (2 * dil)
        u_len = s_len // dil
        nqb = u_len // T
        kw = min(T + 2 * side, u_len)
        slope = slopes_ref[g * A_SLOTS + slot] * float(dil)
        q_ref, k_ref, v_ref = qs[g], ks[g], vs[g]

        offsets = (0, side, kw - T) if nqb > 1 else (0,)
        for case, off in enumerate(offsets):
            rel = jnp.abs(lax.broadcasted_iota(jnp.int32, (T, kw), 0) + off
                          - lax.broadcasted_iota(jnp.int32, (T, kw), 1))
            bias_ref[g, case, :, :kw] = jnp.where(rel <= side, -slope * rel.astype(F32), -1e30)

        def block(idx, carry, g=g, dil=dil, side=side, u_len=u_len, nqb=nqb, kw=kw,
                  q_ref=q_ref, k_ref=k_ref, v_ref=v_ref):
            r = idx // nqb
            j = idx % nqb
            u0 = j * T
            ku0 = jnp.clip(u0 - side, 0, u_len - kw)
            case = jnp.where(j == 0, 0, jnp.where(j == nqb - 1, 2, 1)) if nqb > 1 else 0
            q_rows = pl.ds(r + u0 * dil, T, stride=dil) if dil > 1 else pl.ds(pl.multiple_of(u0, T), T)
            k_rows = pl.ds(r + ku0 * dil, kw, stride=dil) if dil > 1 else pl.ds(pl.multiple_of(ku0, side), kw)
            q = (q_ref[0, q_rows, :] * (dh ** -0.5)).astype(BF16)
            kk = k_ref[0, k_rows, :].astype(BF16)
            vv = v_ref[0, k_rows, :].astype(BF16)
            s = lax.dot_general(q, kk, (((1,), (1,)), ((), ())), preferred_element_type=F32)
            s = s + bias_ref[g, case, :, :kw]
            m = jnp.max(s, axis=1, keepdims=True)
            p = jnp.exp(s - m)
            den = jnp.sum(p, axis=1, keepdims=True)
            o = jnp.dot(p.astype(BF16), vv, preferred_element_type=F32) * (1.0 / den)
            og_ref[g, q_rows, :] = o
            lse_ref[g, q_rows, :] = jnp.broadcast_to(m + jnp.log(den), (T, LANES))
            return carry

        lax.fori_loop(0, dil * nqb, block, 0, unroll=4)

    rows_per = 256

    def merge(i, carry):
        rows = pl.ds(pl.multiple_of(i * rows_per, rows_per), rows_per)
        l0, l1, l2 = lse_ref[0, rows, :], lse_ref[1, rows, :], lse_ref[2, rows, :]
        mx = jnp.maximum(jnp.maximum(l0, l1), l2)
        e0, e1, e2 = jnp.exp(l0 - mx), jnp.exp(l1 - mx), jnp.exp(l2 - mx)
        inv = 1.0 / (e0 + e1 + e2)
        o = (e0 * inv) * og_ref[0, rows, :] + (e1 * inv) * og_ref[1, rows, :] + (e2 * inv) * og_ref[2, rows, :]
        out_ref[0, rows, :] = o.astype(out_ref.dtype)
        return carry

    lax.fori_loop(0, s_len // rows_per, merge, 0)


def attention_branch(proj3, slopes):
    bsz, s, _ = proj3.shape
    dh = A_HEAD_DIM

    def col(base, g):
        return lambda b, t, sl: (b, 0, base // dh + g * A_SLOTS + t)

    grid_spec = pltpu.PrefetchScalarGridSpec(
        num_scalar_prefetch=1,
        grid=(bsz, A_SLOTS),
        in_specs=[pl.BlockSpec((1, s, dh), col(base, g))
                  for base in (COL_Q_A, COL_K_A, COL_V_A) for g in range(len(A_GROUPS))],
        out_specs=pl.BlockSpec((1, s, dh), lambda b, t, sl: (b, 0, t)),
        scratch_shapes=[pltpu.VMEM((3, s, dh), F32), pltpu.VMEM((3, s, LANES), F32),
                        pltpu.VMEM((len(A_GROUPS), 3, 128, 256), F32)],
    )
    return pl.pallas_call(
        _attn_kernel,
        grid_spec=grid_spec,
        out_shape=jax.ShapeDtypeStruct((bsz, s, A_SLOTS * dh), BF16),
        compiler_params=_cparams(("parallel", "parallel")),
        name="dilated_attention",
    )(slopes, *([proj3] * 9))


def _merge_kernel(n0, xp_ref, xs_ref, hm_ref, at_ref, ga_ref, gb_ref, mod_ref, pa_ref, pb_ref, wo_ref, n2_ref,
                  wrh_ref, wrl_ref, br_ref, x1_ref, h2_ref, lg_ref):
    i = pl.program_id(0)
    x = jnp.where(i < n0, xp_ref[...], xs_ref[...])
    y_a = jnp.dot(hm_ref[...], pa_ref[...], preferred_element_type=F32)
    y_b = jnp.dot(at_ref[...], pb_ref[...], preferred_element_type=F32)
    mixin = _sigmoid(ga_ref[...]) * y_a + _sigmoid(gb_ref[...]) * y_b
    mix = jnp.dot(mixin.astype(BF16), wo_ref[...], preferred_element_type=F32)
    x1 = x + mod_ref[0, 2:3, :] * mix
    x1_ref[...] = x1
    y = x1 * lax.rsqrt(jnp.mean(x1 * x1, axis=-1, keepdims=True) + RMS_EPS) * n2_ref[...]
    h2 = y * (1.0 + mod_ref[0, 4:5, :]) + mod_ref[0, 3:4, :]
    h2_ref[...] = h2
    hi = h2.astype(BF16)
    lo = (h2 - hi.astype(F32)).astype(BF16)
    lg_ref[...] = (jnp.dot(hi, wrh_ref[...], preferred_element_type=F32)
                   + (jnp.dot(hi, wrl_ref[...], preferred_element_type=F32)
                      + jnp.dot(lo, wrh_ref[...], preferred_element_type=F32))
                   + br_ref[...])


def merge_project(xp2, xs2, hm, at, proj, mod3, p_a, p_b, w_out, norm2_w, wr_hi, wr_lo, br, seq, tm=256):
    n, d = hm.shape
    n0 = xp2.shape[0] // tm
    n1 = xs2.shape[0] // tm
    per_seq = seq // tm
    const = dict(pipeline_mode=pl.Buffered(1))
    return pl.pallas_call(
        functools.partial(_merge_kernel, n0),
        grid=(n0 + n1,),
        in_specs=[pl.BlockSpec((tm, d), lambda i: (jnp.minimum(i, n0 - 1), 0)),
                  pl.BlockSpec((tm, d), lambda i: (jnp.maximum(i - n0, 0), 0)),
                  pl.BlockSpec((tm, d), lambda i: (i, 0)),
                  pl.BlockSpec((tm, at.shape[1]), lambda i: (i, 0)),
                  pl.BlockSpec((tm, d), lambda i: (i, COL_GATE_A // d)),
                  pl.BlockSpec((tm, d), lambda i: (i, COL_GATE_B // d)),
                  pl.BlockSpec((1, 6, d), lambda i: (i // per_seq, 0, 0)),
                  pl.BlockSpec(p_a.shape, lambda i: (0, 0), **const),
                  pl.BlockSpec(p_b.shape, lambda i: (0, 0), **const),
                  pl.BlockSpec(w_out.shape, lambda i: (0, 0), **const),
                  pl.BlockSpec((1, d), lambda i: (0, 0)),
                  pl.BlockSpec(wr_hi.shape, lambda i: (0, 0), **const),
                  pl.BlockSpec(wr_lo.shape, lambda i: (0, 0), **const),
                  pl.BlockSpec((1, ROUTER_COLS), lambda i: (0, 0))],
        out_specs=[pl.BlockSpec((tm, d), lambda i: (i, 0)),
                   pl.BlockSpec((tm, d), lambda i: (i, 0)),
                   pl.BlockSpec((tm, ROUTER_COLS), lambda i: (i, 0))],
        out_shape=[jax.ShapeDtypeStruct((n, d), F32),
                   jax.ShapeDtypeStruct((n, d), F32),
                   jax.ShapeDtypeStruct((n, ROUTER_COLS), F32)],
        compiler_params=_cparams(("parallel",)),
        name="merge_project",
    )(xp2, xs2, hm, at, proj, proj, mod3, p_a, p_b, w_out, norm2_w.reshape(1, d), wr_hi, wr_lo, br)


def route(logits, tb):
    n = logits.shape[0]
    g_logits = logits[:, :N_GROUPS]
    e_logits = logits[:, N_GROUPS:N_GROUPS + N_EXPERTS].reshape(n, N_GROUPS, EXPERTS_PER_GROUP)
    g_idx = jnp.argmax(g_logits, axis=-1)
    g_w = jnp.take_along_axis(jax.nn.softmax(g_logits, axis=-1), g_idx[:, None], axis=-1)
    e_sel = jnp.take_along_axis(e_logits, g_idx[:, None, None], axis=1)[:, 0]
    top_v, top_i = lax.top_k(e_sel, TOP_K)
    weights = g_w * jax.nn.softmax(top_v, axis=-1)
    expert = (g_idx[:, None] * EXPERTS_PER_GROUP + top_i).astype(jnp.int32)
    a = n * TOP_K
    flat_e = expert.reshape(a)
    e_ids = jnp.arange(N_EXPERTS, dtype=jnp.int32)
    counts = jnp.sum(flat_e[:, None] == e_ids[None, :], axis=0, dtype=jnp.int32)
    padded = (counts + tb - 1) // tb * tb
    pad_end = jnp.cumsum(padded)
    pad_start = pad_end - padded
    filler_e = jnp.repeat(e_ids, tb)
    filler_j = jnp.tile(jnp.arange(tb, dtype=jnp.int32), N_EXPERTS)
    filler_key = jnp.where(filler_j < (padded - counts)[filler_e], 2 * filler_e + 1, 2 * N_EXPERTS + 1)
    keys = jnp.concatenate([2 * flat_e, filler_key])
    ids = jnp.arange(a, dtype=jnp.int32)
    filler0 = jnp.zeros((N_EXPERTS * tb,), jnp.int32)
    tok_src = jnp.concatenate([ids // TOP_K, filler0])
    out_src = jnp.concatenate([(ids % TOP_K) * n + ids // TOP_K, filler0])
    _, row_tok, row_out = lax.sort((keys, tok_src, out_src), num_keys=1)
    n_blocks = (a + N_EXPERTS * tb) // tb
    block_start = jnp.arange(n_blocks, dtype=jnp.int32) * tb
    block_expert = jnp.minimum(jnp.sum(block_start[:, None] >= pad_end[None, :], axis=1), N_EXPERTS - 1).astype(jnp.int32)
    n_valid = jnp.clip(pad_start[block_expert] + counts[block_expert] - block_start, 0, tb).astype(jnp.int32)
    n_used = (pad_end[-1] // tb).astype(jnp.int32).reshape(1)
    return (block_expert, n_used, n_valid, row_tok, row_out), weights


def _expert_kernel(be_ref, nused_ref, nvalid_ref, rtok_ref, rout_ref, h2_hbm, wg_ref, wu_ref, wd_ref, out_hbm,
                   xbuf, ybuf, gsem, ssem):
    tb = xbuf.shape[1]
    unroll = 8
    i = pl.program_id(0)
    nb = pl.num_programs(0)
    nused = nused_ref[0]
    slot = i % 2

    def gather_copy(row, s, j):
        return pltpu.make_async_copy(h2_hbm.at[pl.ds(row, 1), :], xbuf.at[s, pl.ds(j, 1), :], gsem.at[s])

    def scatter_copy(row, s, j):
        return pltpu.make_async_copy(ybuf.at[s, pl.ds(j, 1), :], out_hbm.at[pl.ds(row, 1), :], ssem.at[s])

    def issue_gather(blk, s):
        def body(j, carry):
            gather_copy(rtok_ref[blk * tb + j], s, j).start()
            return carry
        lax.fori_loop(0, tb, body, 0, unroll=unroll)

    def wait_gather(s):
        pltpu.make_async_copy(h2_hbm.at[pl.ds(0, tb), :], xbuf.at[s], gsem.at[s]).wait()

    def issue_scatter(blk, s):
        nv = nvalid_ref[blk]

        def group(gi, carry):
            for t in range(unroll):
                j = gi * unroll + t
                scatter_copy(rout_ref[blk * tb + j], s, j).start()
            return carry

        def single(j, carry):
            scatter_copy(rout_ref[blk * tb + j], s, j).start()
            return carry

        lax.fori_loop(0, nv // unroll, group, 0)
        lax.fori_loop(nv // unroll * unroll, nv, single, 0)

    def wait_scatter(blk, s):
        nv = nvalid_ref[blk]
        n8 = pl.multiple_of(nv // 8 * 8, 8)

        @pl.when(n8 > 0)
        def _():
            pltpu.make_async_copy(ybuf.at[s, pl.ds(0, n8), :], out_hbm.at[pl.ds(0, n8), :], ssem.at[s]).wait()

        def single(j, carry):
            scatter_copy(0, s, 0).wait()
            return carry

        lax.fori_loop(n8, nv, single, 0)

    @pl.when(jnp.logical_and(i == 0, nused > 0))
    def _():
        issue_gather(0, 0)

    @pl.when(i + 1 < nused)
    def _():
        issue_gather(i + 1, 1 - slot)

    @pl.when(jnp.logical_and(i >= 2, i - 2 < nused))
    def _():
        wait_scatter(i - 2, slot)

    @pl.when(i < nused)
    def _():
        wait_gather(slot)
        x = xbuf[slot].astype(BF16)
        g = jnp.dot(x, wg_ref[0], preferred_element_type=F32)
        u = jnp.dot(x, wu_ref[0], preferred_element_type=F32)
        hdn = (g * _sigmoid(g) * u).astype(BF16)
        ybuf[slot] = jnp.dot(hdn, wd_ref[0], preferred_element_type=F32)
        issue_scatter(i, slot)

    @pl.when(i == nb - 1)
    def _():
        @pl.when(jnp.logical_and(nb >= 2, nb - 2 < nused))
        def _():
            wait_scatter(nb - 2, 1 - slot)

        @pl.when(nb - 1 < nused)
        def _():
            wait_scatter(nb - 1, slot)


def expert_ffn(h2, tables, wg, wu, wd, tb=MOE_ROWS):
    block_expert, n_used, n_valid, row_tok, row_out = tables
    n, d = h2.shape
    nb = block_expert.shape[0]
    de = wg.shape[2]
    grid_spec = pltpu.PrefetchScalarGridSpec(
        num_scalar_prefetch=5,
        grid=(nb,),
        in_specs=[pl.BlockSpec(memory_space=pl.ANY),
                  pl.BlockSpec((1, d, de), lambda i, be, *_: (be[i], 0, 0)),
                  pl.BlockSpec((1, d, de), lambda i, be, *_: (be[i], 0, 0)),
                  pl.BlockSpec((1, de, d), lambda i, be, *_: (be[i], 0, 0))],
        out_specs=pl.BlockSpec(memory_space=pl.ANY),
        scratch_shapes=[pltpu.VMEM((2, tb, d), F32), pltpu.VMEM((2, tb, d), F32),
                        pltpu.SemaphoreType.DMA((2,)), pltpu.SemaphoreType.DMA((2,))],
    )
    return pl.pallas_call(
        _expert_kernel,
        grid_spec=grid_spec,
        out_shape=jax.ShapeDtypeStruct((TOP_K * n, d), F32),
        compiler_params=_cparams(("arbitrary",)),
        name="expert_ffn",
    )(block_expert, n_used, n_valid, row_tok, row_out, h2, wg, wu, wd)


def _final_kernel(x1_ref, y0_ref, y1_ref, rw_ref, mod_ref, w_ref, o_ref):
    rw = rw_ref[...]
    moe = y0_ref[...] * rw[:, 0:1] + y1_ref[...] * rw[:, 1:2]
    x = x1_ref[...] + mod_ref[0, 5:6, :] * moe
    o_ref[...] = x * lax.rsqrt(jnp.mean(x * x, axis=-1, keepdims=True) + RMS_EPS) * w_ref[...]


def final_norm(x1, moe, route_w, mod3, final_w, n_tok, row0, rows, seq, tm=512):
    d = x1.shape[1]
    off = row0 // tm
    off1 = (n_tok + row0) // tm
    per_seq = seq // tm
    return pl.pallas_call(
        _final_kernel,
        grid=(rows // tm,),
        in_specs=[pl.BlockSpec((tm, d), lambda i: (off + i, 0)),
                  pl.BlockSpec((tm, d), lambda i: (off + i, 0)),
                  pl.BlockSpec((tm, d), lambda i: (off1 + i, 0)),
                  pl.BlockSpec((tm, TOP_K), lambda i: (off + i, 0)),
                  pl.BlockSpec((1, 6, d), lambda i: ((off + i) // per_seq, 0, 0)),
                  pl.BlockSpec((1, d), lambda i: (0, 0))],
        out_specs=pl.BlockSpec((tm, d), lambda i: (i, 0)),
        out_shape=jax.ShapeDtypeStruct((rows, d), F32),
        compiler_params=_cparams(("parallel",)),
        name="final_norm",
    )(x1, moe, moe, route_w, mod3, final_w.reshape(1, d))


def _permute_cols(w):
    return jnp.concatenate([w[..., lo:hi] for lo, hi in _SRC_RANGES], axis=-1)


def kernel(x_prompt, x_sample, c_prompt, c_sample, w_ada, b_ada, norm1_w, w_in, b_in, mlstm_gate_b, conv_w, conv_b, mlstm_norm_w, p_a, p_b, w_out, norm2_w, w_router_group, b_router_group, w_router_expert, b_router_expert, w_expert_gate, w_expert_up, w_expert_down, final_norm_w):
    bp, seq, d = x_prompt.shape
    bs = x_sample.shape[0]
    bt = bp + bs
    n = bt * seq
    layer = 0

    pad = PROJ_COLS - (COL_GATE_M + 4 * M_HEADS)
    w_in_p = jnp.concatenate([_permute_cols(w_in[layer]), jnp.zeros((d, pad), F32)], axis=1).astype(BF16)
    b_gate = b_in[layer, 6144:6176] + mlstm_gate_b[layer]
    b_in_p = jnp.concatenate([_permute_cols(b_in[layer])[:COL_GATE_M], b_gate, jnp.zeros((pad,), F32)])
    wr = jnp.concatenate([w_router_group[layer], w_router_expert[layer],
                          jnp.zeros((d, ROUTER_COLS - N_GROUPS - N_EXPERTS), F32)], axis=1)
    br = jnp.concatenate([b_router_group[layer], b_router_expert[layer],
                          jnp.zeros((ROUTER_COLS - N_GROUPS - N_EXPERTS,), F32)]).reshape(1, ROUTER_COLS)
    wr_hi = wr.astype(BF16)
    wr_lo = (wr - wr_hi.astype(F32)).astype(BF16)
    slopes = 2.0 ** (-8.0 * jnp.arange(1, A_HEADS + 1, dtype=F32) / A_HEADS)

    c_all = jnp.concatenate([c_prompt, c_sample, jnp.zeros((16 - bt, d), F32)], axis=0)
    mod3 = ada_modulation(c_all, w_ada[layer], b_ada[layer])[:bt].reshape(bt, 6, d)

    h = norm_modulate(x_prompt, x_sample, mod3, norm1_w[layer])
    proj = matmul_bias(h.reshape(n, d), w_in_p, b_in_p)
    proj3 = proj.reshape(bt, seq, PROJ_COLS)

    nc = seq // M_CHUNK
    gates = proj3[:, :, COL_GATE_M:COL_GATE_M + 4 * M_HEADS].reshape(bt, nc, M_CHUNK, 4, M_HEADS)
    gates = gates.transpose(0, 4, 3, 1, 2)
    hm = mlstm_branch(proj3, gates, conv_w[layer], conv_b[layer], mlstm_norm_w[layer])
    at = attention_branch(proj3, slopes)

    x1, h2, logits = merge_project(
        x_prompt.reshape(bp * seq, d), x_sample.reshape(bs * seq, d), hm.reshape(n, -1), at.reshape(n, -1), proj,
        mod3, p_a[layer].astype(BF16), p_b[layer].astype(BF16), w_out[layer].astype(BF16), norm2_w[layer],
        wr_hi, wr_lo, br, seq)

    tables, route_w = route(logits, MOE_ROWS)
    moe = expert_ffn(h2, tables, w_expert_gate[layer].astype(BF16), w_expert_up[layer].astype(BF16),
                     w_expert_down[layer].astype(BF16))

    y_p = final_norm(x1, moe, route_w, mod3, final_norm_w, n, 0, bp * seq, seq)
    y_s = final_norm(x1, moe, route_w, mod3, final_norm_w, n, bp * seq, bs * seq, seq)
    return (y_p.reshape(bp, seq, d), y_s.reshape(bs, seq, d))
```

```python
import functools

import jax
import jax.numpy as jnp
from jax import lax
from jax.experimental import pallas as pl
from jax.experimental.pallas import tpu as pltpu

F32 = jnp.float32
BF16 = jnp.bfloat16

D_MODEL = 2048
RMS_EPS = 1e-6
M_HEADS = 8
M_QK_DIM = 128
M_V_DIM = 256
M_CHUNK = 128
A_GROUPS = ((128, 1), (512, 4), (2048, 16))
A_SLOTS = 4
A_HEADS = A_SLOTS * len(A_GROUPS)
A_HEAD_DIM = 128
N_GROUPS = 4
EXPERTS_PER_GROUP = 8
N_EXPERTS = N_GROUPS * EXPERTS_PER_GROUP
TOP_K = 2
D_EXPERT = 1024

COL_GATE_A = 0
COL_GATE_B = 2048
COL_V_M = 4096
COL_O_M = 6144
COL_Q_M = 8192
COL_K_M = 9216
COL_Q_A = 10240
COL_K_A = 11776
COL_V_A = 13312
COL_GATE_M = 14848
PROJ_COLS = 15360
_SRC_RANGES = ((10784, 14880), (2048, 6144), (0, 2048), (6176, 10784), (6144, 6176))

LANES = 128
MOE_ROWS = 256
ROUTER_COLS = 128
VMEM_LIMIT = 56 * 1024 * 1024


def _sigmoid(x):
    return 1.0 / (1.0 + jnp.exp(-x))


def _cparams(sem, vmem=VMEM_LIMIT):
    return pltpu.CompilerParams(dimension_semantics=sem, vmem_limit_bytes=vmem)


def _ada_kernel(c_ref, w_ref, b_ref, o_ref):
    c = c_ref[...]
    a = (c * _sigmoid(c)).astype(BF16)
    o_ref[...] = jnp.dot(a, w_ref[...].astype(BF16), preferred_element_type=F32) + b_ref[...]


def ada_modulation(c, w_ada, b_ada):
    rows, d = c.shape
    n = w_ada.shape[1]
    tn = 1024
    return pl.pallas_call(
        _ada_kernel,
        grid=(n // tn,),
        in_specs=[pl.BlockSpec((rows, d), lambda j: (0, 0)),
                  pl.BlockSpec((d, tn), lambda j: (0, j)),
                  pl.BlockSpec((1, tn), lambda j: (0, j))],
        out_specs=pl.BlockSpec((rows, tn), lambda j: (0, j)),
        out_shape=jax.ShapeDtypeStruct((rows, n), F32),
        compiler_params=_cparams(("parallel",)),
        name="ada_modulation",
    )(c, w_ada, b_ada.reshape(1, n))


def _norm_mod_kernel(nb0, xp_ref, xs_ref, mod_ref, w_ref, o_ref):
    b = pl.program_id(0)
    x = jnp.where(b < nb0, xp_ref[0], xs_ref[0])
    y = x * lax.rsqrt(jnp.mean(x * x, axis=-1, keepdims=True) + RMS_EPS) * w_ref[...]
    o_ref[0] = (y * (1.0 + mod_ref[0, 1:2, :]) + mod_ref[0, 0:1, :]).astype(o_ref.dtype)


def norm_modulate(xp, xs, mod3, norm_w, ts=512):
    nb0, s, d = xp.shape
    nb1 = xs.shape[0]
    return pl.pallas_call(
        functools.partial(_norm_mod_kernel, nb0),
        grid=(nb0 + nb1, s // ts),
        in_specs=[pl.BlockSpec((1, ts, d), lambda b, t: (jnp.minimum(b, nb0 - 1), jnp.where(b < nb0, t, s // ts - 1), 0)),
                  pl.BlockSpec((1, ts, d), lambda b, t: (jnp.maximum(b - nb0, 0), jnp.where(b < nb0, 0, t), 0)),
                  pl.BlockSpec((1, 6, d), lambda b, t: (b, 0, 0)),
                  pl.BlockSpec((1, d), lambda b, t: (0, 0))],
        out_specs=pl.BlockSpec((1, ts, d), lambda b, t: (b, t, 0)),
        out_shape=jax.ShapeDtypeStruct((nb0 + nb1, s, d), BF16),
        compiler_params=_cparams(("parallel", "parallel")),
        name="norm1_modulate",
    )(xp, xs, mod3, norm_w.reshape(1, d))


def _mm_bias_kernel(a_ref, w_ref, b_ref, o_ref):
    o_ref[...] = jnp.dot(a_ref[...], w_ref[...], preferred_element_type=F32) + b_ref[...]


def matmul_bias(a, w, b, tm=1024, tn=1024):
    m, k = a.shape
    n = w.shape[1]
    return pl.pallas_call(
        _mm_bias_kernel,
        grid=(m // tm, n // tn),
        in_specs=[pl.BlockSpec((tm, k), lambda i, j: (i, 0)),
                  pl.BlockSpec((k, tn), lambda i, j: (0, j)),
                  pl.BlockSpec((1, tn), lambda i, j: (0, j))],
        out_specs=pl.BlockSpec((tm, tn), lambda i, j: (i, j)),
        out_shape=jax.ShapeDtypeStruct((m, n), F32),
        compiler_params=_cparams(("parallel", "parallel")),
        name="in_projection",
    )(a, w, b.reshape(1, n))


def _lane_scan(x, op, fill, reverse):
    n = x.shape[-1]
    axis = x.ndim - 1
    lane = lax.broadcasted_iota(jnp.int32, x.shape, axis)
    k = 1
    while k < n:
        if reverse:
            x = op(x, jnp.where(lane < n - k, pltpu.roll(x, n - k, axis), fill))
        else:
            x = op(x, jnp.where(lane >= k, pltpu.roll(x, k, axis), fill))
        k *= 2
    return x


def _conv_silu_chunk(x_ref, w_ref, b_ref, c, n_chunks):
    L = M_CHUNK
    s = n_chunks * L
    r0 = pl.multiple_of(c * L, L)
    x = x_ref[0, pl.ds(r0, L), :]
    prev_row = x_ref[0, pl.ds(jnp.maximum(r0 - 1, 0), 1), :]
    next_row = x_ref[0, pl.ds(jnp.minimum(r0 + L, s - 1), 1), :]
    prev_row = jnp.where(c > 0, prev_row, 0.0)
    next_row = jnp.where(c < n_chunks - 1, next_row, 0.0)
    rows = lax.broadcasted_iota(jnp.int32, x.shape, 0)
    x_prev = jnp.where(rows == 0, prev_row, pltpu.roll(x, 1, 0))
    x_next = jnp.where(rows == L - 1, next_row, pltpu.roll(x, L - 1, 0))
    y = b_ref[...] + x_prev * w_ref[0:1, :] + x * w_ref[1:2, :] + x_next * w_ref[2:3, :]
    return y * _sigmoid(y)


def _mlstm_kernel(q_ref, k_ref, v_ref, o_ref, g_ref, cwq_ref, cwk_ref, cbq_ref, cbk_ref, nw_ref, out_ref,
                  qs_ref, kt_ref, hf_ref, hr_ref, c_ref,
                  u_ref, negm_ref, ib_ref, wk_ref, decay_ref, m0_ref, m1_ref):
    L = M_CHUNK
    dk = M_QK_DIM
    dv = M_V_DIM
    hp, nc = kt_ref.shape[0], kt_ref.shape[1]

    def prep(c, carry):
        q = _conv_silu_chunk(q_ref, cwq_ref, cbq_ref, c, nc) * (M_QK_DIM ** -0.5)
        qs_ref[c] = q.astype(BF16)
        k = _conv_silu_chunk(k_ref, cwk_ref, cbk_ref, c, nc)
        for hh in range(hp):
            kt_ref[hh, c] = k[:, hh * dk:(hh + 1) * dk].T
        return carry

    lax.fori_loop(0, nc, prep, 0)

    for hh in range(hp):
        for d in range(2):
            rev = d == 1
            i_pre = g_ref[0, hh, 2 * d]
            f_pre = g_ref[0, hh, 2 * d + 1]
            log_f = -(jnp.maximum(-f_pre, 0.0) + jnp.log1p(jnp.exp(-jnp.abs(f_pre))))
            b = _lane_scan(log_f, jnp.add, 0.0, rev)
            a = jnp.broadcast_to(b[:, 0:1] if rev else b[:, L - 1:L], (nc, L))
            g = a - b + i_pre
            g_max = jnp.broadcast_to(jnp.max(g, axis=1, keepdims=True), (nc, L))
            m = jnp.zeros((1, L), F32)
            for c in (range(nc - 1, -1, -1) if rev else range(nc)):
                m0_ref[hh, d, c:c + 1, :] = m
                m = jnp.maximum(a[c:c + 1, :] + m, g_max[c:c + 1, :])
                m1_ref[hh, d, c:c + 1, :] = m
            m0 = m0_ref[hh, d]
            m1 = m1_ref[hh, d]
            ib = i_pre - b
            m_t = jnp.maximum(b + m0, b + _lane_scan(ib, jnp.maximum, -jnp.inf, rev))
            ib_ref[hh, d] = ib
            u_ref[hh, d] = b - m_t
            negm_ref[hh, d] = -m_t
            wk_ref[hh, d] = jnp.exp(g - m1)
            decay_ref[hh, d] = jnp.exp(a + m0 - m1)

    t_idx = lax.broadcasted_iota(jnp.int32, (L, L), 0)
    s_idx = lax.broadcasted_iota(jnp.int32, (L, L), 1)
    ones_ext = jnp.ones((L, LANES), BF16)

    def chunk_step(hh, d, c):
        r0 = pl.multiple_of(c * L, L)
        q = qs_ref[c, :, hh * dk:(hh + 1) * dk]
        kt = kt_ref[hh, c]
        v_ext = jnp.concatenate([v_ref[0, pl.ds(r0, L), hh * dv:(hh + 1) * dv].astype(BF16), ones_ext], axis=1)

        def row(ref):
            return ref[hh, d, pl.ds(c, 1), :]

        umat = jnp.broadcast_to(row(u_ref), (L, L)).T
        nmat = jnp.broadcast_to(row(negm_ref), (L, L)).T
        causal = (s_idx <= t_idx) if d == 0 else (s_idx >= t_idx)
        w_intra = jnp.where(causal, jnp.exp(umat + row(ib_ref)), 0.0)
        w_inter = jnp.exp(umat + row(m0_ref))
        s_qk = jnp.dot(q, kt.astype(BF16), preferred_element_type=F32)
        c_ext = c_ref[hh, d]
        lhs = jnp.concatenate([(w_intra * s_qk).astype(BF16), (q.astype(F32) * w_inter).astype(BF16)], axis=1)
        rhs = jnp.concatenate([v_ext, c_ext.astype(BF16)], axis=0)
        num = jnp.dot(lhs, rhs, preferred_element_type=F32)
        r = 1.0 / jnp.maximum(jnp.abs(num[:, dv:]), jnp.exp(nmat))
        h = num[:, :dv] * jnp.concatenate([r] * (dv // LANES), axis=1)

        upd = jnp.dot((kt * row(wk_ref)).astype(BF16), v_ext, preferred_element_type=F32)
        decay = jnp.broadcast_to(row(decay_ref), (dk, LANES))
        c_ref[hh, d] = jnp.concatenate([decay] * (c_ext.shape[1] // LANES), axis=1) * c_ext + upd
        return h

    def finish(hh, c, hs):
        rows = pl.ds(pl.multiple_of(c * L, L), L)
        cols = slice(hh * dv, (hh + 1) * dv)
        y = hs * lax.rsqrt(jnp.mean(hs * hs, axis=-1, keepdims=True) + RMS_EPS) * nw_ref[:, cols]
        out_ref[0, rows, cols] = (y * _sigmoid(o_ref[0, rows, cols])).astype(out_ref.dtype)

    def first_half(j, carry):
        cf, cr = j, nc - 1 - j
        for hh in range(hp):
            hf_ref[hh, pl.ds(pl.multiple_of(cf * L, L), L), :] = chunk_step(hh, 0, cf)
            hr_ref[hh, pl.ds(pl.multiple_of((cr - nc // 2) * L, L), L), :] = chunk_step(hh, 1, cr)
        return carry

    def second_half(j, carry):
        cf, cr = j, nc - 1 - j
        for hh in range(hp):
            finish(hh, cf, chunk_step(hh, 0, cf) + hr_ref[hh, pl.ds(pl.multiple_of((cf - nc // 2) * L, L), L), :])
            finish(hh, cr, hf_ref[hh, pl.ds(pl.multiple_of(cr * L, L), L), :] + chunk_step(hh, 1, cr))
        return carry

    c_ref[...] = jnp.zeros_like(c_ref)
    lax.fori_loop(0, nc // 2, first_half, 0, unroll=2)
    lax.fori_loop(nc // 2, nc, second_half, 0, unroll=2)


def mlstm_branch(proj3, gates, conv_w, conv_b, norm_w, hp=2):
    bsz, s, _ = proj3.shape
    L = M_CHUNK
    nc = s // L
    dk, dv = M_QK_DIM, M_V_DIM
    assert dk == L and nc % 2 == 0 and M_HEADS % hp == 0
    wq, wv = hp * dk, hp * dv
    k_off = M_HEADS * dk // wq
    return pl.pallas_call(
        _mlstm_kernel,
        grid=(bsz, M_HEADS // hp),
        in_specs=[pl.BlockSpec((1, s, wq), lambda b, h: (b, 0, COL_Q_M // wq + h)),
                  pl.BlockSpec((1, s, wq), lambda b, h: (b, 0, COL_K_M // wq + h)),
                  pl.BlockSpec((1, s, wv), lambda b, h: (b, 0, COL_V_M // wv + h)),
                  pl.BlockSpec((1, s, wv), lambda b, h: (b, 0, COL_O_M // wv + h)),
                  pl.BlockSpec((1, hp, 4, nc, L), lambda b, h: (b, h, 0, 0, 0)),
                  pl.BlockSpec((3, wq), lambda b, h: (0, h)),
                  pl.BlockSpec((3, wq), lambda b, h: (0, k_off + h)),
                  pl.BlockSpec((1, wq), lambda b, h: (0, h)),
                  pl.BlockSpec((1, wq), lambda b, h: (0, k_off + h)),
                  pl.BlockSpec((1, wv), lambda b, h: (0, h))],
        out_specs=pl.BlockSpec((1, s, wv), lambda b, h: (b, 0, h)),
        out_shape=jax.ShapeDtypeStruct((bsz, s, M_HEADS * dv), BF16),
        scratch_shapes=[pltpu.VMEM((nc, L, wq), BF16),
                        pltpu.VMEM((hp, nc, dk, L), F32),
                        pltpu.VMEM((hp, s // 2, dv), F32),
                        pltpu.VMEM((hp, s // 2, dv), F32),
                        pltpu.VMEM((hp, 2, dk, dv + LANES), F32),
                        *([pltpu.VMEM((hp, 2, nc, L), F32)] * 7)],
        compiler_params=_cparams(("parallel", "parallel")),
        name="mlstm_branch",
    )(proj3, proj3, proj3, proj3, gates, conv_w, conv_w, conv_b.reshape(1, -1), conv_b.reshape(1, -1),
      norm_w.reshape(1, -1))


def _attn_kernel(slopes_ref, q0, q1, q2, k0, k1, k2, v0, v1, v2, out_ref, og_ref, lse_ref, bias_ref):
    s_len = out_ref.shape[1]
    dh = A_HEAD_DIM
    T = 128
    slot = pl.program_id(1)
    qs, ks, vs = (q0, q1, q2), (k0, k1, k2), (v0, v1, v2)

    for g, (window, dil) in enumerate(A_GROUPS):
        side = window // (2 * dil)
        u_len = s_len // dil
        nqb = u_len // T
        kw = min(T + 2 * side, u_len)
        slope = slopes_ref[g * A_SLOTS + slot] * float(dil)
        q_ref, k_ref, v_ref = qs[g], ks[g], vs[g]

        offsets = (0, side, kw - T) if nqb > 1 else (0,)
        for case, off in enumerate(offsets):
            rel = jnp.abs(lax.broadcasted_iota(jnp.int32, (T, kw), 0) + off
                          - lax.broadcasted_iota(jnp.int32, (T, kw), 1))
            bias_ref[g, case, :, :kw] = jnp.where(rel <= side, -slope * rel.astype(F32), -1e30)

        def block(idx, carry, g=g, dil=dil, side=side, u_len=u_len, nqb=nqb, kw=kw,
                  q_ref=q_ref, k_ref=k_ref, v_ref=v_ref):
            r = idx // nqb
            j = idx % nqb
            u0 = j * T
            ku0 = jnp.clip(u0 - side, 0, u_len - kw)
            case = jnp.where(j == 0, 0, jnp.where(j == nqb - 1, 2, 1)) if nqb > 1 else 0
            q_rows = pl.ds(r + u0 * dil, T, stride=dil) if dil > 1 else pl.ds(pl.multiple_of(u0, T), T)
            k_rows = pl.ds(r + ku0 * dil, kw, stride=dil) if dil > 1 else pl.ds(pl.multiple_of(ku0, side), kw)
            q = (q_ref[0, q_rows, :] * (dh ** -0.5)).astype(BF16)
            kk = k_ref[0, k_rows, :].astype(BF16)
            vv = v_ref[0, k_rows, :].astype(BF16)
            s = lax.dot_general(q, kk, (((1,), (1,)), ((), ())), preferred_element_type=F32)
            s = s + bias_ref[g, case, :, :kw]
            m = jnp.max(s, axis=1, keepdims=True)
            p = jnp.exp(s - m)
            den = jnp.sum(p, axis=1, keepdims=True)
            o = jnp.dot(p.astype(BF16), vv, preferred_element_type=F32) * (1.0 / den)
            og_ref[g, q_rows, :] = o
            lse_ref[g, q_rows, :] = jnp.broadcast_to(m + jnp.log(den), (T, LANES))
            return carry

        lax.fori_loop(0, dil * nqb, block, 0, unroll=8)

    rows_per = 256

    def merge(i, carry):
        rows = pl.ds(pl.multiple_of(i * rows_per, rows_per), rows_per)
        l0, l1, l2 = lse_ref[0, rows, :], lse_ref[1, rows, :], lse_ref[2, rows, :]
        mx = jnp.maximum(jnp.maximum(l0, l1), l2)
        e0, e1, e2 = jnp.exp(l0 - mx), jnp.exp(l1 - mx), jnp.exp(l2 - mx)
        inv = 1.0 / (e0 + e1 + e2)
        o = (e0 * inv) * og_ref[0, rows, :] + (e1 * inv) * og_ref[1, rows, :] + (e2 * inv) * og_ref[2, rows, :]
        out_ref[0, rows, :] = o.astype(out_ref.dtype)
        return carry

    lax.fori_loop(0, s_len // rows_per, merge, 0)


def attention_branch(proj3, slopes):
    bsz, s, _ = proj3.shape
    dh = A_HEAD_DIM

    def col(base, g):
        return lambda b, t, sl: (b, 0, base // dh + g * A_SLOTS + t)

    grid_spec = pltpu.PrefetchScalarGridSpec(
        num_scalar_prefetch=1,
        grid=(bsz, A_SLOTS),
        in_specs=[pl.BlockSpec((1, s, dh), col(base, g))
                  for base in (COL_Q_A, COL_K_A, COL_V_A) for g in range(len(A_GROUPS))],
        out_specs=pl.BlockSpec((1, s, dh), lambda b, t, sl: (b, 0, t)),
        scratch_shapes=[pltpu.VMEM((3, s, dh), F32), pltpu.VMEM((3, s, LANES), F32),
                        pltpu.VMEM((len(A_GROUPS), 3, 128, 256), F32)],
    )
    return pl.pallas_call(
        _attn_kernel,
        grid_spec=grid_spec,
        out_shape=jax.ShapeDtypeStruct((bsz, s, A_SLOTS * dh), BF16),
        compiler_params=_cparams(("parallel", "parallel")),
        name="dilated_attention",
    )(slopes, *([proj3] * 9))


def _merge_kernel(n0, xp_ref, xs_ref, hm_ref, at_ref, ga_ref, gb_ref, mod_ref, pa_ref, pb_ref, wo_ref, n2_ref,
                  wrh_ref, wrl_ref, br_ref, x1_ref, h2_ref, lg_ref):
    i = pl.program_id(0)
    x = jnp.where(i < n0, xp_ref[...], xs_ref[...])
    y_a = jnp.dot(hm_ref[...], pa_ref[...], preferred_element_type=F32)
    y_b = jnp.dot(at_ref[...], pb_ref[...], preferred_element_type=F32)
    mixin = _sigmoid(ga_ref[...]) * y_a + _sigmoid(gb_ref[...]) * y_b
    mix = jnp.dot(mixin.astype(BF16), wo_ref[...], preferred_element_type=F32)
    x1 = x + mod_ref[0, 2:3, :] * mix
    x1_ref[...] = x1
    y = x1 * lax.rsqrt(jnp.mean(x1 * x1, axis=-1, keepdims=True) + RMS_EPS) * n2_ref[...]
    h2 = y * (1.0 + mod_ref[0, 4:5, :]) + mod_ref[0, 3:4, :]
    h2_ref[...] = h2
    hi = h2.astype(BF16)
    lo = (h2 - hi.astype(F32)).astype(BF16)
    lg_ref[...] = (jnp.dot(hi, wrh_ref[...], preferred_element_type=F32)
                   + (jnp.dot(hi, wrl_ref[...], preferred_element_type=F32)
                      + jnp.dot(lo, wrh_ref[...], preferred_element_type=F32))
                   + br_ref[...])


def merge_project(xp2, xs2, hm, at, proj, mod3, p_a, p_b, w_out, norm2_w, wr_hi, wr_lo, br, seq, tm=256):
    n, d = hm.shape
    n0 = xp2.shape[0] // tm
    n1 = xs2.shape[0] // tm
    per_seq = seq // tm
    const = dict(pipeline_mode=pl.Buffered(1))
    return pl.pallas_call(
        functools.partial(_merge_kernel, n0),
        grid=(n0 + n1,),
        in_specs=[pl.BlockSpec((tm, d), lambda i: (jnp.minimum(i, n0 - 1), 0)),
                  pl.BlockSpec((tm, d), lambda i: (jnp.maximum(i - n0, 0), 0)),
                  pl.BlockSpec((tm, d), lambda i: (i, 0)),
                  pl.BlockSpec((tm, at.shape[1]), lambda i: (i, 0)),
                  pl.BlockSpec((tm, d), lambda i: (i, COL_GATE_A // d)),
                  pl.BlockSpec((tm, d), lambda i: (i, COL_GATE_B // d)),
                  pl.BlockSpec((1, 6, d), lambda i: (i // per_seq, 0, 0)),
                  pl.BlockSpec(p_a.shape, lambda i: (0, 0), **const),
                  pl.BlockSpec(p_b.shape, lambda i: (0, 0), **const),
                  pl.BlockSpec(w_out.shape, lambda i: (0, 0), **const),
                  pl.BlockSpec((1, d), lambda i: (0, 0)),
                  pl.BlockSpec(wr_hi.shape, lambda i: (0, 0), **const),
                  pl.BlockSpec(wr_lo.shape, lambda i: (0, 0), **const),
                  pl.BlockSpec((1, ROUTER_COLS), lambda i: (0, 0))],
        out_specs=[pl.BlockSpec((tm, d), lambda i: (i, 0)),
                   pl.BlockSpec((tm, d), lambda i: (i, 0)),
                   pl.BlockSpec((tm, ROUTER_COLS), lambda i: (i, 0))],
        out_shape=[jax.ShapeDtypeStruct((n, d), F32),
                   jax.ShapeDtypeStruct((n, d), F32),
                   jax.ShapeDtypeStruct((n, ROUTER_COLS), F32)],
        compiler_params=_cparams(("parallel",)),
        name="merge_project",
    )(xp2, xs2, hm, at, proj, proj, mod3, p_a, p_b, w_out, norm2_w.reshape(1, d), wr_hi, wr_lo, br)


def route(logits, tb):
    n = logits.shape[0]
    g_logits = logits[:, :N_GROUPS]
    e_logits = logits[:, N_GROUPS:N_GROUPS + N_EXPERTS].reshape(n, N_GROUPS, EXPERTS_PER_GROUP)
    g_idx = jnp.argmax(g_logits, axis=-1)
    g_w = jnp.take_along_axis(jax.nn.softmax(g_logits, axis=-1), g_idx[:, None], axis=-1)
    e_sel = jnp.take_along_axis(e_logits, g_idx[:, None, None], axis=1)[:, 0]
    top_v, top_i = lax.top_k(e_sel, TOP_K)
    weights = g_w * jax.nn.softmax(top_v, axis=-1)
    expert = (g_idx[:, None] * EXPERTS_PER_GROUP + top_i).astype(jnp.int32)
    a = n * TOP_K
    flat_e = expert.reshape(a)
    e_ids = jnp.arange(N_EXPERTS, dtype=jnp.int32)
    counts = jnp.sum(flat_e[:, None] == e_ids[None, :], axis=0, dtype=jnp.int32)
    padded = (counts + tb - 1) // tb * tb
    pad_end = jnp.cumsum(padded)
    pad_start = pad_end - padded
    filler_e = jnp.repeat(e_ids, tb)
    filler_j = jnp.tile(jnp.arange(tb, dtype=jnp.int32), N_EXPERTS)
    filler_key = jnp.where(filler_j < (padded - counts)[filler_e], 2 * filler_e + 1, 2 * N_EXPERTS + 1)
    keys = jnp.concatenate([2 * flat_e, filler_key])
    ids = jnp.arange(a, dtype=jnp.int32)
    filler0 = jnp.zeros((N_EXPERTS * tb,), jnp.int32)
    tok_src = jnp.concatenate([ids // TOP_K, filler0])
    out_src = jnp.concatenate([(ids % TOP_K) * n + ids // TOP_K, filler0])
    _, row_tok, row_out = lax.sort((keys, tok_src, out_src), num_keys=1)
    n_blocks = (a + N_EXPERTS * tb) // tb
    block_start = jnp.arange(n_blocks, dtype=jnp.int32) * tb
    block_expert = jnp.minimum(jnp.sum(block_start[:, None] >= pad_end[None, :], axis=1), N_EXPERTS - 1).astype(jnp.int32)
    n_valid = jnp.clip(pad_start[block_expert] + counts[block_expert] - block_start, 0, tb).astype(jnp.int32)
    n_used = (pad_end[-1] // tb).astype(jnp.int32).reshape(1)
    return (block_expert, n_used, n_valid, row_tok, row_out), weights


def _expert_kernel(be_ref, nused_ref, nvalid_ref, rtok_ref, rout_ref, h2_hbm, wg_ref, wu_ref, wd_ref, out_hbm,
                   xbuf, ybuf, gsem, ssem):
    tb = xbuf.shape[1]
    unroll = 8
    i = pl.program_id(0)
    nb = pl.num_programs(0)
    nused = nused_ref[0]
    slot = i % 2

    def gather_copy(row, s, j):
        return pltpu.make_async_copy(h2_hbm.at[pl.ds(row, 1), :], xbuf.at[s, pl.ds(j, 1), :], gsem.at[s])

    def scatter_copy(row, s, j):
        return pltpu.make_async_copy(ybuf.at[s, pl.ds(j, 1), :], out_hbm.at[pl.ds(row, 1), :], ssem.at[s])

    def issue_gather(blk, s):
        def body(j, carry):
            gather_copy(rtok_ref[blk * tb + j], s, j).start()
            return carry
        lax.fori_loop(0, tb, body, 0, unroll=unroll)

    def wait_gather(s):
        pltpu.make_async_copy(h2_hbm.at[pl.ds(0, tb), :], xbuf.at[s], gsem.at[s]).wait()

    def issue_scatter(blk, s):
        nv = nvalid_ref[blk]

        def group(gi, carry):
            for t in range(unroll):
                j = gi * unroll + t
                scatter_copy(rout_ref[blk * tb + j], s, j).start()
            return carry

        def single(j, carry):
            scatter_copy(rout_ref[blk * tb + j], s, j).start()
            return carry

        lax.fori_loop(0, nv // unroll, group, 0)
        lax.fori_loop(nv // unroll * unroll, nv, single, 0)

    def wait_scatter(blk, s):
        nv = nvalid_ref[blk]
        n8 = pl.multiple_of(nv // 8 * 8, 8)

        @pl.when(n8 > 0)
        def _():
            pltpu.make_async_copy(ybuf.at[s, pl.ds(0, n8), :], out_hbm.at[pl.ds(0, n8), :], ssem.at[s]).wait()

        def single(j, carry):
            scatter_copy(0, s, 0).wait()
            return carry

        lax.fori_loop(n8, nv, single, 0)

    @pl.when(jnp.logical_and(i == 0, nused > 0))
    def _():
        issue_gather(0, 0)

    @pl.when(i + 1 < nused)
    def _():
        issue_gather(i + 1, 1 - slot)

    @pl.when(jnp.logical_and(i >= 2, i - 2 < nused))
    def _():
        wait_scatter(i - 2, slot)

    @pl.when(i < nused)
    def _():
        wait_gather(slot)
        x = xbuf[slot].astype(BF16)
        g = jnp.dot(x, wg_ref[0], preferred_element_type=F32)
        u = jnp.dot(x, wu_ref[0], preferred_element_type=F32)
        hdn = (g * _sigmoid(g) * u).astype(BF16)
        ybuf[slot] = jnp.dot(hdn, wd_ref[0], preferred_element_type=F32)
        issue_scatter(i, slot)

    @pl.when(i == nb - 1)
    def _():
        @pl.when(jnp.logical_and(nb >= 2, nb - 2 < nused))
        def _():
            wait_scatter(nb - 2, 1 - slot)

        @pl.when(nb - 1 < nused)
        def _():
            wait_scatter(nb - 1, slot)


def expert_ffn(h2, tables, wg, wu, wd, tb=MOE_ROWS):
    block_expert, n_used, n_valid, row_tok, row_out = tables
    n, d = h2.shape
    nb = block_expert.shape[0]
    de = wg.shape[2]
    grid_spec = pltpu.PrefetchScalarGridSpec(
        num_scalar_prefetch=5,
        grid=(nb,),
        in_specs=[pl.BlockSpec(memory_space=pl.ANY),
                  pl.BlockSpec((1, d, de), lambda i, be, *_: (be[i], 0, 0)),
                  pl.BlockSpec((1, d, de), lambda i, be, *_: (be[i], 0, 0)),
                  pl.BlockSpec((1, de, d), lambda i, be, *_: (be[i], 0, 0))],
        out_specs=pl.BlockSpec(memory_space=pl.ANY),
        scratch_shapes=[pltpu.VMEM((2, tb, d), F32), pltpu.VMEM((2, tb, d), F32),
                        pltpu.SemaphoreType.DMA((2,)), pltpu.SemaphoreType.DMA((2,))],
    )
    return pl.pallas_call(
        _expert_kernel,
        grid_spec=grid_spec,
        out_shape=jax.ShapeDtypeStruct((TOP_K * n, d), F32),
        compiler_params=_cparams(("arbitrary",)),
        name="expert_ffn",
    )(block_expert, n_used, n_valid, row_tok, row_out, h2, wg, wu, wd)


def _final_kernel(x1_ref, y0_ref, y1_ref, rw_ref, mod_ref, w_ref, o_ref):
    rw = rw_ref[...]
    moe = y0_ref[...] * rw[:, 0:1] + y1_ref[...] * rw[:, 1:2]
    x = x1_ref[...] + mod_ref[0, 5:6, :] * moe
    o_ref[...] = x * lax.rsqrt(jnp.mean(x * x, axis=-1, keepdims=True) + RMS_EPS) * w_ref[...]


def final_norm(x1, moe, route_w, mod3, final_w, n_tok, row0, rows, seq, tm=512):
    d = x1.shape[1]
    off = row0 // tm
    off1 = (n_tok + row0) // tm
    per_seq = seq // tm
    return pl.pallas_call(
        _final_kernel,
        grid=(rows // tm,),
        in_specs=[pl.BlockSpec((tm, d), lambda i: (off + i, 0)),
                  pl.BlockSpec((tm, d), lambda i: (off + i, 0)),
                  pl.BlockSpec((tm, d), lambda i: (off1 + i, 0)),
                  pl.BlockSpec((tm, TOP_K), lambda i: (off + i, 0)),
                  pl.BlockSpec((1, 6, d), lambda i: ((off + i) // per_seq, 0, 0)),
                  pl.BlockSpec((1, d), lambda i: (0, 0))],
        out_specs=pl.BlockSpec((tm, d), lambda i: (i, 0)),
        out_shape=jax.ShapeDtypeStruct((rows, d), F32),
        compiler_params=_cparams(("parallel",)),
        name="final_norm",
    )(x1, moe, moe, route_w, mod3, final_w.reshape(1, d))


def _permute_cols(w):
    return jnp.concatenate([w[..., lo:hi] for lo, hi in _SRC_RANGES], axis=-1)


def kernel(x_prompt, x_sample, c_prompt, c_sample, w_ada, b_ada, norm1_w, w_in, b_in, mlstm_gate_b, conv_w, conv_b, mlstm_norm_w, p_a, p_b, w_out, norm2_w, w_router_group, b_router_group, w_router_expert, b_router_expert, w_expert_gate, w_expert_up, w_expert_down, final_norm_w):
    bp, seq, d = x_prompt.shape
    bs = x_sample.shape[0]
    bt = bp + bs
    n = bt * seq
    layer = 0

    pad = PROJ_COLS - (COL_GATE_M + 4 * M_HEADS)
    w_in_p = jnp.concatenate([_permute_cols(w_in[layer]), jnp.zeros((d, pad), F32)], axis=1).astype(BF16)
    b_gate = b_in[layer, 6144:6176] + mlstm_gate_b[layer]
    b_in_p = jnp.concatenate([_permute_cols(b_in[layer])[:COL_GATE_M], b_gate, jnp.zeros((pad,), F32)])
    wr = jnp.concatenate([w_router_group[layer], w_router_expert[layer],
                          jnp.zeros((d, ROUTER_COLS - N_GROUPS - N_EXPERTS), F32)], axis=1)
    br = jnp.concatenate([b_router_group[layer], b_router_expert[layer],
                          jnp.zeros((ROUTER_COLS - N_GROUPS - N_EXPERTS,), F32)]).reshape(1, ROUTER_COLS)
    wr_hi = wr.astype(BF16)
    wr_lo = (wr - wr_hi.astype(F32)).astype(BF16)
    slopes = 2.0 ** (-8.0 * jnp.arange(1, A_HEADS + 1, dtype=F32) / A_HEADS)

    c_all = jnp.concatenate([c_prompt, c_sample, jnp.zeros((16 - bt, d), F32)], axis=0)
    mod3 = ada_modulation(c_all, w_ada[layer], b_ada[layer])[:bt].reshape(bt, 6, d)

    h = norm_modulate(x_prompt, x_sample, mod3, norm1_w[layer])
    proj = matmul_bias(h.reshape(n, d), w_in_p, b_in_p)
    proj3 = proj.reshape(bt, seq, PROJ_COLS)

    nc = seq // M_CHUNK
    gates = proj3[:, :, COL_GATE_M:COL_GATE_M + 4 * M_HEADS].reshape(bt, nc, M_CHUNK, 4, M_HEADS)
    gates = gates.transpose(0, 4, 3, 1, 2)
    hm = mlstm_branch(proj3, gates, conv_w[layer], conv_b[layer], mlstm_norm_w[layer])
    at = attention_branch(proj3, slopes)

    x1, h2, logits = merge_project(
        x_prompt.reshape(bp * seq, d), x_sample.reshape(bs * seq, d), hm.reshape(n, -1), at.reshape(n, -1), proj,
        mod3, p_a[layer].astype(BF16), p_b[layer].astype(BF16), w_out[layer].astype(BF16), norm2_w[layer],
        wr_hi, wr_lo, br, seq)

    tables, route_w = route(logits, MOE_ROWS)
    moe = expert_ffn(h2, tables, w_expert_gate[layer].astype(BF16), w_expert_up[layer].astype(BF16),
                     w_expert_down[layer].astype(BF16))

    y_p = final_norm(x1, moe, route_w, mod3, final_norm_w, n, 0, bp * seq, seq)
    y_s = final_norm(x1, moe, route_w, mod3, final_norm_w, n, bp * seq, bs * seq, seq)
    return (y_p.reshape(bp, seq, d), y_s.reshape(bs, seq, d))
```

```python
import functools

import jax
import jax.numpy as jnp
from jax import lax
from jax.experimental import pallas as pl
from jax.experimental.pallas import tpu as pltpu

F32 = jnp.float32
BF16 = jnp.bfloat16

D_MODEL = 2048
RMS_EPS = 1e-6
M_HEADS = 8
M_QK_DIM = 128
M_V_DIM = 256
M_CHUNK = 128
A_GROUPS = ((128, 1), (512, 4), (2048, 16))
A_SLOTS = 4
A_HEADS = A_SLOTS * len(A_GROUPS)
A_HEAD_DIM = 128
N_GROUPS = 4
EXPERTS_PER_GROUP = 8
N_EXPERTS = N_GROUPS * EXPERTS_PER_GROUP
TOP_K = 2
D_EXPERT = 1024

COL_GATE_A = 0
COL_GATE_B = 2048
COL_V_M = 4096
COL_O_M = 6144
COL_Q_M = 8192
COL_K_M = 9216
COL_Q_A = 10240
COL_K_A = 11776
COL_V_A = 13312
COL_GATE_M = 14848
PROJ_COLS = 15360
_SRC_RANGES = ((10784, 14880), (2048, 6144), (0, 2048), (6176, 10784), (6144, 6176))

LANES = 128
MOE_ROWS = 256
ROUTER_COLS = 128
VMEM_LIMIT = 56 * 1024 * 1024


def _sigmoid(x):
    return 1.0 / (1.0 + jnp.exp(-x))


def _cparams(sem, vmem=VMEM_LIMIT):
    return pltpu.CompilerParams(dimension_semantics=sem, vmem_limit_bytes=vmem)


def _ada_kernel(c_ref, w_ref, b_ref, o_ref):
    c = c_ref[...]
    a = (c * _sigmoid(c)).astype(BF16)
    o_ref[...] = jnp.dot(a, w_ref[...].astype(BF16), preferred_element_type=F32) + b_ref[...]


def ada_modulation(c, w_ada, b_ada):
    rows, d = c.shape
    n = w_ada.shape[1]
    tn = 1024
    return pl.pallas_call(
        _ada_kernel,
        grid=(n // tn,),
        in_specs=[pl.BlockSpec((rows, d), lambda j: (0, 0)),
                  pl.BlockSpec((d, tn), lambda j: (0, j)),
                  pl.BlockSpec((1, tn), lambda j: (0, j))],
        out_specs=pl.BlockSpec((rows, tn), lambda j: (0, j)),
        out_shape=jax.ShapeDtypeStruct((rows, n), F32),
        compiler_params=_cparams(("parallel",)),
        name="ada_modulation",
    )(c, w_ada, b_ada.reshape(1, n))


def _norm_mod_kernel(nb0, xp_ref, xs_ref, mod_ref, w_ref, o_ref):
    b = pl.program_id(0)
    x = jnp.where(b < nb0, xp_ref[0], xs_ref[0])
    y = x * lax.rsqrt(jnp.mean(x * x, axis=-1, keepdims=True) + RMS_EPS) * w_ref[...]
    o_ref[0] = (y * (1.0 + mod_ref[0, 1:2, :]) + mod_ref[0, 0:1, :]).astype(o_ref.dtype)


def norm_modulate(xp, xs, mod3, norm_w, ts=512):
    nb0, s, d = xp.shape
    nb1 = xs.shape[0]
    return pl.pallas_call(
        functools.partial(_norm_mod_kernel, nb0),
        grid=(nb0 + nb1, s // ts),
        in_specs=[pl.BlockSpec((1, ts, d), lambda b, t: (jnp.minimum(b, nb0 - 1), jnp.where(b < nb0, t, s // ts - 1), 0)),
                  pl.BlockSpec((1, ts, d), lambda b, t: (jnp.maximum(b - nb0, 0), jnp.where(b < nb0, 0, t), 0)),
                  pl.BlockSpec((1, 6, d), lambda b, t: (b, 0, 0)),
                  pl.BlockSpec((1, d), lambda b, t: (0, 0))],
        out_specs=pl.BlockSpec((1, ts, d), lambda b, t: (b, t, 0)),
        out_shape=jax.ShapeDtypeStruct((nb0 + nb1, s, d), BF16),
        compiler_params=_cparams(("parallel", "parallel")),
        name="norm1_modulate",
    )(xp, xs, mod3, norm_w.reshape(1, d))


def _mm_bias_kernel(a_ref, w_ref, b_ref, o_ref):
    o_ref[...] = jnp.dot(a_ref[...], w_ref[...], preferred_element_type=F32) + b_ref[...]


def matmul_bias(a, w, b, tm=1024, tn=1024):
    m, k = a.shape
    n = w.shape[1]
    return pl.pallas_call(
        _mm_bias_kernel,
        grid=(m // tm, n // tn),
        in_specs=[pl.BlockSpec((tm, k), lambda i, j: (i, 0)),
                  pl.BlockSpec((k, tn), lambda i, j: (0, j)),
                  pl.BlockSpec((1, tn), lambda i, j: (0, j))],
        out_specs=pl.BlockSpec((tm, tn), lambda i, j: (i, j)),
        out_shape=jax.ShapeDtypeStruct((m, n), F32),
        compiler_params=_cparams(("parallel", "parallel")),
        name="in_projection",
    )(a, w, b.reshape(1, n))


def _lane_scan(x, op, fill, reverse):
    n = x.shape[-1]
    axis = x.ndim - 1
    lane = lax.broadcasted_iota(jnp.int32, x.shape, axis)
    k = 1
    while k < n:
        if reverse:
            x = op(x, jnp.where(lane < n - k, pltpu.roll(x, n - k, axis), fill))
        else:
            x = op(x, jnp.where(lane >= k, pltpu.roll(x, k, axis), fill))
        k *= 2
    return x


def _conv_silu_chunk(x_ref, w_ref, b_ref, c, n_chunks):
    L = M_CHUNK
    s = n_chunks * L
    r0 = pl.multiple_of(c * L, L)
    x = x_ref[0, pl.ds(r0, L), :]
    prev_row = x_ref[0, pl.ds(jnp.maximum(r0 - 1, 0), 1), :]
    next_row = x_ref[0, pl.ds(jnp.minimum(r0 + L, s - 1), 1), :]
    prev_row = jnp.where(c > 0, prev_row, 0.0)
    next_row = jnp.where(c < n_chunks - 1, next_row, 0.0)
    rows = lax.broadcasted_iota(jnp.int32, x.shape, 0)
    x_prev = jnp.where(rows == 0, prev_row, pltpu.roll(x, 1, 0))
    x_next = jnp.where(rows == L - 1, next_row, pltpu.roll(x, L - 1, 0))
    y = b_ref[...] + x_prev * w_ref[0:1, :] + x * w_ref[1:2, :] + x_next * w_ref[2:3, :]
    return y * _sigmoid(y)


def _mlstm_kernel(q_ref, k_ref, v_ref, o_ref, g_ref, cwq_ref, cwk_ref, cbq_ref, cbk_ref, nw_ref, out_ref,
                  qs_ref, kt_ref, hf_ref, hr_ref, c_ref,
                  u_ref, negm_ref, ib_ref, wk_ref, decay_ref, m0_ref, m1_ref):
    L = M_CHUNK
    dk = M_QK_DIM
    dv = M_V_DIM
    hp, nc = kt_ref.shape[0], kt_ref.shape[1]

    def prep(c, carry):
        q = _conv_silu_chunk(q_ref, cwq_ref, cbq_ref, c, nc) * (M_QK_DIM ** -0.5)
        qs_ref[c] = q.astype(BF16)
        k = _conv_silu_chunk(k_ref, cwk_ref, cbk_ref, c, nc)
        for hh in range(hp):
            kt_ref[hh, c] = k[:, hh * dk:(hh + 1) * dk].T
        return carry

    lax.fori_loop(0, nc, prep, 0)

    for hh in range(hp):
        for d in range(2):
            rev = d == 1
            i_pre = g_ref[0, hh, 2 * d]
            f_pre = g_ref[0, hh, 2 * d + 1]
            log_f = -(jnp.maximum(-f_pre, 0.0) + jnp.log1p(jnp.exp(-jnp.abs(f_pre))))
            b = _lane_scan(log_f, jnp.add, 0.0, rev)
            a = jnp.broadcast_to(b[:, 0:1] if rev else b[:, L - 1:L], (nc, L))
            g = a - b + i_pre
            g_max = jnp.broadcast_to(jnp.max(g, axis=1, keepdims=True), (nc, L))
            m = jnp.zeros((1, L), F32)
            for c in (range(nc - 1, -1, -1) if rev else range(nc)):
                m0_ref[hh, d, c:c + 1, :] = m
                m = jnp.maximum(a[c:c + 1, :] + m, g_max[c:c + 1, :])
                m1_ref[hh, d, c:c + 1, :] = m
            m0 = m0_ref[hh, d]
            m1 = m1_ref[hh, d]
            ib = i_pre - b
            m_t = jnp.maximum(b + m0, b + _lane_scan(ib, jnp.maximum, -jnp.inf, rev))
            ib_ref[hh, d] = ib
            u_ref[hh, d] = b - m_t
            negm_ref[hh, d] = -m_t
            wk_ref[hh, d] = jnp.exp(g - m1)
            decay_ref[hh, d] = jnp.exp(a + m0 - m1)

    t_idx = lax.broadcasted_iota(jnp.int32, (L, L), 0)
    s_idx = lax.broadcasted_iota(jnp.int32, (L, L), 1)
    ones_ext = jnp.ones((L, LANES), BF16)

    def chunk_step(hh, d, c):
        r0 = pl.multiple_of(c * L, L)
        q = qs_ref[c, :, hh * dk:(hh + 1) * dk]
        kt = kt_ref[hh, c]
        v_ext = jnp.concatenate([v_ref[0, pl.ds(r0, L), hh * dv:(hh + 1) * dv].astype(BF16), ones_ext], axis=1)

        def row(ref):
            return ref[hh, d, pl.ds(c, 1), :]

        umat = jnp.broadcast_to(row(u_ref), (L, L)).T
        nmat = jnp.broadcast_to(row(negm_ref), (L, L)).T
        causal = (s_idx <= t_idx) if d == 0 else (s_idx >= t_idx)
        w_intra = jnp.where(causal, jnp.exp(umat + row(ib_ref)), 0.0)
        w_inter = jnp.exp(umat + row(m0_ref))
        s_qk = jnp.dot(q, kt.astype(BF16), preferred_element_type=F32)
        c_ext = c_ref[hh, d]
        lhs = jnp.concatenate([(w_intra * s_qk).astype(BF16), (q.astype(F32) * w_inter).astype(BF16)], axis=1)
        rhs = jnp.concatenate([v_ext, c_ext.astype(BF16)], axis=0)
        num = jnp.dot(lhs, rhs, preferred_element_type=F32)
        r = 1.0 / jnp.maximum(jnp.abs(num[:, dv:]), jnp.exp(nmat))
        h = num[:, :dv] * jnp.concatenate([r] * (dv // LANES), axis=1)

        upd = jnp.dot((kt * row(wk_ref)).astype(BF16), v_ext, preferred_element_type=F32)
        decay = jnp.broadcast_to(row(decay_ref), (dk, LANES))
        c_ref[hh, d] = jnp.concatenate([decay] * (c_ext.shape[1] // LANES), axis=1) * c_ext + upd
        return h

    def finish(hh, c, hs):
        rows = pl.ds(pl.multiple_of(c * L, L), L)
        cols = slice(hh * dv, (hh + 1) * dv)
        y = hs * lax.rsqrt(jnp.mean(hs * hs, axis=-1, keepdims=True) + RMS_EPS) * nw_ref[:, cols]
        out_ref[0, rows, cols] = (y * _sigmoid(o_ref[0, rows, cols])).astype(out_ref.dtype)

    def first_half(j, carry):
        cf, cr = j, nc - 1 - j
        for hh in range(hp):
            hf_ref[hh, pl.ds(pl.multiple_of(cf * L, L), L), :] = chunk_step(hh, 0, cf)
            hr_ref[hh, pl.ds(pl.multiple_of((cr - nc // 2) * L, L), L), :] = chunk_step(hh, 1, cr)
        return carry

    def second_half(j, carry):
        cf, cr = j, nc - 1 - j
        for hh in range(hp):
            finish(hh, cf, chunk_step(hh, 0, cf) + hr_ref[hh, pl.ds(pl.multiple_of((cf - nc // 2) * L, L), L), :])
            finish(hh, cr, hf_ref[hh, pl.ds(pl.multiple_of(cr * L, L), L), :] + chunk_step(hh, 1, cr))
        return carry

    c_ref[...] = jnp.zeros_like(c_ref)
    lax.fori_loop(0, nc // 2, first_half, 0, unroll=2)
    lax.fori_loop(nc // 2, nc, second_half, 0, unroll=2)


def mlstm_branch(proj3, gates, conv_w, conv_b, norm_w, hp=2):
    bsz, s, _ = proj3.shape
    L = M_CHUNK
    nc = s // L
    dk, dv = M_QK_DIM, M_V_DIM
    assert dk == L and nc % 2 == 0 and M_HEADS % hp == 0
    wq, wv = hp * dk, hp * dv
    k_off = M_HEADS * dk // wq
    return pl.pallas_call(
        _mlstm_kernel,
        grid=(bsz, M_HEADS // hp),
        in_specs=[pl.BlockSpec((1, s, wq), lambda b, h: (b, 0, COL_Q_M // wq + h)),
                  pl.BlockSpec((1, s, wq), lambda b, h: (b, 0, COL_K_M // wq + h)),
                  pl.BlockSpec((1, s, wv), lambda b, h: (b, 0, COL_V_M // wv + h)),
                  pl.BlockSpec((1, s, wv), lambda b, h: (b, 0, COL_O_M // wv + h)),
                  pl.BlockSpec((1, hp, 4, nc, L), lambda b, h: (b, h, 0, 0, 0)),
                  pl.BlockSpec((3, wq), lambda b, h: (0, h)),
                  pl.BlockSpec((3, wq), lambda b, h: (0, k_off + h)),
                  pl.BlockSpec((1, wq), lambda b, h: (0, h)),
                  pl.BlockSpec((1, wq), lambda b, h: (0, k_off + h)),
                  pl.BlockSpec((1, wv), lambda b, h: (0, h))],
        out_specs=pl.BlockSpec((1, s, wv), lambda b, h: (b, 0, h)),
        out_shape=jax.ShapeDtypeStruct((bsz, s, M_HEADS * dv), BF16),
        scratch_shapes=[pltpu.VMEM((nc, L, wq), BF16),
                        pltpu.VMEM((hp, nc, dk, L), F32),
                        pltpu.VMEM((hp, s // 2, dv), F32),
                        pltpu.VMEM((hp, s // 2, dv), F32),
                        pltpu.VMEM((hp, 2, dk, dv + LANES), F32),
                        *([pltpu.VMEM((hp, 2, nc, L), F32)] * 7)],
        compiler_params=_cparams(("parallel", "parallel")),
        name="mlstm_branch",
    )(proj3, proj3, proj3, proj3, gates, conv_w, conv_w, conv_b.reshape(1, -1), conv_b.reshape(1, -1),
      norm_w.reshape(1, -1))


def _attn_kernel(slopes_ref, q0, q1, q2, k0, k1, k2, v0, v1, v2, out_ref, og_ref, lse_ref, bias_ref):
    s_len = out_ref.shape[1]
    dh = A_HEAD_DIM
    T = 128
    slot = pl.program_id(1)
    qs, ks, vs = (q0, q1, q2), (k0, k1, k2), (v0, v1, v2)

    for g, (window, dil) in enumerate(A_GROUPS):
        side = window // (2 * dil)
        u_len = s_len // dil
        nqb = u_len // T
        kw = min(T + 2 * side, u_len)
        slope = slopes_ref[g * A_SLOTS + slot] * float(dil)
        q_ref, k_ref, v_ref = qs[g], ks[g], vs[g]

        offsets = (0, side, kw - T) if nqb > 1 else (0,)
        for case, off in enumerate(offsets):
            rel = jnp.abs(lax.broadcasted_iota(jnp.int32, (T, kw), 0) + off
                          - lax.broadcasted_iota(jnp.int32, (T, kw), 1))
            bias_ref[g, case, :, :kw] = jnp.where(rel <= side, -slope * rel.astype(F32), -1e30)

        def block(idx, carry, g=g, dil=dil, side=side, u_len=u_len, nqb=nqb, kw=kw,
                  q_ref=q_ref, k_ref=k_ref, v_ref=v_ref):
            r = idx // nqb
            j = idx % nqb
            u0 = j * T
            ku0 = jnp.clip(u0 - side, 0, u_len - kw)
            case = jnp.where(j == 0, 0, jnp.where(j == nqb - 1, 2, 1)) if nqb > 1 else 0
            q_rows = pl.ds(r + u0 * dil, T, stride=dil) if dil > 1 else pl.ds(pl.multiple_of(u0, T), T)
            k_rows = pl.ds(r + ku0 * dil, kw, stride=dil) if dil > 1 else pl.ds(pl.multiple_of(ku0, side), kw)
            q = (q_ref[0, q_rows, :] * (dh ** -0.5)).astype(BF16)
            kk = k_ref[0, k_rows, :].astype(BF16)
            vv = v_ref[0, k_rows, :].astype(BF16)
            s = lax.dot_general(q, kk, (((1,), (1,)), ((), ())), preferred_element_type=F32)
            s = s + bias_ref[g, case, :, :kw]
            m = jnp.max(s, axis=1, keepdims=True)
            p = jnp.exp(s - m)
            den = jnp.sum(p, axis=1, keepdims=True)
            o = jnp.dot(p.astype(BF16), vv, preferred_element_type=F32) * (1.0 / den)
            og_ref[g, q_rows, :] = o
            lse_ref[g, q_rows, :] = jnp.broadcast_to(m + jnp.log(den), (T, LANES))
            return carry

        lax.fori_loop(0, dil * nqb, block, 0, unroll=8)

    rows_per = 256

    def merge(i, carry):
        rows = pl.ds(pl.multiple_of(i * rows_per, rows_per), rows_per)
        l0, l1, l2 = lse_ref[0, rows, :], lse_ref[1, rows, :], lse_ref[2, rows, :]
        mx = jnp.maximum(jnp.maximum(l0, l1), l2)
        e0, e1, e2 = jnp.exp(l0 - mx), jnp.exp(l1 - mx), jnp.exp(l2 - mx)
        inv = 1.0 / (e0 + e1 + e2)
        o = (e0 * inv) * og_ref[0, rows, :] + (e1 * inv) * og_ref[1, rows, :] + (e2 * inv) * og_ref[2, rows, :]
        out_ref[0, rows, :] = o.astype(out_ref.dtype)
        return carry

    lax.fori_loop(0, s_len // rows_per, merge, 0)


def attention_branch(proj3, slopes):
    bsz, s, _ = proj3.shape
    dh = A_HEAD_DIM

    def col(base, g):
        return lambda b, t, sl: (b, 0, base // dh + g * A_SLOTS + t)

    grid_spec = pltpu.PrefetchScalarGridSpec(
        num_scalar_prefetch=1,
        grid=(bsz, A_SLOTS),
        in_specs=[pl.BlockSpec((1, s, dh), col(base, g))
                  for base in (COL_Q_A, COL_K_A, COL_V_A) for g in range(len(A_GROUPS))],
        out_specs=pl.BlockSpec((1, s, dh), lambda b, t, sl: (b, 0, t)),
        scratch_shapes=[pltpu.VMEM((3, s, dh), F32), pltpu.VMEM((3, s, LANES), F32),
                        pltpu.VMEM((len(A_GROUPS), 3, 128, 256), F32)],
    )
    return pl.pallas_call(
        _attn_kernel,
        grid_spec=grid_spec,
        out_shape=jax.ShapeDtypeStruct((bsz, s, A_SLOTS * dh), BF16),
        compiler_params=_cparams(("parallel", "parallel")),
        name="dilated_attention",
    )(slopes, *([proj3] * 9))


def _merge_kernel(n0, xp_ref, xs_ref, hm_ref, at_ref, ga_ref, gb_ref, mod_ref, pa_ref, pb_ref, wo_ref, n2_ref,
                  wrh_ref, wrl_ref, br_ref, x1_ref, h2_ref, lg_ref):
    i = pl.program_id(0)
    x = jnp.where(i < n0, xp_ref[...], xs_ref[...])
    y_a = jnp.dot(hm_ref[...], pa_ref[...], preferred_element_type=F32)
    y_b = jnp.dot(at_ref[...], pb_ref[...], preferred_element_type=F32)
    mixin = _sigmoid(ga_ref[...]) * y_a + _sigmoid(gb_ref[...]) * y_b
    mix = jnp.dot(mixin.astype(BF16), wo_ref[...], preferred_element_type=F32)
    x1 = x + mod_ref[0, 2:3, :] * mix
    x1_ref[...] = x1
    y = x1 * lax.rsqrt(jnp.mean(x1 * x1, axis=-1, keepdims=True) + RMS_EPS) * n2_ref[...]
    h2 = y * (1.0 + mod_ref[0, 4:5, :]) + mod_ref[0, 3:4, :]
    h2_ref[...] = h2
    hi = h2.astype(BF16)
    lo = (h2 - hi.astype(F32)).astype(BF16)
    lg_ref[...] = (jnp.dot(hi, wrh_ref[...], preferred_element_type=F32)
                   + (jnp.dot(hi, wrl_ref[...], preferred_element_type=F32)
                      + jnp.dot(lo, wrh_ref[...], preferred_element_type=F32))
                   + br_ref[...])


def merge_project(xp2, xs2, hm, at, proj, mod3, p_a, p_b, w_out, norm2_w, wr_hi, wr_lo, br, seq, tm=256):
    n, d = hm.shape
    n0 = xp2.shape[0] // tm
    n1 = xs2.shape[0] // tm
    per_seq = seq // tm
    const = dict(pipeline_mode=pl.Buffered(1))
    return pl.pallas_call(
        functools.partial(_merge_kernel, n0),
        grid=(n0 + n1,),
        in_specs=[pl.BlockSpec((tm, d), lambda i: (jnp.minimum(i, n0 - 1), 0)),
                  pl.BlockSpec((tm, d), lambda i: (jnp.maximum(i - n0, 0), 0)),
                  pl.BlockSpec((tm, d), lambda i: (i, 0)),
                  pl.BlockSpec((tm, at.shape[1]), lambda i: (i, 0)),
                  pl.BlockSpec((tm, d), lambda i: (i, COL_GATE_A // d)),
                  pl.BlockSpec((tm, d), lambda i: (i, COL_GATE_B // d)),
                  pl.BlockSpec((1, 6, d), lambda i: (i // per_seq, 0, 0)),
                  pl.BlockSpec(p_a.shape, lambda i: (0, 0), **const),
                  pl.BlockSpec(p_b.shape, lambda i: (0, 0), **const),
                  pl.BlockSpec(w_out.shape, lambda i: (0, 0), **const),
                  pl.BlockSpec((1, d), lambda i: (0, 0)),
                  pl.BlockSpec(wr_hi.shape, lambda i: (0, 0), **const),
                  pl.BlockSpec(wr_lo.shape, lambda i: (0, 0), **const),
                  pl.BlockSpec((1, ROUTER_COLS), lambda i: (0, 0))],
        out_specs=[pl.BlockSpec((tm, d), lambda i: (i, 0)),
                   pl.BlockSpec((tm, d), lambda i: (i, 0)),
                   pl.BlockSpec((tm, ROUTER_COLS), lambda i: (i, 0))],
        out_shape=[jax.ShapeDtypeStruct((n, d), F32),
                   jax.ShapeDtypeStruct((n, d), F32),
                   jax.ShapeDtypeStruct((n, ROUTER_COLS), F32)],
        compiler_params=_cparams(("parallel",)),
        name="merge_project",
    )(xp2, xs2, hm, at, proj, proj, mod3, p_a, p_b, w_out, norm2_w.reshape(1, d), wr_hi, wr_lo, br)


def route(logits, tb):
    n = logits.shape[0]
    g_logits = logits[:, :N_GROUPS]
    e_logits = logits[:, N_GROUPS:N_GROUPS + N_EXPERTS].reshape(n, N_GROUPS, EXPERTS_PER_GROUP)
    g_idx = jnp.argmax(g_logits, axis=-1)
    g_w = jnp.take_along_axis(jax.nn.softmax(g_logits, axis=-1), g_idx[:, None], axis=-1)
    e_sel = jnp.take_along_axis(e_logits, g_idx[:, None, None], axis=1)[:, 0]
    top_v, top_i = lax.top_k(e_sel, TOP_K)
    weights = g_w * jax.nn.softmax(top_v, axis=-1)
    expert = (g_idx[:, None] * EXPERTS_PER_GROUP + top_i).astype(jnp.int32)
    a = n * TOP_K
    flat_e = expert.reshape(a)
    e_ids = jnp.arange(N_EXPERTS, dtype=jnp.int32)
    counts = jnp.sum(flat_e[:, None] == e_ids[None, :], axis=0, dtype=jnp.int32)
    padded = (counts + tb - 1) // tb * tb
    pad_end = jnp.cumsum(padded)
    filler_e = jnp.repeat(e_ids, tb)
    filler_j = jnp.tile(jnp.arange(tb, dtype=jnp.int32), N_EXPERTS)
    filler_key = jnp.where(filler_j < (padded - counts)[filler_e], 2 * filler_e + 1, 2 * N_EXPERTS + 1)
    keys = jnp.concatenate([2 * flat_e, filler_key])
    ids = jnp.arange(a, dtype=jnp.int32)
    filler0 = jnp.zeros((N_EXPERTS * tb,), jnp.int32)
    tok_src = jnp.concatenate([ids // TOP_K, filler0])
    out_src = jnp.concatenate([(ids % TOP_K) * n + ids // TOP_K, filler0])
    sorted_keys, row_tok, row_out = lax.sort((keys, tok_src, out_src), num_keys=1)
    r = a + N_EXPERTS * tb
    n_blocks = r // tb
    block_start = jnp.arange(n_blocks, dtype=jnp.int32) * tb
    block_expert = jnp.minimum(jnp.sum(block_start[:, None] >= pad_end[None, :], axis=1), N_EXPERTS - 1).astype(jnp.int32)
    n_used = (pad_end[-1] // tb).astype(jnp.int32).reshape(1)
    pos = jnp.arange(r, dtype=jnp.int32)
    inv_key = jnp.where(sorted_keys % 2 == 0, row_out, a + pos)
    _, inv = lax.sort((inv_key, pos), num_keys=1)
    return (block_expert, n_used, row_tok), inv[:a], weights


ROW_DMA_UNROLL = 8


def _issue_row_gather(src_hbm, dst_buf, sem, index_of, n_rows):
    def body(j, carry):
        pltpu.make_async_copy(src_hbm.at[pl.ds(index_of(j), 1), :], dst_buf.at[pl.ds(j, 1), :], sem).start()
        return carry
    lax.fori_loop(0, n_rows, body, 0, unroll=ROW_DMA_UNROLL)


def _wait_row_gather(src_hbm, dst_buf, sem):
    pltpu.make_async_copy(src_hbm.at[pl.ds(0, dst_buf.shape[0]), :], dst_buf, sem).wait()


def _expert_kernel(be_ref, nused_ref, rtok_ref, h2_hbm, wg_ref, wu_ref, wd_ref, out_ref, xbuf, gsem):
    tb = xbuf.shape[1]
    i = pl.program_id(0)
    nused = nused_ref[0]
    slot = i % 2

    def issue(blk, s):
        _issue_row_gather(h2_hbm, xbuf.at[s], gsem.at[s], lambda j: rtok_ref[blk * tb + j], tb)

    @pl.when(jnp.logical_and(i == 0, nused > 0))
    def _():
        issue(0, 0)

    @pl.when(i + 1 < nused)
    def _():
        issue(i + 1, 1 - slot)

    @pl.when(i < nused)
    def _():
        _wait_row_gather(h2_hbm, xbuf.at[slot], gsem.at[slot])
        x = xbuf[slot].astype(BF16)
        g = jnp.dot(x, wg_ref[0], preferred_element_type=F32)
        u = jnp.dot(x, wu_ref[0], preferred_element_type=F32)
        hdn = (g * _sigmoid(g) * u).astype(BF16)
        out_ref[...] = jnp.dot(hdn, wd_ref[0], preferred_element_type=F32)

    @pl.when(i >= nused)
    def _():
        out_ref[...] = jnp.zeros_like(out_ref)


def expert_ffn(h2, tables, wg, wu, wd, tb=MOE_ROWS):
    block_expert, n_used, row_tok = tables
    n, d = h2.shape
    nb = block_expert.shape[0]
    de = wg.shape[2]
    grid_spec = pltpu.PrefetchScalarGridSpec(
        num_scalar_prefetch=3,
        grid=(nb,),
        in_specs=[pl.BlockSpec(memory_space=pl.ANY),
                  pl.BlockSpec((1, d, de), lambda i, be, *_: (be[i], 0, 0)),
                  pl.BlockSpec((1, d, de), lambda i, be, *_: (be[i], 0, 0)),
                  pl.BlockSpec((1, de, d), lambda i, be, *_: (be[i], 0, 0))],
        out_specs=pl.BlockSpec((tb, d), lambda i, *_: (i, 0)),
        scratch_shapes=[pltpu.VMEM((2, tb, d), F32), pltpu.SemaphoreType.DMA((2,))],
    )
    return pl.pallas_call(
        _expert_kernel,
        grid_spec=grid_spec,
        out_shape=jax.ShapeDtypeStruct((nb * tb, d), F32),
        compiler_params=_cparams(("arbitrary",)),
        name="expert_ffn",
    )(block_expert, n_used, row_tok, h2, wg, wu, wd)


def _final_kernel(n_tok, tile0, inv_ref, x1_ref, rw_ref, mod_ref, w_ref, ys_hbm, o_ref, buf, sem):
    tm = x1_ref.shape[0]
    i = pl.program_id(0)
    slot = i % 2

    def issue(tile, s):
        base = (tile0 + tile) * tm
        for k in range(TOP_K):
            _issue_row_gather(ys_hbm, buf.at[s, k], sem.at[s], lambda j, k=k: inv_ref[k * n_tok + base + j], tm)

    @pl.when(i == 0)
    def _():
        issue(0, 0)

    @pl.when(i + 1 < pl.num_programs(0))
    def _():
        issue(i + 1, 1 - slot)

    for k in range(TOP_K):
        _wait_row_gather(ys_hbm, buf.at[slot, k], sem.at[slot])
    rw = rw_ref[...]
    moe = buf[slot, 0] * rw[:, 0:1]
    for k in range(1, TOP_K):
        moe = moe + buf[slot, k] * rw[:, k:k + 1]
    x = x1_ref[...] + mod_ref[0, 5:6, :] * moe
    o_ref[...] = x * lax.rsqrt(jnp.mean(x * x, axis=-1, keepdims=True) + RMS_EPS) * w_ref[...]


def final_norm(x1, ys, inv, route_w, mod3, final_w, row0, rows, seq, tm=256):
    n_tok, d = x1.shape
    off = row0 // tm
    per_seq = seq // tm
    grid_spec = pltpu.PrefetchScalarGridSpec(
        num_scalar_prefetch=1,
        grid=(rows // tm,),
        in_specs=[pl.BlockSpec((tm, d), lambda i, inv: (off + i, 0)),
                  pl.BlockSpec((tm, TOP_K), lambda i, inv: (off + i, 0)),
                  pl.BlockSpec((1, 6, d), lambda i, inv: ((off + i) // per_seq, 0, 0)),
                  pl.BlockSpec((1, d), lambda i, inv: (0, 0)),
                  pl.BlockSpec(memory_space=pl.ANY)],
        out_specs=pl.BlockSpec((tm, d), lambda i, inv: (i, 0)),
        scratch_shapes=[pltpu.VMEM((2, TOP_K, tm, d), F32), pltpu.SemaphoreType.DMA((2,))],
    )
    return pl.pallas_call(
        functools.partial(_final_kernel, n_tok, off),
        grid_spec=grid_spec,
        out_shape=jax.ShapeDtypeStruct((rows, d), F32),
        compiler_params=_cparams(("arbitrary",)),
        name="final_norm",
    )(inv, x1, route_w, mod3, final_w.reshape(1, d), ys)


def _permute_cols(w):
    return jnp.concatenate([w[..., lo:hi] for lo, hi in _SRC_RANGES], axis=-1)


def kernel(x_prompt, x_sample, c_prompt, c_sample, w_ada, b_ada, norm1_w, w_in, b_in, mlstm_gate_b, conv_w, conv_b, mlstm_norm_w, p_a, p_b, w_out, norm2_w, w_router_group, b_router_group, w_router_expert, b_router_expert, w_expert_gate, w_expert_up, w_expert_down, final_norm_w):
    bp, seq, d = x_prompt.shape
    bs = x_sample.shape[0]
    bt = bp + bs
    n = bt * seq
    layer = 0

    pad = PROJ_COLS - (COL_GATE_M + 4 * M_HEADS)
    w_in_p = jnp.concatenate([_permute_cols(w_in[layer]), jnp.zeros((d, pad), F32)], axis=1).astype(BF16)
    b_gate = b_in[layer, 6144:6176] + mlstm_gate_b[layer]
    b_in_p = jnp.concatenate([_permute_cols(b_in[layer])[:COL_GATE_M], b_gate, jnp.zeros((pad,), F32)])
    wr = jnp.concatenate([w_router_group[layer], w_router_expert[layer],
                          jnp.zeros((d, ROUTER_COLS - N_GROUPS - N_EXPERTS), F32)], axis=1)
    br = jnp.concatenate([b_router_group[layer], b_router_expert[layer],
                          jnp.zeros((ROUTER_COLS - N_GROUPS - N_EXPERTS,), F32)]).reshape(1, ROUTER_COLS)
    wr_hi = wr.astype(BF16)
    wr_lo = (wr - wr_hi.astype(F32)).astype(BF16)
    slopes = 2.0 ** (-8.0 * jnp.arange(1, A_HEADS + 1, dtype=F32) / A_HEADS)

    c_all = jnp.concatenate([c_prompt, c_sample, jnp.zeros((16 - bt, d), F32)], axis=0)
    mod3 = ada_modulation(c_all, w_ada[layer], b_ada[layer])[:bt].reshape(bt, 6, d)

    h = norm_modulate(x_prompt, x_sample, mod3, norm1_w[layer])
    proj = matmul_bias(h.reshape(n, d), w_in_p, b_in_p)
    proj3 = proj.reshape(bt, seq, PROJ_COLS)

    nc = seq // M_CHUNK
    gates = proj3[:, :, COL_GATE_M:COL_GATE_M + 4 * M_HEADS].reshape(bt, nc, M_CHUNK, 4, M_HEADS)
    gates = gates.transpose(0, 4, 3, 1, 2)
    hm = mlstm_branch(proj3, gates, conv_w[layer], conv_b[layer], mlstm_norm_w[layer])
    at = attention_branch(proj3, slopes)

    x1, h2, logits = merge_project(
        x_prompt.reshape(bp * seq, d), x_sample.reshape(bs * seq, d), hm.reshape(n, -1), at.reshape(n, -1), proj,
        mod3, p_a[layer].astype(BF16), p_b[layer].astype(BF16), w_out[layer].astype(BF16), norm2_w[layer],
        wr_hi, wr_lo, br, seq)

    tables, inv, route_w = route(logits, MOE_ROWS)
    ys = expert_ffn(h2, tables, w_expert_gate[layer].astype(BF16), w_expert_up[layer].astype(BF16),
                    w_expert_down[layer].astype(BF16))

    y_p = final_norm(x1, ys, inv, route_w, mod3, final_norm_w, 0, bp * seq, seq)
    y_s = final_norm(x1, ys, inv, route_w, mod3, final_norm_w, bp * seq, bs * seq, seq)
    return (y_p.reshape(bp, seq, d), y_s.reshape(bs, seq, d))
```

```python
import functools

import jax
import jax.numpy as jnp
from jax import lax
from jax.experimental import pallas as pl
from jax.experimental.pallas import tpu as pltpu

F32 = jnp.float32
BF16 = jnp.bfloat16

D_MODEL = 2048
RMS_EPS = 1e-6
M_HEADS = 8
M_QK_DIM = 128
M_V_DIM = 256
M_CHUNK = 128
A_GROUPS = ((128, 1), (512, 4), (2048, 16))
A_SLOTS = 4
A_HEADS = A_SLOTS * len(A_GROUPS)
A_HEAD_DIM = 128
N_GROUPS = 4
EXPERTS_PER_GROUP = 8
N_EXPERTS = N_GROUPS * EXPERTS_PER_GROUP
TOP_K = 2
D_EXPERT = 1024

M_QK_W = M_HEADS * M_QK_DIM
M_V_W = M_HEADS * M_V_DIM
A_W = A_HEADS * A_HEAD_DIM
COL_Q_M = 0
COL_K_M = M_QK_W
COL_V_M = 2 * M_QK_W
COL_O_M = 2 * M_QK_W + M_V_W
PROJ_M_COLS = 2 * M_QK_W + 2 * M_V_W
GATE_M_W = 4 * M_HEADS
COL_GATE_A = 0
COL_GATE_B = D_MODEL
COL_Q_A = 2 * D_MODEL
COL_K_A = COL_Q_A + A_W
COL_V_A = COL_K_A + A_W
COL_GATE_M = COL_V_A + A_W
PROJ_A_COLS = 9216
_SRC_A0 = PROJ_M_COLS + GATE_M_W
_SRC_RANGES = ((_SRC_A0 + 3 * A_W, _SRC_A0 + 3 * A_W + 2 * D_MODEL), (_SRC_A0, _SRC_A0 + 3 * A_W),
               (PROJ_M_COLS, _SRC_A0))

LANES = 128
MOE_ROWS = 256
ROUTER_COLS = 128
VMEM_LIMIT = 56 * 1024 * 1024


def _sigmoid(x):
    return 1.0 / (1.0 + jnp.exp(-x))


def _cparams(sem, vmem=VMEM_LIMIT):
    return pltpu.CompilerParams(dimension_semantics=sem, vmem_limit_bytes=vmem)


def _ada_kernel(c_ref, w_ref, b_ref, o_ref):
    c = c_ref[...]
    a = (c * _sigmoid(c)).astype(BF16)
    o_ref[...] = jnp.dot(a, w_ref[...].astype(BF16), preferred_element_type=F32) + b_ref[...]


def ada_modulation(c, w_ada, b_ada):
    rows, d = c.shape
    n = w_ada.shape[1]
    tn = 1024
    return pl.pallas_call(
        _ada_kernel,
        grid=(n // tn,),
        in_specs=[pl.BlockSpec((rows, d), lambda j: (0, 0)),
                  pl.BlockSpec((d, tn), lambda j: (0, j)),
                  pl.BlockSpec((1, tn), lambda j: (0, j))],
        out_specs=pl.BlockSpec((rows, tn), lambda j: (0, j)),
        out_shape=jax.ShapeDtypeStruct((rows, n), F32),
        compiler_params=_cparams(("parallel",)),
        name="ada_modulation",
    )(c, w_ada, b_ada.reshape(1, n))


def _norm_mod_kernel(nb0, xp_ref, xs_ref, mod_ref, w_ref, o_ref):
    b = pl.program_id(0)
    x = jnp.where(b < nb0, xp_ref[0], xs_ref[0])
    y = x * lax.rsqrt(jnp.mean(x * x, axis=-1, keepdims=True) + RMS_EPS) * w_ref[...]
    o_ref[0] = (y * (1.0 + mod_ref[0, 1:2, :]) + mod_ref[0, 0:1, :]).astype(o_ref.dtype)


def norm_modulate(xp, xs, mod3, norm_w, ts=512):
    nb0, s, d = xp.shape
    nb1 = xs.shape[0]
    return pl.pallas_call(
        functools.partial(_norm_mod_kernel, nb0),
        grid=(nb0 + nb1, s // ts),
        in_specs=[pl.BlockSpec((1, ts, d), lambda b, t: (jnp.minimum(b, nb0 - 1), jnp.where(b < nb0, t, s // ts - 1), 0)),
                  pl.BlockSpec((1, ts, d), lambda b, t: (jnp.maximum(b - nb0, 0), jnp.where(b < nb0, 0, t), 0)),
                  pl.BlockSpec((1, 6, d), lambda b, t: (b, 0, 0)),
                  pl.BlockSpec((1, d), lambda b, t: (0, 0))],
        out_specs=pl.BlockSpec((1, ts, d), lambda b, t: (b, t, 0)),
        out_shape=jax.ShapeDtypeStruct((nb0 + nb1, s, d), BF16),
        compiler_params=_cparams(("parallel", "parallel")),
        name="norm1_modulate",
    )(xp, xs, mod3, norm_w.reshape(1, d))


def _mm_bias_kernel(a_ref, w_ref, b_ref, o_ref):
    o_ref[...] = jnp.dot(a_ref[...], w_ref[...], preferred_element_type=F32) + b_ref[...]


def matmul_bias(a, w, b, tm=1024, tn=1024):
    m, k = a.shape
    n = w.shape[1]
    return pl.pallas_call(
        _mm_bias_kernel,
        grid=(m // tm, n // tn),
        in_specs=[pl.BlockSpec((tm, k), lambda i, j: (i, 0)),
                  pl.BlockSpec((k, tn), lambda i, j: (0, j)),
                  pl.BlockSpec((1, tn), lambda i, j: (0, j))],
        out_specs=pl.BlockSpec((tm, tn), lambda i, j: (i, j)),
        out_shape=jax.ShapeDtypeStruct((m, n), F32),
        compiler_params=_cparams(("parallel", "parallel")),
        name="in_projection",
    )(a, w, b.reshape(1, n))


def _mm_bias_f32w_kernel(a_ref, w_ref, b_ref, o_ref, wb_ref):
    @pl.when(pl.program_id(1) == 0)
    def _():
        wb_ref[...] = w_ref[...].astype(BF16)

    o_ref[...] = jnp.dot(a_ref[...], wb_ref[...], preferred_element_type=F32) + b_ref[...]


def matmul_bias_f32w(a, w, b, n, tm=1024, tn=1024):
    m, k = a.shape
    return pl.pallas_call(
        _mm_bias_f32w_kernel,
        grid=(n // tn, m // tm),
        in_specs=[pl.BlockSpec((tm, k), lambda j, i: (i, 0)),
                  pl.BlockSpec((k, tn), lambda j, i: (0, j)),
                  pl.BlockSpec((1, tn), lambda j, i: (0, j))],
        out_specs=pl.BlockSpec((tm, tn), lambda j, i: (i, j)),
        out_shape=jax.ShapeDtypeStruct((m, n), F32),
        scratch_shapes=[pltpu.VMEM((k, tn), BF16)],
        compiler_params=_cparams(("arbitrary", "arbitrary")),
        name="in_projection_mlstm",
    )(a, w, b.reshape(1, -1))


def _lane_scan(x, op, fill, reverse):
    n = x.shape[-1]
    axis = x.ndim - 1
    lane = lax.broadcasted_iota(jnp.int32, x.shape, axis)
    k = 1
    while k < n:
        if reverse:
            x = op(x, jnp.where(lane < n - k, pltpu.roll(x, n - k, axis), fill))
        else:
            x = op(x, jnp.where(lane >= k, pltpu.roll(x, k, axis), fill))
        k *= 2
    return x


def _conv_silu_chunk(x_ref, w_ref, b_ref, c, n_chunks):
    L = M_CHUNK
    s = n_chunks * L
    r0 = pl.multiple_of(c * L, L)
    x = x_ref[0, pl.ds(r0, L), :]
    prev_row = x_ref[0, pl.ds(jnp.maximum(r0 - 1, 0), 1), :]
    next_row = x_ref[0, pl.ds(jnp.minimum(r0 + L, s - 1), 1), :]
    prev_row = jnp.where(c > 0, prev_row, 0.0)
    next_row = jnp.where(c < n_chunks - 1, next_row, 0.0)
    rows = lax.broadcasted_iota(jnp.int32, x.shape, 0)
    x_prev = jnp.where(rows == 0, prev_row, pltpu.roll(x, 1, 0))
    x_next = jnp.where(rows == L - 1, next_row, pltpu.roll(x, L - 1, 0))
    y = b_ref[...] + x_prev * w_ref[0:1, :] + x * w_ref[1:2, :] + x_next * w_ref[2:3, :]
    return y * _sigmoid(y)


def _mlstm_kernel(q_ref, k_ref, v_ref, o_ref, g_ref, cwq_ref, cwk_ref, cbq_ref, cbk_ref, nw_ref, out_ref,
                  qs_ref, kt_ref, hf_ref, hr_ref, c_ref,
                  u_ref, negm_ref, ib_ref, wk_ref, decay_ref, m0_ref, m1_ref):
    L = M_CHUNK
    dk = M_QK_DIM
    dv = M_V_DIM
    hp, nc = kt_ref.shape[0], kt_ref.shape[1]

    def prep(c, carry):
        q = _conv_silu_chunk(q_ref, cwq_ref, cbq_ref, c, nc) * (M_QK_DIM ** -0.5)
        qs_ref[c] = q.astype(BF16)
        k = _conv_silu_chunk(k_ref, cwk_ref, cbk_ref, c, nc)
        for hh in range(hp):
            kt_ref[hh, c] = k[:, hh * dk:(hh + 1) * dk].T
        return carry

    lax.fori_loop(0, nc, prep, 0)

    for hh in range(hp):
        for d in range(2):
            rev = d == 1
            i_pre = g_ref[0, hh, 2 * d]
            f_pre = g_ref[0, hh, 2 * d + 1]
            log_f = -(jnp.maximum(-f_pre, 0.0) + jnp.log1p(jnp.exp(-jnp.abs(f_pre))))
            b = _lane_scan(log_f, jnp.add, 0.0, rev)
            a = jnp.broadcast_to(b[:, 0:1] if rev else b[:, L - 1:L], (nc, L))
            g = a - b + i_pre
            g_max = jnp.broadcast_to(jnp.max(g, axis=1, keepdims=True), (nc, L))
            m = jnp.zeros((1, L), F32)
            for c in (range(nc - 1, -1, -1) if rev else range(nc)):
                m0_ref[hh, d, c:c + 1, :] = m
                m = jnp.maximum(a[c:c + 1, :] + m, g_max[c:c + 1, :])
                m1_ref[hh, d, c:c + 1, :] = m
            m0 = m0_ref[hh, d]
            m1 = m1_ref[hh, d]
            ib = i_pre - b
            m_t = jnp.maximum(b + m0, b + _lane_scan(ib, jnp.maximum, -jnp.inf, rev))
            ib_ref[hh, d] = ib
            u_ref[hh, d] = b - m_t
            negm_ref[hh, d] = -m_t
            wk_ref[hh, d] = jnp.exp(g - m1)
            decay_ref[hh, d] = jnp.exp(a + m0 - m1)

    t_idx = lax.broadcasted_iota(jnp.int32, (L, L), 0)
    s_idx = lax.broadcasted_iota(jnp.int32, (L, L), 1)
    ones_ext = jnp.ones((L, LANES), BF16)

    def chunk_step(hh, d, c):
        r0 = pl.multiple_of(c * L, L)
        q = qs_ref[c, :, hh * dk:(hh + 1) * dk]
        kt = kt_ref[hh, c]
        v_ext = jnp.concatenate([v_ref[0, pl.ds(r0, L), hh * dv:(hh + 1) * dv].astype(BF16), ones_ext], axis=1)

        def row(ref):
            return ref[hh, d, pl.ds(c, 1), :]

        umat = jnp.broadcast_to(row(u_ref), (L, L)).T
        nmat = jnp.broadcast_to(row(negm_ref), (L, L)).T
        causal = (s_idx <= t_idx) if d == 0 else (s_idx >= t_idx)
        w_intra = jnp.where(causal, jnp.exp(umat + row(ib_ref)), 0.0)
        w_inter = jnp.exp(umat + row(m0_ref))
        s_qk = jnp.dot(q, kt.astype(BF16), preferred_element_type=F32)
        c_ext = c_ref[hh, d]
        lhs = jnp.concatenate([(w_intra * s_qk).astype(BF16), (q.astype(F32) * w_inter).astype(BF16)], axis=1)
        rhs = jnp.concatenate([v_ext, c_ext.astype(BF16)], axis=0)
        num = jnp.dot(lhs, rhs, preferred_element_type=F32)
        r = 1.0 / jnp.maximum(jnp.abs(num[:, dv:]), jnp.exp(nmat))
        h = num[:, :dv] * jnp.concatenate([r] * (dv // LANES), axis=1)

        upd = jnp.dot((kt * row(wk_ref)).astype(BF16), v_ext, preferred_element_type=F32)
        decay = jnp.broadcast_to(row(decay_ref), (dk, LANES))
        c_ref[hh, d] = jnp.concatenate([decay] * (c_ext.shape[1] // LANES), axis=1) * c_ext + upd
        return h

    def finish(hh, c, hs):
        rows = pl.ds(pl.multiple_of(c * L, L), L)
        cols = slice(hh * dv, (hh + 1) * dv)
        y = hs * lax.rsqrt(jnp.mean(hs * hs, axis=-1, keepdims=True) + RMS_EPS) * nw_ref[:, cols]
        out_ref[0, rows, cols] = (y * _sigmoid(o_ref[0, rows, cols])).astype(out_ref.dtype)

    def first_half(j, carry):
        cf, cr = j, nc - 1 - j
        for hh in range(hp):
            hf_ref[hh, pl.ds(pl.multiple_of(cf * L, L), L), :] = chunk_step(hh, 0, cf)
            hr_ref[hh, pl.ds(pl.multiple_of((cr - nc // 2) * L, L), L), :] = chunk_step(hh, 1, cr)
        return carry

    def second_half(j, carry):
        cf, cr = j, nc - 1 - j
        for hh in range(hp):
            finish(hh, cf, chunk_step(hh, 0, cf) + hr_ref[hh, pl.ds(pl.multiple_of((cf - nc // 2) * L, L), L), :])
            finish(hh, cr, hf_ref[hh, pl.ds(pl.multiple_of(cr * L, L), L), :] + chunk_step(hh, 1, cr))
        return carry

    c_ref[...] = jnp.zeros_like(c_ref)
    lax.fori_loop(0, nc // 2, first_half, 0, unroll=2)
    lax.fori_loop(nc // 2, nc, second_half, 0, unroll=2)


def mlstm_branch(proj3, gates, conv_w, conv_b, norm_w, hp=2):
    bsz, s, _ = proj3.shape
    L = M_CHUNK
    nc = s // L
    dk, dv = M_QK_DIM, M_V_DIM
    assert dk == L and nc % 2 == 0 and M_HEADS % hp == 0
    wq, wv = hp * dk, hp * dv
    k_off = M_HEADS * dk // wq
    return pl.pallas_call(
        _mlstm_kernel,
        grid=(bsz, M_HEADS // hp),
        in_specs=[pl.BlockSpec((1, s, wq), lambda b, h: (b, 0, COL_Q_M // wq + h)),
                  pl.BlockSpec((1, s, wq), lambda b, h: (b, 0, COL_K_M // wq + h)),
                  pl.BlockSpec((1, s, wv), lambda b, h: (b, 0, COL_V_M // wv + h)),
                  pl.BlockSpec((1, s, wv), lambda b, h: (b, 0, COL_O_M // wv + h)),
                  pl.BlockSpec((1, hp, 4, nc, L), lambda b, h: (b, h, 0, 0, 0)),
                  pl.BlockSpec((3, wq), lambda b, h: (0, h)),
                  pl.BlockSpec((3, wq), lambda b, h: (0, k_off + h)),
                  pl.BlockSpec((1, wq), lambda b, h: (0, h)),
                  pl.BlockSpec((1, wq), lambda b, h: (0, k_off + h)),
                  pl.BlockSpec((1, wv), lambda b, h: (0, h))],
        out_specs=pl.BlockSpec((1, s, wv), lambda b, h: (b, 0, h)),
        out_shape=jax.ShapeDtypeStruct((bsz, s, M_HEADS * dv), BF16),
        scratch_shapes=[pltpu.VMEM((nc, L, wq), BF16),
                        pltpu.VMEM((hp, nc, dk, L), F32),
                        pltpu.VMEM((hp, s // 2, dv), F32),
                        pltpu.VMEM((hp, s // 2, dv), F32),
                        pltpu.VMEM((hp, 2, dk, dv + LANES), F32),
                        *([pltpu.VMEM((hp, 2, nc, L), F32)] * 7)],
        compiler_params=_cparams(("parallel", "parallel")),
        name="mlstm_branch",
    )(proj3, proj3, proj3, proj3, gates, conv_w, conv_w, conv_b.reshape(1, -1), conv_b.reshape(1, -1),
      norm_w.reshape(1, -1))


def _attn_kernel(slopes_ref, q0, q1, q2, k0, k1, k2, v0, v1, v2, out_ref, og_ref, lse_ref, bias_ref):
    s_len = out_ref.shape[1]
    dh = A_HEAD_DIM
    T = 128
    slot = pl.program_id(1)
    qs, ks, vs = (q0, q1, q2), (k0, k1, k2), (v0, v1, v2)

    for g, (window, dil) in enumerate(A_GROUPS):
        side = window // (2 * dil)
        u_len = s_len // dil
        nqb = u_len // T
        kw = min(T + 2 * side, u_len)
        slope = slopes_ref[g * A_SLOTS + slot] * float(dil)
        q_ref, k_ref, v_ref = qs[g], ks[g], vs[g]

        offsets = (0, side, kw - T) if nqb > 1 else (0,)
        for case, off in enumerate(offsets):
            rel = jnp.abs(lax.broadcasted_iota(jnp.int32, (T, kw), 0) + off
                          - lax.broadcasted_iota(jnp.int32, (T, kw), 1))
            bias_ref[g, case, :, :kw] = jnp.where(rel <= side, -slope * rel.astype(F32), -1e30)

        def block(idx, carry, g=g, dil=dil, side=side, u_len=u_len, nqb=nqb, kw=kw,
                  q_ref=q_ref, k_ref=k_ref, v_ref=v_ref):
            r = idx // nqb
            j = idx % nqb
            u0 = j * T
            ku0 = jnp.clip(u0 - side, 0, u_len - kw)
            case = jnp.where(j == 0, 0, jnp.where(j == nqb - 1, 2, 1)) if nqb > 1 else 0
            q_rows = pl.ds(r + u0 * dil, T, stride=dil) if dil > 1 else pl.ds(pl.multiple_of(u0, T), T)
            k_rows = pl.ds(r + ku0 * dil, kw, stride=dil) if dil > 1 else pl.ds(pl.multiple_of(ku0, side), kw)
            q = (q_ref[0, q_rows, :] * (dh ** -0.5)).astype(BF16)
            kk = k_ref[0, k_rows, :].astype(BF16)
            vv = v_ref[0, k_rows, :].astype(BF16)
            s = lax.dot_general(q, kk, (((1,), (1,)), ((), ())), preferred_element_type=F32)
            s = s + bias_ref[g, case, :, :kw]
            m = jnp.max(s, axis=1, keepdims=True)
            p = jnp.exp(s - m)
            den = jnp.sum(p, axis=1, keepdims=True)
            o = jnp.dot(p.astype(BF16), vv, preferred_element_type=F32) * (1.0 / den)
            og_ref[g, q_rows, :] = o
            lse_ref[g, q_rows, :] = jnp.broadcast_to(m + jnp.log(den), (T, LANES))
            return carry

        lax.fori_loop(0, dil * nqb, block, 0, unroll=8)

    rows_per = 256

    def merge(i, carry):
        rows = pl.ds(pl.multiple_of(i * rows_per, rows_per), rows_per)
        l0, l1, l2 = lse_ref[0, rows, :], lse_ref[1, rows, :], lse_ref[2, rows, :]
        mx = jnp.maximum(jnp.maximum(l0, l1), l2)
        e0, e1, e2 = jnp.exp(l0 - mx), jnp.exp(l1 - mx), jnp.exp(l2 - mx)
        inv = 1.0 / (e0 + e1 + e2)
        o = (e0 * inv) * og_ref[0, rows, :] + (e1 * inv) * og_ref[1, rows, :] + (e2 * inv) * og_ref[2, rows, :]
        out_ref[0, rows, :] = o.astype(out_ref.dtype)
        return carry

    lax.fori_loop(0, s_len // rows_per, merge, 0)


def attention_branch(proj3, slopes):
    bsz, s, _ = proj3.shape
    dh = A_HEAD_DIM

    def col(base, g):
        return lambda b, t, sl: (b, 0, base // dh + g * A_SLOTS + t)

    grid_spec = pltpu.PrefetchScalarGridSpec(
        num_scalar_prefetch=1,
        grid=(bsz, A_SLOTS),
        in_specs=[pl.BlockSpec((1, s, dh), col(base, g))
                  for base in (COL_Q_A, COL_K_A, COL_V_A) for g in range(len(A_GROUPS))],
        out_specs=pl.BlockSpec((1, s, dh), lambda b, t, sl: (b, 0, t)),
        scratch_shapes=[pltpu.VMEM((3, s, dh), F32), pltpu.VMEM((3, s, LANES), F32),
                        pltpu.VMEM((len(A_GROUPS), 3, 128, 256), F32)],
    )
    return pl.pallas_call(
        _attn_kernel,
        grid_spec=grid_spec,
        out_shape=jax.ShapeDtypeStruct((bsz, s, A_SLOTS * dh), BF16),
        compiler_params=_cparams(("parallel", "parallel")),
        name="dilated_attention",
    )(slopes, *([proj3] * 9))


def _merge_kernel(n0, xp_ref, xs_ref, hm_ref, at_ref, ga_ref, gb_ref, mod_ref, pa_ref, pb_ref, wo_ref, n2_ref,
                  wrh_ref, wrl_ref, br_ref, x1_ref, h2_ref, lg_ref):
    i = pl.program_id(0)
    x = jnp.where(i < n0, xp_ref[...], xs_ref[...])
    y_a = jnp.dot(hm_ref[...], pa_ref[...], preferred_element_type=F32)
    y_b = jnp.dot(at_ref[...], pb_ref[...], preferred_element_type=F32)
    mixin = _sigmoid(ga_ref[...]) * y_a + _sigmoid(gb_ref[...]) * y_b
    mix = jnp.dot(mixin.astype(BF16), wo_ref[...], preferred_element_type=F32)
    x1 = x + mod_ref[0, 2:3, :] * mix
    x1_ref[...] = x1
    y = x1 * lax.rsqrt(jnp.mean(x1 * x1, axis=-1, keepdims=True) + RMS_EPS) * n2_ref[...]
    h2 = y * (1.0 + mod_ref[0, 4:5, :]) + mod_ref[0, 3:4, :]
    h2_ref[...] = h2
    hi = h2.astype(BF16)
    lo = (h2 - hi.astype(F32)).astype(BF16)
    lg_ref[...] = (jnp.dot(hi, wrh_ref[...], preferred_element_type=F32)
                   + (jnp.dot(hi, wrl_ref[...], preferred_element_type=F32)
                      + jnp.dot(lo, wrh_ref[...], preferred_element_type=F32))
                   + br_ref[...])


def merge_project(xp2, xs2, hm, at, proj, mod3, p_a, p_b, w_out, norm2_w, wr_hi, wr_lo, br, seq, tm=256):
    n, d = hm.shape
    n0 = xp2.shape[0] // tm
    n1 = xs2.shape[0] // tm
    per_seq = seq // tm
    const = dict(pipeline_mode=pl.Buffered(1))
    return pl.pallas_call(
        functools.partial(_merge_kernel, n0),
        grid=(n0 + n1,),
        in_specs=[pl.BlockSpec((tm, d), lambda i: (jnp.minimum(i, n0 - 1), 0)),
                  pl.BlockSpec((tm, d), lambda i: (jnp.maximum(i - n0, 0), 0)),
                  pl.BlockSpec((tm, d), lambda i: (i, 0)),
                  pl.BlockSpec((tm, at.shape[1]), lambda i: (i, 0)),
                  pl.BlockSpec((tm, d), lambda i: (i, COL_GATE_A // d)),
                  pl.BlockSpec((tm, d), lambda i: (i, COL_GATE_B // d)),
                  pl.BlockSpec((1, 6, d), lambda i: (i // per_seq, 0, 0)),
                  pl.BlockSpec(p_a.shape, lambda i: (0, 0), **const),
                  pl.BlockSpec(p_b.shape, lambda i: (0, 0), **const),
                  pl.BlockSpec(w_out.shape, lambda i: (0, 0), **const),
                  pl.BlockSpec((1, d), lambda i: (0, 0)),
                  pl.BlockSpec(wr_hi.shape, lambda i: (0, 0), **const),
                  pl.BlockSpec(wr_lo.shape, lambda i: (0, 0), **const),
                  pl.BlockSpec((1, ROUTER_COLS), lambda i: (0, 0))],
        out_specs=[pl.BlockSpec((tm, d), lambda i: (i, 0)),
                   pl.BlockSpec((tm, d), lambda i: (i, 0)),
                   pl.BlockSpec((tm, ROUTER_COLS), lambda i: (i, 0))],
        out_shape=[jax.ShapeDtypeStruct((n, d), F32),
                   jax.ShapeDtypeStruct((n, d), F32),
                   jax.ShapeDtypeStruct((n, ROUTER_COLS), F32)],
        compiler_params=_cparams(("parallel",)),
        name="merge_project",
    )(xp2, xs2, hm, at, proj, proj, mod3, p_a, p_b, w_out, norm2_w.reshape(1, d), wr_hi, wr_lo, br)


def route(logits, tb):
    n = logits.shape[0]
    g_logits = logits[:, :N_GROUPS]
    e_logits = logits[:, N_GROUPS:N_GROUPS + N_EXPERTS].reshape(n, N_GROUPS, EXPERTS_PER_GROUP)
    g_idx = jnp.argmax(g_logits, axis=-1)
    g_w = jnp.take_along_axis(jax.nn.softmax(g_logits, axis=-1), g_idx[:, None], axis=-1)
    e_sel = jnp.take_along_axis(e_logits, g_idx[:, None, None], axis=1)[:, 0]
    top_v, top_i = lax.top_k(e_sel, TOP_K)
    weights = g_w * jax.nn.softmax(top_v, axis=-1)
    expert = (g_idx[:, None] * EXPERTS_PER_GROUP + top_i).astype(jnp.int32)
    a = n * TOP_K
    flat_e = expert.reshape(a)
    e_ids = jnp.arange(N_EXPERTS, dtype=jnp.int32)
    counts = jnp.sum(flat_e[:, None] == e_ids[None, :], axis=0, dtype=jnp.int32)
    padded = (counts + tb - 1) // tb * tb
    pad_end = jnp.cumsum(padded)
    filler_e = jnp.repeat(e_ids, tb)
    filler_j = jnp.tile(jnp.arange(tb, dtype=jnp.int32), N_EXPERTS)
    filler_key = jnp.where(filler_j < (padded - counts)[filler_e], 2 * filler_e + 1, 2 * N_EXPERTS + 1)
    keys = jnp.concatenate([2 * flat_e, filler_key])
    ids = jnp.arange(a, dtype=jnp.int32)
    filler0 = jnp.zeros((N_EXPERTS * tb,), jnp.int32)
    tok_src = jnp.concatenate([ids // TOP_K, filler0])
    out_src = jnp.concatenate([(ids % TOP_K) * n + ids // TOP_K, filler0])
    sorted_keys, row_tok, row_out = lax.sort((keys, tok_src, out_src), num_keys=1)
    r = a + N_EXPERTS * tb
    n_blocks = r // tb
    block_start = jnp.arange(n_blocks, dtype=jnp.int32) * tb
    block_expert = jnp.minimum(jnp.sum(block_start[:, None] >= pad_end[None, :], axis=1), N_EXPERTS - 1).astype(jnp.int32)
    n_used = (pad_end[-1] // tb).astype(jnp.int32).reshape(1)
    pos = jnp.arange(r, dtype=jnp.int32)
    inv_key = jnp.where(sorted_keys % 2 == 0, row_out, a + pos)
    _, inv = lax.sort((inv_key, pos), num_keys=1)
    return (block_expert, n_used, row_tok), inv[:a], weights


ROW_DMA_UNROLL = 8


def _issue_row_gather(src_hbm, dst_buf, sem, index_of, n_rows):
    def body(j, carry):
        pltpu.make_async_copy(src_hbm.at[pl.ds(index_of(j), 1), :], dst_buf.at[pl.ds(j, 1), :], sem).start()
        return carry
    lax.fori_loop(0, n_rows, body, 0, unroll=ROW_DMA_UNROLL)


def _wait_row_gather(src_hbm, dst_buf, sem):
    pltpu.make_async_copy(src_hbm.at[pl.ds(0, dst_buf.shape[0]), :], dst_buf, sem).wait()


def _expert_kernel(be_ref, nused_ref, rtok_ref, h2_hbm, wg_ref, wu_ref, wd_ref, out_ref, xbuf, gsem):
    tb = xbuf.shape[1]
    i = pl.program_id(0)
    nused = nused_ref[0]
    slot = i % 2

    def issue(blk, s):
        _issue_row_gather(h2_hbm, xbuf.at[s], gsem.at[s], lambda j: rtok_ref[blk * tb + j], tb)

    @pl.when(jnp.logical_and(i == 0, nused > 0))
    def _():
        issue(0, 0)

    @pl.when(i + 1 < nused)
    def _():
        issue(i + 1, 1 - slot)

    @pl.when(i < nused)
    def _():
        _wait_row_gather(h2_hbm, xbuf.at[slot], gsem.at[slot])
        x = xbuf[slot].astype(BF16)
        g = jnp.dot(x, wg_ref[0], preferred_element_type=F32)
        u = jnp.dot(x, wu_ref[0], preferred_element_type=F32)
        hdn = (g * _sigmoid(g) * u).astype(BF16)
        out_ref[...] = jnp.dot(hdn, wd_ref[0], preferred_element_type=F32)

    @pl.when(i >= nused)
    def _():
        out_ref[...] = jnp.zeros_like(out_ref)


def expert_ffn(h2, tables, wg, wu, wd, tb=MOE_ROWS):
    block_expert, n_used, row_tok = tables
    n, d = h2.shape
    nb = block_expert.shape[0]
    de = wg.shape[2]
    grid_spec = pltpu.PrefetchScalarGridSpec(
        num_scalar_prefetch=3,
        grid=(nb,),
        in_specs=[pl.BlockSpec(memory_space=pl.ANY),
                  pl.BlockSpec((1, d, de), lambda i, be, *_: (be[i], 0, 0)),
                  pl.BlockSpec((1, d, de), lambda i, be, *_: (be[i], 0, 0)),
                  pl.BlockSpec((1, de, d), lambda i, be, *_: (be[i], 0, 0))],
        out_specs=pl.BlockSpec((tb, d), lambda i, *_: (i, 0)),
        scratch_shapes=[pltpu.VMEM((2, tb, d), F32), pltpu.SemaphoreType.DMA((2,))],
    )
    return pl.pallas_call(
        _expert_kernel,
        grid_spec=grid_spec,
        out_shape=jax.ShapeDtypeStruct((nb * tb, d), F32),
        compiler_params=_cparams(("arbitrary",)),
        name="expert_ffn",
    )(block_expert, n_used, row_tok, h2, wg, wu, wd)


def _final_kernel(n_tok, tile0, inv_ref, x1_ref, rw_ref, mod_ref, w_ref, ys_hbm, o_ref, buf, sem):
    tm = x1_ref.shape[0]
    i = pl.program_id(0)
    slot = i % 2

    def issue(tile, s):
        base = (tile0 + tile) * tm
        for k in range(TOP_K):
            _issue_row_gather(ys_hbm, buf.at[s, k], sem.at[s], lambda j, k=k: inv_ref[k * n_tok + base + j], tm)

    @pl.when(i == 0)
    def _():
        issue(0, 0)

    @pl.when(i + 1 < pl.num_programs(0))
    def _():
        issue(i + 1, 1 - slot)

    for k in range(TOP_K):
        _wait_row_gather(ys_hbm, buf.at[slot, k], sem.at[slot])
    rw = rw_ref[...]
    moe = buf[slot, 0] * rw[:, 0:1]
    for k in range(1, TOP_K):
        moe = moe + buf[slot, k] * rw[:, k:k + 1]
    x = x1_ref[...] + mod_ref[0, 5:6, :] * moe
    o_ref[...] = x * lax.rsqrt(jnp.mean(x * x, axis=-1, keepdims=True) + RMS_EPS) * w_ref[...]


def final_norm(x1, ys, inv, route_w, mod3, final_w, row0, rows, seq, tm=256):
    n_tok, d = x1.shape
    off = row0 // tm
    per_seq = seq // tm
    grid_spec = pltpu.PrefetchScalarGridSpec(
        num_scalar_prefetch=1,
        grid=(rows // tm,),
        in_specs=[pl.BlockSpec((tm, d), lambda i, inv: (off + i, 0)),
                  pl.BlockSpec((tm, TOP_K), lambda i, inv: (off + i, 0)),
                  pl.BlockSpec((1, 6, d), lambda i, inv: ((off + i) // per_seq, 0, 0)),
                  pl.BlockSpec((1, d), lambda i, inv: (0, 0)),
                  pl.BlockSpec(memory_space=pl.ANY)],
        out_specs=pl.BlockSpec((tm, d), lambda i, inv: (i, 0)),
        scratch_shapes=[pltpu.VMEM((2, TOP_K, tm, d), F32), pltpu.SemaphoreType.DMA((2,))],
    )
    return pl.pallas_call(
        functools.partial(_final_kernel, n_tok, off),
        grid_spec=grid_spec,
        out_shape=jax.ShapeDtypeStruct((rows, d), F32),
        compiler_params=_cparams(("arbitrary",)),
        name="final_norm",
    )(inv, x1, route_w, mod3, final_w.reshape(1, d), ys)


def _permute_cols(w):
    return jnp.concatenate([w[..., lo:hi] for lo, hi in _SRC_RANGES], axis=-1)


def kernel(x_prompt, x_sample, c_prompt, c_sample, w_ada, b_ada, norm1_w, w_in, b_in, mlstm_gate_b, conv_w, conv_b, mlstm_norm_w, p_a, p_b, w_out, norm2_w, w_router_group, b_router_group, w_router_expert, b_router_expert, w_expert_gate, w_expert_up, w_expert_down, final_norm_w):
    bp, seq, d = x_prompt.shape
    bs = x_sample.shape[0]
    bt = bp + bs
    n = bt * seq
    layer = 0

    pad = PROJ_A_COLS - (COL_GATE_M + GATE_M_W)
    w_in_a = jnp.concatenate([_permute_cols(w_in[layer]).astype(BF16), jnp.zeros((d, pad), BF16)], axis=1)
    b_gate = b_in[layer, PROJ_M_COLS:_SRC_A0] + mlstm_gate_b[layer]
    b_in_a = jnp.concatenate([_permute_cols(b_in[layer])[:COL_GATE_M], b_gate, jnp.zeros((pad,), F32)])
    wr = jnp.concatenate([w_router_group[layer], w_router_expert[layer],
                          jnp.zeros((d, ROUTER_COLS - N_GROUPS - N_EXPERTS), F32)], axis=1)
    br = jnp.concatenate([b_router_group[layer], b_router_expert[layer],
                          jnp.zeros((ROUTER_COLS - N_GROUPS - N_EXPERTS,), F32)]).reshape(1, ROUTER_COLS)
    wr_hi = wr.astype(BF16)
    wr_lo = (wr - wr_hi.astype(F32)).astype(BF16)
    slopes = 2.0 ** (-8.0 * jnp.arange(1, A_HEADS + 1, dtype=F32) / A_HEADS)

    c_all = jnp.concatenate([c_prompt, c_sample, jnp.zeros((16 - bt, d), F32)], axis=0)
    mod3 = ada_modulation(c_all, w_ada[layer], b_ada[layer])[:bt].reshape(bt, 6, d)

    h = norm_modulate(x_prompt, x_sample, mod3, norm1_w[layer])
    h2d = h.reshape(n, d)
    proj_m = matmul_bias_f32w(h2d, w_in[layer], b_in[layer], PROJ_M_COLS)
    proj_a = matmul_bias(h2d, w_in_a, b_in_a)
    proj_a3 = proj_a.reshape(bt, seq, PROJ_A_COLS)

    nc = seq // M_CHUNK
    gates = proj_a3[:, :, COL_GATE_M:COL_GATE_M + GATE_M_W].reshape(bt, nc, M_CHUNK, 4, M_HEADS)
    gates = gates.transpose(0, 4, 3, 1, 2)
    hm = mlstm_branch(proj_m.reshape(bt, seq, PROJ_M_COLS), gates, conv_w[layer], conv_b[layer], mlstm_norm_w[layer])
    at = attention_branch(proj_a3, slopes)

    x1, h2, logits = merge_project(
        x_prompt.reshape(bp * seq, d), x_sample.reshape(bs * seq, d), hm.reshape(n, -1), at.reshape(n, -1), proj_a,
        mod3, p_a[layer].astype(BF16), p_b[layer].astype(BF16), w_out[layer].astype(BF16), norm2_w[layer],
        wr_hi, wr_lo, br, seq)

    tables, inv, route_w = route(logits, MOE_ROWS)
    ys = expert_ffn(h2, tables, w_expert_gate[layer].astype(BF16), w_expert_up[layer].astype(BF16),
                    w_expert_down[layer].astype(BF16))

    y_p = final_norm(x1, ys, inv, route_w, mod3, final_norm_w, 0, bp * seq, seq)
    y_s = final_norm(x1, ys, inv, route_w, mod3, final_norm_w, bp * seq, bs * seq, seq)
    return (y_p.reshape(bp, seq, d), y_s.reshape(bs, seq, d))
```

```python
import functools

import jax
import jax.numpy as jnp
from jax import lax
from jax.experimental import pallas as pl
from jax.experimental.pallas import tpu as pltpu

F32 = jnp.float32
BF16 = jnp.bfloat16

D_MODEL = 2048
RMS_EPS = 1e-6
M_HEADS = 8
M_QK_DIM = 128
M_V_DIM = 256
M_CHUNK = 128
A_GROUPS = ((128, 1), (512, 4), (2048, 16))
A_SLOTS = 4
A_HEADS = A_SLOTS * len(A_GROUPS)
A_HEAD_DIM = 128
N_GROUPS = 4
EXPERTS_PER_GROUP = 8
N_EXPERTS = N_GROUPS * EXPERTS_PER_GROUP
TOP_K = 2
D_EXPERT = 1024

M_QK_W = M_HEADS * M_QK_DIM
M_V_W = M_HEADS * M_V_DIM
A_W = A_HEADS * A_HEAD_DIM
PROJ_TN = 512
COL_Q_M = 0
COL_K_M = M_QK_W
COL_V_M = 2 * M_QK_W
COL_O_M = 2 * M_QK_W + M_V_W
COL_GATE_M = 2 * M_QK_W + 2 * M_V_W
GATE_M_W = 4 * M_HEADS
PROJ_M_COLS = COL_GATE_M + PROJ_TN
_SRC_A0 = COL_GATE_M + GATE_M_W
COL_Q_A = 0
COL_K_A = A_W
COL_V_A = 2 * A_W
COL_GATE_A = 3 * A_W
COL_GATE_B = 3 * A_W + D_MODEL
PROJ_A_COLS = 3 * A_W + 2 * D_MODEL

LANES = 128
MOE_ROWS = 256
ROUTER_COLS = 128
VMEM_LIMIT = 56 * 1024 * 1024


def _sigmoid(x):
    return 1.0 / (1.0 + jnp.exp(-x))


def _cparams(sem, vmem=VMEM_LIMIT):
    return pltpu.CompilerParams(dimension_semantics=sem, vmem_limit_bytes=vmem)


def _ada_kernel(c_ref, w_ref, b_ref, o_ref):
    c = c_ref[...]
    a = (c * _sigmoid(c)).astype(BF16)
    o_ref[...] = jnp.dot(a, w_ref[...].astype(BF16), preferred_element_type=F32) + b_ref[...]


def ada_modulation(c, w_ada, b_ada):
    rows, d = c.shape
    n = w_ada.shape[1]
    tn = 1024
    return pl.pallas_call(
        _ada_kernel,
        grid=(n // tn,),
        in_specs=[pl.BlockSpec((rows, d), lambda j: (0, 0)),
                  pl.BlockSpec((d, tn), lambda j: (0, j)),
                  pl.BlockSpec((1, tn), lambda j: (0, j))],
        out_specs=pl.BlockSpec((rows, tn), lambda j: (0, j)),
        out_shape=jax.ShapeDtypeStruct((rows, n), F32),
        compiler_params=_cparams(("parallel",)),
        name="ada_modulation",
    )(c, w_ada, b_ada.reshape(1, n))


def _norm_mod_kernel(nb0, xp_ref, xs_ref, mod_ref, w_ref, o_ref):
    b = pl.program_id(0)
    x = jnp.where(b < nb0, xp_ref[0], xs_ref[0])
    y = x * lax.rsqrt(jnp.mean(x * x, axis=-1, keepdims=True) + RMS_EPS) * w_ref[...]
    o_ref[0] = (y * (1.0 + mod_ref[0, 1:2, :]) + mod_ref[0, 0:1, :]).astype(o_ref.dtype)


def norm_modulate(xp, xs, mod3, norm_w, ts=512):
    nb0, s, d = xp.shape
    nb1 = xs.shape[0]
    return pl.pallas_call(
        functools.partial(_norm_mod_kernel, nb0),
        grid=(nb0 + nb1, s // ts),
        in_specs=[pl.BlockSpec((1, ts, d), lambda b, t: (jnp.minimum(b, nb0 - 1), jnp.where(b < nb0, t, s // ts - 1), 0)),
                  pl.BlockSpec((1, ts, d), lambda b, t: (jnp.maximum(b - nb0, 0), jnp.where(b < nb0, 0, t), 0)),
                  pl.BlockSpec((1, 6, d), lambda b, t: (b, 0, 0)),
                  pl.BlockSpec((1, d), lambda b, t: (0, 0))],
        out_specs=pl.BlockSpec((1, ts, d), lambda b, t: (b, t, 0)),
        out_shape=jax.ShapeDtypeStruct((nb0 + nb1, s, d), BF16),
        compiler_params=_cparams(("parallel", "parallel")),
        name="norm1_modulate",
    )(xp, xs, mod3, norm_w.reshape(1, d))


def _mm_bias_kernel(a_ref, w_ref, b_ref, o_ref):
    o_ref[...] = jnp.dot(a_ref[...], w_ref[...].astype(BF16), preferred_element_type=F32) + b_ref[...]


def matmul_bias(a, w, b, n, name, tm=2048, tn=PROJ_TN):
    m, k = a.shape
    return pl.pallas_call(
        _mm_bias_kernel,
        grid=(m // tm, n // tn),
        in_specs=[pl.BlockSpec((tm, k), lambda i, j: (i, 0)),
                  pl.BlockSpec((k, tn), lambda i, j: (0, j)),
                  pl.BlockSpec((1, tn), lambda i, j: (0, j))],
        out_specs=pl.BlockSpec((tm, tn), lambda i, j: (i, j)),
        out_shape=jax.ShapeDtypeStruct((m, n), F32),
        compiler_params=_cparams(("parallel", "parallel")),
        name=name,
    )(a, w, b.reshape(1, -1))


def _lane_scan(x, op, fill, reverse):
    n = x.shape[-1]
    axis = x.ndim - 1
    lane = lax.broadcasted_iota(jnp.int32, x.shape, axis)
    k = 1
    while k < n:
        if reverse:
            x = op(x, jnp.where(lane < n - k, pltpu.roll(x, n - k, axis), fill))
        else:
            x = op(x, jnp.where(lane >= k, pltpu.roll(x, k, axis), fill))
        k *= 2
    return x


def _conv_silu_chunk(x_ref, w_ref, b_ref, c, n_chunks):
    L = M_CHUNK
    s = n_chunks * L
    r0 = pl.multiple_of(c * L, L)
    x = x_ref[0, pl.ds(r0, L), :]
    prev_row = x_ref[0, pl.ds(jnp.maximum(r0 - 1, 0), 1), :]
    next_row = x_ref[0, pl.ds(jnp.minimum(r0 + L, s - 1), 1), :]
    prev_row = jnp.where(c > 0, prev_row, 0.0)
    next_row = jnp.where(c < n_chunks - 1, next_row, 0.0)
    rows = lax.broadcasted_iota(jnp.int32, x.shape, 0)
    x_prev = jnp.where(rows == 0, prev_row, pltpu.roll(x, 1, 0))
    x_next = jnp.where(rows == L - 1, next_row, pltpu.roll(x, L - 1, 0))
    y = b_ref[...] + x_prev * w_ref[0:1, :] + x * w_ref[1:2, :] + x_next * w_ref[2:3, :]
    return y * _sigmoid(y)


def _mlstm_kernel(q_ref, k_ref, v_ref, o_ref, g_ref, cwq_ref, cwk_ref, cbq_ref, cbk_ref, nw_ref, out_ref,
                  qs_ref, kt_ref, hf_ref, hr_ref, c_ref,
                  u_ref, negm_ref, ib_ref, wk_ref, decay_ref, m0_ref, m1_ref):
    L = M_CHUNK
    dk = M_QK_DIM
    dv = M_V_DIM
    hp, nc = kt_ref.shape[0], kt_ref.shape[1]

    def prep(c, carry):
        q = _conv_silu_chunk(q_ref, cwq_ref, cbq_ref, c, nc) * (M_QK_DIM ** -0.5)
        qs_ref[c] = q.astype(BF16)
        k = _conv_silu_chunk(k_ref, cwk_ref, cbk_ref, c, nc)
        for hh in range(hp):
            kt_ref[hh, c] = k[:, hh * dk:(hh + 1) * dk].T
        return carry

    lax.fori_loop(0, nc, prep, 0)

    for hh in range(hp):
        for d in range(2):
            rev = d == 1
            i_pre = g_ref[0, hh, 2 * d]
            f_pre = g_ref[0, hh, 2 * d + 1]
            log_f = -(jnp.maximum(-f_pre, 0.0) + jnp.log1p(jnp.exp(-jnp.abs(f_pre))))
            b = _lane_scan(log_f, jnp.add, 0.0, rev)
            a = jnp.broadcast_to(b[:, 0:1] if rev else b[:, L - 1:L], (nc, L))
            g = a - b + i_pre
            g_max = jnp.broadcast_to(jnp.max(g, axis=1, keepdims=True), (nc, L))
            m = jnp.zeros((1, L), F32)
            for c in (range(nc - 1, -1, -1) if rev else range(nc)):
                m0_ref[hh, d, c:c + 1, :] = m
                m = jnp.maximum(a[c:c + 1, :] + m, g_max[c:c + 1, :])
                m1_ref[hh, d, c:c + 1, :] = m
            m0 = m0_ref[hh, d]
            m1 = m1_ref[hh, d]
            ib = i_pre - b
            m_t = jnp.maximum(b + m0, b + _lane_scan(ib, jnp.maximum, -jnp.inf, rev))
            ib_ref[hh, d] = ib
            u_ref[hh, d] = b - m_t
            negm_ref[hh, d] = -m_t
            wk_ref[hh, d] = jnp.exp(g - m1)
            decay_ref[hh, d] = jnp.exp(a + m0 - m1)

    t_idx = lax.broadcasted_iota(jnp.int32, (L, L), 0)
    s_idx = lax.broadcasted_iota(jnp.int32, (L, L), 1)
    ones_ext = jnp.ones((L, LANES), BF16)

    def chunk_step(hh, d, c):
        r0 = pl.multiple_of(c * L, L)
        q = qs_ref[c, :, hh * dk:(hh + 1) * dk]
        kt = kt_ref[hh, c]
        v_ext = jnp.concatenate([v_ref[0, pl.ds(r0, L), hh * dv:(hh + 1) * dv].astype(BF16), ones_ext], axis=1)

        def row(ref):
            return ref[hh, d, pl.ds(c, 1), :]

        umat = jnp.broadcast_to(row(u_ref), (L, L)).T
        nmat = jnp.broadcast_to(row(negm_ref), (L, L)).T
        causal = (s_idx <= t_idx) if d == 0 else (s_idx >= t_idx)
        w_intra = jnp.where(causal, jnp.exp(umat + row(ib_ref)), 0.0)
        w_inter = jnp.exp(umat + row(m0_ref))
        s_qk = jnp.dot(q, kt.astype(BF16), preferred_element_type=F32)
        c_ext = c_ref[hh, d]
        lhs = jnp.concatenate([(w_intra * s_qk).astype(BF16), (q.astype(F32) * w_inter).astype(BF16)], axis=1)
        rhs = jnp.concatenate([v_ext, c_ext.astype(BF16)], axis=0)
        num = jnp.dot(lhs, rhs, preferred_element_type=F32)
        r = 1.0 / jnp.maximum(jnp.abs(num[:, dv:]), jnp.exp(nmat))
        h = num[:, :dv] * jnp.concatenate([r] * (dv // LANES), axis=1)

        upd = jnp.dot((kt * row(wk_ref)).astype(BF16), v_ext, preferred_element_type=F32)
        decay = jnp.broadcast_to(row(decay_ref), (dk, LANES))
        c_ref[hh, d] = jnp.concatenate([decay] * (c_ext.shape[1] // LANES), axis=1) * c_ext + upd
        return h

    def finish(hh, c, hs):
        rows = pl.ds(pl.multiple_of(c * L, L), L)
        cols = slice(hh * dv, (hh + 1) * dv)
        y = hs * lax.rsqrt(jnp.mean(hs * hs, axis=-1, keepdims=True) + RMS_EPS) * nw_ref[:, cols]
        out_ref[0, rows, cols] = (y * _sigmoid(o_ref[0, rows, cols])).astype(out_ref.dtype)

    def first_half(j, carry):
        cf, cr = j, nc - 1 - j
        for hh in range(hp):
            hf_ref[hh, pl.ds(pl.multiple_of(cf * L, L), L), :] = chunk_step(hh, 0, cf)
            hr_ref[hh, pl.ds(pl.multiple_of((cr - nc // 2) * L, L), L), :] = chunk_step(hh, 1, cr)
        return carry

    def second_half(j, carry):
        cf, cr = j, nc - 1 - j
        for hh in range(hp):
            finish(hh, cf, chunk_step(hh, 0, cf) + hr_ref[hh, pl.ds(pl.multiple_of((cf - nc // 2) * L, L), L), :])
            finish(hh, cr, hf_ref[hh, pl.ds(pl.multiple_of(cr * L, L), L), :] + chunk_step(hh, 1, cr))
        return carry

    c_ref[...] = jnp.zeros_like(c_ref)
    lax.fori_loop(0, nc // 2, first_half, 0, unroll=2)
    lax.fori_loop(nc // 2, nc, second_half, 0, unroll=2)


def mlstm_branch(proj3, gates, conv_w, conv_b, norm_w, hp=2):
    bsz, s, _ = proj3.shape
    L = M_CHUNK
    nc = s // L
    dk, dv = M_QK_DIM, M_V_DIM
    assert dk == L and nc % 2 == 0 and M_HEADS % hp == 0
    wq, wv = hp * dk, hp * dv
    k_off = M_HEADS * dk // wq
    return pl.pallas_call(
        _mlstm_kernel,
        grid=(bsz, M_HEADS // hp),
        in_specs=[pl.BlockSpec((1, s, wq), lambda b, h: (b, 0, COL_Q_M // wq + h)),
                  pl.BlockSpec((1, s, wq), lambda b, h: (b, 0, COL_K_M // wq + h)),
                  pl.BlockSpec((1, s, wv), lambda b, h: (b, 0, COL_V_M // wv + h)),
                  pl.BlockSpec((1, s, wv), lambda b, h: (b, 0, COL_O_M // wv + h)),
                  pl.BlockSpec((1, hp, 4, nc, L), lambda b, h: (b, h, 0, 0, 0)),
                  pl.BlockSpec((3, wq), lambda b, h: (0, h)),
                  pl.BlockSpec((3, wq), lambda b, h: (0, k_off + h)),
                  pl.BlockSpec((1, wq), lambda b, h: (0, h)),
                  pl.BlockSpec((1, wq), lambda b, h: (0, k_off + h)),
                  pl.BlockSpec((1, wv), lambda b, h: (0, h))],
        out_specs=pl.BlockSpec((1, s, wv), lambda b, h: (b, 0, h)),
        out_shape=jax.ShapeDtypeStruct((bsz, s, M_HEADS * dv), BF16),
        scratch_shapes=[pltpu.VMEM((nc, L, wq), BF16),
                        pltpu.VMEM((hp, nc, dk, L), F32),
                        pltpu.VMEM((hp, s // 2, dv), F32),
                        pltpu.VMEM((hp, s // 2, dv), F32),
                        pltpu.VMEM((hp, 2, dk, dv + LANES), F32),
                        *([pltpu.VMEM((hp, 2, nc, L), F32)] * 7)],
        compiler_params=_cparams(("parallel", "parallel")),
        name="mlstm_branch",
    )(proj3, proj3, proj3, proj3, gates, conv_w, conv_w, conv_b.reshape(1, -1), conv_b.reshape(1, -1),
      norm_w.reshape(1, -1))


def _attn_kernel(slopes_ref, q0, q1, q2, k0, k1, k2, v0, v1, v2, out_ref, og_ref, lse_ref, bias_ref):
    s_len = out_ref.shape[1]
    dh = A_HEAD_DIM
    T = 128
    slot = pl.program_id(1)
    qs, ks, vs = (q0, q1, q2), (k0, k1, k2), (v0, v1, v2)

    for g, (window, dil) in enumerate(A_GROUPS):
        side = window // (2 * dil)
        u_len = s_len // dil
        nqb = u_len // T
        kw = min(T + 2 * side, u_len)
        slope = slopes_ref[g * A_SLOTS + slot] * float(dil)
        q_ref, k_ref, v_ref = qs[g], ks[g], vs[g]

        offsets = (0, side, kw - T) if nqb > 1 else (0,)
        for case, off in enumerate(offsets):
            rel = jnp.abs(lax.broadcasted_iota(jnp.int32, (T, kw), 0) + off
                          - lax.broadcasted_iota(jnp.int32, (T, kw), 1))
            bias_ref[g, case, :, :kw] = jnp.where(rel <= side, -slope * rel.astype(F32), -1e30)

        def block(idx, carry, g=g, dil=dil, side=side, u_len=u_len, nqb=nqb, kw=kw,
                  q_ref=q_ref, k_ref=k_ref, v_ref=v_ref):
            r = idx // nqb
            j = idx % nqb
            u0 = j * T
            ku0 = jnp.clip(u0 - side, 0, u_len - kw)
            case = jnp.where(j == 0, 0, jnp.where(j == nqb - 1, 2, 1)) if nqb > 1 else 0
            q_rows = pl.ds(r + u0 * dil, T, stride=dil) if dil > 1 else pl.ds(pl.multiple_of(u0, T), T)
            k_rows = pl.ds(r + ku0 * dil, kw, stride=dil) if dil > 1 else pl.ds(pl.multiple_of(ku0, side), kw)
            q = (q_ref[0, q_rows, :] * (dh ** -0.5)).astype(BF16)
            kk = k_ref[0, k_rows, :].astype(BF16)
            vv = v_ref[0, k_rows, :].astype(BF16)
            s = lax.dot_general(q, kk, (((1,), (1,)), ((), ())), preferred_element_type=F32)
            s = s + bias_ref[g, case, :, :kw]
            m = jnp.max(s, axis=1, keepdims=True)
            p = jnp.exp(s - m)
            den = jnp.sum(p, axis=1, keepdims=True)
            o = jnp.dot(p.astype(BF16), vv, preferred_element_type=F32) * (1.0 / den)
            og_ref[g, q_rows, :] = o
            lse_ref[g, q_rows, :] = jnp.broadcast_to(m + jnp.log(den), (T, LANES))
            return carry

        lax.fori_loop(0, dil * nqb, block, 0, unroll=8)

    rows_per = 256

    def merge(i, carry):
        rows = pl.ds(pl.multiple_of(i * rows_per, rows_per), rows_per)
        l0, l1, l2 = lse_ref[0, rows, :], lse_ref[1, rows, :], lse_ref[2, rows, :]
        mx = jnp.maximum(jnp.maximum(l0, l1), l2)
        e0, e1, e2 = jnp.exp(l0 - mx), jnp.exp(l1 - mx), jnp.exp(l2 - mx)
        inv = 1.0 / (e0 + e1 + e2)
        o = (e0 * inv) * og_ref[0, rows, :] + (e1 * inv) * og_ref[1, rows, :] + (e2 * inv) * og_ref[2, rows, :]
        out_ref[0, rows, :] = o.astype(out_ref.dtype)
        return carry

    lax.fori_loop(0, s_len // rows_per, merge, 0)


def attention_branch(proj3, slopes):
    bsz, s, _ = proj3.shape
    dh = A_HEAD_DIM

    def col(base, g):
        return lambda b, t, sl: (b, 0, base // dh + g * A_SLOTS + t)

    grid_spec = pltpu.PrefetchScalarGridSpec(
        num_scalar_prefetch=1,
        grid=(bsz, A_SLOTS),
        in_specs=[pl.BlockSpec((1, s, dh), col(base, g))
                  for base in (COL_Q_A, COL_K_A, COL_V_A) for g in range(len(A_GROUPS))],
        out_specs=pl.BlockSpec((1, s, dh), lambda b, t, sl: (b, 0, t)),
        scratch_shapes=[pltpu.VMEM((3, s, dh), F32), pltpu.VMEM((3, s, LANES), F32),
                        pltpu.VMEM((len(A_GROUPS), 3, 128, 256), F32)],
    )
    return pl.pallas_call(
        _attn_kernel,
        grid_spec=grid_spec,
        out_shape=jax.ShapeDtypeStruct((bsz, s, A_SLOTS * dh), BF16),
        compiler_params=_cparams(("parallel", "parallel")),
        name="dilated_attention",
    )(slopes, *([proj3] * 9))


def _merge_kernel(n0, n_gate, xp_ref, xs_ref, hm_ref, at_ref, *refs):
    ga_refs, gb_refs = refs[:n_gate], refs[n_gate:2 * n_gate]
    (mod_ref, pa_ref, pb_ref, wo_ref, n2_ref, wrh_ref, wrl_ref, br_ref, x1_ref, h2_ref, lg_ref) = refs[2 * n_gate:]
    i = pl.program_id(0)
    x = jnp.where(i < n0, xp_ref[...], xs_ref[...])
    y_a = jnp.dot(hm_ref[...], pa_ref[...], preferred_element_type=F32)
    y_b = jnp.dot(at_ref[...], pb_ref[...], preferred_element_type=F32)
    gate_a = jnp.concatenate([r[...] for r in ga_refs], axis=1)
    gate_b = jnp.concatenate([r[...] for r in gb_refs], axis=1)
    mixin = _sigmoid(gate_a) * y_a + _sigmoid(gate_b) * y_b
    mix = jnp.dot(mixin.astype(BF16), wo_ref[...], preferred_element_type=F32)
    x1 = x + mod_ref[0, 2:3, :] * mix
    x1_ref[...] = x1
    y = x1 * lax.rsqrt(jnp.mean(x1 * x1, axis=-1, keepdims=True) + RMS_EPS) * n2_ref[...]
    h2 = y * (1.0 + mod_ref[0, 4:5, :]) + mod_ref[0, 3:4, :]
    h2_ref[...] = h2
    hi = h2.astype(BF16)
    lo = (h2 - hi.astype(F32)).astype(BF16)
    lg_ref[...] = (jnp.dot(hi, wrh_ref[...], preferred_element_type=F32)
                   + (jnp.dot(hi, wrl_ref[...], preferred_element_type=F32)
                      + jnp.dot(lo, wrh_ref[...], preferred_element_type=F32))
                   + br_ref[...])


def merge_project(xp2, xs2, hm, at, proj, mod3, p_a, p_b, w_out, norm2_w, wr_hi, wr_lo, br, seq, tm=256):
    n, d = hm.shape
    n0 = xp2.shape[0] // tm
    n1 = xs2.shape[0] // tm
    per_seq = seq // tm
    const = dict(pipeline_mode=pl.Buffered(1))
    gw = PROJ_TN
    n_gate = d // gw

    def gate_specs(col0):
        return [pl.BlockSpec((tm, gw), lambda i, t=t: (i, col0 // gw + t)) for t in range(n_gate)]

    return pl.pallas_call(
        functools.partial(_merge_kernel, n0, n_gate),
        grid=(n0 + n1,),
        in_specs=[pl.BlockSpec((tm, d), lambda i: (jnp.minimum(i, n0 - 1), 0)),
                  pl.BlockSpec((tm, d), lambda i: (jnp.maximum(i - n0, 0), 0)),
                  pl.BlockSpec((tm, d), lambda i: (i, 0)),
                  pl.BlockSpec((tm, at.shape[1]), lambda i: (i, 0)),
                  *gate_specs(COL_GATE_A), *gate_specs(COL_GATE_B),
                  pl.BlockSpec((1, 6, d), lambda i: (i // per_seq, 0, 0)),
                  pl.BlockSpec(p_a.shape, lambda i: (0, 0), **const),
                  pl.BlockSpec(p_b.shape, lambda i: (0, 0), **const),
                  pl.BlockSpec(w_out.shape, lambda i: (0, 0), **const),
                  pl.BlockSpec((1, d), lambda i: (0, 0)),
                  pl.BlockSpec(wr_hi.shape, lambda i: (0, 0), **const),
                  pl.BlockSpec(wr_lo.shape, lambda i: (0, 0), **const),
                  pl.BlockSpec((1, ROUTER_COLS), lambda i: (0, 0))],
        out_specs=[pl.BlockSpec((tm, d), lambda i: (i, 0)),
                   pl.BlockSpec((tm, d), lambda i: (i, 0)),
                   pl.BlockSpec((tm, ROUTER_COLS), lambda i: (i, 0))],
        out_shape=[jax.ShapeDtypeStruct((n, d), F32),
                   jax.ShapeDtypeStruct((n, d), F32),
                   jax.ShapeDtypeStruct((n, ROUTER_COLS), F32)],
        compiler_params=_cparams(("parallel",)),
        name="merge_project",
    )(xp2, xs2, hm, at, *([proj] * (2 * n_gate)), mod3, p_a, p_b, w_out, norm2_w.reshape(1, d), wr_hi, wr_lo, br)


def route(logits, tb):
    n = logits.shape[0]
    g_logits = logits[:, :N_GROUPS]
    e_logits = logits[:, N_GROUPS:N_GROUPS + N_EXPERTS].reshape(n, N_GROUPS, EXPERTS_PER_GROUP)
    g_idx = jnp.argmax(g_logits, axis=-1)
    g_w = jnp.take_along_axis(jax.nn.softmax(g_logits, axis=-1), g_idx[:, None], axis=-1)
    e_sel = jnp.take_along_axis(e_logits, g_idx[:, None, None], axis=1)[:, 0]
    top_v, top_i = lax.top_k(e_sel, TOP_K)
    weights = g_w * jax.nn.softmax(top_v, axis=-1)
    expert = (g_idx[:, None] * EXPERTS_PER_GROUP + top_i).astype(jnp.int32)
    a = n * TOP_K
    flat_e = expert.reshape(a)
    e_ids = jnp.arange(N_EXPERTS, dtype=jnp.int32)
    counts = jnp.sum(flat_e[:, None] == e_ids[None, :], axis=0, dtype=jnp.int32)
    padded = (counts + tb - 1) // tb * tb
    pad_end = jnp.cumsum(padded)
    filler_e = jnp.repeat(e_ids, tb)
    filler_j = jnp.tile(jnp.arange(tb, dtype=jnp.int32), N_EXPERTS)
    filler_key = jnp.where(filler_j < (padded - counts)[filler_e], 2 * filler_e + 1, 2 * N_EXPERTS + 1)
    keys = jnp.concatenate([2 * flat_e, filler_key])
    ids = jnp.arange(a, dtype=jnp.int32)
    filler0 = jnp.zeros((N_EXPERTS * tb,), jnp.int32)
    tok_src = jnp.concatenate([ids // TOP_K, filler0])
    out_src = jnp.concatenate([(ids % TOP_K) * n + ids // TOP_K, filler0])
    sorted_keys, row_tok, row_out = lax.sort((keys, tok_src, out_src), num_keys=1)
    r = a + N_EXPERTS * tb
    n_blocks = r // tb
    block_start = jnp.arange(n_blocks, dtype=jnp.int32) * tb
    block_expert = jnp.minimum(jnp.sum(block_start[:, None] >= pad_end[None, :], axis=1), N_EXPERTS - 1).astype(jnp.int32)
    n_used = (pad_end[-1] // tb).astype(jnp.int32).reshape(1)
    pos = jnp.arange(r, dtype=jnp.int32)
    inv_key = jnp.where(sorted_keys % 2 == 0, row_out, a + pos)
    _, inv = lax.sort((inv_key, pos), num_keys=1)
    return (block_expert, n_used, row_tok), inv[:a], weights


ROW_DMA_UNROLL = 8


def _issue_row_gather(src_hbm, dst_buf, sem, index_of, n_rows):
    def body(j, carry):
        pltpu.make_async_copy(src_hbm.at[pl.ds(index_of(j), 1), :], dst_buf.at[pl.ds(j, 1), :], sem).start()
        return carry
    lax.fori_loop(0, n_rows, body, 0, unroll=ROW_DMA_UNROLL)


def _wait_row_gather(src_hbm, dst_buf, sem):
    pltpu.make_async_copy(src_hbm.at[pl.ds(0, dst_buf.shape[0]), :], dst_buf, sem).wait()


def _expert_kernel(be_ref, nused_ref, rtok_ref, h2_hbm, wg_ref, wu_ref, wd_ref, out_ref, xbuf, gsem):
    tb = xbuf.shape[1]
    i = pl.program_id(0)
    nused = nused_ref[0]
    slot = i % 2

    def issue(blk, s):
        _issue_row_gather(h2_hbm, xbuf.at[s], gsem.at[s], lambda j: rtok_ref[blk * tb + j], tb)

    @pl.when(jnp.logical_and(i == 0, nused > 0))
    def _():
        issue(0, 0)

    @pl.when(i + 1 < nused)
    def _():
        issue(i + 1, 1 - slot)

    @pl.when(i < nused)
    def _():
        _wait_row_gather(h2_hbm, xbuf.at[slot], gsem.at[slot])
        x = xbuf[slot].astype(BF16)
        g = jnp.dot(x, wg_ref[0], preferred_element_type=F32)
        u = jnp.dot(x, wu_ref[0], preferred_element_type=F32)
        hdn = (g * _sigmoid(g) * u).astype(BF16)
        out_ref[...] = jnp.dot(hdn, wd_ref[0], preferred_element_type=F32)

    @pl.when(i >= nused)
    def _():
        out_ref[...] = jnp.zeros_like(out_ref)


def expert_ffn(h2, tables, wg, wu, wd, tb=MOE_ROWS):
    block_expert, n_used, row_tok = tables
    n, d = h2.shape
    nb = block_expert.shape[0]
    de = wg.shape[2]
    grid_spec = pltpu.PrefetchScalarGridSpec(
        num_scalar_prefetch=3,
        grid=(nb,),
        in_specs=[pl.BlockSpec(memory_space=pl.ANY),
                  pl.BlockSpec((1, d, de), lambda i, be, *_: (be[i], 0, 0)),
                  pl.BlockSpec((1, d, de), lambda i, be, *_: (be[i], 0, 0)),
                  pl.BlockSpec((1, de, d), lambda i, be, *_: (be[i], 0, 0))],
        out_specs=pl.BlockSpec((tb, d), lambda i, *_: (i, 0)),
        scratch_shapes=[pltpu.VMEM((2, tb, d), F32), pltpu.SemaphoreType.DMA((2,))],
    )
    return pl.pallas_call(
        _expert_kernel,
        grid_spec=grid_spec,
        out_shape=jax.ShapeDtypeStruct((nb * tb, d), F32),
        compiler_params=_cparams(("arbitrary",)),
        name="expert_ffn",
    )(block_expert, n_used, row_tok, h2, wg, wu, wd)


def _final_kernel(n_tok, tile0, inv_ref, x1_ref, rw_ref, mod_ref, w_ref, ys_hbm, o_ref, buf, sem):
    tm = x1_ref.shape[0]
    i = pl.program_id(0)
    slot = i % 2

    def issue(tile, s):
        base = (tile0 + tile) * tm
        for k in range(TOP_K):
            _issue_row_gather(ys_hbm, buf.at[s, k], sem.at[s], lambda j, k=k: inv_ref[k * n_tok + base + j], tm)

    @pl.when(i == 0)
    def _():
        issue(0, 0)

    @pl.when(i + 1 < pl.num_programs(0))
    def _():
        issue(i + 1, 1 - slot)

    for k in range(TOP_K):
        _wait_row_gather(ys_hbm, buf.at[slot, k], sem.at[slot])
    rw = rw_ref[...]
    moe = buf[slot, 0] * rw[:, 0:1]
    for k in range(1, TOP_K):
        moe = moe + buf[slot, k] * rw[:, k:k + 1]
    x = x1_ref[...] + mod_ref[0, 5:6, :] * moe
    o_ref[...] = x * lax.rsqrt(jnp.mean(x * x, axis=-1, keepdims=True) + RMS_EPS) * w_ref[...]


def final_norm(x1, ys, inv, route_w, mod3, final_w, row0, rows, seq, tm=256):
    n_tok, d = x1.shape
    off = row0 // tm
    per_seq = seq // tm
    grid_spec = pltpu.PrefetchScalarGridSpec(
        num_scalar_prefetch=1,
        grid=(rows // tm,),
        in_specs=[pl.BlockSpec((tm, d), lambda i, inv: (off + i, 0)),
                  pl.BlockSpec((tm, TOP_K), lambda i, inv: (off + i, 0)),
                  pl.BlockSpec((1, 6, d), lambda i, inv: ((off + i) // per_seq, 0, 0)),
                  pl.BlockSpec((1, d), lambda i, inv: (0, 0)),
                  pl.BlockSpec(memory_space=pl.ANY)],
        out_specs=pl.BlockSpec((tm, d), lambda i, inv: (i, 0)),
        scratch_shapes=[pltpu.VMEM((2, TOP_K, tm, d), F32), pltpu.SemaphoreType.DMA((2,))],
    )
    return pl.pallas_call(
        functools.partial(_final_kernel, n_tok, off),
        grid_spec=grid_spec,
        out_shape=jax.ShapeDtypeStruct((rows, d), F32),
        compiler_params=_cparams(("arbitrary",)),
        name="final_norm",
    )(inv, x1, route_w, mod3, final_w.reshape(1, d), ys)


def kernel(x_prompt, x_sample, c_prompt, c_sample, w_ada, b_ada, norm1_w, w_in, b_in, mlstm_gate_b, conv_w, conv_b, mlstm_norm_w, p_a, p_b, w_out, norm2_w, w_router_group, b_router_group, w_router_expert, b_router_expert, w_expert_gate, w_expert_up, w_expert_down, final_norm_w):
    bp, seq, d = x_prompt.shape
    bs = x_sample.shape[0]
    bt = bp + bs
    n = bt * seq
    layer = 0

    w_in_a = w_in[layer][:, _SRC_A0:].astype(BF16)
    b_in_a = b_in[layer, _SRC_A0:]
    b_in_m = b_in[layer, :PROJ_M_COLS] + jnp.pad(mlstm_gate_b[layer], (COL_GATE_M, PROJ_M_COLS - _SRC_A0))
    wr = jnp.concatenate([w_router_group[layer], w_router_expert[layer],
                          jnp.zeros((d, ROUTER_COLS - N_GROUPS - N_EXPERTS), F32)], axis=1)
    br = jnp.concatenate([b_router_group[layer], b_router_expert[layer],
                          jnp.zeros((ROUTER_COLS - N_GROUPS - N_EXPERTS,), F32)]).reshape(1, ROUTER_COLS)
    wr_hi = wr.astype(BF16)
    wr_lo = (wr - wr_hi.astype(F32)).astype(BF16)
    slopes = 2.0 ** (-8.0 * jnp.arange(1, A_HEADS + 1, dtype=F32) / A_HEADS)

    c_all = jnp.concatenate([c_prompt, c_sample, jnp.zeros((16 - bt, d), F32)], axis=0)
    mod3 = ada_modulation(c_all, w_ada[layer], b_ada[layer])[:bt].reshape(bt, 6, d)

    h = norm_modulate(x_prompt, x_sample, mod3, norm1_w[layer])
    h2d = h.reshape(n, d)
    proj_m = matmul_bias(h2d, w_in[layer], b_in_m, PROJ_M_COLS, "in_projection_mlstm")
    proj_a = matmul_bias(h2d, w_in_a, b_in_a, PROJ_A_COLS, "in_projection")
    proj_m3 = proj_m.reshape(bt, seq, PROJ_M_COLS)
    proj_a3 = proj_a.reshape(bt, seq, PROJ_A_COLS)

    nc = seq // M_CHUNK
    gates = proj_m3[:, :, COL_GATE_M:COL_GATE_M + GATE_M_W].reshape(bt, nc, M_CHUNK, 4, M_HEADS)
    gates = gates.transpose(0, 4, 3, 1, 2)
    hm = mlstm_branch(proj_m3, gates, conv_w[layer], conv_b[layer], mlstm_norm_w[layer])
    at = attention_branch(proj_a3, slopes)

    x1, h2, logits = merge_project(
        x_prompt.reshape(bp * seq, d), x_sample.reshape(bs * seq, d), hm.reshape(n, -1), at.reshape(n, -1), proj_a,
        mod3, p_a[layer].astype(BF16), p_b[layer].astype(BF16), w_out[layer].astype(BF16), norm2_w[layer],
        wr_hi, wr_lo, br, seq)

    tables, inv, route_w = route(logits, MOE_ROWS)
    ys = expert_ffn(h2, tables, w_expert_gate[layer].astype(BF16), w_expert_up[layer].astype(BF16),
                    w_expert_down[layer].astype(BF16))

    y_p = final_norm(x1, ys, inv, route_w, mod3, final_norm_w, 0, bp * seq, seq)
    y_s = final_norm(x1, ys, inv, route_w, mod3, final_norm_w, bp * seq, bs * seq, seq)
    return (y_p.reshape(bp, seq, d), y_s.reshape(bs, seq, d))
```

```python
import functools

import jax
import jax.numpy as jnp
from jax import lax
from jax.experimental import pallas as pl
from jax.experimental.pallas import tpu as pltpu

F32 = jnp.float32
BF16 = jnp.bfloat16

D_MODEL = 2048
RMS_EPS = 1e-6
M_HEADS = 8
M_QK_DIM = 128
M_V_DIM = 256
M_CHUNK = 128
A_GROUPS = ((128, 1), (512, 4), (2048, 16))
A_SLOTS = 4
A_HEADS = A_SLOTS * len(A_GROUPS)
A_HEAD_DIM = 128
N_GROUPS = 4
EXPERTS_PER_GROUP = 8
N_EXPERTS = N_GROUPS * EXPERTS_PER_GROUP
TOP_K = 2
D_EXPERT = 1024

M_QK_W = M_HEADS * M_QK_DIM
M_V_W = M_HEADS * M_V_DIM
A_W = A_HEADS * A_HEAD_DIM
PROJ_TN = 512
COL_Q_M = 0
COL_K_M = M_QK_W
COL_V_M = 2 * M_QK_W
COL_O_M = 2 * M_QK_W + M_V_W
COL_GATE_M = 2 * M_QK_W + 2 * M_V_W
GATE_M_W = 4 * M_HEADS
PROJ_M_COLS = COL_GATE_M + PROJ_TN
_SRC_A0 = COL_GATE_M + GATE_M_W
COL_Q_A = 0
COL_K_A = A_W
COL_V_A = 2 * A_W
COL_GATE_A = 3 * A_W
COL_GATE_B = 3 * A_W + D_MODEL
PROJ_A_COLS = 3 * A_W + 2 * D_MODEL

LANES = 128
MOE_ROWS = 256
ROUTER_COLS = 128
VMEM_LIMIT = 56 * 1024 * 1024
EXPERT_VMEM_LIMIT = 60 * 1024 * 1024


def _sigmoid(x):
    return 1.0 / (1.0 + jnp.exp(-x))


def _cparams(sem, vmem=VMEM_LIMIT):
    return pltpu.CompilerParams(dimension_semantics=sem, vmem_limit_bytes=vmem)


def _ada_kernel(c_ref, w_ref, b_ref, o_ref):
    c = c_ref[...]
    a = (c * _sigmoid(c)).astype(BF16)
    o_ref[...] = jnp.dot(a, w_ref[...].astype(BF16), preferred_element_type=F32) + b_ref[...]


def ada_modulation(c, w_ada, b_ada):
    rows, d = c.shape
    n = w_ada.shape[1]
    tn = 1024
    return pl.pallas_call(
        _ada_kernel,
        grid=(n // tn,),
        in_specs=[pl.BlockSpec((rows, d), lambda j: (0, 0)),
                  pl.BlockSpec((d, tn), lambda j: (0, j)),
                  pl.BlockSpec((1, tn), lambda j: (0, j))],
        out_specs=pl.BlockSpec((rows, tn), lambda j: (0, j)),
        out_shape=jax.ShapeDtypeStruct((rows, n), F32),
        compiler_params=_cparams(("parallel",)),
        name="ada_modulation",
    )(c, w_ada, b_ada.reshape(1, n))


def _norm_mod_kernel(nb0, xp_ref, xs_ref, mod_ref, w_ref, o_ref):
    b = pl.program_id(0)
    x = jnp.where(b < nb0, xp_ref[0], xs_ref[0])
    y = x * lax.rsqrt(jnp.mean(x * x, axis=-1, keepdims=True) + RMS_EPS) * w_ref[...]
    o_ref[0] = (y * (1.0 + mod_ref[0, 1:2, :]) + mod_ref[0, 0:1, :]).astype(o_ref.dtype)


def norm_modulate(xp, xs, mod3, norm_w, ts=512):
    nb0, s, d = xp.shape
    nb1 = xs.shape[0]
    return pl.pallas_call(
        functools.partial(_norm_mod_kernel, nb0),
        grid=(nb0 + nb1, s // ts),
        in_specs=[pl.BlockSpec((1, ts, d), lambda b, t: (jnp.minimum(b, nb0 - 1), jnp.where(b < nb0, t, s // ts - 1), 0)),
                  pl.BlockSpec((1, ts, d), lambda b, t: (jnp.maximum(b - nb0, 0), jnp.where(b < nb0, 0, t), 0)),
                  pl.BlockSpec((1, 6, d), lambda b, t: (b, 0, 0)),
                  pl.BlockSpec((1, d), lambda b, t: (0, 0))],
        out_specs=pl.BlockSpec((1, ts, d), lambda b, t: (b, t, 0)),
        out_shape=jax.ShapeDtypeStruct((nb0 + nb1, s, d), BF16),
        compiler_params=_cparams(("parallel", "parallel")),
        name="norm1_modulate",
    )(xp, xs, mod3, norm_w.reshape(1, d))


def _mm_bias_kernel(a_ref, w_ref, b_ref, o_ref):
    o_ref[...] = jnp.dot(a_ref[...], w_ref[...].astype(BF16), preferred_element_type=F32) + b_ref[...]


def matmul_bias(a, w, b, n, name, tm=2048, tn=PROJ_TN):
    m, k = a.shape
    return pl.pallas_call(
        _mm_bias_kernel,
        grid=(m // tm, n // tn),
        in_specs=[pl.BlockSpec((tm, k), lambda i, j: (i, 0)),
                  pl.BlockSpec((k, tn), lambda i, j: (0, j)),
                  pl.BlockSpec((1, tn), lambda i, j: (0, j))],
        out_specs=pl.BlockSpec((tm, tn), lambda i, j: (i, j)),
        out_shape=jax.ShapeDtypeStruct((m, n), F32),
        compiler_params=_cparams(("parallel", "parallel")),
        name=name,
    )(a, w, b.reshape(1, -1))


def _lane_scan(x, op, fill, reverse):
    n = x.shape[-1]
    axis = x.ndim - 1
    lane = lax.broadcasted_iota(jnp.int32, x.shape, axis)
    k = 1
    while k < n:
        if reverse:
            x = op(x, jnp.where(lane < n - k, pltpu.roll(x, n - k, axis), fill))
        else:
            x = op(x, jnp.where(lane >= k, pltpu.roll(x, k, axis), fill))
        k *= 2
    return x


def _conv_silu_chunk(x_ref, w_ref, b_ref, c, n_chunks):
    L = M_CHUNK
    s = n_chunks * L
    r0 = pl.multiple_of(c * L, L)
    x = x_ref[0, pl.ds(r0, L), :]
    prev_row = x_ref[0, pl.ds(jnp.maximum(r0 - 1, 0), 1), :]
    next_row = x_ref[0, pl.ds(jnp.minimum(r0 + L, s - 1), 1), :]
    prev_row = jnp.where(c > 0, prev_row, 0.0)
    next_row = jnp.where(c < n_chunks - 1, next_row, 0.0)
    rows = lax.broadcasted_iota(jnp.int32, x.shape, 0)
    x_prev = jnp.where(rows == 0, prev_row, pltpu.roll(x, 1, 0))
    x_next = jnp.where(rows == L - 1, next_row, pltpu.roll(x, L - 1, 0))
    y = b_ref[...] + x_prev * w_ref[0:1, :] + x * w_ref[1:2, :] + x_next * w_ref[2:3, :]
    return y * _sigmoid(y)


def _mlstm_kernel(q_ref, k_ref, v_ref, o_ref, g_ref, cwq_ref, cwk_ref, cbq_ref, cbk_ref, nw_ref, out_ref,
                  qs_ref, kt_ref, hf_ref, hr_ref, c_ref,
                  u_ref, negm_ref, ib_ref, wk_ref, decay_ref, m0_ref, m1_ref):
    L = M_CHUNK
    dk = M_QK_DIM
    dv = M_V_DIM
    hp, nc = kt_ref.shape[0], kt_ref.shape[1]

    def prep(c, carry):
        q = _conv_silu_chunk(q_ref, cwq_ref, cbq_ref, c, nc) * (M_QK_DIM ** -0.5)
        qs_ref[c] = q.astype(BF16)
        k = _conv_silu_chunk(k_ref, cwk_ref, cbk_ref, c, nc)
        for hh in range(hp):
            kt_ref[hh, c] = k[:, hh * dk:(hh + 1) * dk].T
        return carry

    lax.fori_loop(0, nc, prep, 0)

    for hh in range(hp):
        for d in range(2):
            rev = d == 1
            i_pre = g_ref[0, hh, 2 * d]
            f_pre = g_ref[0, hh, 2 * d + 1]
            log_f = -(jnp.maximum(-f_pre, 0.0) + jnp.log1p(jnp.exp(-jnp.abs(f_pre))))
            b = _lane_scan(log_f, jnp.add, 0.0, rev)
            a = jnp.broadcast_to(b[:, 0:1] if rev else b[:, L - 1:L], (nc, L))
            g = a - b + i_pre
            g_max = jnp.broadcast_to(jnp.max(g, axis=1, keepdims=True), (nc, L))
            m = jnp.zeros((1, L), F32)
            for c in (range(nc - 1, -1, -1) if rev else range(nc)):
                m0_ref[hh, d, c:c + 1, :] = m
                m = jnp.maximum(a[c:c + 1, :] + m, g_max[c:c + 1, :])
                m1_ref[hh, d, c:c + 1, :] = m
            m0 = m0_ref[hh, d]
            m1 = m1_ref[hh, d]
            ib = i_pre - b
            m_t = jnp.maximum(b + m0, b + _lane_scan(ib, jnp.maximum, -jnp.inf, rev))
            ib_ref[hh, d] = ib
            u_ref[hh, d] = b - m_t
            negm_ref[hh, d] = -m_t
            wk_ref[hh, d] = jnp.exp(g - m1)
            decay_ref[hh, d] = jnp.exp(a + m0 - m1)

    t_idx = lax.broadcasted_iota(jnp.int32, (L, L), 0)
    s_idx = lax.broadcasted_iota(jnp.int32, (L, L), 1)
    ones_ext = jnp.ones((L, LANES), BF16)

    def chunk_step(hh, d, c):
        r0 = pl.multiple_of(c * L, L)
        q = qs_ref[c, :, hh * dk:(hh + 1) * dk]
        kt = kt_ref[hh, c]
        v_ext = jnp.concatenate([v_ref[0, pl.ds(r0, L), hh * dv:(hh + 1) * dv].astype(BF16), ones_ext], axis=1)

        def row(ref):
            return ref[hh, d, pl.ds(c, 1), :]

        umat = jnp.broadcast_to(row(u_ref), (L, L)).T
        nmat = jnp.broadcast_to(row(negm_ref), (L, L)).T
        causal = (s_idx <= t_idx) if d == 0 else (s_idx >= t_idx)
        w_intra = jnp.where(causal, jnp.exp(umat + row(ib_ref)), 0.0)
        w_inter = jnp.exp(umat + row(m0_ref))
        s_qk = jnp.dot(q, kt.astype(BF16), preferred_element_type=F32)
        c_ext = c_ref[hh, d]
        lhs = jnp.concatenate([(w_intra * s_qk).astype(BF16), (q.astype(F32) * w_inter).astype(BF16)], axis=1)
        rhs = jnp.concatenate([v_ext, c_ext.astype(BF16)], axis=0)
        num = jnp.dot(lhs, rhs, preferred_element_type=F32)
        r = 1.0 / jnp.maximum(jnp.abs(num[:, dv:]), jnp.exp(nmat))
        h = num[:, :dv] * jnp.concatenate([r] * (dv // LANES), axis=1)

        upd = jnp.dot((kt * row(wk_ref)).astype(BF16), v_ext, preferred_element_type=F32)
        decay = jnp.broadcast_to(row(decay_ref), (dk, LANES))
        c_ref[hh, d] = jnp.concatenate([decay] * (c_ext.shape[1] // LANES), axis=1) * c_ext + upd
        return h

    def finish(hh, c, hs):
        rows = pl.ds(pl.multiple_of(c * L, L), L)
        cols = slice(hh * dv, (hh + 1) * dv)
        y = hs * lax.rsqrt(jnp.mean(hs * hs, axis=-1, keepdims=True) + RMS_EPS) * nw_ref[:, cols]
        out_ref[0, rows, cols] = (y * _sigmoid(o_ref[0, rows, cols])).astype(out_ref.dtype)

    def first_half(j, carry):
        cf, cr = j, nc - 1 - j
        for hh in range(hp):
            hf_ref[hh, pl.ds(pl.multiple_of(cf * L, L), L), :] = chunk_step(hh, 0, cf)
            hr_ref[hh, pl.ds(pl.multiple_of((cr - nc // 2) * L, L), L), :] = chunk_step(hh, 1, cr)
        return carry

    def second_half(j, carry):
        cf, cr = j, nc - 1 - j
        for hh in range(hp):
            finish(hh, cf, chunk_step(hh, 0, cf) + hr_ref[hh, pl.ds(pl.multiple_of((cf - nc // 2) * L, L), L), :])
            finish(hh, cr, hf_ref[hh, pl.ds(pl.multiple_of(cr * L, L), L), :] + chunk_step(hh, 1, cr))
        return carry

    c_ref[...] = jnp.zeros_like(c_ref)
    lax.fori_loop(0, nc // 2, first_half, 0, unroll=2)
    lax.fori_loop(nc // 2, nc, second_half, 0, unroll=2)


def mlstm_branch(proj3, gates, conv_w, conv_b, norm_w, hp=2):
    bsz, s, _ = proj3.shape
    L = M_CHUNK
    nc = s // L
    dk, dv = M_QK_DIM, M_V_DIM
    assert dk == L and nc % 2 == 0 and M_HEADS % hp == 0
    wq, wv = hp * dk, hp * dv
    k_off = M_HEADS * dk // wq
    return pl.pallas_call(
        _mlstm_kernel,
        grid=(bsz, M_HEADS // hp),
        in_specs=[pl.BlockSpec((1, s, wq), lambda b, h: (b, 0, COL_Q_M // wq + h)),
                  pl.BlockSpec((1, s, wq), lambda b, h: (b, 0, COL_K_M // wq + h)),
                  pl.BlockSpec((1, s, wv), lambda b, h: (b, 0, COL_V_M // wv + h)),
                  pl.BlockSpec((1, s, wv), lambda b, h: (b, 0, COL_O_M // wv + h)),
                  pl.BlockSpec((1, hp, 4, nc, L), lambda b, h: (b, h, 0, 0, 0)),
                  pl.BlockSpec((3, wq), lambda b, h: (0, h)),
                  pl.BlockSpec((3, wq), lambda b, h: (0, k_off + h)),
                  pl.BlockSpec((1, wq), lambda b, h: (0, h)),
                  pl.BlockSpec((1, wq), lambda b, h: (0, k_off + h)),
                  pl.BlockSpec((1, wv), lambda b, h: (0, h))],
        out_specs=pl.BlockSpec((1, s, wv), lambda b, h: (b, 0, h)),
        out_shape=jax.ShapeDtypeStruct((bsz, s, M_HEADS * dv), BF16),
        scratch_shapes=[pltpu.VMEM((nc, L, wq), BF16),
                        pltpu.VMEM((hp, nc, dk, L), F32),
                        pltpu.VMEM((hp, s // 2, dv), F32),
                        pltpu.VMEM((hp, s // 2, dv), F32),
                        pltpu.VMEM((hp, 2, dk, dv + LANES), F32),
                        *([pltpu.VMEM((hp, 2, nc, L), F32)] * 7)],
        compiler_params=_cparams(("parallel", "parallel")),
        name="mlstm_branch",
    )(proj3, proj3, proj3, proj3, gates, conv_w, conv_w, conv_b.reshape(1, -1), conv_b.reshape(1, -1),
      norm_w.reshape(1, -1))


def _attn_kernel(slopes_ref, q0, q1, q2, k0, k1, k2, v0, v1, v2, out_ref, og_ref, lse_ref, bias_ref):
    s_len = out_ref.shape[1]
    dh = A_HEAD_DIM
    T = 128
    slot = pl.program_id(1)
    qs, ks, vs = (q0, q1, q2), (k0, k1, k2), (v0, v1, v2)

    for g, (window, dil) in enumerate(A_GROUPS):
        side = window // (2 * dil)
        u_len = s_len // dil
        nqb = u_len // T
        kw = min(T + 2 * side, u_len)
        slope = slopes_ref[g * A_SLOTS + slot] * float(dil)
        q_ref, k_ref, v_ref = qs[g], ks[g], vs[g]

        offsets = (0, side, kw - T) if nqb > 1 else (0,)
        for case, off in enumerate(offsets):
            rel = jnp.abs(lax.broadcasted_iota(jnp.int32, (T, kw), 0) + off
                          - lax.broadcasted_iota(jnp.int32, (T, kw), 1))
            bias_ref[g, case, :, :kw] = jnp.where(rel <= side, -slope * rel.astype(F32), -1e30)

        def block(idx, carry, g=g, dil=dil, side=side, u_len=u_len, nqb=nqb, kw=kw,
                  q_ref=q_ref, k_ref=k_ref, v_ref=v_ref):
            r = idx // nqb
            j = idx % nqb
            u0 = j * T
            ku0 = jnp.clip(u0 - side, 0, u_len - kw)
            case = jnp.where(j == 0, 0, jnp.where(j == nqb - 1, 2, 1)) if nqb > 1 else 0
            q_rows = pl.ds(r + u0 * dil, T, stride=dil) if dil > 1 else pl.ds(pl.multiple_of(u0, T), T)
            k_rows = pl.ds(r + ku0 * dil, kw, stride=dil) if dil > 1 else pl.ds(pl.multiple_of(ku0, side), kw)
            q = (q_ref[0, q_rows, :] * (dh ** -0.5)).astype(BF16)
            kk = k_ref[0, k_rows, :].astype(BF16)
            vv = v_ref[0, k_rows, :].astype(BF16)
            s = lax.dot_general(q, kk, (((1,), (1,)), ((), ())), preferred_element_type=F32)
            s = s + bias_ref[g, case, :, :kw]
            m = jnp.max(s, axis=1, keepdims=True)
            p = jnp.exp(s - m)
            den = jnp.sum(p, axis=1, keepdims=True)
            o = jnp.dot(p.astype(BF16), vv, preferred_element_type=F32) * (1.0 / den)
            og_ref[g, q_rows, :] = o
            lse_ref[g, q_rows, :] = jnp.broadcast_to(m + jnp.log(den), (T, LANES))
            return carry

        lax.fori_loop(0, dil * nqb, block, 0, unroll=8)

    rows_per = 256

    def merge(i, carry):
        rows = pl.ds(pl.multiple_of(i * rows_per, rows_per), rows_per)
        l0, l1, l2 = lse_ref[0, rows, :], lse_ref[1, rows, :], lse_ref[2, rows, :]
        mx = jnp.maximum(jnp.maximum(l0, l1), l2)
        e0, e1, e2 = jnp.exp(l0 - mx), jnp.exp(l1 - mx), jnp.exp(l2 - mx)
        inv = 1.0 / (e0 + e1 + e2)
        o = (e0 * inv) * og_ref[0, rows, :] + (e1 * inv) * og_ref[1, rows, :] + (e2 * inv) * og_ref[2, rows, :]
        out_ref[0, rows, :] = o.astype(out_ref.dtype)
        return carry

    lax.fori_loop(0, s_len // rows_per, merge, 0)


def attention_branch(proj3, slopes):
    bsz, s, _ = proj3.shape
    dh = A_HEAD_DIM

    def col(base, g):
        return lambda b, t, sl: (b, 0, base // dh + g * A_SLOTS + t)

    grid_spec = pltpu.PrefetchScalarGridSpec(
        num_scalar_prefetch=1,
        grid=(bsz, A_SLOTS),
        in_specs=[pl.BlockSpec((1, s, dh), col(base, g))
                  for base in (COL_Q_A, COL_K_A, COL_V_A) for g in range(len(A_GROUPS))],
        out_specs=pl.BlockSpec((1, s, dh), lambda b, t, sl: (b, 0, t)),
        scratch_shapes=[pltpu.VMEM((3, s, dh), F32), pltpu.VMEM((3, s, LANES), F32),
                        pltpu.VMEM((len(A_GROUPS), 3, 128, 256), F32)],
    )
    return pl.pallas_call(
        _attn_kernel,
        grid_spec=grid_spec,
        out_shape=jax.ShapeDtypeStruct((bsz, s, A_SLOTS * dh), BF16),
        compiler_params=_cparams(("parallel", "parallel")),
        name="dilated_attention",
    )(slopes, *([proj3] * 9))


def _merge_kernel(n0, n_gate, xp_ref, xs_ref, hm_ref, at_ref, *refs):
    ga_refs, gb_refs = refs[:n_gate], refs[n_gate:2 * n_gate]
    (mod_ref, pa_ref, pb_ref, wo_ref, n2_ref, wrh_ref, wrl_ref, br_ref, x1_ref, h2_ref, lg_ref) = refs[2 * n_gate:]
    i = pl.program_id(0)
    x = jnp.where(i < n0, xp_ref[...], xs_ref[...])
    y_a = jnp.dot(hm_ref[...], pa_ref[...], preferred_element_type=F32)
    y_b = jnp.dot(at_ref[...], pb_ref[...], preferred_element_type=F32)
    gate_a = jnp.concatenate([r[...] for r in ga_refs], axis=1)
    gate_b = jnp.concatenate([r[...] for r in gb_refs], axis=1)
    mixin = _sigmoid(gate_a) * y_a + _sigmoid(gate_b) * y_b
    mix = jnp.dot(mixin.astype(BF16), wo_ref[...], preferred_element_type=F32)
    x1 = x + mod_ref[0, 2:3, :] * mix
    x1_ref[...] = x1
    y = x1 * lax.rsqrt(jnp.mean(x1 * x1, axis=-1, keepdims=True) + RMS_EPS) * n2_ref[...]
    h2 = y * (1.0 + mod_ref[0, 4:5, :]) + mod_ref[0, 3:4, :]
    h2_ref[...] = h2
    hi = h2.astype(BF16)
    lo = (h2 - hi.astype(F32)).astype(BF16)
    lg_ref[...] = (jnp.dot(hi, wrh_ref[...], preferred_element_type=F32)
                   + (jnp.dot(hi, wrl_ref[...], preferred_element_type=F32)
                      + jnp.dot(lo, wrh_ref[...], preferred_element_type=F32))
                   + br_ref[...])


def merge_project(xp2, xs2, hm, at, proj, mod3, p_a, p_b, w_out, norm2_w, wr_hi, wr_lo, br, seq, tm=256):
    n, d = hm.shape
    n0 = xp2.shape[0] // tm
    n1 = xs2.shape[0] // tm
    per_seq = seq // tm
    const = dict(pipeline_mode=pl.Buffered(1))
    gw = PROJ_TN
    n_gate = d // gw

    def gate_specs(col0):
        return [pl.BlockSpec((tm, gw), lambda i, t=t: (i, col0 // gw + t)) for t in range(n_gate)]

    return pl.pallas_call(
        functools.partial(_merge_kernel, n0, n_gate),
        grid=(n0 + n1,),
        in_specs=[pl.BlockSpec((tm, d), lambda i: (jnp.minimum(i, n0 - 1), 0)),
                  pl.BlockSpec((tm, d), lambda i: (jnp.maximum(i - n0, 0), 0)),
                  pl.BlockSpec((tm, d), lambda i: (i, 0)),
                  pl.BlockSpec((tm, at.shape[1]), lambda i: (i, 0)),
                  *gate_specs(COL_GATE_A), *gate_specs(COL_GATE_B),
                  pl.BlockSpec((1, 6, d), lambda i: (i // per_seq, 0, 0)),
                  pl.BlockSpec(p_a.shape, lambda i: (0, 0), **const),
                  pl.BlockSpec(p_b.shape, lambda i: (0, 0), **const),
                  pl.BlockSpec(w_out.shape, lambda i: (0, 0), **const),
                  pl.BlockSpec((1, d), lambda i: (0, 0)),
                  pl.BlockSpec(wr_hi.shape, lambda i: (0, 0), **const),
                  pl.BlockSpec(wr_lo.shape, lambda i: (0, 0), **const),
                  pl.BlockSpec((1, ROUTER_COLS), lambda i: (0, 0))],
        out_specs=[pl.BlockSpec((tm, d), lambda i: (i, 0)),
                   pl.BlockSpec((tm, d), lambda i: (i, 0)),
                   pl.BlockSpec((tm, ROUTER_COLS), lambda i: (i, 0))],
        out_shape=[jax.ShapeDtypeStruct((n, d), F32),
                   jax.ShapeDtypeStruct((n, d), F32),
                   jax.ShapeDtypeStruct((n, ROUTER_COLS), F32)],
        compiler_params=_cparams(("parallel",)),
        name="merge_project",
    )(xp2, xs2, hm, at, *([proj] * (2 * n_gate)), mod3, p_a, p_b, w_out, norm2_w.reshape(1, d), wr_hi, wr_lo, br)


def route(logits, tb):
    n = logits.shape[0]
    g_logits = logits[:, :N_GROUPS]
    e_logits = logits[:, N_GROUPS:N_GROUPS + N_EXPERTS].reshape(n, N_GROUPS, EXPERTS_PER_GROUP)
    g_idx = jnp.argmax(g_logits, axis=-1)
    g_w = jnp.take_along_axis(jax.nn.softmax(g_logits, axis=-1), g_idx[:, None], axis=-1)
    e_sel = jnp.take_along_axis(e_logits, g_idx[:, None, None], axis=1)[:, 0]
    top_v, top_i = lax.top_k(e_sel, TOP_K)
    weights = g_w * jax.nn.softmax(top_v, axis=-1)
    expert = (g_idx[:, None] * EXPERTS_PER_GROUP + top_i).astype(jnp.int32)
    a = n * TOP_K
    flat_e = expert.reshape(a)
    e_ids = jnp.arange(N_EXPERTS, dtype=jnp.int32)
    counts = jnp.sum(flat_e[:, None] == e_ids[None, :], axis=0, dtype=jnp.int32)
    padded = (counts + tb - 1) // tb * tb
    pad_end = jnp.cumsum(padded)
    filler_e = jnp.repeat(e_ids, tb)
    filler_j = jnp.tile(jnp.arange(tb, dtype=jnp.int32), N_EXPERTS)
    filler_key = jnp.where(filler_j < (padded - counts)[filler_e], 2 * filler_e + 1, 2 * N_EXPERTS + 1)
    keys = jnp.concatenate([2 * flat_e, filler_key])
    ids = jnp.arange(a, dtype=jnp.int32)
    filler0 = jnp.zeros((N_EXPERTS * tb,), jnp.int32)
    tok_src = jnp.concatenate([ids // TOP_K, filler0])
    out_src = jnp.concatenate([(ids % TOP_K) * n + ids // TOP_K, filler0])
    sorted_keys, row_tok, row_out = lax.sort((keys, tok_src, out_src), num_keys=1)
    r = a + N_EXPERTS * tb
    n_blocks = r // tb
    block_start = jnp.arange(n_blocks, dtype=jnp.int32) * tb
    block_expert = jnp.minimum(jnp.sum(block_start[:, None] >= pad_end[None, :], axis=1), N_EXPERTS - 1).astype(jnp.int32)
    n_used = (pad_end[-1] // tb).astype(jnp.int32).reshape(1)
    later = jnp.logical_and(e_ids[None, :] > e_ids[:, None], (counts > 0)[None, :])
    next_of = jnp.min(jnp.where(later, e_ids[None, :], N_EXPERTS), axis=1)
    next_expert = jnp.where(next_of < N_EXPERTS, next_of, -1)[block_expert].astype(jnp.int32)
    pos = jnp.arange(r, dtype=jnp.int32)
    inv_key = jnp.where(sorted_keys % 2 == 0, row_out, a + pos)
    _, inv = lax.sort((inv_key, pos), num_keys=1)
    return (block_expert, next_expert, n_used, row_tok), inv[:a], weights


ROW_DMA_UNROLL = 8


def _issue_row_gather(src_hbm, dst_buf, sem, index_of, n_rows):
    def body(j, carry):
        pltpu.make_async_copy(src_hbm.at[pl.ds(index_of(j), 1), :], dst_buf.at[pl.ds(j, 1), :], sem).start()
        return carry
    lax.fori_loop(0, n_rows, body, 0, unroll=ROW_DMA_UNROLL)


def _wait_row_gather(src_hbm, dst_buf, sem):
    pltpu.make_async_copy(src_hbm.at[pl.ds(0, dst_buf.shape[0]), :], dst_buf, sem).wait()


def _round_rows_to_bf16(src_ref, dst_ref, rows=256):
    def body(t, carry):
        r = pl.ds(pl.multiple_of(t * rows, rows), rows)
        dst_ref[r, :] = src_ref[r, :].astype(BF16)
        return carry
    lax.fori_loop(0, src_ref.shape[0] // rows, body, 0)


def _expert_kernel(be_ref, nxt_ref, nused_ref, rtok_ref, h2_hbm, wg_hbm, wu_hbm, wd_hbm, out_ref,
                   xbuf, stg_g, stg_u, stg_d, wg_b, wu_b, wd_b, gsem, wsem):
    tb = xbuf.shape[1]
    i = pl.program_id(0)
    nused = nused_ref[0]
    slot = i % 2
    e = be_ref[i]
    first_of_expert = jnp.logical_or(i == 0, e != be_ref[jnp.maximum(i - 1, 0)])

    def issue(blk, s):
        _issue_row_gather(h2_hbm, xbuf.at[s], gsem.at[s], lambda j: rtok_ref[blk * tb + j], tb)

    def weight_copies(expert):
        return [pltpu.make_async_copy(src.at[expert], dst, wsem.at[k])
                for k, (src, dst) in enumerate(((wg_hbm, stg_g), (wu_hbm, stg_u), (wd_hbm, stg_d)))]

    @pl.when(jnp.logical_and(i == 0, nused > 0))
    def _():
        issue(0, 0)

    @pl.when(i + 1 < nused)
    def _():
        issue(i + 1, 1 - slot)

    @pl.when(i < nused)
    def _():
        @pl.when(first_of_expert)
        def _():
            @pl.when(i == 0)
            def _():
                for c in weight_copies(e):
                    c.start()

            for c in weight_copies(e):
                c.wait()
            for stg, wb in ((stg_g, wg_b), (stg_u, wu_b), (stg_d, wd_b)):
                _round_rows_to_bf16(stg, wb)
            nxt = nxt_ref[i]

            @pl.when(nxt >= 0)
            def _():
                for c in weight_copies(nxt):
                    c.start()

        _wait_row_gather(h2_hbm, xbuf.at[slot], gsem.at[slot])
        x = xbuf[slot].astype(BF16)
        g = jnp.dot(x, wg_b[...], preferred_element_type=F32)
        u = jnp.dot(x, wu_b[...], preferred_element_type=F32)
        hdn = (g * _sigmoid(g) * u).astype(BF16)
        out_ref[...] = jnp.dot(hdn, wd_b[...], preferred_element_type=F32)

    @pl.when(i >= nused)
    def _():
        out_ref[...] = jnp.zeros_like(out_ref)


def expert_ffn(h2, tables, wg, wu, wd, tb=MOE_ROWS):
    block_expert, next_expert, n_used, row_tok = tables
    n, d = h2.shape
    nb = block_expert.shape[0]
    de = wg.shape[2]
    any_spec = pl.BlockSpec(memory_space=pl.ANY)
    grid_spec = pltpu.PrefetchScalarGridSpec(
        num_scalar_prefetch=4,
        grid=(nb,),
        in_specs=[any_spec, any_spec, any_spec, any_spec],
        out_specs=pl.BlockSpec((tb, d), lambda i, *_: (i, 0)),
        scratch_shapes=[pltpu.VMEM((2, tb, d), F32),
                        pltpu.VMEM((d, de), F32), pltpu.VMEM((d, de), F32), pltpu.VMEM((de, d), F32),
                        pltpu.VMEM((d, de), BF16), pltpu.VMEM((d, de), BF16), pltpu.VMEM((de, d), BF16),
                        pltpu.SemaphoreType.DMA((2,)), pltpu.SemaphoreType.DMA((3,))],
    )
    return pl.pallas_call(
        _expert_kernel,
        grid_spec=grid_spec,
        out_shape=jax.ShapeDtypeStruct((nb * tb, d), F32),
        compiler_params=_cparams(("arbitrary",), EXPERT_VMEM_LIMIT),
        name="expert_ffn",
    )(block_expert, next_expert, n_used, row_tok, h2, wg, wu, wd)


def _final_kernel(n_tok, tile0, inv_ref, x1_ref, rw_ref, mod_ref, w_ref, ys_hbm, o_ref, buf, sem):
    tm = x1_ref.shape[0]
    i = pl.program_id(0)
    slot = i % 2

    def issue(tile, s):
        base = (tile0 + tile) * tm
        for k in range(TOP_K):
            _issue_row_gather(ys_hbm, buf.at[s, k], sem.at[s], lambda j, k=k: inv_ref[k * n_tok + base + j], tm)

    @pl.when(i == 0)
    def _():
        issue(0, 0)

    @pl.when(i + 1 < pl.num_programs(0))
    def _():
        issue(i + 1, 1 - slot)

    for k in range(TOP_K):
        _wait_row_gather(ys_hbm, buf.at[slot, k], sem.at[slot])
    rw = rw_ref[...]
    moe = buf[slot, 0] * rw[:, 0:1]
    for k in range(1, TOP_K):
        moe = moe + buf[slot, k] * rw[:, k:k + 1]
    x = x1_ref[...] + mod_ref[0, 5:6, :] * moe
    o_ref[...] = x * lax.rsqrt(jnp.mean(x * x, axis=-1, keepdims=True) + RMS_EPS) * w_ref[...]


def final_norm(x1, ys, inv, route_w, mod3, final_w, row0, rows, seq, tm=256):
    n_tok, d = x1.shape
    off = row0 // tm
    per_seq = seq // tm
    grid_spec = pltpu.PrefetchScalarGridSpec(
        num_scalar_prefetch=1,
        grid=(rows // tm,),
        in_specs=[pl.BlockSpec((tm, d), lambda i, inv: (off + i, 0)),
                  pl.BlockSpec((tm, TOP_K), lambda i, inv: (off + i, 0)),
                  pl.BlockSpec((1, 6, d), lambda i, inv: ((off + i) // per_seq, 0, 0)),
                  pl.BlockSpec((1, d), lambda i, inv: (0, 0)),
                  pl.BlockSpec(memory_space=pl.ANY)],
        out_specs=pl.BlockSpec((tm, d), lambda i, inv: (i, 0)),
        scratch_shapes=[pltpu.VMEM((2, TOP_K, tm, d), F32), pltpu.SemaphoreType.DMA((2,))],
    )
    return pl.pallas_call(
        functools.partial(_final_kernel, n_tok, off),
        grid_spec=grid_spec,
        out_shape=jax.ShapeDtypeStruct((rows, d), F32),
        compiler_params=_cparams(("arbitrary",)),
        name="final_norm",
    )(inv, x1, route_w, mod3, final_w.reshape(1, d), ys)


def kernel(x_prompt, x_sample, c_prompt, c_sample, w_ada, b_ada, norm1_w, w_in, b_in, mlstm_gate_b, conv_w, conv_b, mlstm_norm_w, p_a, p_b, w_out, norm2_w, w_router_group, b_router_group, w_router_expert, b_router_expert, w_expert_gate, w_expert_up, w_expert_down, final_norm_w):
    bp, seq, d = x_prompt.shape
    bs = x_sample.shape[0]
    bt = bp + bs
    n = bt * seq
    layer = 0

    w_in_a = w_in[layer][:, _SRC_A0:]
    b_in_a = b_in[layer, _SRC_A0:]
    b_in_m = b_in[layer, :PROJ_M_COLS] + jnp.pad(mlstm_gate_b[layer], (COL_GATE_M, PROJ_M_COLS - _SRC_A0))
    wr = jnp.concatenate([w_router_group[layer], w_router_expert[layer],
                          jnp.zeros((d, ROUTER_COLS - N_GROUPS - N_EXPERTS), F32)], axis=1)
    br = jnp.concatenate([b_router_group[layer], b_router_expert[layer],
                          jnp.zeros((ROUTER_COLS - N_GROUPS - N_EXPERTS,), F32)]).reshape(1, ROUTER_COLS)
    wr_hi = wr.astype(BF16)
    wr_lo = (wr - wr_hi.astype(F32)).astype(BF16)
    slopes = 2.0 ** (-8.0 * jnp.arange(1, A_HEADS + 1, dtype=F32) / A_HEADS)

    c_all = jnp.concatenate([c_prompt, c_sample, jnp.zeros((16 - bt, d), F32)], axis=0)
    mod3 = ada_modulation(c_all, w_ada[layer], b_ada[layer])[:bt].reshape(bt, 6, d)

    h = norm_modulate(x_prompt, x_sample, mod3, norm1_w[layer])
    h2d = h.reshape(n, d)
    proj_m = matmul_bias(h2d, w_in[layer], b_in_m, PROJ_M_COLS, "in_projection_mlstm")
    proj_a = matmul_bias(h2d, w_in_a, b_in_a, PROJ_A_COLS, "in_projection")
    proj_m3 = proj_m.reshape(bt, seq, PROJ_M_COLS)
    proj_a3 = proj_a.reshape(bt, seq, PROJ_A_COLS)

    nc = seq // M_CHUNK
    gates = proj_m3[:, :, COL_GATE_M:COL_GATE_M + GATE_M_W].reshape(bt, nc, M_CHUNK, 4, M_HEADS)
    gates = gates.transpose(0, 4, 3, 1, 2)
    hm = mlstm_branch(proj_m3, gates, conv_w[layer], conv_b[layer], mlstm_norm_w[layer])
    at = attention_branch(proj_a3, slopes)

    x1, h2, logits = merge_project(
        x_prompt.reshape(bp * seq, d), x_sample.reshape(bs * seq, d), hm.reshape(n, -1), at.reshape(n, -1), proj_a,
        mod3, p_a[layer].astype(BF16), p_b[layer].astype(BF16), w_out[layer].astype(BF16), norm2_w[layer],
        wr_hi, wr_lo, br, seq)

    tables, inv, route_w = route(logits, MOE_ROWS)
    ys = expert_ffn(h2, tables, w_expert_gate[layer], w_expert_up[layer], w_expert_down[layer])

    y_p = final_norm(x1, ys, inv, route_w, mod3, final_norm_w, 0, bp * seq, seq)
    y_s = final_norm(x1, ys, inv, route_w, mod3, final_norm_w, bp * seq, bs * seq, seq)
    return (y_p.reshape(bp, seq, d), y_s.reshape(bs, seq, d))
```

```python
import functools

import jax
import jax.numpy as jnp
from jax import lax
from jax.experimental import pallas as pl
from jax.experimental.pallas import tpu as pltpu

F32 = jnp.float32
BF16 = jnp.bfloat16

D_MODEL = 2048
RMS_EPS = 1e-6
M_HEADS = 8
M_QK_DIM = 128
M_V_DIM = 256
M_CHUNK = 128
A_GROUPS = ((128, 1), (512, 4), (2048, 16))
A_SLOTS = 4
A_HEADS = A_SLOTS * len(A_GROUPS)
A_HEAD_DIM = 128
N_GROUPS = 4
EXPERTS_PER_GROUP = 8
N_EXPERTS = N_GROUPS * EXPERTS_PER_GROUP
TOP_K = 2
D_EXPERT = 1024

M_QK_W = M_HEADS * M_QK_DIM
M_V_W = M_HEADS * M_V_DIM
A_W = A_HEADS * A_HEAD_DIM
PROJ_TN = 512
COL_Q_M = 0
COL_K_M = M_QK_W
COL_V_M = 2 * M_QK_W
COL_O_M = 2 * M_QK_W + M_V_W
COL_GATE_M = 2 * M_QK_W + 2 * M_V_W
GATE_M_W = 4 * M_HEADS
PROJ_M_COLS = COL_GATE_M + PROJ_TN
_SRC_A0 = COL_GATE_M + GATE_M_W
COL_Q_A = 0
COL_K_A = A_W
COL_V_A = 2 * A_W
COL_GATE_A = 3 * A_W
COL_GATE_B = 3 * A_W + D_MODEL
PROJ_A_COLS = 3 * A_W + 2 * D_MODEL

LANES = 128
MOE_ROWS = 256
ROUTER_COLS = 128
VMEM_LIMIT = 56 * 1024 * 1024
EXPERT_VMEM_LIMIT = 60 * 1024 * 1024


def _sigmoid(x):
    return 1.0 / (1.0 + jnp.exp(-x))


def _cparams(sem, vmem=VMEM_LIMIT):
    return pltpu.CompilerParams(dimension_semantics=sem, vmem_limit_bytes=vmem)


def _ada_kernel(c_ref, w_ref, b_ref, o_ref):
    c = c_ref[...]
    a = (c * _sigmoid(c)).astype(BF16)
    o_ref[...] = jnp.dot(a, w_ref[...].astype(BF16), preferred_element_type=F32) + b_ref[...]


def ada_modulation(c, w_ada, b_ada):
    rows, d = c.shape
    n = w_ada.shape[1]
    tn = 1024
    return pl.pallas_call(
        _ada_kernel,
        grid=(n // tn,),
        in_specs=[pl.BlockSpec((rows, d), lambda j: (0, 0)),
                  pl.BlockSpec((d, tn), lambda j: (0, j)),
                  pl.BlockSpec((1, tn), lambda j: (0, j))],
        out_specs=pl.BlockSpec((rows, tn), lambda j: (0, j)),
        out_shape=jax.ShapeDtypeStruct((rows, n), F32),
        compiler_params=_cparams(("parallel",)),
        name="ada_modulation",
    )(c, w_ada, b_ada.reshape(1, n))


def _norm_mod_kernel(nb0, xp_ref, xs_ref, mod_ref, w_ref, o_ref):
    b = pl.program_id(0)
    x = jnp.where(b < nb0, xp_ref[0], xs_ref[0])
    y = x * lax.rsqrt(jnp.mean(x * x, axis=-1, keepdims=True) + RMS_EPS) * w_ref[...]
    o_ref[0] = (y * (1.0 + mod_ref[0, 1:2, :]) + mod_ref[0, 0:1, :]).astype(o_ref.dtype)


def norm_modulate(xp, xs, mod3, norm_w, ts=512):
    nb0, s, d = xp.shape
    nb1 = xs.shape[0]
    return pl.pallas_call(
        functools.partial(_norm_mod_kernel, nb0),
        grid=(nb0 + nb1, s // ts),
        in_specs=[pl.BlockSpec((1, ts, d), lambda b, t: (jnp.minimum(b, nb0 - 1), jnp.where(b < nb0, t, s // ts - 1), 0)),
                  pl.BlockSpec((1, ts, d), lambda b, t: (jnp.maximum(b - nb0, 0), jnp.where(b < nb0, 0, t), 0)),
                  pl.BlockSpec((1, 6, d), lambda b, t: (b, 0, 0)),
                  pl.BlockSpec((1, d), lambda b, t: (0, 0))],
        out_specs=pl.BlockSpec((1, ts, d), lambda b, t: (b, t, 0)),
        out_shape=jax.ShapeDtypeStruct((nb0 + nb1, s, d), BF16),
        compiler_params=_cparams(("parallel", "parallel")),
        name="norm1_modulate",
    )(xp, xs, mod3, norm_w.reshape(1, d))


def _mm_bias_kernel(a_ref, w_ref, b_ref, o_ref):
    o_ref[...] = jnp.dot(a_ref[...], w_ref[...].astype(BF16), preferred_element_type=F32) + b_ref[...]


def matmul_bias(a, w, b, n, name, layer=None, tm=2048, tn=PROJ_TN):
    m, k = a.shape
    if layer is None:
        w_spec = pl.BlockSpec((k, tn), lambda i, j: (0, j))
    else:
        w_spec = pl.BlockSpec((None, k, tn), lambda i, j: (layer, 0, j))
    return pl.pallas_call(
        _mm_bias_kernel,
        grid=(m // tm, n // tn),
        in_specs=[pl.BlockSpec((tm, k), lambda i, j: (i, 0)),
                  w_spec,
                  pl.BlockSpec((1, tn), lambda i, j: (0, j))],
        out_specs=pl.BlockSpec((tm, tn), lambda i, j: (i, j)),
        out_shape=jax.ShapeDtypeStruct((m, n), F32),
        compiler_params=_cparams(("parallel", "parallel")),
        name=name,
    )(a, w, b.reshape(1, -1))


def _lane_scan(x, op, fill, reverse):
    n = x.shape[-1]
    axis = x.ndim - 1
    lane = lax.broadcasted_iota(jnp.int32, x.shape, axis)
    k = 1
    while k < n:
        if reverse:
            x = op(x, jnp.where(lane < n - k, pltpu.roll(x, n - k, axis), fill))
        else:
            x = op(x, jnp.where(lane >= k, pltpu.roll(x, k, axis), fill))
        k *= 2
    return x


def _conv_silu_chunk(x_ref, w_ref, b_ref, c, n_chunks):
    L = M_CHUNK
    s = n_chunks * L
    r0 = pl.multiple_of(c * L, L)
    x = x_ref[0, pl.ds(r0, L), :]
    prev_row = x_ref[0, pl.ds(jnp.maximum(r0 - 1, 0), 1), :]
    next_row = x_ref[0, pl.ds(jnp.minimum(r0 + L, s - 1), 1), :]
    prev_row = jnp.where(c > 0, prev_row, 0.0)
    next_row = jnp.where(c < n_chunks - 1, next_row, 0.0)
    rows = lax.broadcasted_iota(jnp.int32, x.shape, 0)
    x_prev = jnp.where(rows == 0, prev_row, pltpu.roll(x, 1, 0))
    x_next = jnp.where(rows == L - 1, next_row, pltpu.roll(x, L - 1, 0))
    y = b_ref[...] + x_prev * w_ref[0:1, :] + x * w_ref[1:2, :] + x_next * w_ref[2:3, :]
    return y * _sigmoid(y)


def _mlstm_kernel(q_ref, k_ref, v_ref, o_ref, g_ref, cwq_ref, cwk_ref, cbq_ref, cbk_ref, nw_ref, out_ref,
                  qs_ref, kt_ref, hf_ref, hr_ref, c_ref,
                  u_ref, negm_ref, ib_ref, wk_ref, decay_ref, m0_ref, m1_ref):
    L = M_CHUNK
    dk = M_QK_DIM
    dv = M_V_DIM
    hp, nc = kt_ref.shape[0], kt_ref.shape[1]

    def prep(c, carry):
        q = _conv_silu_chunk(q_ref, cwq_ref, cbq_ref, c, nc) * (M_QK_DIM ** -0.5)
        qs_ref[c] = q.astype(BF16)
        k = _conv_silu_chunk(k_ref, cwk_ref, cbk_ref, c, nc)
        for hh in range(hp):
            kt_ref[hh, c] = k[:, hh * dk:(hh + 1) * dk].T
        return carry

    lax.fori_loop(0, nc, prep, 0)

    for hh in range(hp):
        for d in range(2):
            rev = d == 1
            i_pre = g_ref[0, hh, 2 * d]
            f_pre = g_ref[0, hh, 2 * d + 1]
            log_f = -(jnp.maximum(-f_pre, 0.0) + jnp.log1p(jnp.exp(-jnp.abs(f_pre))))
            b = _lane_scan(log_f, jnp.add, 0.0, rev)
            a = jnp.broadcast_to(b[:, 0:1] if rev else b[:, L - 1:L], (nc, L))
            g = a - b + i_pre
            g_max = jnp.broadcast_to(jnp.max(g, axis=1, keepdims=True), (nc, L))
            m = jnp.zeros((1, L), F32)
            for c in (range(nc - 1, -1, -1) if rev else range(nc)):
                m0_ref[hh, d, c:c + 1, :] = m
                m = jnp.maximum(a[c:c + 1, :] + m, g_max[c:c + 1, :])
                m1_ref[hh, d, c:c + 1, :] = m
            m0 = m0_ref[hh, d]
            m1 = m1_ref[hh, d]
            ib = i_pre - b
            m_t = jnp.maximum(b + m0, b + _lane_scan(ib, jnp.maximum, -jnp.inf, rev))
            ib_ref[hh, d] = ib
            u_ref[hh, d] = b - m_t
            negm_ref[hh, d] = -m_t
            wk_ref[hh, d] = jnp.exp(g - m1)
            decay_ref[hh, d] = jnp.exp(a + m0 - m1)

    t_idx = lax.broadcasted_iota(jnp.int32, (L, L), 0)
    s_idx = lax.broadcasted_iota(jnp.int32, (L, L), 1)
    ones_ext = jnp.ones((L, LANES), BF16)

    def chunk_step(hh, d, c):
        r0 = pl.multiple_of(c * L, L)
        q = qs_ref[c, :, hh * dk:(hh + 1) * dk]
        kt = kt_ref[hh, c]
        v_ext = jnp.concatenate([v_ref[0, pl.ds(r0, L), hh * dv:(hh + 1) * dv].astype(BF16), ones_ext], axis=1)

        def row(ref):
            return ref[hh, d, pl.ds(c, 1), :]

        umat = jnp.broadcast_to(row(u_ref), (L, L)).T
        nmat = jnp.broadcast_to(row(negm_ref), (L, L)).T
        causal = (s_idx <= t_idx) if d == 0 else (s_idx >= t_idx)
        w_intra = jnp.where(causal, jnp.exp(umat + row(ib_ref)), 0.0)
        w_inter = jnp.exp(umat + row(m0_ref))
        s_qk = jnp.dot(q, kt.astype(BF16), preferred_element_type=F32)
        c_ext = c_ref[hh, d]
        lhs = jnp.concatenate([(w_intra * s_qk).astype(BF16), (q.astype(F32) * w_inter).astype(BF16)], axis=1)
        rhs = jnp.concatenate([v_ext, c_ext.astype(BF16)], axis=0)
        num = jnp.dot(lhs, rhs, preferred_element_type=F32)
        r = 1.0 / jnp.maximum(jnp.abs(num[:, dv:]), jnp.exp(nmat))
        h = num[:, :dv] * jnp.concatenate([r] * (dv // LANES), axis=1)

        upd = jnp.dot((kt * row(wk_ref)).astype(BF16), v_ext, preferred_element_type=F32)
        decay = jnp.broadcast_to(row(decay_ref), (dk, LANES))
        c_ref[hh, d] = jnp.concatenate([decay] * (c_ext.shape[1] // LANES), axis=1) * c_ext + upd
        return h

    def finish(hh, c, hs):
        rows = pl.ds(pl.multiple_of(c * L, L), L)
        cols = slice(hh * dv, (hh + 1) * dv)
        y = hs * lax.rsqrt(jnp.mean(hs * hs, axis=-1, keepdims=True) + RMS_EPS) * nw_ref[:, cols]
        out_ref[0, rows, cols] = (y * _sigmoid(o_ref[0, rows, cols])).astype(out_ref.dtype)

    def first_half(j, carry):
        cf, cr = j, nc - 1 - j
        for hh in range(hp):
            hf_ref[hh, pl.ds(pl.multiple_of(cf * L, L), L), :] = chunk_step(hh, 0, cf)
            hr_ref[hh, pl.ds(pl.multiple_of((cr - nc // 2) * L, L), L), :] = chunk_step(hh, 1, cr)
        return carry

    def second_half(j, carry):
        cf, cr = j, nc - 1 - j
        for hh in range(hp):
            finish(hh, cf, chunk_step(hh, 0, cf) + hr_ref[hh, pl.ds(pl.multiple_of((cf - nc // 2) * L, L), L), :])
            finish(hh, cr, hf_ref[hh, pl.ds(pl.multiple_of(cr * L, L), L), :] + chunk_step(hh, 1, cr))
        return carry

    c_ref[...] = jnp.zeros_like(c_ref)
    lax.fori_loop(0, nc // 2, first_half, 0, unroll=2)
    lax.fori_loop(nc // 2, nc, second_half, 0, unroll=2)


def mlstm_branch(proj3, gates, conv_w, conv_b, norm_w, hp=2):
    bsz, s, _ = proj3.shape
    L = M_CHUNK
    nc = s // L
    dk, dv = M_QK_DIM, M_V_DIM
    assert dk == L and nc % 2 == 0 and M_HEADS % hp == 0
    wq, wv = hp * dk, hp * dv
    k_off = M_HEADS * dk // wq
    return pl.pallas_call(
        _mlstm_kernel,
        grid=(bsz, M_HEADS // hp),
        in_specs=[pl.BlockSpec((1, s, wq), lambda b, h: (b, 0, COL_Q_M // wq + h)),
                  pl.BlockSpec((1, s, wq), lambda b, h: (b, 0, COL_K_M // wq + h)),
                  pl.BlockSpec((1, s, wv), lambda b, h: (b, 0, COL_V_M // wv + h)),
                  pl.BlockSpec((1, s, wv), lambda b, h: (b, 0, COL_O_M // wv + h)),
                  pl.BlockSpec((1, hp, 4, nc, L), lambda b, h: (b, h, 0, 0, 0)),
                  pl.BlockSpec((3, wq), lambda b, h: (0, h)),
                  pl.BlockSpec((3, wq), lambda b, h: (0, k_off + h)),
                  pl.BlockSpec((1, wq), lambda b, h: (0, h)),
                  pl.BlockSpec((1, wq), lambda b, h: (0, k_off + h)),
                  pl.BlockSpec((1, wv), lambda b, h: (0, h))],
        out_specs=pl.BlockSpec((1, s, wv), lambda b, h: (b, 0, h)),
        out_shape=jax.ShapeDtypeStruct((bsz, s, M_HEADS * dv), BF16),
        scratch_shapes=[pltpu.VMEM((nc, L, wq), BF16),
                        pltpu.VMEM((hp, nc, dk, L), F32),
                        pltpu.VMEM((hp, s // 2, dv), F32),
                        pltpu.VMEM((hp, s // 2, dv), F32),
                        pltpu.VMEM((hp, 2, dk, dv + LANES), F32),
                        *([pltpu.VMEM((hp, 2, nc, L), F32)] * 7)],
        compiler_params=_cparams(("parallel", "parallel")),
        name="mlstm_branch",
    )(proj3, proj3, proj3, proj3, gates, conv_w, conv_w, conv_b.reshape(1, -1), conv_b.reshape(1, -1),
      norm_w.reshape(1, -1))


def _attn_kernel(slopes_ref, q0, q1, q2, k0, k1, k2, v0, v1, v2, out_ref, og_ref, lse_ref, bias_ref,
                 qd_ref, kd_ref, vd_ref, od_ref, ld_ref):
    s_len = out_ref.shape[1]
    dh = A_HEAD_DIM
    T = 128
    slot = pl.program_id(1)
    qs, ks, vs = (q0, q1, q2), (k0, k1, k2), (v0, v1, v2)

    for g, (window, dil) in enumerate(A_GROUPS):
        side = window // (2 * dil)
        u_len = s_len // dil
        nqb = u_len // T
        kw = min(T + 2 * side, u_len)
        slope = slopes_ref[g * A_SLOTS + slot] * float(dil)
        q_ref, k_ref, v_ref = qs[g], ks[g], vs[g]

        regroup = dil >= 8
        paired = regroup and nqb == 1 and dil % 2 == 0
        t_io = lax.broadcasted_iota(jnp.int32, (T, 2 * u_len if paired else kw), 0)
        k_io = lax.broadcasted_iota(jnp.int32, (T, 2 * u_len if paired else kw), 1)
        if paired:
            kw = 2 * u_len
            for case in range(2):
                rel = jnp.abs(t_io - (k_io - case * u_len))
                own = jnp.logical_and(k_io // u_len == case, rel <= side)
                bias_ref[g, case, :, :kw] = jnp.where(own, -slope * rel.astype(F32), -1e30)
        else:
            offsets = (0, side, kw - T) if nqb > 1 else (0,)
            for case, off in enumerate(offsets):
                rel = jnp.abs(t_io + off - k_io)
                bias_ref[g, case, :, :kw] = jnp.where(rel <= side, -slope * rel.astype(F32), -1e30)

        regroup = dil >= 8

        def residue_rows(r, dil=dil, u_len=u_len):
            return pl.ds(r, u_len, stride=dil), pl.ds(pl.multiple_of(r * u_len, u_len), u_len)

        if regroup:
            def gather_residue(r, carry, q_ref=q_ref, k_ref=k_ref, v_ref=v_ref):
                src, dst = residue_rows(r)
                qd_ref[dst, :] = (q_ref[0, src, :] * (dh ** -0.5)).astype(BF16)
                kd_ref[dst, :] = k_ref[0, src, :].astype(BF16)
                vd_ref[dst, :] = v_ref[0, src, :].astype(BF16)
                return carry

            lax.fori_loop(0, dil, gather_residue, 0, unroll=4)

        def block(idx, carry, g=g, dil=dil, side=side, u_len=u_len, nqb=nqb, kw=kw, regroup=regroup, paired=paired,
                  q_ref=q_ref, k_ref=k_ref, v_ref=v_ref):
            r = idx // nqb
            j = idx % nqb
            u0 = j * T
            ku0 = jnp.clip(u0 - side, 0, u_len - kw) if not paired else 0
            case = jnp.where(j == 0, 0, jnp.where(j == nqb - 1, 2, 1)) if nqb > 1 else 0
            if paired:
                case = r % 2
                q_rows = pl.ds(pl.multiple_of(r * u_len, T), T)
                k_rows = pl.ds(pl.multiple_of((r // 2) * kw, kw), kw)
                q, kk, vv = qd_ref[q_rows, :], kd_ref[k_rows, :], vd_ref[k_rows, :]
            elif regroup:
                q_rows = pl.ds(pl.multiple_of(r * u_len + u0, T), T)
                k_rows = pl.ds(pl.multiple_of(r * u_len + ku0, side), kw)
                q, kk, vv = qd_ref[q_rows, :], kd_ref[k_rows, :], vd_ref[k_rows, :]
            else:
                q_rows = pl.ds(r + u0 * dil, T, stride=dil) if dil > 1 else pl.ds(pl.multiple_of(u0, T), T)
                k_rows = pl.ds(r + ku0 * dil, kw, stride=dil) if dil > 1 else pl.ds(pl.multiple_of(ku0, side), kw)
                q = (q_ref[0, q_rows, :] * (dh ** -0.5)).astype(BF16)
                kk = k_ref[0, k_rows, :].astype(BF16)
                vv = v_ref[0, k_rows, :].astype(BF16)
            s = lax.dot_general(q, kk, (((1,), (1,)), ((), ())), preferred_element_type=F32)
            s = s + bias_ref[g, case, :, :kw]
            m = jnp.max(s, axis=1, keepdims=True)
            p = jnp.exp(s - m)
            den = jnp.sum(p, axis=1, keepdims=True)
            o = jnp.dot(p.astype(BF16), vv, preferred_element_type=F32) * (1.0 / den)
            lse = jnp.broadcast_to(m + jnp.log(den), (T, LANES))
            if regroup:
                od_ref[q_rows, :] = o
                ld_ref[q_rows, :] = lse
            else:
                og_ref[g, q_rows, :] = o
                lse_ref[g, q_rows, :] = lse
            return carry

        lax.fori_loop(0, dil * nqb, block, 0, unroll=8)

        if regroup:
            def scatter_residue(r, carry, g=g):
                dst, src = residue_rows(r)
                og_ref[g, dst, :] = od_ref[src, :]
                lse_ref[g, dst, :] = ld_ref[src, :]
                return carry

            lax.fori_loop(0, dil, scatter_residue, 0, unroll=4)

    rows_per = 256

    def merge(i, carry):
        rows = pl.ds(pl.multiple_of(i * rows_per, rows_per), rows_per)
        l0, l1, l2 = lse_ref[0, rows, :], lse_ref[1, rows, :], lse_ref[2, rows, :]
        mx = jnp.maximum(jnp.maximum(l0, l1), l2)
        e0, e1, e2 = jnp.exp(l0 - mx), jnp.exp(l1 - mx), jnp.exp(l2 - mx)
        inv = 1.0 / (e0 + e1 + e2)
        o = (e0 * inv) * og_ref[0, rows, :] + (e1 * inv) * og_ref[1, rows, :] + (e2 * inv) * og_ref[2, rows, :]
        out_ref[0, rows, :] = o.astype(out_ref.dtype)
        return carry

    lax.fori_loop(0, s_len // rows_per, merge, 0)


def attention_branch(proj3, slopes):
    bsz, s, _ = proj3.shape
    dh = A_HEAD_DIM

    def col(base, g):
        return lambda b, t, sl: (b, 0, base // dh + g * A_SLOTS + t)

    grid_spec = pltpu.PrefetchScalarGridSpec(
        num_scalar_prefetch=1,
        grid=(bsz, A_SLOTS),
        in_specs=[pl.BlockSpec((1, s, dh), col(base, g))
                  for base in (COL_Q_A, COL_K_A, COL_V_A) for g in range(len(A_GROUPS))],
        out_specs=pl.BlockSpec((1, s, dh), lambda b, t, sl: (b, 0, t)),
        scratch_shapes=[pltpu.VMEM((3, s, dh), F32), pltpu.VMEM((3, s, LANES), F32),
                        pltpu.VMEM((len(A_GROUPS), 3, 128, 256), F32),
                        pltpu.VMEM((s, dh), BF16), pltpu.VMEM((s, dh), BF16), pltpu.VMEM((s, dh), BF16),
                        pltpu.VMEM((s, dh), F32), pltpu.VMEM((s, LANES), F32)],
    )
    return pl.pallas_call(
        _attn_kernel,
        grid_spec=grid_spec,
        out_shape=jax.ShapeDtypeStruct((bsz, s, A_SLOTS * dh), BF16),
        compiler_params=_cparams(("parallel", "parallel")),
        name="dilated_attention",
    )(slopes, *([proj3] * 9))


def _merge_kernel(n0, n_gate, xp_ref, xs_ref, hm_ref, at_ref, *refs):
    ga_refs, gb_refs = refs[:n_gate], refs[n_gate:2 * n_gate]
    (mod_ref, pa_ref, pb_ref, wo_ref, n2_ref, wrh_ref, wrl_ref, br_ref, x1_ref, h2_ref, lg_ref) = refs[2 * n_gate:]
    i = pl.program_id(0)
    x = jnp.where(i < n0, xp_ref[...], xs_ref[...])
    y_a = jnp.dot(hm_ref[...], pa_ref[...], preferred_element_type=F32)
    y_b = jnp.dot(at_ref[...], pb_ref[...], preferred_element_type=F32)
    gate_a = jnp.concatenate([r[...] for r in ga_refs], axis=1)
    gate_b = jnp.concatenate([r[...] for r in gb_refs], axis=1)
    mixin = _sigmoid(gate_a) * y_a + _sigmoid(gate_b) * y_b
    mix = jnp.dot(mixin.astype(BF16), wo_ref[...], preferred_element_type=F32)
    x1 = x + mod_ref[0, 2:3, :] * mix
    x1_ref[...] = x1
    y = x1 * lax.rsqrt(jnp.mean(x1 * x1, axis=-1, keepdims=True) + RMS_EPS) * n2_ref[...]
    h2 = y * (1.0 + mod_ref[0, 4:5, :]) + mod_ref[0, 3:4, :]
    h2_ref[...] = h2
    hi = h2.astype(BF16)
    lo = (h2 - hi.astype(F32)).astype(BF16)
    lg_ref[...] = (jnp.dot(hi, wrh_ref[...], preferred_element_type=F32)
                   + (jnp.dot(hi, wrl_ref[...], preferred_element_type=F32)
                      + jnp.dot(lo, wrh_ref[...], preferred_element_type=F32))
                   + br_ref[...])


def merge_project(xp2, xs2, hm, at, proj, mod3, p_a, p_b, w_out, norm2_w, wr_hi, wr_lo, br, seq, tm=256):
    n, d = hm.shape
    n0 = xp2.shape[0] // tm
    n1 = xs2.shape[0] // tm
    per_seq = seq // tm
    const = dict(pipeline_mode=pl.Buffered(1))
    gw = PROJ_TN
    n_gate = d // gw

    def gate_specs(col0):
        return [pl.BlockSpec((tm, gw), lambda i, t=t: (i, col0 // gw + t)) for t in range(n_gate)]

    return pl.pallas_call(
        functools.partial(_merge_kernel, n0, n_gate),
        grid=(n0 + n1,),
        in_specs=[pl.BlockSpec((tm, d), lambda i: (jnp.minimum(i, n0 - 1), 0)),
                  pl.BlockSpec((tm, d), lambda i: (jnp.maximum(i - n0, 0), 0)),
                  pl.BlockSpec((tm, d), lambda i: (i, 0)),
                  pl.BlockSpec((tm, at.shape[1]), lambda i: (i, 0)),
                  *gate_specs(COL_GATE_A), *gate_specs(COL_GATE_B),
                  pl.BlockSpec((1, 6, d), lambda i: (i // per_seq, 0, 0)),
                  pl.BlockSpec(p_a.shape, lambda i: (0, 0), **const),
                  pl.BlockSpec(p_b.shape, lambda i: (0, 0), **const),
                  pl.BlockSpec(w_out.shape, lambda i: (0, 0), **const),
                  pl.BlockSpec((1, d), lambda i: (0, 0)),
                  pl.BlockSpec(wr_hi.shape, lambda i: (0, 0), **const),
                  pl.BlockSpec(wr_lo.shape, lambda i: (0, 0), **const),
                  pl.BlockSpec((1, ROUTER_COLS), lambda i: (0, 0))],
        out_specs=[pl.BlockSpec((tm, d), lambda i: (i, 0)),
                   pl.BlockSpec((tm, d), lambda i: (i, 0)),
                   pl.BlockSpec((tm, ROUTER_COLS), lambda i: (i, 0))],
        out_shape=[jax.ShapeDtypeStruct((n, d), F32),
                   jax.ShapeDtypeStruct((n, d), F32),
                   jax.ShapeDtypeStruct((n, ROUTER_COLS), F32)],
        compiler_params=_cparams(("parallel",)),
        name="merge_project",
    )(xp2, xs2, hm, at, *([proj] * (2 * n_gate)), mod3, p_a, p_b, w_out, norm2_w.reshape(1, d), wr_hi, wr_lo, br)


def route(logits, tb):
    n = logits.shape[0]
    g_logits = logits[:, :N_GROUPS]
    e_logits = logits[:, N_GROUPS:N_GROUPS + N_EXPERTS].reshape(n, N_GROUPS, EXPERTS_PER_GROUP)
    g_idx = jnp.argmax(g_logits, axis=-1)
    g_w = jnp.take_along_axis(jax.nn.softmax(g_logits, axis=-1), g_idx[:, None], axis=-1)
    e_sel = jnp.take_along_axis(e_logits, g_idx[:, None, None], axis=1)[:, 0]
    top_v, top_i = lax.top_k(e_sel, TOP_K)
    weights = g_w * jax.nn.softmax(top_v, axis=-1)
    expert = (g_idx[:, None] * EXPERTS_PER_GROUP + top_i).astype(jnp.int32)
    a = n * TOP_K
    flat_e = expert.reshape(a)
    e_ids = jnp.arange(N_EXPERTS, dtype=jnp.int32)
    counts = jnp.sum(flat_e[:, None] == e_ids[None, :], axis=0, dtype=jnp.int32)
    padded = (counts + tb - 1) // tb * tb
    pad_end = jnp.cumsum(padded)
    filler_e = jnp.repeat(e_ids, tb)
    filler_j = jnp.tile(jnp.arange(tb, dtype=jnp.int32), N_EXPERTS)
    filler_key = jnp.where(filler_j < (padded - counts)[filler_e], 2 * filler_e + 1, 2 * N_EXPERTS + 1)
    keys = jnp.concatenate([2 * flat_e, filler_key])
    ids = jnp.arange(a, dtype=jnp.int32)
    filler0 = jnp.zeros((N_EXPERTS * tb,), jnp.int32)
    tok_src = jnp.concatenate([ids // TOP_K, filler0])
    out_src = jnp.concatenate([(ids % TOP_K) * n + ids // TOP_K, filler0])
    sorted_keys, row_tok, row_out = lax.sort((keys, tok_src, out_src), num_keys=1)
    r = a + N_EXPERTS * tb
    n_blocks = r // tb
    block_start = jnp.arange(n_blocks, dtype=jnp.int32) * tb
    block_expert = jnp.minimum(jnp.sum(block_start[:, None] >= pad_end[None, :], axis=1), N_EXPERTS - 1).astype(jnp.int32)
    n_used = (pad_end[-1] // tb).astype(jnp.int32).reshape(1)
    later = jnp.logical_and(e_ids[None, :] > e_ids[:, None], (counts > 0)[None, :])
    next_of = jnp.min(jnp.where(later, e_ids[None, :], N_EXPERTS), axis=1)
    next_expert = jnp.where(next_of < N_EXPERTS, next_of, -1)[block_expert].astype(jnp.int32)
    pos = jnp.arange(r, dtype=jnp.int32)
    inv_key = jnp.where(sorted_keys % 2 == 0, row_out, a + pos)
    _, inv = lax.sort((inv_key, pos), num_keys=1)
    return (block_expert, next_expert, n_used, row_tok), inv[:a], weights


ROW_DMA_UNROLL = 8


def _issue_row_gather(src_hbm, dst_buf, sem, index_of, n_rows):
    def body(j, carry):
        pltpu.make_async_copy(src_hbm.at[pl.ds(index_of(j), 1), :], dst_buf.at[pl.ds(j, 1), :], sem).start()
        return carry
    lax.fori_loop(0, n_rows, body, 0, unroll=ROW_DMA_UNROLL)


def _wait_row_gather(src_hbm, dst_buf, sem):
    pltpu.make_async_copy(src_hbm.at[pl.ds(0, dst_buf.shape[0]), :], dst_buf, sem).wait()


def _round_rows_to_bf16(src_ref, dst_ref, rows=256):
    def body(t, carry):
        r = pl.ds(pl.multiple_of(t * rows, rows), rows)
        dst_ref[r, :] = src_ref[r, :].astype(BF16)
        return carry
    lax.fori_loop(0, src_ref.shape[0] // rows, body, 0)


def _expert_kernel(be_ref, nxt_ref, nused_ref, rtok_ref, h2_hbm, wg_hbm, wu_hbm, wd_hbm, out_ref,
                   xbuf, stg_g, stg_u, stg_d, wg_b, wu_b, wd_b, gsem, wsem):
    tb = xbuf.shape[1]
    i = pl.program_id(0)
    nused = nused_ref[0]
    slot = i % 2
    e = be_ref[i]
    first_of_expert = jnp.logical_or(i == 0, e != be_ref[jnp.maximum(i - 1, 0)])

    def issue(blk, s):
        _issue_row_gather(h2_hbm, xbuf.at[s], gsem.at[s], lambda j: rtok_ref[blk * tb + j], tb)

    def weight_copies(expert):
        return [pltpu.make_async_copy(src.at[expert], dst, wsem.at[k])
                for k, (src, dst) in enumerate(((wg_hbm, stg_g), (wu_hbm, stg_u), (wd_hbm, stg_d)))]

    @pl.when(jnp.logical_and(i == 0, nused > 0))
    def _():
        issue(0, 0)

    @pl.when(i + 1 < nused)
    def _():
        issue(i + 1, 1 - slot)

    @pl.when(i < nused)
    def _():
        @pl.when(first_of_expert)
        def _():
            @pl.when(i == 0)
            def _():
                for c in weight_copies(e):
                    c.start()

            for c in weight_copies(e):
                c.wait()
            for stg, wb in ((stg_g, wg_b), (stg_u, wu_b), (stg_d, wd_b)):
                _round_rows_to_bf16(stg, wb)
            nxt = nxt_ref[i]

            @pl.when(nxt >= 0)
            def _():
                for c in weight_copies(nxt):
                    c.start()

        _wait_row_gather(h2_hbm, xbuf.at[slot], gsem.at[slot])
        x = xbuf[slot].astype(BF16)
        g = jnp.dot(x, wg_b[...], preferred_element_type=F32)
        u = jnp.dot(x, wu_b[...], preferred_element_type=F32)
        hdn = (g * _sigmoid(g) * u).astype(BF16)
        out_ref[...] = jnp.dot(hdn, wd_b[...], preferred_element_type=F32)

    @pl.when(i >= nused)
    def _():
        out_ref[...] = jnp.zeros_like(out_ref)


def expert_ffn(h2, tables, wg, wu, wd, tb=MOE_ROWS):
    block_expert, next_expert, n_used, row_tok = tables
    n, d = h2.shape
    nb = block_expert.shape[0]
    de = wg.shape[2]
    any_spec = pl.BlockSpec(memory_space=pl.ANY)
    grid_spec = pltpu.PrefetchScalarGridSpec(
        num_scalar_prefetch=4,
        grid=(nb,),
        in_specs=[any_spec, any_spec, any_spec, any_spec],
        out_specs=pl.BlockSpec((tb, d), lambda i, *_: (i, 0)),
        scratch_shapes=[pltpu.VMEM((2, tb, d), F32),
                        pltpu.VMEM((d, de), F32), pltpu.VMEM((d, de), F32), pltpu.VMEM((de, d), F32),
                        pltpu.VMEM((d, de), BF16), pltpu.VMEM((d, de), BF16), pltpu.VMEM((de, d), BF16),
                        pltpu.SemaphoreType.DMA((2,)), pltpu.SemaphoreType.DMA((3,))],
    )
    return pl.pallas_call(
        _expert_kernel,
        grid_spec=grid_spec,
        out_shape=jax.ShapeDtypeStruct((nb * tb, d), F32),
        compiler_params=_cparams(("arbitrary",), EXPERT_VMEM_LIMIT),
        name="expert_ffn",
    )(block_expert, next_expert, n_used, row_tok, h2, wg, wu, wd)


def _final_kernel(n_tok, tile0, inv_ref, x1_ref, rw_ref, mod_ref, w_ref, ys_hbm, o_ref, buf, sem):
    tm = x1_ref.shape[0]
    i = pl.program_id(0)
    slot = i % 2

    def issue(tile, s):
        base = (tile0 + tile) * tm
        for k in range(TOP_K):
            _issue_row_gather(ys_hbm, buf.at[s, k], sem.at[s], lambda j, k=k: inv_ref[k * n_tok + base + j], tm)

    @pl.when(i == 0)
    def _():
        issue(0, 0)

    @pl.when(i + 1 < pl.num_programs(0))
    def _():
        issue(i + 1, 1 - slot)

    for k in range(TOP_K):
        _wait_row_gather(ys_hbm, buf.at[slot, k], sem.at[slot])
    rw = rw_ref[...]
    moe = buf[slot, 0] * rw[:, 0:1]
    for k in range(1, TOP_K):
        moe = moe + buf[slot, k] * rw[:, k:k + 1]
    x = x1_ref[...] + mod_ref[0, 5:6, :] * moe
    o_ref[...] = x * lax.rsqrt(jnp.mean(x * x, axis=-1, keepdims=True) + RMS_EPS) * w_ref[...]


def final_norm(x1, ys, inv, route_w, mod3, final_w, row0, rows, seq, tm=256):
    n_tok, d = x1.shape
    off = row0 // tm
    per_seq = seq // tm
    grid_spec = pltpu.PrefetchScalarGridSpec(
        num_scalar_prefetch=1,
        grid=(rows // tm,),
        in_specs=[pl.BlockSpec((tm, d), lambda i, inv: (off + i, 0)),
                  pl.BlockSpec((tm, TOP_K), lambda i, inv: (off + i, 0)),
                  pl.BlockSpec((1, 6, d), lambda i, inv: ((off + i) // per_seq, 0, 0)),
                  pl.BlockSpec((1, d), lambda i, inv: (0, 0)),
                  pl.BlockSpec(memory_space=pl.ANY)],
        out_specs=pl.BlockSpec((tm, d), lambda i, inv: (i, 0)),
        scratch_shapes=[pltpu.VMEM((2, TOP_K, tm, d), F32), pltpu.SemaphoreType.DMA((2,))],
    )
    return pl.pallas_call(
        functools.partial(_final_kernel, n_tok, off),
        grid_spec=grid_spec,
        out_shape=jax.ShapeDtypeStruct((rows, d), F32),
        compiler_params=_cparams(("arbitrary",)),
        name="final_norm",
    )(inv, x1, route_w, mod3, final_w.reshape(1, d), ys)


def kernel(x_prompt, x_sample, c_prompt, c_sample, w_ada, b_ada, norm1_w, w_in, b_in, mlstm_gate_b, conv_w, conv_b, mlstm_norm_w, p_a, p_b, w_out, norm2_w, w_router_group, b_router_group, w_router_expert, b_router_expert, w_expert_gate, w_expert_up, w_expert_down, final_norm_w):
    bp, seq, d = x_prompt.shape
    bs = x_sample.shape[0]
    bt = bp + bs
    n = bt * seq
    layer = 0

    w_in_a = w_in[layer][:, _SRC_A0:]
    b_in_a = b_in[layer, _SRC_A0:]
    b_in_m = b_in[layer, :PROJ_M_COLS] + jnp.pad(mlstm_gate_b[layer], (COL_GATE_M, PROJ_M_COLS - _SRC_A0))
    wr = jnp.concatenate([w_router_group[layer], w_router_expert[layer],
                          jnp.zeros((d, ROUTER_COLS - N_GROUPS - N_EXPERTS), F32)], axis=1)
    br = jnp.concatenate([b_router_group[layer], b_router_expert[layer],
                          jnp.zeros((ROUTER_COLS - N_GROUPS - N_EXPERTS,), F32)]).reshape(1, ROUTER_COLS)
    wr_hi = wr.astype(BF16)
    wr_lo = (wr - wr_hi.astype(F32)).astype(BF16)
    slopes = 2.0 ** (-8.0 * jnp.arange(1, A_HEADS + 1, dtype=F32) / A_HEADS)

    c_all = jnp.concatenate([c_prompt, c_sample, jnp.zeros((16 - bt, d), F32)], axis=0)
    mod3 = ada_modulation(c_all, w_ada[layer], b_ada[layer])[:bt].reshape(bt, 6, d)

    h = norm_modulate(x_prompt, x_sample, mod3, norm1_w[layer])
    h2d = h.reshape(n, d)
    proj_m = matmul_bias(h2d, w_in, b_in_m, PROJ_M_COLS, "in_projection_mlstm", layer=layer)
    proj_a = matmul_bias(h2d, w_in_a, b_in_a, PROJ_A_COLS, "in_projection")
    proj_m3 = proj_m.reshape(bt, seq, PROJ_M_COLS)
    proj_a3 = proj_a.reshape(bt, seq, PROJ_A_COLS)

    nc = seq // M_CHUNK
    gates = proj_m3[:, :, COL_GATE_M:COL_GATE_M + GATE_M_W].reshape(bt, nc, M_CHUNK, 4, M_HEADS)
    gates = gates.transpose(0, 4, 3, 1, 2)
    hm = mlstm_branch(proj_m3, gates, conv_w[layer], conv_b[layer], mlstm_norm_w[layer])
    at = attention_branch(proj_a3, slopes)

    x1, h2, logits = merge_project(
        x_prompt.reshape(bp * seq, d), x_sample.reshape(bs * seq, d), hm.reshape(n, -1), at.reshape(n, -1), proj_a,
        mod3, p_a[layer].astype(BF16), p_b[layer].astype(BF16), w_out[layer].astype(BF16), norm2_w[layer],
        wr_hi, wr_lo, br, seq)

    tables, inv, route_w = route(logits, MOE_ROWS)
    ys = expert_ffn(h2, tables, w_expert_gate[layer], w_expert_up[layer], w_expert_down[layer])

    y_p = final_norm(x1, ys, inv, route_w, mod3, final_norm_w, 0, bp * seq, seq)
    y_s = final_norm(x1, ys, inv, route_w, mod3, final_norm_w, bp * seq, bs * seq, seq)
    return (y_p.reshape(bp, seq, d), y_s.reshape(bs, seq, d))
```

```python
import functools

import jax
import jax.numpy as jnp
from jax import lax
from jax.experimental import pallas as pl
from jax.experimental.pallas import tpu as pltpu

F32 = jnp.float32
BF16 = jnp.bfloat16

D_MODEL = 2048
RMS_EPS = 1e-6
M_HEADS = 8
M_QK_DIM = 128
M_V_DIM = 256
M_CHUNK = 128
A_GROUPS = ((128, 1), (512, 4), (2048, 16))
A_SLOTS = 4
A_HEADS = A_SLOTS * len(A_GROUPS)
A_HEAD_DIM = 128
N_GROUPS = 4
EXPERTS_PER_GROUP = 8
N_EXPERTS = N_GROUPS * EXPERTS_PER_GROUP
TOP_K = 2
D_EXPERT = 1024

M_QK_W = M_HEADS * M_QK_DIM
M_V_W = M_HEADS * M_V_DIM
A_W = A_HEADS * A_HEAD_DIM
PROJ_TN = 512
COL_Q_M = 0
COL_K_M = M_QK_W
COL_V_M = 2 * M_QK_W
COL_O_M = 2 * M_QK_W + M_V_W
COL_GATE_M = 2 * M_QK_W + 2 * M_V_W
GATE_M_W = 4 * M_HEADS
PROJ_M_COLS = COL_GATE_M + PROJ_TN
_SRC_A0 = COL_GATE_M + GATE_M_W
COL_Q_A = 0
COL_K_A = A_W
COL_V_A = 2 * A_W
COL_GATE_A = 3 * A_W
COL_GATE_B = 3 * A_W + D_MODEL
PROJ_A_COLS = 3 * A_W + 2 * D_MODEL

LANES = 128
MOE_ROWS = 256
ROUTER_COLS = 128
VMEM_LIMIT = 56 * 1024 * 1024
EXPERT_VMEM_LIMIT = 60 * 1024 * 1024


def _sigmoid(x):
    return 1.0 / (1.0 + jnp.exp(-x))


def _cparams(sem, vmem=VMEM_LIMIT):
    return pltpu.CompilerParams(dimension_semantics=sem, vmem_limit_bytes=vmem)


def _ada_kernel(c_ref, w_ref, b_ref, o_ref):
    c = c_ref[...]
    a = (c * _sigmoid(c)).astype(BF16)
    o_ref[...] = jnp.dot(a, w_ref[...].astype(BF16), preferred_element_type=F32) + b_ref[...]


def ada_modulation(c, w_ada, b_ada):
    rows, d = c.shape
    n = w_ada.shape[1]
    tn = 1024
    return pl.pallas_call(
        _ada_kernel,
        grid=(n // tn,),
        in_specs=[pl.BlockSpec((rows, d), lambda j: (0, 0)),
                  pl.BlockSpec((d, tn), lambda j: (0, j)),
                  pl.BlockSpec((1, tn), lambda j: (0, j))],
        out_specs=pl.BlockSpec((rows, tn), lambda j: (0, j)),
        out_shape=jax.ShapeDtypeStruct((rows, n), F32),
        compiler_params=_cparams(("parallel",)),
        name="ada_modulation",
    )(c, w_ada, b_ada.reshape(1, n))


def _norm_mod_kernel(nb0, xp_ref, xs_ref, mod_ref, w_ref, o_ref):
    b = pl.program_id(0)
    x = jnp.where(b < nb0, xp_ref[0], xs_ref[0])
    y = x * lax.rsqrt(jnp.mean(x * x, axis=-1, keepdims=True) + RMS_EPS) * w_ref[...]
    o_ref[0] = (y * (1.0 + mod_ref[0, 1:2, :]) + mod_ref[0, 0:1, :]).astype(o_ref.dtype)


def norm_modulate(xp, xs, mod3, norm_w, ts=512):
    nb0, s, d = xp.shape
    nb1 = xs.shape[0]
    return pl.pallas_call(
        functools.partial(_norm_mod_kernel, nb0),
        grid=(nb0 + nb1, s // ts),
        in_specs=[pl.BlockSpec((1, ts, d), lambda b, t: (jnp.minimum(b, nb0 - 1), jnp.where(b < nb0, t, s // ts - 1), 0)),
                  pl.BlockSpec((1, ts, d), lambda b, t: (jnp.maximum(b - nb0, 0), jnp.where(b < nb0, 0, t), 0)),
                  pl.BlockSpec((1, 6, d), lambda b, t: (b, 0, 0)),
                  pl.BlockSpec((1, d), lambda b, t: (0, 0))],
        out_specs=pl.BlockSpec((1, ts, d), lambda b, t: (b, t, 0)),
        out_shape=jax.ShapeDtypeStruct((nb0 + nb1, s, d), BF16),
        compiler_params=_cparams(("parallel", "parallel")),
        name="norm1_modulate",
    )(xp, xs, mod3, norm_w.reshape(1, d))


def _mm_bias_kernel(a_ref, w_ref, b_ref, o_ref):
    o_ref[...] = jnp.dot(a_ref[...], w_ref[...].astype(BF16), preferred_element_type=F32) + b_ref[...]


def _slice_cols_kernel(col0, w_ref, o_ref):
    o_ref[...] = w_ref[:, col0:col0 + o_ref.shape[1]].astype(o_ref.dtype)


def slice_cols_bf16(w, layer, col0, n, tr=256):
    _, k, n_all = w.shape
    return pl.pallas_call(
        functools.partial(_slice_cols_kernel, col0),
        grid=(k // tr,),
        in_specs=[pl.BlockSpec((None, tr, n_all), lambda i: (layer, i, 0))],
        out_specs=pl.BlockSpec((tr, n), lambda i: (i, 0)),
        out_shape=jax.ShapeDtypeStruct((k, n), BF16),
        compiler_params=_cparams(("parallel",)),
        name="slice_cols_bf16",
    )(w)


def matmul_bias(a, w, b, n, name, layer=None, tm=2048, tn=PROJ_TN):
    m, k = a.shape
    if layer is None:
        w_spec = pl.BlockSpec((k, tn), lambda i, j: (0, j))
    else:
        w_spec = pl.BlockSpec((None, k, tn), lambda i, j: (layer, 0, j))
    return pl.pallas_call(
        _mm_bias_kernel,
        grid=(m // tm, n // tn),
        in_specs=[pl.BlockSpec((tm, k), lambda i, j: (i, 0)),
                  w_spec,
                  pl.BlockSpec((1, tn), lambda i, j: (0, j))],
        out_specs=pl.BlockSpec((tm, tn), lambda i, j: (i, j)),
        out_shape=jax.ShapeDtypeStruct((m, n), F32),
        compiler_params=_cparams(("parallel", "parallel")),
        name=name,
    )(a, w, b.reshape(1, -1))


def _lane_scan(x, op, fill, reverse):
    n = x.shape[-1]
    axis = x.ndim - 1
    lane = lax.broadcasted_iota(jnp.int32, x.shape, axis)
    k = 1
    while k < n:
        if reverse:
            x = op(x, jnp.where(lane < n - k, pltpu.roll(x, n - k, axis), fill))
        else:
            x = op(x, jnp.where(lane >= k, pltpu.roll(x, k, axis), fill))
        k *= 2
    return x


def _conv_silu_chunk(x_ref, w_ref, b_ref, c, n_chunks):
    L = M_CHUNK
    s = n_chunks * L
    r0 = pl.multiple_of(c * L, L)
    x = x_ref[0, pl.ds(r0, L), :]
    prev_row = x_ref[0, pl.ds(jnp.maximum(r0 - 1, 0), 1), :]
    next_row = x_ref[0, pl.ds(jnp.minimum(r0 + L, s - 1), 1), :]
    prev_row = jnp.where(c > 0, prev_row, 0.0)
    next_row = jnp.where(c < n_chunks - 1, next_row, 0.0)
    rows = lax.broadcasted_iota(jnp.int32, x.shape, 0)
    x_prev = jnp.where(rows == 0, prev_row, pltpu.roll(x, 1, 0))
    x_next = jnp.where(rows == L - 1, next_row, pltpu.roll(x, L - 1, 0))
    y = b_ref[...] + x_prev * w_ref[0:1, :] + x * w_ref[1:2, :] + x_next * w_ref[2:3, :]
    return y * _sigmoid(y)


def _mlstm_kernel(q_ref, k_ref, v_ref, o_ref, g_ref, cwq_ref, cwk_ref, cbq_ref, cbk_ref, nw_ref, out_ref,
                  qs_ref, kt_ref, hf_ref, hr_ref, c_ref,
                  u_ref, negm_ref, ib_ref, wk_ref, decay_ref, m0_ref, m1_ref):
    L = M_CHUNK
    dk = M_QK_DIM
    dv = M_V_DIM
    hp, nc = kt_ref.shape[0], kt_ref.shape[1]

    def prep(c, carry):
        q = _conv_silu_chunk(q_ref, cwq_ref, cbq_ref, c, nc) * (M_QK_DIM ** -0.5)
        qs_ref[c] = q.astype(BF16)
        k = _conv_silu_chunk(k_ref, cwk_ref, cbk_ref, c, nc)
        for hh in range(hp):
            kt_ref[hh, c] = k[:, hh * dk:(hh + 1) * dk].T
        return carry

    lax.fori_loop(0, nc, prep, 0)

    for hh in range(hp):
        for d in range(2):
            rev = d == 1
            i_pre = g_ref[0, hh, 2 * d]
            f_pre = g_ref[0, hh, 2 * d + 1]
            log_f = -(jnp.maximum(-f_pre, 0.0) + jnp.log1p(jnp.exp(-jnp.abs(f_pre))))
            b = _lane_scan(log_f, jnp.add, 0.0, rev)
            a = jnp.broadcast_to(b[:, 0:1] if rev else b[:, L - 1:L], (nc, L))
            g = a - b + i_pre
            g_max = jnp.broadcast_to(jnp.max(g, axis=1, keepdims=True), (nc, L))
            m = jnp.zeros((1, L), F32)
            for c in (range(nc - 1, -1, -1) if rev else range(nc)):
                m0_ref[hh, d, c:c + 1, :] = m
                m = jnp.maximum(a[c:c + 1, :] + m, g_max[c:c + 1, :])
                m1_ref[hh, d, c:c + 1, :] = m
            m0 = m0_ref[hh, d]
            m1 = m1_ref[hh, d]
            ib = i_pre - b
            m_t = jnp.maximum(b + m0, b + _lane_scan(ib, jnp.maximum, -jnp.inf, rev))
            ib_ref[hh, d] = ib
            u_ref[hh, d] = b - m_t
            negm_ref[hh, d] = -m_t
            wk_ref[hh, d] = jnp.exp(g - m1)
            decay_ref[hh, d] = jnp.exp(a + m0 - m1)

    t_idx = lax.broadcasted_iota(jnp.int32, (L, L), 0)
    s_idx = lax.broadcasted_iota(jnp.int32, (L, L), 1)
    ones_ext = jnp.ones((L, LANES), BF16)

    def chunk_step(hh, d, c):
        r0 = pl.multiple_of(c * L, L)
        q = qs_ref[c, :, hh * dk:(hh + 1) * dk]
        kt = kt_ref[hh, c]
        v_ext = jnp.concatenate([v_ref[0, pl.ds(r0, L), hh * dv:(hh + 1) * dv].astype(BF16), ones_ext], axis=1)

        def row(ref):
            return ref[hh, d, pl.ds(c, 1), :]

        umat = jnp.broadcast_to(row(u_ref), (L, L)).T
        nmat = jnp.broadcast_to(row(negm_ref), (L, L)).T
        causal = (s_idx <= t_idx) if d == 0 else (s_idx >= t_idx)
        w_intra = jnp.where(causal, jnp.exp(umat + row(ib_ref)), 0.0)
        w_inter = jnp.exp(umat + row(m0_ref))
        s_qk = jnp.dot(q, kt.astype(BF16), preferred_element_type=F32)
        c_ext = c_ref[hh, d]
        lhs = jnp.concatenate([(w_intra * s_qk).astype(BF16), (q.astype(F32) * w_inter).astype(BF16)], axis=1)
        rhs = jnp.concatenate([v_ext, c_ext.astype(BF16)], axis=0)
        num = jnp.dot(lhs, rhs, preferred_element_type=F32)
        r = 1.0 / jnp.maximum(jnp.abs(num[:, dv:]), jnp.exp(nmat))
        h = num[:, :dv] * jnp.concatenate([r] * (dv // LANES), axis=1)

        upd = jnp.dot((kt * row(wk_ref)).astype(BF16), v_ext, preferred_element_type=F32)
        decay = jnp.broadcast_to(row(decay_ref), (dk, LANES))
        c_ref[hh, d] = jnp.concatenate([decay] * (c_ext.shape[1] // LANES), axis=1) * c_ext + upd
        return h

    def finish(hh, c, hs):
        rows = pl.ds(pl.multiple_of(c * L, L), L)
        cols = slice(hh * dv, (hh + 1) * dv)
        y = hs * lax.rsqrt(jnp.mean(hs * hs, axis=-1, keepdims=True) + RMS_EPS) * nw_ref[:, cols]
        out_ref[0, rows, cols] = (y * _sigmoid(o_ref[0, rows, cols])).astype(out_ref.dtype)

    def first_half(j, carry):
        cf, cr = j, nc - 1 - j
        for hh in range(hp):
            hf_ref[hh, pl.ds(pl.multiple_of(cf * L, L), L), :] = chunk_step(hh, 0, cf)
            hr_ref[hh, pl.ds(pl.multiple_of((cr - nc // 2) * L, L), L), :] = chunk_step(hh, 1, cr)
        return carry

    def second_half(j, carry):
        cf, cr = j, nc - 1 - j
        for hh in range(hp):
            finish(hh, cf, chunk_step(hh, 0, cf) + hr_ref[hh, pl.ds(pl.multiple_of((cf - nc // 2) * L, L), L), :])
            finish(hh, cr, hf_ref[hh, pl.ds(pl.multiple_of(cr * L, L), L), :] + chunk_step(hh, 1, cr))
        return carry

    c_ref[...] = jnp.zeros_like(c_ref)
    lax.fori_loop(0, nc // 2, first_half, 0, unroll=2)
    lax.fori_loop(nc // 2, nc, second_half, 0, unroll=2)


def mlstm_branch(proj3, gates, conv_w, conv_b, norm_w, hp=2):
    bsz, s, _ = proj3.shape
    L = M_CHUNK
    nc = s // L
    dk, dv = M_QK_DIM, M_V_DIM
    assert dk == L and nc % 2 == 0 and M_HEADS % hp == 0
    wq, wv = hp * dk, hp * dv
    k_off = M_HEADS * dk // wq
    return pl.pallas_call(
        _mlstm_kernel,
        grid=(bsz, M_HEADS // hp),
        in_specs=[pl.BlockSpec((1, s, wq), lambda b, h: (b, 0, COL_Q_M // wq + h)),
                  pl.BlockSpec((1, s, wq), lambda b, h: (b, 0, COL_K_M // wq + h)),
                  pl.BlockSpec((1, s, wv), lambda b, h: (b, 0, COL_V_M // wv + h)),
                  pl.BlockSpec((1, s, wv), lambda b, h: (b, 0, COL_O_M // wv + h)),
                  pl.BlockSpec((1, hp, 4, nc, L), lambda b, h: (b, h, 0, 0, 0)),
                  pl.BlockSpec((3, wq), lambda b, h: (0, h)),
                  pl.BlockSpec((3, wq), lambda b, h: (0, k_off + h)),
                  pl.BlockSpec((1, wq), lambda b, h: (0, h)),
                  pl.BlockSpec((1, wq), lambda b, h: (0, k_off + h)),
                  pl.BlockSpec((1, wv), lambda b, h: (0, h))],
        out_specs=pl.BlockSpec((1, s, wv), lambda b, h: (b, 0, h)),
        out_shape=jax.ShapeDtypeStruct((bsz, s, M_HEADS * dv), BF16),
        scratch_shapes=[pltpu.VMEM((nc, L, wq), BF16),
                        pltpu.VMEM((hp, nc, dk, L), F32),
                        pltpu.VMEM((hp, s // 2, dv), F32),
                        pltpu.VMEM((hp, s // 2, dv), F32),
                        pltpu.VMEM((hp, 2, dk, dv + LANES), F32),
                        *([pltpu.VMEM((hp, 2, nc, L), F32)] * 7)],
        compiler_params=_cparams(("parallel", "parallel")),
        name="mlstm_branch",
    )(proj3, proj3, proj3, proj3, gates, conv_w, conv_w, conv_b.reshape(1, -1), conv_b.reshape(1, -1),
      norm_w.reshape(1, -1))


def _attn_kernel(slopes_ref, q0, q1, q2, k0, k1, k2, v0, v1, v2, out_ref, og_ref, lse_ref, bias_ref,
                 qd_ref, kd_ref, vd_ref, od_ref, ld_ref):
    s_len = out_ref.shape[1]
    dh = A_HEAD_DIM
    T = 128
    slot = pl.program_id(1)
    qs, ks, vs = (q0, q1, q2), (k0, k1, k2), (v0, v1, v2)

    for g, (window, dil) in enumerate(A_GROUPS):
        side = window // (2 * dil)
        u_len = s_len // dil
        nqb = u_len // T
        kw = min(T + 2 * side, u_len)
        slope = slopes_ref[g * A_SLOTS + slot] * float(dil)
        q_ref, k_ref, v_ref = qs[g], ks[g], vs[g]

        regroup = dil >= 8
        paired = regroup and nqb == 1 and dil % 2 == 0
        t_io = lax.broadcasted_iota(jnp.int32, (T, 2 * u_len if paired else kw), 0)
        k_io = lax.broadcasted_iota(jnp.int32, (T, 2 * u_len if paired else kw), 1)
        if paired:
            kw = 2 * u_len
            for case in range(2):
                rel = jnp.abs(t_io - (k_io - case * u_len))
                own = jnp.logical_and(k_io // u_len == case, rel <= side)
                bias_ref[g, case, :, :kw] = jnp.where(own, -slope * rel.astype(F32), -1e30)
        else:
            offsets = (0, side, kw - T) if nqb > 1 else (0,)
            for case, off in enumerate(offsets):
                rel = jnp.abs(t_io + off - k_io)
                bias_ref[g, case, :, :kw] = jnp.where(rel <= side, -slope * rel.astype(F32), -1e30)

        regroup = dil >= 8

        def residue_rows(r, dil=dil, u_len=u_len):
            return pl.ds(r, u_len, stride=dil), pl.ds(pl.multiple_of(r * u_len, u_len), u_len)

        if regroup:
            def gather_residue(r, carry, q_ref=q_ref, k_ref=k_ref, v_ref=v_ref):
                src, dst = residue_rows(r)
                qd_ref[dst, :] = (q_ref[0, src, :] * (dh ** -0.5)).astype(BF16)
                kd_ref[dst, :] = k_ref[0, src, :].astype(BF16)
                vd_ref[dst, :] = v_ref[0, src, :].astype(BF16)
                return carry

            lax.fori_loop(0, dil, gather_residue, 0, unroll=4)

        def block(idx, carry, g=g, dil=dil, side=side, u_len=u_len, nqb=nqb, kw=kw, regroup=regroup, paired=paired,
                  q_ref=q_ref, k_ref=k_ref, v_ref=v_ref):
            r = idx // nqb
            j = idx % nqb
            u0 = j * T
            ku0 = jnp.clip(u0 - side, 0, u_len - kw) if not paired else 0
            case = jnp.where(j == 0, 0, jnp.where(j == nqb - 1, 2, 1)) if nqb > 1 else 0
            if paired:
                case = r % 2
                q_rows = pl.ds(pl.multiple_of(r * u_len, T), T)
                k_rows = pl.ds(pl.multiple_of((r // 2) * kw, kw), kw)
                q, kk, vv = qd_ref[q_rows, :], kd_ref[k_rows, :], vd_ref[k_rows, :]
            elif regroup:
                q_rows = pl.ds(pl.multiple_of(r * u_len + u0, T), T)
                k_rows = pl.ds(pl.multiple_of(r * u_len + ku0, side), kw)
                q, kk, vv = qd_ref[q_rows, :], kd_ref[k_rows, :], vd_ref[k_rows, :]
            else:
                q_rows = pl.ds(r + u0 * dil, T, stride=dil) if dil > 1 else pl.ds(pl.multiple_of(u0, T), T)
                k_rows = pl.ds(r + ku0 * dil, kw, stride=dil) if dil > 1 else pl.ds(pl.multiple_of(ku0, side), kw)
                q = (q_ref[0, q_rows, :] * (dh ** -0.5)).astype(BF16)
                kk = k_ref[0, k_rows, :].astype(BF16)
                vv = v_ref[0, k_rows, :].astype(BF16)
            s = lax.dot_general(q, kk, (((1,), (1,)), ((), ())), preferred_element_type=F32)
            s = s + bias_ref[g, case, :, :kw]
            m = jnp.max(s, axis=1, keepdims=True)
            p = jnp.exp(s - m)
            den = jnp.sum(p, axis=1, keepdims=True)
            o = jnp.dot(p.astype(BF16), vv, preferred_element_type=F32) * (1.0 / den)
            lse = jnp.broadcast_to(m + jnp.log(den), (T, LANES))
            if regroup:
                od_ref[q_rows, :] = o
                ld_ref[q_rows, :] = lse
            else:
                og_ref[g, q_rows, :] = o
                lse_ref[g, q_rows, :] = lse
            return carry

        lax.fori_loop(0, dil * nqb, block, 0, unroll=8)

        if regroup:
            def scatter_residue(r, carry, g=g):
                dst, src = residue_rows(r)
                og_ref[g, dst, :] = od_ref[src, :]
                lse_ref[g, dst, :] = ld_ref[src, :]
                return carry

            lax.fori_loop(0, dil, scatter_residue, 0, unroll=4)

    rows_per = 256

    def merge(i, carry):
        rows = pl.ds(pl.multiple_of(i * rows_per, rows_per), rows_per)
        l0, l1, l2 = lse_ref[0, rows, :], lse_ref[1, rows, :], lse_ref[2, rows, :]
        mx = jnp.maximum(jnp.maximum(l0, l1), l2)
        e0, e1, e2 = jnp.exp(l0 - mx), jnp.exp(l1 - mx), jnp.exp(l2 - mx)
        inv = 1.0 / (e0 + e1 + e2)
        o = (e0 * inv) * og_ref[0, rows, :] + (e1 * inv) * og_ref[1, rows, :] + (e2 * inv) * og_ref[2, rows, :]
        out_ref[0, rows, :] = o.astype(out_ref.dtype)
        return carry

    lax.fori_loop(0, s_len // rows_per, merge, 0)


def attention_branch(proj3, slopes):
    bsz, s, _ = proj3.shape
    dh = A_HEAD_DIM

    def col(base, g):
        return lambda b, t, sl: (b, 0, base // dh + g * A_SLOTS + t)

    grid_spec = pltpu.PrefetchScalarGridSpec(
        num_scalar_prefetch=1,
        grid=(bsz, A_SLOTS),
        in_specs=[pl.BlockSpec((1, s, dh), col(base, g))
                  for base in (COL_Q_A, COL_K_A, COL_V_A) for g in range(len(A_GROUPS))],
        out_specs=pl.BlockSpec((1, s, dh), lambda b, t, sl: (b, 0, t)),
        scratch_shapes=[pltpu.VMEM((3, s, dh), F32), pltpu.VMEM((3, s, LANES), F32),
                        pltpu.VMEM((len(A_GROUPS), 3, 128, 256), F32),
                        pltpu.VMEM((s, dh), BF16), pltpu.VMEM((s, dh), BF16), pltpu.VMEM((s, dh), BF16),
                        pltpu.VMEM((s, dh), F32), pltpu.VMEM((s, LANES), F32)],
    )
    return pl.pallas_call(
        _attn_kernel,
        grid_spec=grid_spec,
        out_shape=jax.ShapeDtypeStruct((bsz, s, A_SLOTS * dh), BF16),
        compiler_params=_cparams(("parallel", "parallel")),
        name="dilated_attention",
    )(slopes, *([proj3] * 9))


def _merge_kernel(n0, n_gate, xp_ref, xs_ref, hm_ref, at_ref, *refs):
    ga_refs, gb_refs = refs[:n_gate], refs[n_gate:2 * n_gate]
    (mod_ref, pa_ref, pb_ref, wo_ref, n2_ref, wrh_ref, wrl_ref, br_ref, x1_ref, h2_ref, lg_ref) = refs[2 * n_gate:]
    i = pl.program_id(0)
    x = jnp.where(i < n0, xp_ref[...], xs_ref[...])
    y_a = jnp.dot(hm_ref[...], pa_ref[...], preferred_element_type=F32)
    y_b = jnp.dot(at_ref[...], pb_ref[...], preferred_element_type=F32)
    gate_a = jnp.concatenate([r[...] for r in ga_refs], axis=1)
    gate_b = jnp.concatenate([r[...] for r in gb_refs], axis=1)
    mixin = _sigmoid(gate_a) * y_a + _sigmoid(gate_b) * y_b
    mix = jnp.dot(mixin.astype(BF16), wo_ref[...], preferred_element_type=F32)
    x1 = x + mod_ref[0, 2:3, :] * mix
    x1_ref[...] = x1
    y = x1 * lax.rsqrt(jnp.mean(x1 * x1, axis=-1, keepdims=True) + RMS_EPS) * n2_ref[...]
    h2 = y * (1.0 + mod_ref[0, 4:5, :]) + mod_ref[0, 3:4, :]
    h2_ref[...] = h2
    hi = h2.astype(BF16)
    lo = (h2 - hi.astype(F32)).astype(BF16)
    lg_ref[...] = (jnp.dot(hi, wrh_ref[...], preferred_element_type=F32)
                   + (jnp.dot(hi, wrl_ref[...], preferred_element_type=F32)
                      + jnp.dot(lo, wrh_ref[...], preferred_element_type=F32))
                   + br_ref[...])


def merge_project(xp2, xs2, hm, at, proj, mod3, p_a, p_b, w_out, norm2_w, wr_hi, wr_lo, br, seq, tm=256):
    n, d = hm.shape
    n0 = xp2.shape[0] // tm
    n1 = xs2.shape[0] // tm
    per_seq = seq // tm
    const = dict(pipeline_mode=pl.Buffered(1))
    gw = PROJ_TN
    n_gate = d // gw

    def gate_specs(col0):
        return [pl.BlockSpec((tm, gw), lambda i, t=t: (i, col0 // gw + t)) for t in range(n_gate)]

    return pl.pallas_call(
        functools.partial(_merge_kernel, n0, n_gate),
        grid=(n0 + n1,),
        in_specs=[pl.BlockSpec((tm, d), lambda i: (jnp.minimum(i, n0 - 1), 0)),
                  pl.BlockSpec((tm, d), lambda i: (jnp.maximum(i - n0, 0), 0)),
                  pl.BlockSpec((tm, d), lambda i: (i, 0)),
                  pl.BlockSpec((tm, at.shape[1]), lambda i: (i, 0)),
                  *gate_specs(COL_GATE_A), *gate_specs(COL_GATE_B),
                  pl.BlockSpec((1, 6, d), lambda i: (i // per_seq, 0, 0)),
                  pl.BlockSpec(p_a.shape, lambda i: (0, 0), **const),
                  pl.BlockSpec(p_b.shape, lambda i: (0, 0), **const),
                  pl.BlockSpec(w_out.shape, lambda i: (0, 0), **const),
                  pl.BlockSpec((1, d), lambda i: (0, 0)),
                  pl.BlockSpec(wr_hi.shape, lambda i: (0, 0), **const),
                  pl.BlockSpec(wr_lo.shape, lambda i: (0, 0), **const),
                  pl.BlockSpec((1, ROUTER_COLS), lambda i: (0, 0))],
        out_specs=[pl.BlockSpec((tm, d), lambda i: (i, 0)),
                   pl.BlockSpec((tm, d), lambda i: (i, 0)),
                   pl.BlockSpec((tm, ROUTER_COLS), lambda i: (i, 0))],
        out_shape=[jax.ShapeDtypeStruct((n, d), F32),
                   jax.ShapeDtypeStruct((n, d), F32),
                   jax.ShapeDtypeStruct((n, ROUTER_COLS), F32)],
        compiler_params=_cparams(("parallel",)),
        name="merge_project",
    )(xp2, xs2, hm, at, *([proj] * (2 * n_gate)), mod3, p_a, p_b, w_out, norm2_w.reshape(1, d), wr_hi, wr_lo, br)


def route(logits, tb):
    n = logits.shape[0]
    g_logits = logits[:, :N_GROUPS]
    e_logits = logits[:, N_GROUPS:N_GROUPS + N_EXPERTS].reshape(n, N_GROUPS, EXPERTS_PER_GROUP)
    g_idx = jnp.argmax(g_logits, axis=-1)
    g_w = jnp.take_along_axis(jax.nn.softmax(g_logits, axis=-1), g_idx[:, None], axis=-1)
    e_sel = jnp.take_along_axis(e_logits, g_idx[:, None, None], axis=1)[:, 0]
    top_v, top_i = lax.top_k(e_sel, TOP_K)
    weights = g_w * jax.nn.softmax(top_v, axis=-1)
    expert = (g_idx[:, None] * EXPERTS_PER_GROUP + top_i).astype(jnp.int32)
    a = n * TOP_K
    flat_e = expert.reshape(a)
    e_ids = jnp.arange(N_EXPERTS, dtype=jnp.int32)
    counts = jnp.sum(flat_e[:, None] == e_ids[None, :], axis=0, dtype=jnp.int32)
    padded = (counts + tb - 1) // tb * tb
    pad_end = jnp.cumsum(padded)
    filler_e = jnp.repeat(e_ids, tb)
    filler_j = jnp.tile(jnp.arange(tb, dtype=jnp.int32), N_EXPERTS)
    filler_key = jnp.where(filler_j < (padded - counts)[filler_e], 2 * filler_e + 1, 2 * N_EXPERTS + 1)
    keys = jnp.concatenate([2 * flat_e, filler_key])
    ids = jnp.arange(a, dtype=jnp.int32)
    filler0 = jnp.zeros((N_EXPERTS * tb,), jnp.int32)
    tok_src = jnp.concatenate([ids // TOP_K, filler0])
    out_src = jnp.concatenate([(ids % TOP_K) * n + ids // TOP_K, filler0])
    sorted_keys, row_tok, row_out = lax.sort((keys, tok_src, out_src), num_keys=1)
    r = a + N_EXPERTS * tb
    n_blocks = r // tb
    block_start = jnp.arange(n_blocks, dtype=jnp.int32) * tb
    block_expert = jnp.minimum(jnp.sum(block_start[:, None] >= pad_end[None, :], axis=1), N_EXPERTS - 1).astype(jnp.int32)
    n_used = (pad_end[-1] // tb).astype(jnp.int32).reshape(1)
    later = jnp.logical_and(e_ids[None, :] > e_ids[:, None], (counts > 0)[None, :])
    next_of = jnp.min(jnp.where(later, e_ids[None, :], N_EXPERTS), axis=1)
    next_expert = jnp.where(next_of < N_EXPERTS, next_of, -1)[block_expert].astype(jnp.int32)
    pos = jnp.arange(r, dtype=jnp.int32)
    inv_key = jnp.where(sorted_keys % 2 == 0, row_out, a + pos)
    _, inv = lax.sort((inv_key, pos), num_keys=1)
    return (block_expert, next_expert, n_used, row_tok), inv[:a], weights


ROW_DMA_UNROLL = 16


def _issue_row_gather(src_hbm, dst_buf, sem, index_of, n_rows):
    def body(j, carry):
        pltpu.make_async_copy(src_hbm.at[pl.ds(index_of(j), 1), :], dst_buf.at[pl.ds(j, 1), :], sem).start()
        return carry
    lax.fori_loop(0, n_rows, body, 0, unroll=ROW_DMA_UNROLL)


def _wait_row_gather(src_hbm, dst_buf, sem):
    pltpu.make_async_copy(src_hbm.at[pl.ds(0, dst_buf.shape[0]), :], dst_buf, sem).wait()


def _round_rows_to_bf16(src_ref, dst_ref, rows=256):
    def body(t, carry):
        r = pl.ds(pl.multiple_of(t * rows, rows), rows)
        dst_ref[r, :] = src_ref[r, :].astype(BF16)
        return carry
    lax.fori_loop(0, src_ref.shape[0] // rows, body, 0)


def _expert_kernel(be_ref, nxt_ref, nused_ref, rtok_ref, h2_hbm, wg_hbm, wu_hbm, wd_hbm, out_ref,
                   xbuf, stg_g, stg_u, stg_d, wg_b, wu_b, wd_b, gsem, wsem):
    tb = xbuf.shape[1]
    i = pl.program_id(0)
    nused = nused_ref[0]
    slot = i % 2
    e = be_ref[i]
    first_of_expert = jnp.logical_or(i == 0, e != be_ref[jnp.maximum(i - 1, 0)])

    def issue(blk, s):
        _issue_row_gather(h2_hbm, xbuf.at[s], gsem.at[s], lambda j: rtok_ref[blk * tb + j], tb)

    def weight_copies(expert):
        return [pltpu.make_async_copy(src.at[expert], dst, wsem.at[k])
                for k, (src, dst) in enumerate(((wg_hbm, stg_g), (wu_hbm, stg_u), (wd_hbm, stg_d)))]

    @pl.when(jnp.logical_and(i == 0, nused > 0))
    def _():
        issue(0, 0)

    @pl.when(i + 1 < nused)
    def _():
        issue(i + 1, 1 - slot)

    @pl.when(i < nused)
    def _():
        @pl.when(first_of_expert)
        def _():
            @pl.when(i == 0)
            def _():
                for c in weight_copies(e):
                    c.start()

            for c in weight_copies(e):
                c.wait()
            for stg, wb in ((stg_g, wg_b), (stg_u, wu_b), (stg_d, wd_b)):
                _round_rows_to_bf16(stg, wb)
            nxt = nxt_ref[i]

            @pl.when(nxt >= 0)
            def _():
                for c in weight_copies(nxt):
                    c.start()

        _wait_row_gather(h2_hbm, xbuf.at[slot], gsem.at[slot])
        x = xbuf[slot].astype(BF16)
        g = jnp.dot(x, wg_b[...], preferred_element_type=F32)
        u = jnp.dot(x, wu_b[...], preferred_element_type=F32)
        hdn = (g * _sigmoid(g) * u).astype(BF16)
        out_ref[...] = jnp.dot(hdn, wd_b[...], preferred_element_type=F32)

    @pl.when(i >= nused)
    def _():
        out_ref[...] = jnp.zeros_like(out_ref)


def expert_ffn(h2, tables, wg, wu, wd, tb=MOE_ROWS):
    block_expert, next_expert, n_used, row_tok = tables
    n, d = h2.shape
    nb = block_expert.shape[0]
    de = wg.shape[2]
    any_spec = pl.BlockSpec(memory_space=pl.ANY)
    grid_spec = pltpu.PrefetchScalarGridSpec(
        num_scalar_prefetch=4,
        grid=(nb,),
        in_specs=[any_spec, any_spec, any_spec, any_spec],
        out_specs=pl.BlockSpec((tb, d), lambda i, *_: (i, 0)),
        scratch_shapes=[pltpu.VMEM((2, tb, d), F32),
                        pltpu.VMEM((d, de), F32), pltpu.VMEM((d, de), F32), pltpu.VMEM((de, d), F32),
                        pltpu.VMEM((d, de), BF16), pltpu.VMEM((d, de), BF16), pltpu.VMEM((de, d), BF16),
                        pltpu.SemaphoreType.DMA((2,)), pltpu.SemaphoreType.DMA((3,))],
    )
    return pl.pallas_call(
        _expert_kernel,
        grid_spec=grid_spec,
        out_shape=jax.ShapeDtypeStruct((nb * tb, d), F32),
        compiler_params=_cparams(("arbitrary",), EXPERT_VMEM_LIMIT),
        name="expert_ffn",
    )(block_expert, next_expert, n_used, row_tok, h2, wg, wu, wd)


def _final_kernel(n_tok, tile0, inv_ref, x1_ref, rw_ref, mod_ref, w_ref, ys_hbm, o_ref, buf, sem):
    tm = x1_ref.shape[0]
    i = pl.program_id(0)
    slot = i % 2

    def issue(tile, s):
        base = (tile0 + tile) * tm
        for k in range(TOP_K):
            _issue_row_gather(ys_hbm, buf.at[s, k], sem.at[s], lambda j, k=k: inv_ref[k * n_tok + base + j], tm)

    @pl.when(i == 0)
    def _():
        issue(0, 0)

    @pl.when(i + 1 < pl.num_programs(0))
    def _():
        issue(i + 1, 1 - slot)

    for k in range(TOP_K):
        _wait_row_gather(ys_hbm, buf.at[slot, k], sem.at[slot])
    rw = rw_ref[...]
    moe = buf[slot, 0] * rw[:, 0:1]
    for k in range(1, TOP_K):
        moe = moe + buf[slot, k] * rw[:, k:k + 1]
    x = x1_ref[...] + mod_ref[0, 5:6, :] * moe
    o_ref[...] = x * lax.rsqrt(jnp.mean(x * x, axis=-1, keepdims=True) + RMS_EPS) * w_ref[...]


def final_norm(x1, ys, inv, route_w, mod3, final_w, row0, rows, seq, tm=256):
    n_tok, d = x1.shape
    off = row0 // tm
    per_seq = seq // tm
    grid_spec = pltpu.PrefetchScalarGridSpec(
        num_scalar_prefetch=1,
        grid=(rows // tm,),
        in_specs=[pl.BlockSpec((tm, d), lambda i, inv: (off + i, 0)),
                  pl.BlockSpec((tm, TOP_K), lambda i, inv: (off + i, 0)),
                  pl.BlockSpec((1, 6, d), lambda i, inv: ((off + i) // per_seq, 0, 0)),
                  pl.BlockSpec((1, d), lambda i, inv: (0, 0)),
                  pl.BlockSpec(memory_space=pl.ANY)],
        out_specs=pl.BlockSpec((tm, d), lambda i, inv: (i, 0)),
        scratch_shapes=[pltpu.VMEM((2, TOP_K, tm, d), F32), pltpu.SemaphoreType.DMA((2,))],
    )
    return pl.pallas_call(
        functools.partial(_final_kernel, n_tok, off),
        grid_spec=grid_spec,
        out_shape=jax.ShapeDtypeStruct((rows, d), F32),
        compiler_params=_cparams(("arbitrary",)),
        name="final_norm",
    )(inv, x1, route_w, mod3, final_w.reshape(1, d), ys)


def kernel(x_prompt, x_sample, c_prompt, c_sample, w_ada, b_ada, norm1_w, w_in, b_in, mlstm_gate_b, conv_w, conv_b, mlstm_norm_w, p_a, p_b, w_out, norm2_w, w_router_group, b_router_group, w_router_expert, b_router_expert, w_expert_gate, w_expert_up, w_expert_down, final_norm_w):
    bp, seq, d = x_prompt.shape
    bs = x_sample.shape[0]
    bt = bp + bs
    n = bt * seq
    layer = 0

    w_in_a = slice_cols_bf16(w_in, layer, _SRC_A0, PROJ_A_COLS)
    b_in_a = b_in[layer, _SRC_A0:]
    b_in_m = b_in[layer, :PROJ_M_COLS] + jnp.pad(mlstm_gate_b[layer], (COL_GATE_M, PROJ_M_COLS - _SRC_A0))
    wr = jnp.concatenate([w_router_group[layer], w_router_expert[layer],
                          jnp.zeros((d, ROUTER_COLS - N_GROUPS - N_EXPERTS), F32)], axis=1)
    br = jnp.concatenate([b_router_group[layer], b_router_expert[layer],
                          jnp.zeros((ROUTER_COLS - N_GROUPS - N_EXPERTS,), F32)]).reshape(1, ROUTER_COLS)
    wr_hi = wr.astype(BF16)
    wr_lo = (wr - wr_hi.astype(F32)).astype(BF16)
    slopes = 2.0 ** (-8.0 * jnp.arange(1, A_HEADS + 1, dtype=F32) / A_HEADS)

    c_all = jnp.concatenate([c_prompt, c_sample, jnp.zeros((16 - bt, d), F32)], axis=0)
    mod3 = ada_modulation(c_all, w_ada[layer], b_ada[layer])[:bt].reshape(bt, 6, d)

    h = norm_modulate(x_prompt, x_sample, mod3, norm1_w[layer])
    h2d = h.reshape(n, d)
    proj_m = matmul_bias(h2d, w_in, b_in_m, PROJ_M_COLS, "in_projection_mlstm", layer=layer)
    proj_a = matmul_bias(h2d, w_in_a, b_in_a, PROJ_A_COLS, "in_projection")
    proj_m3 = proj_m.reshape(bt, seq, PROJ_M_COLS)
    proj_a3 = proj_a.reshape(bt, seq, PROJ_A_COLS)

    nc = seq // M_CHUNK
    gates = proj_m3[:, :, COL_GATE_M:COL_GATE_M + GATE_M_W].reshape(bt, nc, M_CHUNK, 4, M_HEADS)
    gates = gates.transpose(0, 4, 3, 1, 2)
    hm = mlstm_branch(proj_m3, gates, conv_w[layer], conv_b[layer], mlstm_norm_w[layer])
    at = attention_branch(proj_a3, slopes)

    x1, h2, logits = merge_project(
        x_prompt.reshape(bp * seq, d), x_sample.reshape(bs * seq, d), hm.reshape(n, -1), at.reshape(n, -1), proj_a,
        mod3, p_a[layer].astype(BF16), p_b[layer].astype(BF16), w_out[layer].astype(BF16), norm2_w[layer],
        wr_hi, wr_lo, br, seq)

    tables, inv, route_w = route(logits, MOE_ROWS)
    ys = expert_ffn(h2, tables, w_expert_gate[layer], w_expert_up[layer], w_expert_down[layer])

    y_p = final_norm(x1, ys, inv, route_w, mod3, final_norm_w, 0, bp * seq, seq)
    y_s = final_norm(x1, ys, inv, route_w, mod3, final_norm_w, bp * seq, bs * seq, seq)
    return (y_p.reshape(bp, seq, d), y_s.reshape(bs, seq, d))
```

```python
import functools

import jax
import jax.numpy as jnp
from jax import lax
from jax.experimental import pallas as pl
from jax.experimental.pallas import tpu as pltpu

F32 = jnp.float32
BF16 = jnp.bfloat16

D_MODEL = 2048
RMS_EPS = 1e-6
M_HEADS = 8
M_QK_DIM = 128
M_V_DIM = 256
M_CHUNK = 128
A_GROUPS = ((128, 1), (512, 4), (2048, 16))
A_SLOTS = 4
A_HEADS = A_SLOTS * len(A_GROUPS)
A_HEAD_DIM = 128
N_GROUPS = 4
EXPERTS_PER_GROUP = 8
N_EXPERTS = N_GROUPS * EXPERTS_PER_GROUP
TOP_K = 2
D_EXPERT = 1024

M_QK_W = M_HEADS * M_QK_DIM
M_V_W = M_HEADS * M_V_DIM
A_W = A_HEADS * A_HEAD_DIM
PROJ_TN = 512
COL_Q_M = 0
COL_K_M = M_QK_W
COL_V_M = 2 * M_QK_W
COL_O_M = 2 * M_QK_W + M_V_W
COL_GATE_M = 2 * M_QK_W + 2 * M_V_W
GATE_M_W = 4 * M_HEADS
PROJ_M_COLS = COL_GATE_M + PROJ_TN
_SRC_A0 = COL_GATE_M + GATE_M_W
COL_Q_A = 0
COL_K_A = A_W
COL_V_A = 2 * A_W
COL_GATE_A = 3 * A_W
COL_GATE_B = 3 * A_W + D_MODEL
PROJ_A_COLS = 3 * A_W + 2 * D_MODEL

LANES = 128
MOE_ROWS = 256
ROUTER_COLS = 128
VMEM_LIMIT = 56 * 1024 * 1024
EXPERT_VMEM_LIMIT = 60 * 1024 * 1024


def _sigmoid(x):
    return 1.0 / (1.0 + jnp.exp(-x))


def _cparams(sem, vmem=VMEM_LIMIT):
    return pltpu.CompilerParams(dimension_semantics=sem, vmem_limit_bytes=vmem)


def _ada_kernel(c_ref, w_ref, b_ref, o_ref):
    c = c_ref[...]
    a = (c * _sigmoid(c)).astype(BF16)
    o_ref[...] = jnp.dot(a, w_ref[...].astype(BF16), preferred_element_type=F32) + b_ref[...]


def ada_modulation(c, w_ada, b_ada):
    rows, d = c.shape
    n = w_ada.shape[1]
    tn = 1024
    return pl.pallas_call(
        _ada_kernel,
        grid=(n // tn,),
        in_specs=[pl.BlockSpec((rows, d), lambda j: (0, 0)),
                  pl.BlockSpec((d, tn), lambda j: (0, j)),
                  pl.BlockSpec((1, tn), lambda j: (0, j))],
        out_specs=pl.BlockSpec((rows, tn), lambda j: (0, j)),
        out_shape=jax.ShapeDtypeStruct((rows, n), F32),
        compiler_params=_cparams(("parallel",)),
        name="ada_modulation",
    )(c, w_ada, b_ada.reshape(1, n))


def _norm_mod_kernel(nb0, xp_ref, xs_ref, mod_ref, w_ref, o_ref):
    b = pl.program_id(0)
    x = jnp.where(b < nb0, xp_ref[0], xs_ref[0])
    y = x * lax.rsqrt(jnp.mean(x * x, axis=-1, keepdims=True) + RMS_EPS) * w_ref[...]
    o_ref[0] = (y * (1.0 + mod_ref[0, 1:2, :]) + mod_ref[0, 0:1, :]).astype(o_ref.dtype)


def norm_modulate(xp, xs, mod3, norm_w, ts=512):
    nb0, s, d = xp.shape
    nb1 = xs.shape[0]
    return pl.pallas_call(
        functools.partial(_norm_mod_kernel, nb0),
        grid=(nb0 + nb1, s // ts),
        in_specs=[pl.BlockSpec((1, ts, d), lambda b, t: (jnp.minimum(b, nb0 - 1), jnp.where(b < nb0, t, s // ts - 1), 0)),
                  pl.BlockSpec((1, ts, d), lambda b, t: (jnp.maximum(b - nb0, 0), jnp.where(b < nb0, 0, t), 0)),
                  pl.BlockSpec((1, 6, d), lambda b, t: (b, 0, 0)),
                  pl.BlockSpec((1, d), lambda b, t: (0, 0))],
        out_specs=pl.BlockSpec((1, ts, d), lambda b, t: (b, t, 0)),
        out_shape=jax.ShapeDtypeStruct((nb0 + nb1, s, d), BF16),
        compiler_params=_cparams(("parallel", "parallel")),
        name="norm1_modulate",
    )(xp, xs, mod3, norm_w.reshape(1, d))


def _mm_bias_kernel(a_ref, w_ref, b_ref, o_ref):
    o_ref[...] = jnp.dot(a_ref[...], w_ref[...].astype(BF16), preferred_element_type=F32) + b_ref[...]


def _slice_cols_kernel(col0, w_ref, o_ref):
    o_ref[...] = w_ref[:, col0:col0 + o_ref.shape[1]].astype(o_ref.dtype)


def slice_cols_bf16(w, layer, col0, n, tr=256):
    _, k, n_all = w.shape
    return pl.pallas_call(
        functools.partial(_slice_cols_kernel, col0),
        grid=(k // tr,),
        in_specs=[pl.BlockSpec((None, tr, n_all), lambda i: (layer, i, 0))],
        out_specs=pl.BlockSpec((tr, n), lambda i: (i, 0)),
        out_shape=jax.ShapeDtypeStruct((k, n), BF16),
        compiler_params=_cparams(("parallel",)),
        name="slice_cols_bf16",
    )(w)


def matmul_bias(a, w, b, n, name, layer=None, tm=2048, tn=PROJ_TN):
    m, k = a.shape
    if layer is None:
        w_spec = pl.BlockSpec((k, tn), lambda i, j: (0, j))
    else:
        w_spec = pl.BlockSpec((None, k, tn), lambda i, j: (layer, 0, j))
    return pl.pallas_call(
        _mm_bias_kernel,
        grid=(m // tm, n // tn),
        in_specs=[pl.BlockSpec((tm, k), lambda i, j: (i, 0)),
                  w_spec,
                  pl.BlockSpec((1, tn), lambda i, j: (0, j))],
        out_specs=pl.BlockSpec((tm, tn), lambda i, j: (i, j)),
        out_shape=jax.ShapeDtypeStruct((m, n), F32),
        compiler_params=_cparams(("parallel", "parallel")),
        name=name,
    )(a, w, b.reshape(1, -1))


def _lane_scan(x, op, fill, reverse):
    n = x.shape[-1]
    axis = x.ndim - 1
    lane = lax.broadcasted_iota(jnp.int32, x.shape, axis)
    k = 1
    while k < n:
        if reverse:
            x = op(x, jnp.where(lane < n - k, pltpu.roll(x, n - k, axis), fill))
        else:
            x = op(x, jnp.where(lane >= k, pltpu.roll(x, k, axis), fill))
        k *= 2
    return x


def _conv_silu_chunk(x_ref, w_ref, b_ref, c, n_chunks):
    L = M_CHUNK
    s = n_chunks * L
    r0 = pl.multiple_of(c * L, L)
    x = x_ref[0, pl.ds(r0, L), :]
    prev_row = x_ref[0, pl.ds(jnp.maximum(r0 - 1, 0), 1), :]
    next_row = x_ref[0, pl.ds(jnp.minimum(r0 + L, s - 1), 1), :]
    prev_row = jnp.where(c > 0, prev_row, 0.0)
    next_row = jnp.where(c < n_chunks - 1, next_row, 0.0)
    rows = lax.broadcasted_iota(jnp.int32, x.shape, 0)
    x_prev = jnp.where(rows == 0, prev_row, pltpu.roll(x, 1, 0))
    x_next = jnp.where(rows == L - 1, next_row, pltpu.roll(x, L - 1, 0))
    y = b_ref[...] + x_prev * w_ref[0:1, :] + x * w_ref[1:2, :] + x_next * w_ref[2:3, :]
    return y * _sigmoid(y)


def _mlstm_kernel(q_ref, k_ref, v_ref, o_ref, g_ref, cwq_ref, cwk_ref, cbq_ref, cbk_ref, nw_ref, out_ref,
                  qs_ref, kt_ref, hf_ref, hr_ref, c_ref,
                  u_ref, negm_ref, ib_ref, wk_ref, decay_ref, m0_ref, m1_ref):
    L = M_CHUNK
    dk = M_QK_DIM
    dv = M_V_DIM
    hp, nc = kt_ref.shape[0], kt_ref.shape[1]

    def prep(c, carry):
        q = _conv_silu_chunk(q_ref, cwq_ref, cbq_ref, c, nc) * (M_QK_DIM ** -0.5)
        qs_ref[c] = q.astype(BF16)
        k = _conv_silu_chunk(k_ref, cwk_ref, cbk_ref, c, nc)
        for hh in range(hp):
            kt_ref[hh, c] = k[:, hh * dk:(hh + 1) * dk].T
        return carry

    lax.fori_loop(0, nc, prep, 0)

    for hh in range(hp):
        for d in range(2):
            rev = d == 1
            i_pre = g_ref[0, hh, 2 * d]
            f_pre = g_ref[0, hh, 2 * d + 1]
            log_f = -(jnp.maximum(-f_pre, 0.0) + jnp.log1p(jnp.exp(-jnp.abs(f_pre))))
            b = _lane_scan(log_f, jnp.add, 0.0, rev)
            a = jnp.broadcast_to(b[:, 0:1] if rev else b[:, L - 1:L], (nc, L))
            g = a - b + i_pre
            g_max = jnp.broadcast_to(jnp.max(g, axis=1, keepdims=True), (nc, L))
            m = jnp.zeros((1, L), F32)
            for c in (range(nc - 1, -1, -1) if rev else range(nc)):
                m0_ref[hh, d, c:c + 1, :] = m
                m = jnp.maximum(a[c:c + 1, :] + m, g_max[c:c + 1, :])
                m1_ref[hh, d, c:c + 1, :] = m
            m0 = m0_ref[hh, d]
            m1 = m1_ref[hh, d]
            ib = i_pre - b
            m_t = jnp.maximum(b + m0, b + _lane_scan(ib, jnp.maximum, -jnp.inf, rev))
            ib_ref[hh, d] = ib
            u_ref[hh, d] = b - m_t
            negm_ref[hh, d] = -m_t
            wk_ref[hh, d] = jnp.exp(g - m1)
            decay_ref[hh, d] = jnp.exp(a + m0 - m1)

    t_idx = lax.broadcasted_iota(jnp.int32, (L, L), 0)
    s_idx = lax.broadcasted_iota(jnp.int32, (L, L), 1)
    ones_ext = jnp.ones((L, LANES), BF16)

    def chunk_step(hh, d, c):
        r0 = pl.multiple_of(c * L, L)
        q = qs_ref[c, :, hh * dk:(hh + 1) * dk]
        kt = kt_ref[hh, c]
        v_ext = jnp.concatenate([v_ref[0, pl.ds(r0, L), hh * dv:(hh + 1) * dv].astype(BF16), ones_ext], axis=1)

        def row(ref):
            return ref[hh, d, pl.ds(c, 1), :]

        umat = jnp.broadcast_to(row(u_ref), (L, L)).T
        nmat = jnp.broadcast_to(row(negm_ref), (L, L)).T
        causal = (s_idx <= t_idx) if d == 0 else (s_idx >= t_idx)
        w_intra = jnp.where(causal, jnp.exp(umat + row(ib_ref)), 0.0)
        w_inter = jnp.exp(umat + row(m0_ref))
        s_qk = jnp.dot(q, kt.astype(BF16), preferred_element_type=F32)
        c_ext = c_ref[hh, d]
        lhs = jnp.concatenate([(w_intra * s_qk).astype(BF16), (q.astype(F32) * w_inter).astype(BF16)], axis=1)
        rhs = jnp.concatenate([v_ext, c_ext.astype(BF16)], axis=0)
        num = jnp.dot(lhs, rhs, preferred_element_type=F32)
        r = 1.0 / jnp.maximum(jnp.abs(num[:, dv:]), jnp.exp(nmat))
        h = num[:, :dv] * jnp.concatenate([r] * (dv // LANES), axis=1)

        upd = jnp.dot((kt * row(wk_ref)).astype(BF16), v_ext, preferred_element_type=F32)
        decay = jnp.broadcast_to(row(decay_ref), (dk, LANES))
        c_ref[hh, d] = jnp.concatenate([decay] * (c_ext.shape[1] // LANES), axis=1) * c_ext + upd
        return h

    def finish(hh, c, hs):
        rows = pl.ds(pl.multiple_of(c * L, L), L)
        cols = slice(hh * dv, (hh + 1) * dv)
        y = hs * lax.rsqrt(jnp.mean(hs * hs, axis=-1, keepdims=True) + RMS_EPS) * nw_ref[:, cols]
        out_ref[0, rows, cols] = (y * _sigmoid(o_ref[0, rows, cols])).astype(out_ref.dtype)

    def first_half(j, carry):
        cf, cr = j, nc - 1 - j
        for hh in range(hp):
            hf_ref[hh, pl.ds(pl.multiple_of(cf * L, L), L), :] = chunk_step(hh, 0, cf)
            hr_ref[hh, pl.ds(pl.multiple_of((cr - nc // 2) * L, L), L), :] = chunk_step(hh, 1, cr)
        return carry

    def second_half(j, carry):
        cf, cr = j, nc - 1 - j
        for hh in range(hp):
            finish(hh, cf, chunk_step(hh, 0, cf) + hr_ref[hh, pl.ds(pl.multiple_of((cf - nc // 2) * L, L), L), :])
            finish(hh, cr, hf_ref[hh, pl.ds(pl.multiple_of(cr * L, L), L), :] + chunk_step(hh, 1, cr))
        return carry

    c_ref[...] = jnp.zeros_like(c_ref)
    lax.fori_loop(0, nc // 2, first_half, 0, unroll=2)
    lax.fori_loop(nc // 2, nc, second_half, 0, unroll=2)


def mlstm_branch(proj3, gates, conv_w, conv_b, norm_w, hp=2):
    bsz, s, _ = proj3.shape
    L = M_CHUNK
    nc = s // L
    dk, dv = M_QK_DIM, M_V_DIM
    assert dk == L and nc % 2 == 0 and M_HEADS % hp == 0
    wq, wv = hp * dk, hp * dv
    k_off = M_HEADS * dk // wq
    return pl.pallas_call(
        _mlstm_kernel,
        grid=(bsz, M_HEADS // hp),
        in_specs=[pl.BlockSpec((1, s, wq), lambda b, h: (b, 0, COL_Q_M // wq + h)),
                  pl.BlockSpec((1, s, wq), lambda b, h: (b, 0, COL_K_M // wq + h)),
                  pl.BlockSpec((1, s, wv), lambda b, h: (b, 0, COL_V_M // wv + h)),
                  pl.BlockSpec((1, s, wv), lambda b, h: (b, 0, COL_O_M // wv + h)),
                  pl.BlockSpec((1, hp, 4, nc, L), lambda b, h: (b, h, 0, 0, 0)),
                  pl.BlockSpec((3, wq), lambda b, h: (0, h)),
                  pl.BlockSpec((3, wq), lambda b, h: (0, k_off + h)),
                  pl.BlockSpec((1, wq), lambda b, h: (0, h)),
                  pl.BlockSpec((1, wq), lambda b, h: (0, k_off + h)),
                  pl.BlockSpec((1, wv), lambda b, h: (0, h))],
        out_specs=pl.BlockSpec((1, s, wv), lambda b, h: (b, 0, h)),
        out_shape=jax.ShapeDtypeStruct((bsz, s, M_HEADS * dv), BF16),
        scratch_shapes=[pltpu.VMEM((nc, L, wq), BF16),
                        pltpu.VMEM((hp, nc, dk, L), F32),
                        pltpu.VMEM((hp, s // 2, dv), F32),
                        pltpu.VMEM((hp, s // 2, dv), F32),
                        pltpu.VMEM((hp, 2, dk, dv + LANES), F32),
                        *([pltpu.VMEM((hp, 2, nc, L), F32)] * 7)],
        compiler_params=_cparams(("parallel", "parallel")),
        name="mlstm_branch",
    )(proj3, proj3, proj3, proj3, gates, conv_w, conv_w, conv_b.reshape(1, -1), conv_b.reshape(1, -1),
      norm_w.reshape(1, -1))


def _attn_kernel(slopes_ref, q0, q1, q2, k0, k1, k2, v0, v1, v2, out_ref, og_ref, lse_ref, bias_ref,
                 qd_ref, kd_ref, vd_ref, od_ref, ld_ref):
    s_len = out_ref.shape[1]
    dh = A_HEAD_DIM
    T = 128
    slot = pl.program_id(1)
    qs, ks, vs = (q0, q1, q2), (k0, k1, k2), (v0, v1, v2)

    for g, (window, dil) in enumerate(A_GROUPS):
        side = window // (2 * dil)
        u_len = s_len // dil
        nqb = u_len // T
        kw = min(T + 2 * side, u_len)
        slope = slopes_ref[g * A_SLOTS + slot] * float(dil)
        q_ref, k_ref, v_ref = qs[g], ks[g], vs[g]

        regroup = dil >= 8
        paired = regroup and nqb == 1 and dil % 2 == 0
        t_io = lax.broadcasted_iota(jnp.int32, (T, 2 * u_len if paired else kw), 0)
        k_io = lax.broadcasted_iota(jnp.int32, (T, 2 * u_len if paired else kw), 1)
        if paired:
            kw = 2 * u_len
            for case in range(2):
                rel = jnp.abs(t_io - (k_io - case * u_len))
                own = jnp.logical_and(k_io // u_len == case, rel <= side)
                bias_ref[g, case, :, :kw] = jnp.where(own, -slope * rel.astype(F32), -1e30)
        else:
            offsets = (0, side, kw - T) if nqb > 1 else (0,)
            for case, off in enumerate(offsets):
                rel = jnp.abs(t_io + off - k_io)
                bias_ref[g, case, :, :kw] = jnp.where(rel <= side, -slope * rel.astype(F32), -1e30)

        regroup = dil >= 8

        def residue_rows(r, dil=dil, u_len=u_len):
            return pl.ds(r, u_len, stride=dil), pl.ds(pl.multiple_of(r * u_len, u_len), u_len)

        if regroup:
            def gather_residue(r, carry, q_ref=q_ref, k_ref=k_ref, v_ref=v_ref):
                src, dst = residue_rows(r)
                qd_ref[dst, :] = (q_ref[0, src, :] * (dh ** -0.5)).astype(BF16)
                kd_ref[dst, :] = k_ref[0, src, :].astype(BF16)
                vd_ref[dst, :] = v_ref[0, src, :].astype(BF16)
                return carry

            lax.fori_loop(0, dil, gather_residue, 0, unroll=4)

        def block(idx, carry, g=g, dil=dil, side=side, u_len=u_len, nqb=nqb, kw=kw, regroup=regroup, paired=paired,
                  q_ref=q_ref, k_ref=k_ref, v_ref=v_ref):
            r = idx // nqb
            j = idx % nqb
            u0 = j * T
            ku0 = jnp.clip(u0 - side, 0, u_len - kw) if not paired else 0
            case = jnp.where(j == 0, 0, jnp.where(j == nqb - 1, 2, 1)) if nqb > 1 else 0
            if paired:
                case = r % 2
                q_rows = pl.ds(pl.multiple_of(r * u_len, T), T)
                k_rows = pl.ds(pl.multiple_of((r // 2) * kw, kw), kw)
                q, kk, vv = qd_ref[q_rows, :], kd_ref[k_rows, :], vd_ref[k_rows, :]
            elif regroup:
                q_rows = pl.ds(pl.multiple_of(r * u_len + u0, T), T)
                k_rows = pl.ds(pl.multiple_of(r * u_len + ku0, side), kw)
                q, kk, vv = qd_ref[q_rows, :], kd_ref[k_rows, :], vd_ref[k_rows, :]
            else:
                q_rows = pl.ds(r + u0 * dil, T, stride=dil) if dil > 1 else pl.ds(pl.multiple_of(u0, T), T)
                k_rows = pl.ds(r + ku0 * dil, kw, stride=dil) if dil > 1 else pl.ds(pl.multiple_of(ku0, side), kw)
                q = (q_ref[0, q_rows, :] * (dh ** -0.5)).astype(BF16)
                kk = k_ref[0, k_rows, :].astype(BF16)
                vv = v_ref[0, k_rows, :].astype(BF16)
            s = lax.dot_general(q, kk, (((1,), (1,)), ((), ())), preferred_element_type=F32)
            s = s + bias_ref[g, case, :, :kw]
            m = jnp.max(s, axis=1, keepdims=True)
            p = jnp.exp(s - m)
            den = jnp.sum(p, axis=1, keepdims=True)
            o = jnp.dot(p.astype(BF16), vv, preferred_element_type=F32) * (1.0 / den)
            lse = jnp.broadcast_to(m + jnp.log(den), (T, LANES))
            if regroup:
                od_ref[q_rows, :] = o
                ld_ref[q_rows, :] = lse
            else:
                og_ref[g, q_rows, :] = o
                lse_ref[g, q_rows, :] = lse
            return carry

        lax.fori_loop(0, dil * nqb, block, 0, unroll=8)

        if regroup:
            def scatter_residue(r, carry, g=g):
                dst, src = residue_rows(r)
                og_ref[g, dst, :] = od_ref[src, :]
                lse_ref[g, dst, :] = ld_ref[src, :]
                return carry

            lax.fori_loop(0, dil, scatter_residue, 0, unroll=4)

    rows_per = 256

    def merge(i, carry):
        rows = pl.ds(pl.multiple_of(i * rows_per, rows_per), rows_per)
        l0, l1, l2 = lse_ref[0, rows, :], lse_ref[1, rows, :], lse_ref[2, rows, :]
        mx = jnp.maximum(jnp.maximum(l0, l1), l2)
        e0, e1, e2 = jnp.exp(l0 - mx), jnp.exp(l1 - mx), jnp.exp(l2 - mx)
        inv = 1.0 / (e0 + e1 + e2)
        o = (e0 * inv) * og_ref[0, rows, :] + (e1 * inv) * og_ref[1, rows, :] + (e2 * inv) * og_ref[2, rows, :]
        out_ref[0, rows, :] = o.astype(out_ref.dtype)
        return carry

    lax.fori_loop(0, s_len // rows_per, merge, 0)


def attention_branch(proj3, slopes):
    bsz, s, _ = proj3.shape
    dh = A_HEAD_DIM

    def col(base, g):
        return lambda b, t, sl: (b, 0, base // dh + g * A_SLOTS + t)

    grid_spec = pltpu.PrefetchScalarGridSpec(
        num_scalar_prefetch=1,
        grid=(bsz, A_SLOTS),
        in_specs=[pl.BlockSpec((1, s, dh), col(base, g))
                  for base in (COL_Q_A, COL_K_A, COL_V_A) for g in range(len(A_GROUPS))],
        out_specs=pl.BlockSpec((1, s, dh), lambda b, t, sl: (b, 0, t)),
        scratch_shapes=[pltpu.VMEM((3, s, dh), F32), pltpu.VMEM((3, s, LANES), F32),
                        pltpu.VMEM((len(A_GROUPS), 3, 128, 256), F32),
                        pltpu.VMEM((s, dh), BF16), pltpu.VMEM((s, dh), BF16), pltpu.VMEM((s, dh), BF16),
                        pltpu.VMEM((s, dh), F32), pltpu.VMEM((s, LANES), F32)],
    )
    return pl.pallas_call(
        _attn_kernel,
        grid_spec=grid_spec,
        out_shape=jax.ShapeDtypeStruct((bsz, s, A_SLOTS * dh), BF16),
        compiler_params=_cparams(("parallel", "parallel")),
        name="dilated_attention",
    )(slopes, *([proj3] * 9))


def _merge_kernel(n0, n_gate, xp_ref, xs_ref, hm_ref, at_ref, *refs):
    ga_refs, gb_refs = refs[:n_gate], refs[n_gate:2 * n_gate]
    (mod_ref, pa_ref, pb_ref, wo_ref, n2_ref, wrh_ref, wrl_ref, br_ref, x1_ref, h2_ref, lg_ref) = refs[2 * n_gate:]
    i = pl.program_id(0)
    x = jnp.where(i < n0, xp_ref[...], xs_ref[...])
    y_a = jnp.dot(hm_ref[...], pa_ref[...], preferred_element_type=F32)
    y_b = jnp.dot(at_ref[...], pb_ref[...], preferred_element_type=F32)
    gate_a = jnp.concatenate([r[...] for r in ga_refs], axis=1)
    gate_b = jnp.concatenate([r[...] for r in gb_refs], axis=1)
    mixin = _sigmoid(gate_a) * y_a + _sigmoid(gate_b) * y_b
    mix = jnp.dot(mixin.astype(BF16), wo_ref[...], preferred_element_type=F32)
    x1 = x + mod_ref[0, 2:3, :] * mix
    x1_ref[...] = x1
    y = x1 * lax.rsqrt(jnp.mean(x1 * x1, axis=-1, keepdims=True) + RMS_EPS) * n2_ref[...]
    h2 = y * (1.0 + mod_ref[0, 4:5, :]) + mod_ref[0, 3:4, :]
    h2_ref[...] = h2
    hi = h2.astype(BF16)
    lo = (h2 - hi.astype(F32)).astype(BF16)
    lg_ref[...] = (jnp.dot(hi, wrh_ref[...], preferred_element_type=F32)
                   + (jnp.dot(hi, wrl_ref[...], preferred_element_type=F32)
                      + jnp.dot(lo, wrh_ref[...], preferred_element_type=F32))
                   + br_ref[...])


def merge_project(xp2, xs2, hm, at, proj, mod3, p_a, p_b, w_out, norm2_w, wr_hi, wr_lo, br, seq, tm=256):
    n, d = hm.shape
    n0 = xp2.shape[0] // tm
    n1 = xs2.shape[0] // tm
    per_seq = seq // tm
    const = dict(pipeline_mode=pl.Buffered(1))
    gw = PROJ_TN
    n_gate = d // gw

    def gate_specs(col0):
        return [pl.BlockSpec((tm, gw), lambda i, t=t: (i, col0 // gw + t)) for t in range(n_gate)]

    return pl.pallas_call(
        functools.partial(_merge_kernel, n0, n_gate),
        grid=(n0 + n1,),
        in_specs=[pl.BlockSpec((tm, d), lambda i: (jnp.minimum(i, n0 - 1), 0)),
                  pl.BlockSpec((tm, d), lambda i: (jnp.maximum(i - n0, 0), 0)),
                  pl.BlockSpec((tm, d), lambda i: (i, 0)),
                  pl.BlockSpec((tm, at.shape[1]), lambda i: (i, 0)),
                  *gate_specs(COL_GATE_A), *gate_specs(COL_GATE_B),
                  pl.BlockSpec((1, 6, d), lambda i: (i // per_seq, 0, 0)),
                  pl.BlockSpec(p_a.shape, lambda i: (0, 0), **const),
                  pl.BlockSpec(p_b.shape, lambda i: (0, 0), **const),
                  pl.BlockSpec(w_out.shape, lambda i: (0, 0), **const),
                  pl.BlockSpec((1, d), lambda i: (0, 0)),
                  pl.BlockSpec(wr_hi.shape, lambda i: (0, 0), **const),
                  pl.BlockSpec(wr_lo.shape, lambda i: (0, 0), **const),
                  pl.BlockSpec((1, ROUTER_COLS), lambda i: (0, 0))],
        out_specs=[pl.BlockSpec((tm, d), lambda i: (i, 0)),
                   pl.BlockSpec((tm, d), lambda i: (i, 0)),
                   pl.BlockSpec((tm, ROUTER_COLS), lambda i: (i, 0))],
        out_shape=[jax.ShapeDtypeStruct((n, d), F32),
                   jax.ShapeDtypeStruct((n, d), F32),
                   jax.ShapeDtypeStruct((n, ROUTER_COLS), F32)],
        compiler_params=_cparams(("parallel",)),
        name="merge_project",
    )(xp2, xs2, hm, at, *([proj] * (2 * n_gate)), mod3, p_a, p_b, w_out, norm2_w.reshape(1, d), wr_hi, wr_lo, br)


def route(logits, tb):
    n = logits.shape[0]
    g_logits = logits[:, :N_GROUPS]
    e_logits = logits[:, N_GROUPS:N_GROUPS + N_EXPERTS].reshape(n, N_GROUPS, EXPERTS_PER_GROUP)
    g_idx = jnp.argmax(g_logits, axis=-1)
    g_w = jnp.take_along_axis(jax.nn.softmax(g_logits, axis=-1), g_idx[:, None], axis=-1)
    e_sel = jnp.take_along_axis(e_logits, g_idx[:, None, None], axis=1)[:, 0]
    top_v, top_i = lax.top_k(e_sel, TOP_K)
    weights = g_w * jax.nn.softmax(top_v, axis=-1)
    expert = (g_idx[:, None] * EXPERTS_PER_GROUP + top_i).astype(jnp.int32)
    a = n * TOP_K
    flat_e = expert.reshape(a)
    e_ids = jnp.arange(N_EXPERTS, dtype=jnp.int32)
    counts = jnp.sum(flat_e[:, None] == e_ids[None, :], axis=0, dtype=jnp.int32)
    padded = (counts + tb - 1) // tb * tb
    pad_end = jnp.cumsum(padded)
    filler_e = jnp.repeat(e_ids, tb)
    filler_j = jnp.tile(jnp.arange(tb, dtype=jnp.int32), N_EXPERTS)
    filler_key = jnp.where(filler_j < (padded - counts)[filler_e], 2 * filler_e + 1, 2 * N_EXPERTS + 1)
    keys = jnp.concatenate([2 * flat_e, filler_key])
    ids = jnp.arange(a, dtype=jnp.int32)
    filler0 = jnp.zeros((N_EXPERTS * tb,), jnp.int32)
    tok_src = jnp.concatenate([ids // TOP_K, filler0])
    out_src = jnp.concatenate([(ids % TOP_K) * n + ids // TOP_K, filler0])
    sorted_keys, row_tok, row_out = lax.sort((keys, tok_src, out_src), num_keys=1)
    r = a + N_EXPERTS * tb
    n_blocks = r // tb
    block_start = jnp.arange(n_blocks, dtype=jnp.int32) * tb
    block_expert = jnp.minimum(jnp.sum(block_start[:, None] >= pad_end[None, :], axis=1), N_EXPERTS - 1).astype(jnp.int32)
    n_used = (pad_end[-1] // tb).astype(jnp.int32).reshape(1)
    later = jnp.logical_and(e_ids[None, :] > e_ids[:, None], (counts > 0)[None, :])
    next_of = jnp.min(jnp.where(later, e_ids[None, :], N_EXPERTS), axis=1)
    next_expert = jnp.where(next_of < N_EXPERTS, next_of, -1)[block_expert].astype(jnp.int32)
    pos = jnp.arange(r, dtype=jnp.int32)
    inv_key = jnp.where(sorted_keys % 2 == 0, row_out, a + pos)
    _, inv = lax.sort((inv_key, pos), num_keys=1)
    return (block_expert, next_expert, n_used, row_tok), inv[:a], weights


ROW_DMA_UNROLL = 16

def _issue_row_gather(src_hbm, dst_buf, sem, index_of, n_rows):
    def body(j, carry):
        pltpu.make_async_copy(src_hbm.at[pl.ds(index_of(j), 1), :], dst_buf.at[pl.ds(j, 1), :], sem).start()
        return carry
    lax.fori_loop(0, n_rows, body, 0, unroll=ROW_DMA_UNROLL)


def _wait_row_gather(src_hbm, dst_buf, sem):
    pltpu.make_async_copy(src_hbm.at[pl.ds(0, dst_buf.shape[0]), :], dst_buf, sem).wait()


def _round_rows_to_bf16(src_ref, dst_ref, rows=256):
    def body(t, carry):
        r = pl.ds(pl.multiple_of(t * rows, rows), rows)
        dst_ref[r, :] = src_ref[r, :].astype(BF16)
        return carry
    lax.fori_loop(0, src_ref.shape[0] // rows, body, 0)


def _expert_kernel(be_ref, nxt_ref, nused_ref, rtok_ref, h2_hbm, wg_hbm, wu_hbm, wd_hbm, out_ref,
                   xbuf, stg_g, stg_u, stg_d, wg_b, wu_b, wd_b, gsem, wsem):
    tb = xbuf.shape[1]
    i = pl.program_id(0)
    nused = nused_ref[0]
    slot = i % 2
    e = be_ref[i]
    first_of_expert = jnp.logical_or(i == 0, e != be_ref[jnp.maximum(i - 1, 0)])

    def issue(blk, s):
        _issue_row_gather(h2_hbm, xbuf.at[s], gsem.at[s], lambda j: rtok_ref[blk * tb + j], tb)

    def weight_copies(expert):
        return [pltpu.make_async_copy(src.at[expert], dst, wsem.at[k])
                for k, (src, dst) in enumerate(((wg_hbm, stg_g), (wu_hbm, stg_u), (wd_hbm, stg_d)))]

    @pl.when(jnp.logical_and(i == 0, nused > 0))
    def _():
        issue(0, 0)

    @pl.when(i + 1 < nused)
    def _():
        issue(i + 1, 1 - slot)

    @pl.when(i < nused)
    def _():
        @pl.when(first_of_expert)
        def _():
            @pl.when(i == 0)
            def _():
                for c in weight_copies(e):
                    c.start()

            for c in weight_copies(e):
                c.wait()
            for stg, wb in ((stg_g, wg_b), (stg_u, wu_b), (stg_d, wd_b)):
                _round_rows_to_bf16(stg, wb)
            nxt = nxt_ref[i]

            @pl.when(nxt >= 0)
            def _():
                for c in weight_copies(nxt):
                    c.start()

        _wait_row_gather(h2_hbm, xbuf.at[slot], gsem.at[slot])
        x = xbuf[slot].astype(BF16)
        g = jnp.dot(x, wg_b[...], preferred_element_type=F32)
        u = jnp.dot(x, wu_b[...], preferred_element_type=F32)
        hdn = (g * _sigmoid(g) * u).astype(BF16)
        out_ref[...] = jnp.dot(hdn, wd_b[...], preferred_element_type=F32)

    @pl.when(i >= nused)
    def _():
        out_ref[...] = jnp.zeros_like(out_ref)


def expert_ffn(h2, tables, wg, wu, wd, tb=MOE_ROWS):
    block_expert, next_expert, n_used, row_tok = tables
    n, d = h2.shape
    nb = block_expert.shape[0]
    de = wg.shape[2]
    any_spec = pl.BlockSpec(memory_space=pl.ANY)
    grid_spec = pltpu.PrefetchScalarGridSpec(
        num_scalar_prefetch=4,
        grid=(nb,),
        in_specs=[any_spec, any_spec, any_spec, any_spec],
        out_specs=pl.BlockSpec((tb, d), lambda i, *_: (i, 0)),
        scratch_shapes=[pltpu.VMEM((2, tb, d), F32),
                        pltpu.VMEM((d, de), F32), pltpu.VMEM((d, de), F32), pltpu.VMEM((de, d), F32),
                        pltpu.VMEM((d, de), BF16), pltpu.VMEM((d, de), BF16), pltpu.VMEM((de, d), BF16),
                        pltpu.SemaphoreType.DMA((2,)), pltpu.SemaphoreType.DMA((3,))],
    )
    return pl.pallas_call(
        _expert_kernel,
        grid_spec=grid_spec,
        out_shape=jax.ShapeDtypeStruct((nb * tb, d), F32),
        compiler_params=_cparams(("arbitrary",), EXPERT_VMEM_LIMIT),
        name="expert_ffn",
    )(block_expert, next_expert, n_used, row_tok, h2, wg, wu, wd)


def _final_kernel(n_tok, tile0, inv_ref, x1_ref, rw_ref, mod_ref, w_ref, ys_hbm, o_ref, buf, sem):
    tm = x1_ref.shape[0]
    i = pl.program_id(0)
    slot = i % 2

    def issue(tile, s):
        base = (tile0 + tile) * tm
        for k in range(TOP_K):
            _issue_row_gather(ys_hbm, buf.at[s, k], sem.at[s], lambda j, k=k: inv_ref[k * n_tok + base + j], tm)

    @pl.when(i == 0)
    def _():
        issue(0, 0)

    @pl.when(i + 1 < pl.num_programs(0))
    def _():
        issue(i + 1, 1 - slot)

    for k in range(TOP_K):
        _wait_row_gather(ys_hbm, buf.at[slot, k], sem.at[slot])
    rw = rw_ref[...]
    moe = buf[slot, 0] * rw[:, 0:1]
    for k in range(1, TOP_K):
        moe = moe + buf[slot, k] * rw[:, k:k + 1]
    x = x1_ref[...] + mod_ref[0, 5:6, :] * moe
    o_ref[...] = x * lax.rsqrt(jnp.mean(x * x, axis=-1, keepdims=True) + RMS_EPS) * w_ref[...]


def final_norm(x1, ys, inv, route_w, mod3, final_w, row0, rows, seq, tm=512):
    n_tok, d = x1.shape
    off = row0 // tm
    per_seq = seq // tm
    grid_spec = pltpu.PrefetchScalarGridSpec(
        num_scalar_prefetch=1,
        grid=(rows // tm,),
        in_specs=[pl.BlockSpec((tm, d), lambda i, inv: (off + i, 0)),
                  pl.BlockSpec((tm, TOP_K), lambda i, inv: (off + i, 0)),
                  pl.BlockSpec((1, 6, d), lambda i, inv: ((off + i) // per_seq, 0, 0)),
                  pl.BlockSpec((1, d), lambda i, inv: (0, 0)),
                  pl.BlockSpec(memory_space=pl.ANY)],
        out_specs=pl.BlockSpec((tm, d), lambda i, inv: (i, 0)),
        scratch_shapes=[pltpu.VMEM((2, TOP_K, tm, d), F32), pltpu.SemaphoreType.DMA((2,))],
    )
    return pl.pallas_call(
        functools.partial(_final_kernel, n_tok, off),
        grid_spec=grid_spec,
        out_shape=jax.ShapeDtypeStruct((rows, d), F32),
        compiler_params=_cparams(("arbitrary",)),
        name="final_norm",
    )(inv, x1, route_w, mod3, final_w.reshape(1, d), ys)


def kernel(x_prompt, x_sample, c_prompt, c_sample, w_ada, b_ada, norm1_w, w_in, b_in, mlstm_gate_b, conv_w, conv_b, mlstm_norm_w, p_a, p_b, w_out, norm2_w, w_router_group, b_router_group, w_router_expert, b_router_expert, w_expert_gate, w_expert_up, w_expert_down, final_norm_w):
    bp, seq, d = x_prompt.shape
    bs = x_sample.shape[0]
    bt = bp + bs
    n = bt * seq
    layer = 0

    w_in_bf = w_in.astype(BF16)
    w_in_a = slice_cols_bf16(w_in_bf, layer, _SRC_A0, PROJ_A_COLS)
    b_in_a = b_in[layer, _SRC_A0:]
    b_in_m = b_in[layer, :PROJ_M_COLS] + jnp.pad(mlstm_gate_b[layer], (COL_GATE_M, PROJ_M_COLS - _SRC_A0))
    wr = jnp.concatenate([w_router_group[layer], w_router_expert[layer],
                          jnp.zeros((d, ROUTER_COLS - N_GROUPS - N_EXPERTS), F32)], axis=1)
    br = jnp.concatenate([b_router_group[layer], b_router_expert[layer],
                          jnp.zeros((ROUTER_COLS - N_GROUPS - N_EXPERTS,), F32)]).reshape(1, ROUTER_COLS)
    wr_hi = wr.astype(BF16)
    wr_lo = (wr - wr_hi.astype(F32)).astype(BF16)
    slopes = 2.0 ** (-8.0 * jnp.arange(1, A_HEADS + 1, dtype=F32) / A_HEADS)

    c_all = jnp.concatenate([c_prompt, c_sample, jnp.zeros((16 - bt, d), F32)], axis=0)
    mod3 = ada_modulation(c_all, w_ada[layer], b_ada[layer])[:bt].reshape(bt, 6, d)

    h = norm_modulate(x_prompt, x_sample, mod3, norm1_w[layer])
    h2d = h.reshape(n, d)
    proj_m = matmul_bias(h2d, w_in_bf, b_in_m, PROJ_M_COLS, "in_projection_mlstm", layer=layer)
    proj_a = matmul_bias(h2d, w_in_a, b_in_a, PROJ_A_COLS, "in_projection")
    proj_m3 = proj_m.reshape(bt, seq, PROJ_M_COLS)
    proj_a3 = proj_a.reshape(bt, seq, PROJ_A_COLS)

    nc = seq // M_CHUNK
    gates = proj_m3[:, :, COL_GATE_M:COL_GATE_M + GATE_M_W].reshape(bt, nc, M_CHUNK, 4, M_HEADS)
    gates = gates.transpose(0, 4, 3, 1, 2)
    hm = mlstm_branch(proj_m3, gates, conv_w[layer], conv_b[layer], mlstm_norm_w[layer])
    at = attention_branch(proj_a3, slopes)

    x1, h2, logits = merge_project(
        x_prompt.reshape(bp * seq, d), x_sample.reshape(bs * seq, d), hm.reshape(n, -1), at.reshape(n, -1), proj_a,
        mod3, p_a[layer].astype(BF16), p_b[layer].astype(BF16), w_out[layer].astype(BF16), norm2_w[layer],
        wr_hi, wr_lo, br, seq)

    tables, inv, route_w = route(logits, MOE_ROWS)
    ys = expert_ffn(h2, tables, w_expert_gate[layer], w_expert_up[layer], w_expert_down[layer])

    y_p = final_norm(x1, ys, inv, route_w, mod3, final_norm_w, 0, bp * seq, seq)
    y_s = final_norm(x1, ys, inv, route_w, mod3, final_norm_w, bp * seq, bs * seq, seq)
    return (y_p.reshape(bp, seq, d), y_s.reshape(bs, seq, d))
```

```python
import functools

import jax
import jax.numpy as jnp
from jax import lax
from jax.experimental import pallas as pl
from jax.experimental.pallas import tpu as pltpu

F32 = jnp.float32
BF16 = jnp.bfloat16

D_MODEL = 2048
RMS_EPS = 1e-6
M_HEADS = 8
M_QK_DIM = 128
M_V_DIM = 256
M_CHUNK = 128
A_GROUPS = ((128, 1), (512, 4), (2048, 16))
A_SLOTS = 4
A_HEADS = A_SLOTS * len(A_GROUPS)
A_HEAD_DIM = 128
N_GROUPS = 4
EXPERTS_PER_GROUP = 8
N_EXPERTS = N_GROUPS * EXPERTS_PER_GROUP
TOP_K = 2
D_EXPERT = 1024

M_QK_W = M_HEADS * M_QK_DIM
M_V_W = M_HEADS * M_V_DIM
A_W = A_HEADS * A_HEAD_DIM
PROJ_TN = 512
COL_Q_M = 0
COL_K_M = M_QK_W
COL_V_M = 2 * M_QK_W
COL_O_M = 2 * M_QK_W + M_V_W
COL_GATE_M = 2 * M_QK_W + 2 * M_V_W
GATE_M_W = 4 * M_HEADS
PROJ_M_COLS = COL_GATE_M + PROJ_TN
_SRC_A0 = COL_GATE_M + GATE_M_W
COL_Q_A = 0
COL_K_A = A_W
COL_V_A = 2 * A_W
COL_GATE_A = 3 * A_W
COL_GATE_B = 3 * A_W + D_MODEL
PROJ_A_COLS = 3 * A_W + 2 * D_MODEL

LANES = 128
MOE_ROWS = 256
ROUTER_COLS = 128
V7X_VMEM_BYTES = 64 * 1024 * 1024
VMEM_LIMIT = V7X_VMEM_BYTES * 7 // 8
EXPERT_VMEM_LIMIT = V7X_VMEM_BYTES * 15 // 16


def _sigmoid(x):
    return 1.0 / (1.0 + jnp.exp(-x))


def _cparams(sem, vmem=VMEM_LIMIT):
    return pltpu.CompilerParams(dimension_semantics=sem, vmem_limit_bytes=vmem)


def _ada_kernel(c_ref, w_ref, b_ref, o_ref):
    c = c_ref[...]
    a = (c * _sigmoid(c)).astype(BF16)
    o_ref[...] = jnp.dot(a, w_ref[...].astype(BF16), preferred_element_type=F32) + b_ref[...]


def ada_modulation(c, w_ada, b_ada):
    rows, d = c.shape
    n = w_ada.shape[1]
    tn = 1024
    return pl.pallas_call(
        _ada_kernel,
        grid=(n // tn,),
        in_specs=[pl.BlockSpec((rows, d), lambda j: (0, 0)),
                  pl.BlockSpec((d, tn), lambda j: (0, j)),
                  pl.BlockSpec((1, tn), lambda j: (0, j))],
        out_specs=pl.BlockSpec((rows, tn), lambda j: (0, j)),
        out_shape=jax.ShapeDtypeStruct((rows, n), F32),
        compiler_params=_cparams(("parallel",)),
        name="ada_modulation",
    )(c, w_ada, b_ada.reshape(1, n))


def _norm_mod_kernel(nb0, xp_ref, xs_ref, mod_ref, w_ref, o_ref):
    b = pl.program_id(0)
    x = jnp.where(b < nb0, xp_ref[0], xs_ref[0])
    y = x * lax.rsqrt(jnp.mean(x * x, axis=-1, keepdims=True) + RMS_EPS) * w_ref[...]
    o_ref[0] = (y * (1.0 + mod_ref[0, 1:2, :]) + mod_ref[0, 0:1, :]).astype(o_ref.dtype)


def norm_modulate(xp, xs, mod3, norm_w, ts=512):
    nb0, s, d = xp.shape
    nb1 = xs.shape[0]
    return pl.pallas_call(
        functools.partial(_norm_mod_kernel, nb0),
        grid=(nb0 + nb1, s // ts),
        in_specs=[pl.BlockSpec((1, ts, d), lambda b, t: (jnp.minimum(b, nb0 - 1), jnp.where(b < nb0, t, s // ts - 1), 0)),
                  pl.BlockSpec((1, ts, d), lambda b, t: (jnp.maximum(b - nb0, 0), jnp.where(b < nb0, 0, t), 0)),
                  pl.BlockSpec((1, 6, d), lambda b, t: (b, 0, 0)),
                  pl.BlockSpec((1, d), lambda b, t: (0, 0))],
        out_specs=pl.BlockSpec((1, ts, d), lambda b, t: (b, t, 0)),
        out_shape=jax.ShapeDtypeStruct((nb0 + nb1, s, d), BF16),
        compiler_params=_cparams(("parallel", "parallel")),
        name="norm1_modulate",
    )(xp, xs, mod3, norm_w.reshape(1, d))


def _mm_bias_kernel(a_ref, w_ref, b_ref, o_ref):
    o_ref[...] = jnp.dot(a_ref[...], w_ref[...], preferred_element_type=F32) + b_ref[...]


def _slice_cols_kernel(col0, w_ref, o_ref):
    o_ref[...] = w_ref[:, col0:col0 + o_ref.shape[1]].astype(o_ref.dtype)


def slice_cols_bf16(w, layer, col0, n, tr=256):
    _, k, n_all = w.shape
    return pl.pallas_call(
        functools.partial(_slice_cols_kernel, col0),
        grid=(k // tr,),
        in_specs=[pl.BlockSpec((None, tr, n_all), lambda i: (layer, i, 0))],
        out_specs=pl.BlockSpec((tr, n), lambda i: (i, 0)),
        out_shape=jax.ShapeDtypeStruct((k, n), BF16),
        compiler_params=_cparams(("parallel",)),
        name="slice_cols_bf16",
    )(w)


def matmul_bias(a, w, b, n, name, layer=None, tm=2048, tn=PROJ_TN):
    m, k = a.shape
    if layer is None:
        w_spec = pl.BlockSpec((k, tn), lambda i, j: (0, j))
    else:
        w_spec = pl.BlockSpec((None, k, tn), lambda i, j: (layer, 0, j))
    return pl.pallas_call(
        _mm_bias_kernel,
        grid=(m // tm, n // tn),
        in_specs=[pl.BlockSpec((tm, k), lambda i, j: (i, 0)),
                  w_spec,
                  pl.BlockSpec((1, tn), lambda i, j: (0, j))],
        out_specs=pl.BlockSpec((tm, tn), lambda i, j: (i, j)),
        out_shape=jax.ShapeDtypeStruct((m, n), F32),
        compiler_params=_cparams(("parallel", "parallel")),
        name=name,
    )(a, w, b.reshape(1, -1))


def _lane_scan(x, op, fill, reverse):
    n = x.shape[-1]
    axis = x.ndim - 1
    lane = lax.broadcasted_iota(jnp.int32, x.shape, axis)
    k = 1
    while k < n:
        if reverse:
            x = op(x, jnp.where(lane < n - k, pltpu.roll(x, n - k, axis), fill))
        else:
            x = op(x, jnp.where(lane >= k, pltpu.roll(x, k, axis), fill))
        k *= 2
    return x


def _conv_silu_chunk(x_ref, w_ref, b_ref, c, n_chunks):
    L = M_CHUNK
    s = n_chunks * L
    r0 = pl.multiple_of(c * L, L)
    x = x_ref[0, pl.ds(r0, L), :]
    prev_row = x_ref[0, pl.ds(jnp.maximum(r0 - 1, 0), 1), :]
    next_row = x_ref[0, pl.ds(jnp.minimum(r0 + L, s - 1), 1), :]
    prev_row = jnp.where(c > 0, prev_row, 0.0)
    next_row = jnp.where(c < n_chunks - 1, next_row, 0.0)
    rows = lax.broadcasted_iota(jnp.int32, x.shape, 0)
    x_prev = jnp.where(rows == 0, prev_row, pltpu.roll(x, 1, 0))
    x_next = jnp.where(rows == L - 1, next_row, pltpu.roll(x, L - 1, 0))
    y = b_ref[...] + x_prev * w_ref[0:1, :] + x * w_ref[1:2, :] + x_next * w_ref[2:3, :]
    return y * _sigmoid(y)


def _mlstm_kernel(q_ref, k_ref, v_ref, o_ref, g_ref, cwq_ref, cwk_ref, cbq_ref, cbk_ref, nw_ref, out_ref,
                  qs_ref, kt_ref, hf_ref, hr_ref, c_ref,
                  u_ref, negm_ref, ib_ref, wk_ref, decay_ref, m0_ref, m1_ref):
    L = M_CHUNK
    dk = M_QK_DIM
    dv = M_V_DIM
    hp, nc = kt_ref.shape[0], kt_ref.shape[1]

    def prep(c, carry):
        q = _conv_silu_chunk(q_ref, cwq_ref, cbq_ref, c, nc) * (M_QK_DIM ** -0.5)
        qs_ref[c] = q.astype(BF16)
        k = _conv_silu_chunk(k_ref, cwk_ref, cbk_ref, c, nc)
        for hh in range(hp):
            kt_ref[hh, c] = k[:, hh * dk:(hh + 1) * dk].T
        return carry

    lax.fori_loop(0, nc, prep, 0)

    for hh in range(hp):
        for d in range(2):
            rev = d == 1
            i_pre = g_ref[0, hh, 2 * d]
            f_pre = g_ref[0, hh, 2 * d + 1]
            log_f = -(jnp.maximum(-f_pre, 0.0) + jnp.log1p(jnp.exp(-jnp.abs(f_pre))))
            b = _lane_scan(log_f, jnp.add, 0.0, rev)
            a = jnp.broadcast_to(b[:, 0:1] if rev else b[:, L - 1:L], (nc, L))
            g = a - b + i_pre
            g_max = jnp.broadcast_to(jnp.max(g, axis=1, keepdims=True), (nc, L))
            m = jnp.zeros((1, L), F32)
            for c in (range(nc - 1, -1, -1) if rev else range(nc)):
                m0_ref[hh, d, c:c + 1, :] = m
                m = jnp.maximum(a[c:c + 1, :] + m, g_max[c:c + 1, :])
                m1_ref[hh, d, c:c + 1, :] = m
            m0 = m0_ref[hh, d]
            m1 = m1_ref[hh, d]
            ib = i_pre - b
            m_t = jnp.maximum(b + m0, b + _lane_scan(ib, jnp.maximum, -jnp.inf, rev))
            ib_ref[hh, d] = ib
            u_ref[hh, d] = b - m_t
            negm_ref[hh, d] = -m_t
            wk_ref[hh, d] = jnp.exp(g - m1)
            decay_ref[hh, d] = jnp.exp(a + m0 - m1)

    t_idx = lax.broadcasted_iota(jnp.int32, (L, L), 0)
    s_idx = lax.broadcasted_iota(jnp.int32, (L, L), 1)
    ones_ext = jnp.ones((L, LANES), BF16)

    def chunk_step(hh, d, c):
        r0 = pl.multiple_of(c * L, L)
        q = qs_ref[c, :, hh * dk:(hh + 1) * dk]
        kt = kt_ref[hh, c]
        v_ext = jnp.concatenate([v_ref[0, pl.ds(r0, L), hh * dv:(hh + 1) * dv].astype(BF16), ones_ext], axis=1)

        def row(ref):
            return ref[hh, d, pl.ds(c, 1), :]

        umat = jnp.broadcast_to(row(u_ref), (L, L)).T
        nmat = jnp.broadcast_to(row(negm_ref), (L, L)).T
        causal = (s_idx <= t_idx) if d == 0 else (s_idx >= t_idx)
        w_intra = jnp.where(causal, jnp.exp(umat + row(ib_ref)), 0.0)
        w_inter = jnp.exp(umat + row(m0_ref))
        s_qk = jnp.dot(q, kt.astype(BF16), preferred_element_type=F32)
        c_ext = c_ref[hh, d]
        lhs = jnp.concatenate([(w_intra * s_qk).astype(BF16), (q.astype(F32) * w_inter).astype(BF16)], axis=1)
        rhs = jnp.concatenate([v_ext, c_ext.astype(BF16)], axis=0)
        num = jnp.dot(lhs, rhs, preferred_element_type=F32)
        r = 1.0 / jnp.maximum(jnp.abs(num[:, dv:]), jnp.exp(nmat))
        h = num[:, :dv] * jnp.concatenate([r] * (dv // LANES), axis=1)

        upd = jnp.dot((kt * row(wk_ref)).astype(BF16), v_ext, preferred_element_type=F32)
        decay = jnp.broadcast_to(row(decay_ref), (dk, LANES))
        c_ref[hh, d] = jnp.concatenate([decay] * (c_ext.shape[1] // LANES), axis=1) * c_ext + upd
        return h

    def finish(hh, c, hs):
        rows = pl.ds(pl.multiple_of(c * L, L), L)
        cols = slice(hh * dv, (hh + 1) * dv)
        y = hs * lax.rsqrt(jnp.mean(hs * hs, axis=-1, keepdims=True) + RMS_EPS) * nw_ref[:, cols]
        out_ref[0, rows, cols] = (y * _sigmoid(o_ref[0, rows, cols])).astype(out_ref.dtype)

    def first_half(j, carry):
        cf, cr = j, nc - 1 - j
        for hh in range(hp):
            hf_ref[hh, pl.ds(pl.multiple_of(cf * L, L), L), :] = chunk_step(hh, 0, cf)
            hr_ref[hh, pl.ds(pl.multiple_of((cr - nc // 2) * L, L), L), :] = chunk_step(hh, 1, cr)
        return carry

    def second_half(j, carry):
        cf, cr = j, nc - 1 - j
        for hh in range(hp):
            finish(hh, cf, chunk_step(hh, 0, cf) + hr_ref[hh, pl.ds(pl.multiple_of((cf - nc // 2) * L, L), L), :])
            finish(hh, cr, hf_ref[hh, pl.ds(pl.multiple_of(cr * L, L), L), :] + chunk_step(hh, 1, cr))
        return carry

    c_ref[...] = jnp.zeros_like(c_ref)
    lax.fori_loop(0, nc // 2, first_half, 0, unroll=2)
    lax.fori_loop(nc // 2, nc, second_half, 0, unroll=2)


def mlstm_branch(proj3, gates, conv_w, conv_b, norm_w, hp=2):
    bsz, s, _ = proj3.shape
    L = M_CHUNK
    nc = s // L
    dk, dv = M_QK_DIM, M_V_DIM
    assert dk == L and nc % 2 == 0 and M_HEADS % hp == 0
    wq, wv = hp * dk, hp * dv
    k_off = M_HEADS * dk // wq
    return pl.pallas_call(
        _mlstm_kernel,
        grid=(bsz, M_HEADS // hp),
        in_specs=[pl.BlockSpec((1, s, wq), lambda b, h: (b, 0, COL_Q_M // wq + h)),
                  pl.BlockSpec((1, s, wq), lambda b, h: (b, 0, COL_K_M // wq + h)),
                  pl.BlockSpec((1, s, wv), lambda b, h: (b, 0, COL_V_M // wv + h)),
                  pl.BlockSpec((1, s, wv), lambda b, h: (b, 0, COL_O_M // wv + h)),
                  pl.BlockSpec((1, hp, 4, nc, L), lambda b, h: (b, h, 0, 0, 0)),
                  pl.BlockSpec((3, wq), lambda b, h: (0, h)),
                  pl.BlockSpec((3, wq), lambda b, h: (0, k_off + h)),
                  pl.BlockSpec((1, wq), lambda b, h: (0, h)),
                  pl.BlockSpec((1, wq), lambda b, h: (0, k_off + h)),
                  pl.BlockSpec((1, wv), lambda b, h: (0, h))],
        out_specs=pl.BlockSpec((1, s, wv), lambda b, h: (b, 0, h)),
        out_shape=jax.ShapeDtypeStruct((bsz, s, M_HEADS * dv), BF16),
        scratch_shapes=[pltpu.VMEM((nc, L, wq), BF16),
                        pltpu.VMEM((hp, nc, dk, L), F32),
                        pltpu.VMEM((hp, s // 2, dv), F32),
                        pltpu.VMEM((hp, s // 2, dv), F32),
                        pltpu.VMEM((hp, 2, dk, dv + LANES), F32),
                        *([pltpu.VMEM((hp, 2, nc, L), F32)] * 7)],
        compiler_params=_cparams(("parallel", "parallel")),
        name="mlstm_branch",
    )(proj3, proj3, proj3, proj3, gates, conv_w, conv_w, conv_b.reshape(1, -1), conv_b.reshape(1, -1),
      norm_w.reshape(1, -1))


def _attn_kernel(slopes_ref, q0, q1, q2, k0, k1, k2, v0, v1, v2, out_ref, og_ref, lse_ref, bias_ref,
                 qd_ref, kd_ref, vd_ref, od_ref, ld_ref):
    s_len = out_ref.shape[1]
    dh = A_HEAD_DIM
    T = 128
    slot = pl.program_id(1)
    qs, ks, vs = (q0, q1, q2), (k0, k1, k2), (v0, v1, v2)

    for g, (window, dil) in enumerate(A_GROUPS):
        side = window // (2 * dil)
        u_len = s_len // dil
        nqb = u_len // T
        kw = min(T + 2 * side, u_len)
        slope = slopes_ref[g * A_SLOTS + slot] * float(dil)
        q_ref, k_ref, v_ref = qs[g], ks[g], vs[g]

        regroup = dil >= 8
        paired = regroup and nqb == 1 and dil % 2 == 0
        t_io = lax.broadcasted_iota(jnp.int32, (T, 2 * u_len if paired else kw), 0)
        k_io = lax.broadcasted_iota(jnp.int32, (T, 2 * u_len if paired else kw), 1)
        if paired:
            kw = 2 * u_len
            for case in range(2):
                rel = jnp.abs(t_io - (k_io - case * u_len))
                own = jnp.logical_and(k_io // u_len == case, rel <= side)
                bias_ref[g, case, :, :kw] = jnp.where(own, -slope * rel.astype(F32), -1e30)
        else:
            offsets = (0, side, kw - T) if nqb > 1 else (0,)
            for case, off in enumerate(offsets):
                rel = jnp.abs(t_io + off - k_io)
                bias_ref[g, case, :, :kw] = jnp.where(rel <= side, -slope * rel.astype(F32), -1e30)

        regroup = dil >= 8

        def residue_rows(r, dil=dil, u_len=u_len):
            return pl.ds(r, u_len, stride=dil), pl.ds(pl.multiple_of(r * u_len, u_len), u_len)

        if regroup:
            def gather_residue(r, carry, q_ref=q_ref, k_ref=k_ref, v_ref=v_ref):
                src, dst = residue_rows(r)
                qd_ref[dst, :] = (q_ref[0, src, :] * (dh ** -0.5)).astype(BF16)
                kd_ref[dst, :] = k_ref[0, src, :].astype(BF16)
                vd_ref[dst, :] = v_ref[0, src, :].astype(BF16)
                return carry

            lax.fori_loop(0, dil, gather_residue, 0, unroll=4)

        def block(idx, carry, g=g, dil=dil, side=side, u_len=u_len, nqb=nqb, kw=kw, regroup=regroup, paired=paired,
                  q_ref=q_ref, k_ref=k_ref, v_ref=v_ref):
            r = idx // nqb
            j = idx % nqb
            u0 = j * T
            ku0 = jnp.clip(u0 - side, 0, u_len - kw) if not paired else 0
            case = jnp.where(j == 0, 0, jnp.where(j == nqb - 1, 2, 1)) if nqb > 1 else 0
            if paired:
                case = r % 2
                q_rows = pl.ds(pl.multiple_of(r * u_len, T), T)
                k_rows = pl.ds(pl.multiple_of((r // 2) * kw, kw), kw)
                q, kk, vv = qd_ref[q_rows, :], kd_ref[k_rows, :], vd_ref[k_rows, :]
            elif regroup:
                q_rows = pl.ds(pl.multiple_of(r * u_len + u0, T), T)
                k_rows = pl.ds(pl.multiple_of(r * u_len + ku0, side), kw)
                q, kk, vv = qd_ref[q_rows, :], kd_ref[k_rows, :], vd_ref[k_rows, :]
            else:
                q_rows = pl.ds(r + u0 * dil, T, stride=dil) if dil > 1 else pl.ds(pl.multiple_of(u0, T), T)
                k_rows = pl.ds(r + ku0 * dil, kw, stride=dil) if dil > 1 else pl.ds(pl.multiple_of(ku0, side), kw)
                q = (q_ref[0, q_rows, :] * (dh ** -0.5)).astype(BF16)
                kk = k_ref[0, k_rows, :].astype(BF16)
                vv = v_ref[0, k_rows, :].astype(BF16)
            s = lax.dot_general(q, kk, (((1,), (1,)), ((), ())), preferred_element_type=F32)
            s = s + bias_ref[g, case, :, :kw]
            m = jnp.max(s, axis=1, keepdims=True)
            p = jnp.exp(s - m)
            den = jnp.sum(p, axis=1, keepdims=True)
            o = jnp.dot(p.astype(BF16), vv, preferred_element_type=F32) * (1.0 / den)
            lse = jnp.broadcast_to(m + jnp.log(den), (T, LANES))
            if regroup:
                od_ref[q_rows, :] = o
                ld_ref[q_rows, :] = lse
            else:
                og_ref[g, q_rows, :] = o
                lse_ref[g, q_rows, :] = lse
            return carry

        lax.fori_loop(0, dil * nqb, block, 0, unroll=8)

        if regroup:
            def scatter_residue(r, carry, g=g):
                dst, src = residue_rows(r)
                og_ref[g, dst, :] = od_ref[src, :]
                lse_ref[g, dst, :] = ld_ref[src, :]
                return carry

            lax.fori_loop(0, dil, scatter_residue, 0, unroll=4)

    rows_per = 256

    def merge(i, carry):
        rows = pl.ds(pl.multiple_of(i * rows_per, rows_per), rows_per)
        l0, l1, l2 = lse_ref[0, rows, :], lse_ref[1, rows, :], lse_ref[2, rows, :]
        mx = jnp.maximum(jnp.maximum(l0, l1), l2)
        e0, e1, e2 = jnp.exp(l0 - mx), jnp.exp(l1 - mx), jnp.exp(l2 - mx)
        inv = 1.0 / (e0 + e1 + e2)
        o = (e0 * inv) * og_ref[0, rows, :] + (e1 * inv) * og_ref[1, rows, :] + (e2 * inv) * og_ref[2, rows, :]
        out_ref[0, rows, :] = o.astype(out_ref.dtype)
        return carry

    lax.fori_loop(0, s_len // rows_per, merge, 0)


def attention_branch(proj3, slopes):
    bsz, s, _ = proj3.shape
    dh = A_HEAD_DIM

    def col(base, g):
        return lambda b, t, sl: (b, 0, base // dh + g * A_SLOTS + t)

    grid_spec = pltpu.PrefetchScalarGridSpec(
        num_scalar_prefetch=1,
        grid=(bsz, A_SLOTS),
        in_specs=[pl.BlockSpec((1, s, dh), col(base, g))
                  for base in (COL_Q_A, COL_K_A, COL_V_A) for g in range(len(A_GROUPS))],
        out_specs=pl.BlockSpec((1, s, dh), lambda b, t, sl: (b, 0, t)),
        scratch_shapes=[pltpu.VMEM((3, s, dh), F32), pltpu.VMEM((3, s, LANES), F32),
                        pltpu.VMEM((len(A_GROUPS), 3, 128, 256), F32),
                        pltpu.VMEM((s, dh), BF16), pltpu.VMEM((s, dh), BF16), pltpu.VMEM((s, dh), BF16),
                        pltpu.VMEM((s, dh), F32), pltpu.VMEM((s, LANES), F32)],
    )
    return pl.pallas_call(
        _attn_kernel,
        grid_spec=grid_spec,
        out_shape=jax.ShapeDtypeStruct((bsz, s, A_SLOTS * dh), BF16),
        compiler_params=_cparams(("parallel", "parallel")),
        name="dilated_attention",
    )(slopes, *([proj3] * 9))


def _merge_kernel(n0, n_gate, xp_ref, xs_ref, hm_ref, at_ref, *refs):
    ga_refs, gb_refs = refs[:n_gate], refs[n_gate:2 * n_gate]
    (mod_ref, pa_ref, pb_ref, wo_ref, n2_ref, wrh_ref, wrl_ref, br_ref, x1_ref, h2_ref, lg_ref) = refs[2 * n_gate:]
    i = pl.program_id(0)
    x = jnp.where(i < n0, xp_ref[...], xs_ref[...])
    y_a = jnp.dot(hm_ref[...], pa_ref[...], preferred_element_type=F32)
    y_b = jnp.dot(at_ref[...], pb_ref[...], preferred_element_type=F32)
    gate_a = jnp.concatenate([r[...] for r in ga_refs], axis=1)
    gate_b = jnp.concatenate([r[...] for r in gb_refs], axis=1)
    mixin = _sigmoid(gate_a) * y_a + _sigmoid(gate_b) * y_b
    mix = jnp.dot(mixin.astype(BF16), wo_ref[...], preferred_element_type=F32)
    x1 = x + mod_ref[0, 2:3, :] * mix
    x1_ref[...] = x1
    y = x1 * lax.rsqrt(jnp.mean(x1 * x1, axis=-1, keepdims=True) + RMS_EPS) * n2_ref[...]
    h2 = y * (1.0 + mod_ref[0, 4:5, :]) + mod_ref[0, 3:4, :]
    h2_ref[...] = h2
    hi = h2.astype(BF16)
    lo = (h2 - hi.astype(F32)).astype(BF16)
    lg_ref[...] = (jnp.dot(hi, wrh_ref[...], preferred_element_type=F32)
                   + (jnp.dot(hi, wrl_ref[...], preferred_element_type=F32)
                      + jnp.dot(lo, wrh_ref[...], preferred_element_type=F32))
                   + br_ref[...])


def merge_project(xp2, xs2, hm, at, proj, mod3, p_a, p_b, w_out, norm2_w, wr_hi, wr_lo, br, seq, tm=256):
    n, d = hm.shape
    n0 = xp2.shape[0] // tm
    n1 = xs2.shape[0] // tm
    per_seq = seq // tm
    const = dict(pipeline_mode=pl.Buffered(1))
    gw = PROJ_TN
    n_gate = d // gw

    def gate_specs(col0):
        return [pl.BlockSpec((tm, gw), lambda i, t=t: (i, col0 // gw + t)) for t in range(n_gate)]

    return pl.pallas_call(
        functools.partial(_merge_kernel, n0, n_gate),
        grid=(n0 + n1,),
        in_specs=[pl.BlockSpec((tm, d), lambda i: (jnp.minimum(i, n0 - 1), 0)),
                  pl.BlockSpec((tm, d), lambda i: (jnp.maximum(i - n0, 0), 0)),
                  pl.BlockSpec((tm, d), lambda i: (i, 0)),
                  pl.BlockSpec((tm, at.shape[1]), lambda i: (i, 0)),
                  *gate_specs(COL_GATE_A), *gate_specs(COL_GATE_B),
                  pl.BlockSpec((1, 6, d), lambda i: (i // per_seq, 0, 0)),
                  pl.BlockSpec(p_a.shape, lambda i: (0, 0), **const),
                  pl.BlockSpec(p_b.shape, lambda i: (0, 0), **const),
                  pl.BlockSpec(w_out.shape, lambda i: (0, 0), **const),
                  pl.BlockSpec((1, d), lambda i: (0, 0)),
                  pl.BlockSpec(wr_hi.shape, lambda i: (0, 0), **const),
                  pl.BlockSpec(wr_lo.shape, lambda i: (0, 0), **const),
                  pl.BlockSpec((1, ROUTER_COLS), lambda i: (0, 0))],
        out_specs=[pl.BlockSpec((tm, d), lambda i: (i, 0)),
                   pl.BlockSpec((tm, d), lambda i: (i, 0)),
                   pl.BlockSpec((tm, ROUTER_COLS), lambda i: (i, 0))],
        out_shape=[jax.ShapeDtypeStruct((n, d), F32),
                   jax.ShapeDtypeStruct((n, d), F32),
                   jax.ShapeDtypeStruct((n, ROUTER_COLS), F32)],
        compiler_params=_cparams(("parallel",)),
        name="merge_project",
    )(xp2, xs2, hm, at, *([proj] * (2 * n_gate)), mod3, p_a, p_b, w_out, norm2_w.reshape(1, d), wr_hi, wr_lo, br)


def route(logits, tb):
    n = logits.shape[0]
    g_logits = logits[:, :N_GROUPS]
    e_logits = logits[:, N_GROUPS:N_GROUPS + N_EXPERTS].reshape(n, N_GROUPS, EXPERTS_PER_GROUP)
    g_idx = jnp.argmax(g_logits, axis=-1)
    g_w = jnp.take_along_axis(jax.nn.softmax(g_logits, axis=-1), g_idx[:, None], axis=-1)
    e_sel = jnp.take_along_axis(e_logits, g_idx[:, None, None], axis=1)[:, 0]
    top_v, top_i = lax.top_k(e_sel, TOP_K)
    weights = g_w * jax.nn.softmax(top_v, axis=-1)
    expert = (g_idx[:, None] * EXPERTS_PER_GROUP + top_i).astype(jnp.int32)
    a = n * TOP_K
    flat_e = expert.reshape(a)
    e_ids = jnp.arange(N_EXPERTS, dtype=jnp.int32)
    counts = jnp.sum(flat_e[:, None] == e_ids[None, :], axis=0, dtype=jnp.int32)
    padded = (counts + tb - 1) // tb * tb
    pad_end = jnp.cumsum(padded)
    filler_e = jnp.repeat(e_ids, tb)
    filler_j = jnp.tile(jnp.arange(tb, dtype=jnp.int32), N_EXPERTS)
    filler_key = jnp.where(filler_j < (padded - counts)[filler_e], 2 * filler_e + 1, 2 * N_EXPERTS + 1)
    keys = jnp.concatenate([2 * flat_e, filler_key])
    ids = jnp.arange(a, dtype=jnp.int32)
    filler0 = jnp.zeros((N_EXPERTS * tb,), jnp.int32)
    tok_src = jnp.concatenate([ids // TOP_K, filler0])
    out_src = jnp.concatenate([(ids % TOP_K) * n + ids // TOP_K, filler0])
    sorted_keys, row_tok, row_out = lax.sort((keys, tok_src, out_src), num_keys=1)
    r = a + N_EXPERTS * tb
    n_blocks = r // tb
    block_start = jnp.arange(n_blocks, dtype=jnp.int32) * tb
    block_expert = jnp.minimum(jnp.sum(block_start[:, None] >= pad_end[None, :], axis=1), N_EXPERTS - 1).astype(jnp.int32)
    n_used = (pad_end[-1] // tb).astype(jnp.int32).reshape(1)
    later = jnp.logical_and(e_ids[None, :] > e_ids[:, None], (counts > 0)[None, :])
    next_of = jnp.min(jnp.where(later, e_ids[None, :], N_EXPERTS), axis=1)
    next_expert = jnp.where(next_of < N_EXPERTS, next_of, -1)[block_expert].astype(jnp.int32)
    pos = jnp.arange(r, dtype=jnp.int32)
    inv_key = jnp.where(sorted_keys % 2 == 0, row_out, a + pos)
    _, inv = lax.sort((inv_key, pos), num_keys=1)
    return (block_expert, next_expert, n_used, row_tok), inv[:a], weights


ROW_DMA_UNROLL = 16

def _issue_row_gather(src_hbm, dst_buf, sem, index_of, n_rows):
    def body(j, carry):
        pltpu.make_async_copy(src_hbm.at[pl.ds(index_of(j), 1), :], dst_buf.at[pl.ds(j, 1), :], sem).start()
        return carry
    lax.fori_loop(0, n_rows, body, 0, unroll=ROW_DMA_UNROLL)


def _wait_row_gather(src_hbm, dst_buf, sem):
    pltpu.make_async_copy(src_hbm.at[pl.ds(0, dst_buf.shape[0]), :], dst_buf, sem).wait()


def _round_rows_to_bf16(src_ref, dst_ref, rows=256):
    def body(t, carry):
        r = pl.ds(pl.multiple_of(t * rows, rows), rows)
        dst_ref[r, :] = src_ref[r, :].astype(BF16)
        return carry
    lax.fori_loop(0, src_ref.shape[0] // rows, body, 0)


def _expert_kernel(be_ref, nxt_ref, nused_ref, rtok_ref, h2_hbm, wg_hbm, wu_hbm, wd_hbm, out_ref,
                   xbuf, stg_g, stg_u, stg_d, wg_b, wu_b, wd_b, gsem, wsem):
    tb = xbuf.shape[1]
    i = pl.program_id(0)
    nused = nused_ref[0]
    slot = i % 2
    e = be_ref[i]
    first_of_expert = jnp.logical_or(i == 0, e != be_ref[jnp.maximum(i - 1, 0)])

    def issue(blk, s):
        _issue_row_gather(h2_hbm, xbuf.at[s], gsem.at[s], lambda j: rtok_ref[blk * tb + j], tb)

    def weight_copies(expert):
        return [pltpu.make_async_copy(src.at[expert], dst, wsem.at[k])
                for k, (src, dst) in enumerate(((wg_hbm, stg_g), (wu_hbm, stg_u), (wd_hbm, stg_d)))]

    @pl.when(jnp.logical_and(i == 0, nused > 0))
    def _():
        issue(0, 0)

    @pl.when(i + 1 < nused)
    def _():
        issue(i + 1, 1 - slot)

    @pl.when(i < nused)
    def _():
        @pl.when(first_of_expert)
        def _():
            @pl.when(i == 0)
            def _():
                for c in weight_copies(e):
                    c.start()

            for c in weight_copies(e):
                c.wait()
            for stg, wb in ((stg_g, wg_b), (stg_u, wu_b), (stg_d, wd_b)):
                _round_rows_to_bf16(stg, wb)
            nxt = nxt_ref[i]

            @pl.when(nxt >= 0)
            def _():
                for c in weight_copies(nxt):
                    c.start()

        _wait_row_gather(h2_hbm, xbuf.at[slot], gsem.at[slot])
        x = xbuf[slot].astype(BF16)
        g = jnp.dot(x, wg_b[...], preferred_element_type=F32)
        u = jnp.dot(x, wu_b[...], preferred_element_type=F32)
        hdn = (g * _sigmoid(g) * u).astype(BF16)
        out_ref[...] = jnp.dot(hdn, wd_b[...], preferred_element_type=F32)

    @pl.when(i >= nused)
    def _():
        out_ref[...] = jnp.zeros_like(out_ref)


def expert_ffn(h2, tables, wg, wu, wd, tb=MOE_ROWS):
    block_expert, next_expert, n_used, row_tok = tables
    n, d = h2.shape
    nb = block_expert.shape[0]
    de = wg.shape[2]
    any_spec = pl.BlockSpec(memory_space=pl.ANY)
    grid_spec = pltpu.PrefetchScalarGridSpec(
        num_scalar_prefetch=4,
        grid=(nb,),
        in_specs=[any_spec, any_spec, any_spec, any_spec],
        out_specs=pl.BlockSpec((tb, d), lambda i, *_: (i, 0)),
        scratch_shapes=[pltpu.VMEM((2, tb, d), F32),
                        pltpu.VMEM((d, de), F32), pltpu.VMEM((d, de), F32), pltpu.VMEM((de, d), F32),
                        pltpu.VMEM((d, de), BF16), pltpu.VMEM((d, de), BF16), pltpu.VMEM((de, d), BF16),
                        pltpu.SemaphoreType.DMA((2,)), pltpu.SemaphoreType.DMA((3,))],
    )
    return pl.pallas_call(
        _expert_kernel,
        grid_spec=grid_spec,
        out_shape=jax.ShapeDtypeStruct((nb * tb, d), F32),
        compiler_params=_cparams(("arbitrary",), EXPERT_VMEM_LIMIT),
        name="expert_ffn",
    )(block_expert, next_expert, n_used, row_tok, h2, wg, wu, wd)


def _final_kernel(n_tok, tile0, inv_ref, x1_ref, rw_ref, mod_ref, w_ref, ys_hbm, o_ref, buf, sem):
    tm = x1_ref.shape[0]
    i = pl.program_id(0)
    slot = i % 2

    def issue(tile, s):
        base = (tile0 + tile) * tm
        for k in range(TOP_K):
            _issue_row_gather(ys_hbm, buf.at[s, k], sem.at[s], lambda j, k=k: inv_ref[k * n_tok + base + j], tm)

    @pl.when(i == 0)
    def _():
        issue(0, 0)

    @pl.when(i + 1 < pl.num_programs(0))
    def _():
        issue(i + 1, 1 - slot)

    for k in range(TOP_K):
        _wait_row_gather(ys_hbm, buf.at[slot, k], sem.at[slot])
    rw = rw_ref[...]
    moe = buf[slot, 0] * rw[:, 0:1]
    for k in range(1, TOP_K):
        moe = moe + buf[slot, k] * rw[:, k:k + 1]
    x = x1_ref[...] + mod_ref[0, 5:6, :] * moe
    o_ref[...] = x * lax.rsqrt(jnp.mean(x * x, axis=-1, keepdims=True) + RMS_EPS) * w_ref[...]


def final_norm(x1, ys, inv, route_w, mod3, final_w, row0, rows, seq, tm=512):
    n_tok, d = x1.shape
    off = row0 // tm
    per_seq = seq // tm
    grid_spec = pltpu.PrefetchScalarGridSpec(
        num_scalar_prefetch=1,
        grid=(rows // tm,),
        in_specs=[pl.BlockSpec((tm, d), lambda i, inv: (off + i, 0)),
                  pl.BlockSpec((tm, TOP_K), lambda i, inv: (off + i, 0)),
                  pl.BlockSpec((1, 6, d), lambda i, inv: ((off + i) // per_seq, 0, 0)),
                  pl.BlockSpec((1, d), lambda i, inv: (0, 0)),
                  pl.BlockSpec(memory_space=pl.ANY)],
        out_specs=pl.BlockSpec((tm, d), lambda i, inv: (i, 0)),
        scratch_shapes=[pltpu.VMEM((2, TOP_K, tm, d), F32), pltpu.SemaphoreType.DMA((2,))],
    )
    return pl.pallas_call(
        functools.partial(_final_kernel, n_tok, off),
        grid_spec=grid_spec,
        out_shape=jax.ShapeDtypeStruct((rows, d), F32),
        compiler_params=_cparams(("arbitrary",)),
        name="final_norm",
    )(inv, x1, route_w, mod3, final_w.reshape(1, d), ys)


def kernel(x_prompt, x_sample, c_prompt, c_sample, w_ada, b_ada, norm1_w, w_in, b_in, mlstm_gate_b, conv_w, conv_b, mlstm_norm_w, p_a, p_b, w_out, norm2_w, w_router_group, b_router_group, w_router_expert, b_router_expert, w_expert_gate, w_expert_up, w_expert_down, final_norm_w):
    bp, seq, d = x_prompt.shape
    bs = x_sample.shape[0]
    bt = bp + bs
    n = bt * seq
    layer = 0

    w_in_bf = w_in.astype(BF16)
    w_in_a = slice_cols_bf16(w_in_bf, layer, _SRC_A0, PROJ_A_COLS)
    b_in_a = b_in[layer, _SRC_A0:]
    b_in_m = b_in[layer, :PROJ_M_COLS] + jnp.pad(mlstm_gate_b[layer], (COL_GATE_M, PROJ_M_COLS - _SRC_A0))
    wr = jnp.concatenate([w_router_group[layer], w_router_expert[layer],
                          jnp.zeros((d, ROUTER_COLS - N_GROUPS - N_EXPERTS), F32)], axis=1)
    br = jnp.concatenate([b_router_group[layer], b_router_expert[layer],
                          jnp.zeros((ROUTER_COLS - N_GROUPS - N_EXPERTS,), F32)]).reshape(1, ROUTER_COLS)
    wr_hi = wr.astype(BF16)
    wr_lo = (wr - wr_hi.astype(F32)).astype(BF16)
    slopes = 2.0 ** (-8.0 * jnp.arange(1, A_HEADS + 1, dtype=F32) / A_HEADS)

    c_all = jnp.concatenate([c_prompt, c_sample, jnp.zeros((16 - bt, d), F32)], axis=0)
    mod3 = ada_modulation(c_all, w_ada[layer], b_ada[layer])[:bt].reshape(bt, 6, d)

    h = norm_modulate(x_prompt, x_sample, mod3, norm1_w[layer])
    h2d = h.reshape(n, d)
    proj_m = matmul_bias(h2d, w_in_bf, b_in_m, PROJ_M_COLS, "in_projection_mlstm", layer=layer)
    proj_a = matmul_bias(h2d, w_in_a, b_in_a, PROJ_A_COLS, "in_projection")
    proj_m3 = proj_m.reshape(bt, seq, PROJ_M_COLS)
    proj_a3 = proj_a.reshape(bt, seq, PROJ_A_COLS)

    nc = seq // M_CHUNK
    gates = proj_m3[:, :, COL_GATE_M:COL_GATE_M + GATE_M_W].reshape(bt, nc, M_CHUNK, 4, M_HEADS)
    gates = gates.transpose(0, 4, 3, 1, 2)
    hm = mlstm_branch(proj_m3, gates, conv_w[layer], conv_b[layer], mlstm_norm_w[layer])
    at = attention_branch(proj_a3, slopes)

    x1, h2, logits = merge_project(
        x_prompt.reshape(bp * seq, d), x_sample.reshape(bs * seq, d), hm.reshape(n, -1), at.reshape(n, -1), proj_a,
        mod3, p_a[layer].astype(BF16), p_b[layer].astype(BF16), w_out[layer].astype(BF16), norm2_w[layer],
        wr_hi, wr_lo, br, seq)

    tables, inv, route_w = route(logits, MOE_ROWS)
    ys = expert_ffn(h2, tables, w_expert_gate[layer], w_expert_up[layer], w_expert_down[layer])

    y_p = final_norm(x1, ys, inv, route_w, mod3, final_norm_w, 0, bp * seq, seq)
    y_s = final_norm(x1, ys, inv, route_w, mod3, final_norm_w, bp * seq, bs * seq, seq)
    return (y_p.reshape(bp, seq, d), y_s.reshape(bs, seq, d))
```

```python
import functools

import jax
import jax.numpy as jnp
from jax import lax
from jax.experimental import pallas as pl
from jax.experimental.pallas import tpu as pltpu

F32 = jnp.float32
BF16 = jnp.bfloat16

D_MODEL = 2048
RMS_EPS = 1e-6
M_HEADS = 8
M_QK_DIM = 128
M_V_DIM = 256
M_CHUNK = 128
A_GROUPS = ((128, 1), (512, 4), (2048, 16))
A_SLOTS = 4
A_HEADS = A_SLOTS * len(A_GROUPS)
A_HEAD_DIM = 128
N_GROUPS = 4
EXPERTS_PER_GROUP = 8
N_EXPERTS = N_GROUPS * EXPERTS_PER_GROUP
TOP_K = 2
D_EXPERT = 1024

M_QK_W = M_HEADS * M_QK_DIM
M_V_W = M_HEADS * M_V_DIM
A_W = A_HEADS * A_HEAD_DIM
PROJ_TN = 512
COL_Q_M = 0
COL_K_M = M_QK_W
COL_V_M = 2 * M_QK_W
COL_O_M = 2 * M_QK_W + M_V_W
COL_GATE_M = 2 * M_QK_W + 2 * M_V_W
GATE_M_W = 4 * M_HEADS
PROJ_M_COLS = COL_GATE_M + PROJ_TN
_SRC_A0 = COL_GATE_M + GATE_M_W
COL_Q_A = 0
COL_K_A = A_W
COL_V_A = 2 * A_W
COL_GATE_A = 3 * A_W
COL_GATE_B = 3 * A_W + D_MODEL
PROJ_A_COLS = 3 * A_W + 2 * D_MODEL

LANES = 128
MOE_ROWS = 256
ROUTER_COLS = 128
V7X_VMEM_BYTES = 64 * 1024 * 1024
VMEM_LIMIT = V7X_VMEM_BYTES * 7 // 8
EXPERT_VMEM_LIMIT = V7X_VMEM_BYTES * 15 // 16


def _sigmoid(x):
    return 1.0 / (1.0 + jnp.exp(-x))


def _cparams(sem, vmem=VMEM_LIMIT):
    return pltpu.CompilerParams(dimension_semantics=sem, vmem_limit_bytes=vmem)


def _ada_kernel(c_ref, w_ref, b_ref, o_ref):
    c = c_ref[...]
    a = (c * _sigmoid(c)).astype(BF16)
    o_ref[...] = jnp.dot(a, w_ref[...].astype(BF16), preferred_element_type=F32) + b_ref[...]


def ada_modulation(c, w_ada, b_ada):
    rows, d = c.shape
    n = w_ada.shape[1]
    tn = 1024
    return pl.pallas_call(
        _ada_kernel,
        grid=(n // tn,),
        in_specs=[pl.BlockSpec((rows, d), lambda j: (0, 0)),
                  pl.BlockSpec((d, tn), lambda j: (0, j)),
                  pl.BlockSpec((1, tn), lambda j: (0, j))],
        out_specs=pl.BlockSpec((rows, tn), lambda j: (0, j)),
        out_shape=jax.ShapeDtypeStruct((rows, n), F32),
        compiler_params=_cparams(("parallel",)),
        name="ada_modulation",
    )(c, w_ada, b_ada.reshape(1, n))


def _norm_mod_kernel(nb0, xp_ref, xs_ref, mod_ref, w_ref, o_ref):
    b = pl.program_id(0)
    x = jnp.where(b < nb0, xp_ref[0], xs_ref[0])
    y = x * lax.rsqrt(jnp.mean(x * x, axis=-1, keepdims=True) + RMS_EPS) * w_ref[...]
    o_ref[0] = (y * (1.0 + mod_ref[0, 1:2, :]) + mod_ref[0, 0:1, :]).astype(o_ref.dtype)


def norm_modulate(xp, xs, mod3, norm_w, ts=1024):
    nb0, s, d = xp.shape
    nb1 = xs.shape[0]
    return pl.pallas_call(
        functools.partial(_norm_mod_kernel, nb0),
        grid=(nb0 + nb1, s // ts),
        in_specs=[pl.BlockSpec((1, ts, d), lambda b, t: (jnp.minimum(b, nb0 - 1), jnp.where(b < nb0, t, s // ts - 1), 0)),
                  pl.BlockSpec((1, ts, d), lambda b, t: (jnp.maximum(b - nb0, 0), jnp.where(b < nb0, 0, t), 0)),
                  pl.BlockSpec((1, 6, d), lambda b, t: (b, 0, 0)),
                  pl.BlockSpec((1, d), lambda b, t: (0, 0))],
        out_specs=pl.BlockSpec((1, ts, d), lambda b, t: (b, t, 0)),
        out_shape=jax.ShapeDtypeStruct((nb0 + nb1, s, d), BF16),
        compiler_params=_cparams(("parallel", "parallel")),
        name="norm1_modulate",
    )(xp, xs, mod3, norm_w.reshape(1, d))


def _mm_bias_kernel(a_ref, w_ref, b_ref, o_ref):
    o_ref[...] = jnp.dot(a_ref[...], w_ref[...], preferred_element_type=F32) + b_ref[...]


def _slice_cols_kernel(col0, w_ref, o_ref):
    o_ref[...] = w_ref[:, col0:col0 + o_ref.shape[1]].astype(o_ref.dtype)


def slice_cols_bf16(w, layer, col0, n, tr=256):
    _, k, n_all = w.shape
    return pl.pallas_call(
        functools.partial(_slice_cols_kernel, col0),
        grid=(k // tr,),
        in_specs=[pl.BlockSpec((None, tr, n_all), lambda i: (layer, i, 0))],
        out_specs=pl.BlockSpec((tr, n), lambda i: (i, 0)),
        out_shape=jax.ShapeDtypeStruct((k, n), BF16),
        compiler_params=_cparams(("parallel",)),
        name="slice_cols_bf16",
    )(w)


def matmul_bias(a, w, b, n, name, layer=None, tm=2048, tn=PROJ_TN):
    m, k = a.shape
    if layer is None:
        w_spec = pl.BlockSpec((k, tn), lambda i, j: (0, j))
    else:
        w_spec = pl.BlockSpec((None, k, tn), lambda i, j: (layer, 0, j))
    return pl.pallas_call(
        _mm_bias_kernel,
        grid=(m // tm, n // tn),
        in_specs=[pl.BlockSpec((tm, k), lambda i, j: (i, 0)),
                  w_spec,
                  pl.BlockSpec((1, tn), lambda i, j: (0, j))],
        out_specs=pl.BlockSpec((tm, tn), lambda i, j: (i, j)),
        out_shape=jax.ShapeDtypeStruct((m, n), F32),
        compiler_params=_cparams(("parallel", "parallel")),
        name=name,
    )(a, w, b.reshape(1, -1))


def _lane_scan(x, op, fill, reverse):
    n = x.shape[-1]
    axis = x.ndim - 1
    lane = lax.broadcasted_iota(jnp.int32, x.shape, axis)
    k = 1
    while k < n:
        if reverse:
            x = op(x, jnp.where(lane < n - k, pltpu.roll(x, n - k, axis), fill))
        else:
            x = op(x, jnp.where(lane >= k, pltpu.roll(x, k, axis), fill))
        k *= 2
    return x


def _conv_silu_chunk(x_ref, w_ref, b_ref, c, n_chunks):
    L = M_CHUNK
    s = n_chunks * L
    r0 = pl.multiple_of(c * L, L)
    x = x_ref[0, pl.ds(r0, L), :]
    prev_row = x_ref[0, pl.ds(jnp.maximum(r0 - 1, 0), 1), :]
    next_row = x_ref[0, pl.ds(jnp.minimum(r0 + L, s - 1), 1), :]
    prev_row = jnp.where(c > 0, prev_row, 0.0)
    next_row = jnp.where(c < n_chunks - 1, next_row, 0.0)
    rows = lax.broadcasted_iota(jnp.int32, x.shape, 0)
    x_prev = jnp.where(rows == 0, prev_row, pltpu.roll(x, 1, 0))
    x_next = jnp.where(rows == L - 1, next_row, pltpu.roll(x, L - 1, 0))
    y = b_ref[...] + x_prev * w_ref[0:1, :] + x * w_ref[1:2, :] + x_next * w_ref[2:3, :]
    return y * _sigmoid(y)


def _mlstm_kernel(q_ref, k_ref, v_ref, o_ref, g_ref, cwq_ref, cwk_ref, cbq_ref, cbk_ref, nw_ref, out_ref,
                  qs_ref, kt_ref, hf_ref, hr_ref, c_ref,
                  u_ref, negm_ref, ib_ref, wk_ref, decay_ref, m0_ref, m1_ref):
    L = M_CHUNK
    dk = M_QK_DIM
    dv = M_V_DIM
    hp, nc = kt_ref.shape[0], kt_ref.shape[1]

    def prep(c, carry):
        q = _conv_silu_chunk(q_ref, cwq_ref, cbq_ref, c, nc) * (M_QK_DIM ** -0.5)
        qs_ref[c] = q.astype(BF16)
        k = _conv_silu_chunk(k_ref, cwk_ref, cbk_ref, c, nc)
        for hh in range(hp):
            kt_ref[hh, c] = k[:, hh * dk:(hh + 1) * dk].T
        return carry

    lax.fori_loop(0, nc, prep, 0)

    for hh in range(hp):
        for d in range(2):
            rev = d == 1
            i_pre = g_ref[0, hh, 2 * d]
            f_pre = g_ref[0, hh, 2 * d + 1]
            log_f = -(jnp.maximum(-f_pre, 0.0) + jnp.log1p(jnp.exp(-jnp.abs(f_pre))))
            b = _lane_scan(log_f, jnp.add, 0.0, rev)
            a = jnp.broadcast_to(b[:, 0:1] if rev else b[:, L - 1:L], (nc, L))
            g = a - b + i_pre
            g_max = jnp.broadcast_to(jnp.max(g, axis=1, keepdims=True), (nc, L))
            m = jnp.zeros((1, L), F32)
            for c in (range(nc - 1, -1, -1) if rev else range(nc)):
                m0_ref[hh, d, c:c + 1, :] = m
                m = jnp.maximum(a[c:c + 1, :] + m, g_max[c:c + 1, :])
                m1_ref[hh, d, c:c + 1, :] = m
            m0 = m0_ref[hh, d]
            m1 = m1_ref[hh, d]
            ib = i_pre - b
            m_t = jnp.maximum(b + m0, b + _lane_scan(ib, jnp.maximum, -jnp.inf, rev))
            ib_ref[hh, d] = ib
            u_ref[hh, d] = b - m_t
            negm_ref[hh, d] = -m_t
            wk_ref[hh, d] = jnp.exp(g - m1)
            decay_ref[hh, d] = jnp.exp(a + m0 - m1)

    t_idx = lax.broadcasted_iota(jnp.int32, (L, L), 0)
    s_idx = lax.broadcasted_iota(jnp.int32, (L, L), 1)
    ones_ext = jnp.ones((L, LANES), BF16)

    def chunk_step(hh, d, c):
        r0 = pl.multiple_of(c * L, L)
        q = qs_ref[c, :, hh * dk:(hh + 1) * dk]
        kt = kt_ref[hh, c]
        v_ext = jnp.concatenate([v_ref[0, pl.ds(r0, L), hh * dv:(hh + 1) * dv].astype(BF16), ones_ext], axis=1)

        def row(ref):
            return ref[hh, d, pl.ds(c, 1), :]

        umat = jnp.broadcast_to(row(u_ref), (L, L)).T
        nmat = jnp.broadcast_to(row(negm_ref), (L, L)).T
        causal = (s_idx <= t_idx) if d == 0 else (s_idx >= t_idx)
        w_intra = jnp.where(causal, jnp.exp(umat + row(ib_ref)), 0.0)
        w_inter = jnp.exp(umat + row(m0_ref))
        s_qk = jnp.dot(q, kt.astype(BF16), preferred_element_type=F32)
        c_ext = c_ref[hh, d]
        lhs = jnp.concatenate([(w_intra * s_qk).astype(BF16), (q.astype(F32) * w_inter).astype(BF16)], axis=1)
        rhs = jnp.concatenate([v_ext, c_ext.astype(BF16)], axis=0)
        num = jnp.dot(lhs, rhs, preferred_element_type=F32)
        r = 1.0 / jnp.maximum(jnp.abs(num[:, dv:]), jnp.exp(nmat))
        h = num[:, :dv] * jnp.concatenate([r] * (dv // LANES), axis=1)

        upd = jnp.dot((kt * row(wk_ref)).astype(BF16), v_ext, preferred_element_type=F32)
        decay = jnp.broadcast_to(row(decay_ref), (dk, LANES))
        c_ref[hh, d] = jnp.concatenate([decay] * (c_ext.shape[1] // LANES), axis=1) * c_ext + upd
        return h

    def finish(hh, c, hs):
        rows = pl.ds(pl.multiple_of(c * L, L), L)
        cols = slice(hh * dv, (hh + 1) * dv)
        y = hs * lax.rsqrt(jnp.mean(hs * hs, axis=-1, keepdims=True) + RMS_EPS) * nw_ref[:, cols]
        out_ref[0, rows, cols] = (y * _sigmoid(o_ref[0, rows, cols])).astype(out_ref.dtype)

    def first_half(j, carry):
        cf, cr = j, nc - 1 - j
        for hh in range(hp):
            hf_ref[hh, pl.ds(pl.multiple_of(cf * L, L), L), :] = chunk_step(hh, 0, cf)
            hr_ref[hh, pl.ds(pl.multiple_of((cr - nc // 2) * L, L), L), :] = chunk_step(hh, 1, cr)
        return carry

    def second_half(j, carry):
        cf, cr = j, nc - 1 - j
        for hh in range(hp):
            finish(hh, cf, chunk_step(hh, 0, cf) + hr_ref[hh, pl.ds(pl.multiple_of((cf - nc // 2) * L, L), L), :])
            finish(hh, cr, hf_ref[hh, pl.ds(pl.multiple_of(cr * L, L), L), :] + chunk_step(hh, 1, cr))
        return carry

    c_ref[...] = jnp.zeros_like(c_ref)
    lax.fori_loop(0, nc // 2, first_half, 0, unroll=2)
    lax.fori_loop(nc // 2, nc, second_half, 0, unroll=2)


def mlstm_branch(proj3, gates, conv_w, conv_b, norm_w, hp=2):
    bsz, s, _ = proj3.shape
    L = M_CHUNK
    nc = s // L
    dk, dv = M_QK_DIM, M_V_DIM
    assert dk == L and nc % 2 == 0 and M_HEADS % hp == 0
    wq, wv = hp * dk, hp * dv
    k_off = M_HEADS * dk // wq
    return pl.pallas_call(
        _mlstm_kernel,
        grid=(bsz, M_HEADS // hp),
        in_specs=[pl.BlockSpec((1, s, wq), lambda b, h: (b, 0, COL_Q_M // wq + h)),
                  pl.BlockSpec((1, s, wq), lambda b, h: (b, 0, COL_K_M // wq + h)),
                  pl.BlockSpec((1, s, wv), lambda b, h: (b, 0, COL_V_M // wv + h)),
                  pl.BlockSpec((1, s, wv), lambda b, h: (b, 0, COL_O_M // wv + h)),
                  pl.BlockSpec((1, hp, 4, nc, L), lambda b, h: (b, h, 0, 0, 0)),
                  pl.BlockSpec((3, wq), lambda b, h: (0, h)),
                  pl.BlockSpec((3, wq), lambda b, h: (0, k_off + h)),
                  pl.BlockSpec((1, wq), lambda b, h: (0, h)),
                  pl.BlockSpec((1, wq), lambda b, h: (0, k_off + h)),
                  pl.BlockSpec((1, wv), lambda b, h: (0, h))],
        out_specs=pl.BlockSpec((1, s, wv), lambda b, h: (b, 0, h)),
        out_shape=jax.ShapeDtypeStruct((bsz, s, M_HEADS * dv), BF16),
        scratch_shapes=[pltpu.VMEM((nc, L, wq), BF16),
                        pltpu.VMEM((hp, nc, dk, L), F32),
                        pltpu.VMEM((hp, s // 2, dv), F32),
                        pltpu.VMEM((hp, s // 2, dv), F32),
                        pltpu.VMEM((hp, 2, dk, dv + LANES), F32),
                        *([pltpu.VMEM((hp, 2, nc, L), F32)] * 7)],
        compiler_params=_cparams(("parallel", "parallel")),
        name="mlstm_branch",
    )(proj3, proj3, proj3, proj3, gates, conv_w, conv_w, conv_b.reshape(1, -1), conv_b.reshape(1, -1),
      norm_w.reshape(1, -1))


def _attn_kernel(slopes_ref, q0, q1, q2, k0, k1, k2, v0, v1, v2, out_ref, og_ref, lse_ref, bias_ref,
                 qd_ref, kd_ref, vd_ref, od_ref, ld_ref):
    s_len = out_ref.shape[1]
    dh = A_HEAD_DIM
    T = 128
    slot = pl.program_id(1)
    qs, ks, vs = (q0, q1, q2), (k0, k1, k2), (v0, v1, v2)

    for g, (window, dil) in enumerate(A_GROUPS):
        side = window // (2 * dil)
        u_len = s_len // dil
        nqb = u_len // T
        kw = min(T + 2 * side, u_len)
        slope = slopes_ref[g * A_SLOTS + slot] * float(dil)
        q_ref, k_ref, v_ref = qs[g], ks[g], vs[g]

        regroup = dil >= 8
        paired = regroup and nqb == 1 and dil % 2 == 0
        t_io = lax.broadcasted_iota(jnp.int32, (T, 2 * u_len if paired else kw), 0)
        k_io = lax.broadcasted_iota(jnp.int32, (T, 2 * u_len if paired else kw), 1)
        if paired:
            kw = 2 * u_len
            for case in range(2):
                rel = jnp.abs(t_io - (k_io - case * u_len))
                own = jnp.logical_and(k_io // u_len == case, rel <= side)
                bias_ref[g, case, :, :kw] = jnp.where(own, -slope * rel.astype(F32), -1e30)
        else:
            offsets = (0, side, kw - T) if nqb > 1 else (0,)
            for case, off in enumerate(offsets):
                rel = jnp.abs(t_io + off - k_io)
                bias_ref[g, case, :, :kw] = jnp.where(rel <= side, -slope * rel.astype(F32), -1e30)

        regroup = dil >= 8

        def residue_rows(r, dil=dil, u_len=u_len):
            return pl.ds(r, u_len, stride=dil), pl.ds(pl.multiple_of(r * u_len, u_len), u_len)

        if regroup:
            def gather_residue(r, carry, q_ref=q_ref, k_ref=k_ref, v_ref=v_ref):
                src, dst = residue_rows(r)
                qd_ref[dst, :] = (q_ref[0, src, :] * (dh ** -0.5)).astype(BF16)
                kd_ref[dst, :] = k_ref[0, src, :].astype(BF16)
                vd_ref[dst, :] = v_ref[0, src, :].astype(BF16)
                return carry

            lax.fori_loop(0, dil, gather_residue, 0, unroll=4)

        def block(idx, carry, g=g, dil=dil, side=side, u_len=u_len, nqb=nqb, kw=kw, regroup=regroup, paired=paired,
                  q_ref=q_ref, k_ref=k_ref, v_ref=v_ref):
            r = idx // nqb
            j = idx % nqb
            u0 = j * T
            ku0 = jnp.clip(u0 - side, 0, u_len - kw) if not paired else 0
            case = jnp.where(j == 0, 0, jnp.where(j == nqb - 1, 2, 1)) if nqb > 1 else 0
            if paired:
                case = r % 2
                q_rows = pl.ds(pl.multiple_of(r * u_len, T), T)
                k_rows = pl.ds(pl.multiple_of((r // 2) * kw, kw), kw)
                q, kk, vv = qd_ref[q_rows, :], kd_ref[k_rows, :], vd_ref[k_rows, :]
            elif regroup:
                q_rows = pl.ds(pl.multiple_of(r * u_len + u0, T), T)
                k_rows = pl.ds(pl.multiple_of(r * u_len + ku0, side), kw)
                q, kk, vv = qd_ref[q_rows, :], kd_ref[k_rows, :], vd_ref[k_rows, :]
            else:
                q_rows = pl.ds(r + u0 * dil, T, stride=dil) if dil > 1 else pl.ds(pl.multiple_of(u0, T), T)
                k_rows = pl.ds(r + ku0 * dil, kw, stride=dil) if dil > 1 else pl.ds(pl.multiple_of(ku0, side), kw)
                q = (q_ref[0, q_rows, :] * (dh ** -0.5)).astype(BF16)
                kk = k_ref[0, k_rows, :].astype(BF16)
                vv = v_ref[0, k_rows, :].astype(BF16)
            s = lax.dot_general(q, kk, (((1,), (1,)), ((), ())), preferred_element_type=F32)
            s = s + bias_ref[g, case, :, :kw]
            m = jnp.max(s, axis=1, keepdims=True)
            p = jnp.exp(s - m)
            den = jnp.sum(p, axis=1, keepdims=True)
            o = jnp.dot(p.astype(BF16), vv, preferred_element_type=F32) * (1.0 / den)
            lse = jnp.broadcast_to(m + jnp.log(den), (T, LANES))
            if regroup:
                od_ref[q_rows, :] = o
                ld_ref[q_rows, :] = lse
            else:
                og_ref[g, q_rows, :] = o
                lse_ref[g, q_rows, :] = lse
            return carry

        lax.fori_loop(0, dil * nqb, block, 0, unroll=8)

        if regroup:
            def scatter_residue(r, carry, g=g):
                dst, src = residue_rows(r)
                og_ref[g, dst, :] = od_ref[src, :]
                lse_ref[g, dst, :] = ld_ref[src, :]
                return carry

            lax.fori_loop(0, dil, scatter_residue, 0, unroll=4)

    rows_per = 256

    def merge(i, carry):
        rows = pl.ds(pl.multiple_of(i * rows_per, rows_per), rows_per)
        l0, l1, l2 = lse_ref[0, rows, :], lse_ref[1, rows, :], lse_ref[2, rows, :]
        mx = jnp.maximum(jnp.maximum(l0, l1), l2)
        e0, e1, e2 = jnp.exp(l0 - mx), jnp.exp(l1 - mx), jnp.exp(l2 - mx)
        inv = 1.0 / (e0 + e1 + e2)
        o = (e0 * inv) * og_ref[0, rows, :] + (e1 * inv) * og_ref[1, rows, :] + (e2 * inv) * og_ref[2, rows, :]
        out_ref[0, rows, :] = o.astype(out_ref.dtype)
        return carry

    lax.fori_loop(0, s_len // rows_per, merge, 0)


def attention_branch(proj3, slopes):
    bsz, s, _ = proj3.shape
    dh = A_HEAD_DIM

    def col(base, g):
        return lambda b, t, sl: (b, 0, base // dh + g * A_SLOTS + t)

    grid_spec = pltpu.PrefetchScalarGridSpec(
        num_scalar_prefetch=1,
        grid=(bsz, A_SLOTS),
        in_specs=[pl.BlockSpec((1, s, dh), col(base, g))
                  for base in (COL_Q_A, COL_K_A, COL_V_A) for g in range(len(A_GROUPS))],
        out_specs=pl.BlockSpec((1, s, dh), lambda b, t, sl: (b, 0, t)),
        scratch_shapes=[pltpu.VMEM((3, s, dh), F32), pltpu.VMEM((3, s, LANES), F32),
                        pltpu.VMEM((len(A_GROUPS), 3, 128, 256), F32),
                        pltpu.VMEM((s, dh), BF16), pltpu.VMEM((s, dh), BF16), pltpu.VMEM((s, dh), BF16),
                        pltpu.VMEM((s, dh), F32), pltpu.VMEM((s, LANES), F32)],
    )
    return pl.pallas_call(
        _attn_kernel,
        grid_spec=grid_spec,
        out_shape=jax.ShapeDtypeStruct((bsz, s, A_SLOTS * dh), BF16),
        compiler_params=_cparams(("parallel", "parallel")),
        name="dilated_attention",
    )(slopes, *([proj3] * 9))


def _merge_kernel(n0, n_gate, xp_ref, xs_ref, hm_ref, at_ref, *refs):
    ga_refs, gb_refs = refs[:n_gate], refs[n_gate:2 * n_gate]
    (mod_ref, pa_ref, pb_ref, wo_ref, n2_ref, wrh_ref, wrl_ref, br_ref, x1_ref, h2_ref, lg_ref) = refs[2 * n_gate:]
    i = pl.program_id(0)
    x = jnp.where(i < n0, xp_ref[...], xs_ref[...])
    y_a = jnp.dot(hm_ref[...], pa_ref[...], preferred_element_type=F32)
    y_b = jnp.dot(at_ref[...], pb_ref[...], preferred_element_type=F32)
    gate_a = jnp.concatenate([r[...] for r in ga_refs], axis=1)
    gate_b = jnp.concatenate([r[...] for r in gb_refs], axis=1)
    mixin = _sigmoid(gate_a) * y_a + _sigmoid(gate_b) * y_b
    mix = jnp.dot(mixin.astype(BF16), wo_ref[...], preferred_element_type=F32)
    x1 = x + mod_ref[0, 2:3, :] * mix
    x1_ref[...] = x1
    y = x1 * lax.rsqrt(jnp.mean(x1 * x1, axis=-1, keepdims=True) + RMS_EPS) * n2_ref[...]
    h2 = y * (1.0 + mod_ref[0, 4:5, :]) + mod_ref[0, 3:4, :]
    h2_ref[...] = h2
    hi = h2.astype(BF16)
    lo = (h2 - hi.astype(F32)).astype(BF16)
    lg_ref[...] = (jnp.dot(hi, wrh_ref[...], preferred_element_type=F32)
                   + (jnp.dot(hi, wrl_ref[...], preferred_element_type=F32)
                      + jnp.dot(lo, wrh_ref[...], preferred_element_type=F32))
                   + br_ref[...])


def merge_project(xp2, xs2, hm, at, proj, mod3, p_a, p_b, w_out, norm2_w, wr_hi, wr_lo, br, seq, tm=256):
    n, d = hm.shape
    n0 = xp2.shape[0] // tm
    n1 = xs2.shape[0] // tm
    per_seq = seq // tm
    const = dict(pipeline_mode=pl.Buffered(1))
    gw = PROJ_TN
    n_gate = d // gw

    def gate_specs(col0):
        return [pl.BlockSpec((tm, gw), lambda i, t=t: (i, col0 // gw + t)) for t in range(n_gate)]

    return pl.pallas_call(
        functools.partial(_merge_kernel, n0, n_gate),
        grid=(n0 + n1,),
        in_specs=[pl.BlockSpec((tm, d), lambda i: (jnp.minimum(i, n0 - 1), 0)),
                  pl.BlockSpec((tm, d), lambda i: (jnp.maximum(i - n0, 0), 0)),
                  pl.BlockSpec((tm, d), lambda i: (i, 0)),
                  pl.BlockSpec((tm, at.shape[1]), lambda i: (i, 0)),
                  *gate_specs(COL_GATE_A), *gate_specs(COL_GATE_B),
                  pl.BlockSpec((1, 6, d), lambda i: (i // per_seq, 0, 0)),
                  pl.BlockSpec(p_a.shape, lambda i: (0, 0), **const),
                  pl.BlockSpec(p_b.shape, lambda i: (0, 0), **const),
                  pl.BlockSpec(w_out.shape, lambda i: (0, 0), **const),
                  pl.BlockSpec((1, d), lambda i: (0, 0)),
                  pl.BlockSpec(wr_hi.shape, lambda i: (0, 0), **const),
                  pl.BlockSpec(wr_lo.shape, lambda i: (0, 0), **const),
                  pl.BlockSpec((1, ROUTER_COLS), lambda i: (0, 0))],
        out_specs=[pl.BlockSpec((tm, d), lambda i: (i, 0)),
                   pl.BlockSpec((tm, d), lambda i: (i, 0)),
                   pl.BlockSpec((tm, ROUTER_COLS), lambda i: (i, 0))],
        out_shape=[jax.ShapeDtypeStruct((n, d), F32),
                   jax.ShapeDtypeStruct((n, d), F32),
                   jax.ShapeDtypeStruct((n, ROUTER_COLS), F32)],
        compiler_params=_cparams(("parallel",)),
        name="merge_project",
    )(xp2, xs2, hm, at, *([proj] * (2 * n_gate)), mod3, p_a, p_b, w_out, norm2_w.reshape(1, d), wr_hi, wr_lo, br)


def route(logits, tb):
    n = logits.shape[0]
    g_logits = logits[:, :N_GROUPS]
    e_logits = logits[:, N_GROUPS:N_GROUPS + N_EXPERTS].reshape(n, N_GROUPS, EXPERTS_PER_GROUP)
    g_idx = jnp.argmax(g_logits, axis=-1)
    g_w = jnp.take_along_axis(jax.nn.softmax(g_logits, axis=-1), g_idx[:, None], axis=-1)
    e_sel = jnp.take_along_axis(e_logits, g_idx[:, None, None], axis=1)[:, 0]
    top_v, top_i = lax.top_k(e_sel, TOP_K)
    weights = g_w * jax.nn.softmax(top_v, axis=-1)
    expert = (g_idx[:, None] * EXPERTS_PER_GROUP + top_i).astype(jnp.int32)
    a = n * TOP_K
    flat_e = expert.reshape(a)
    e_ids = jnp.arange(N_EXPERTS, dtype=jnp.int32)
    counts = jnp.sum(flat_e[:, None] == e_ids[None, :], axis=0, dtype=jnp.int32)
    padded = (counts + tb - 1) // tb * tb
    pad_end = jnp.cumsum(padded)
    filler_e = jnp.repeat(e_ids, tb)
    filler_j = jnp.tile(jnp.arange(tb, dtype=jnp.int32), N_EXPERTS)
    filler_key = jnp.where(filler_j < (padded - counts)[filler_e], 2 * filler_e + 1, 2 * N_EXPERTS + 1)
    keys = jnp.concatenate([2 * flat_e, filler_key])
    ids = jnp.arange(a, dtype=jnp.int32)
    filler0 = jnp.zeros((N_EXPERTS * tb,), jnp.int32)
    tok_src = jnp.concatenate([ids // TOP_K, filler0])
    out_src = jnp.concatenate([(ids % TOP_K) * n + ids // TOP_K, filler0])
    sorted_keys, row_tok, row_out = lax.sort((keys, tok_src, out_src), num_keys=1)
    r = a + N_EXPERTS * tb
    n_blocks = r // tb
    block_start = jnp.arange(n_blocks, dtype=jnp.int32) * tb
    block_expert = jnp.minimum(jnp.sum(block_start[:, None] >= pad_end[None, :], axis=1), N_EXPERTS - 1).astype(jnp.int32)
    n_used = (pad_end[-1] // tb).astype(jnp.int32).reshape(1)
    later = jnp.logical_and(e_ids[None, :] > e_ids[:, None], (counts > 0)[None, :])
    next_of = jnp.min(jnp.where(later, e_ids[None, :], N_EXPERTS), axis=1)
    next_expert = jnp.where(next_of < N_EXPERTS, next_of, -1)[block_expert].astype(jnp.int32)
    pos = jnp.arange(r, dtype=jnp.int32)
    inv_key = jnp.where(sorted_keys % 2 == 0, row_out, a + pos)
    _, inv = lax.sort((inv_key, pos), num_keys=1)
    return (block_expert, next_expert, n_used, row_tok), inv[:a], weights


ROW_DMA_UNROLL = 32

def _issue_row_gather(src_hbm, dst_buf, sem, index_of, n_rows):
    def body(j, carry):
        pltpu.make_async_copy(src_hbm.at[pl.ds(index_of(j), 1), :], dst_buf.at[pl.ds(j, 1), :], sem).start()
        return carry
    lax.fori_loop(0, n_rows, body, 0, unroll=ROW_DMA_UNROLL)


def _wait_row_gather(src_hbm, dst_buf, sem):
    pltpu.make_async_copy(src_hbm.at[pl.ds(0, dst_buf.shape[0]), :], dst_buf, sem).wait()


def _round_rows_to_bf16(src_ref, dst_ref, rows=256):
    def body(t, carry):
        r = pl.ds(pl.multiple_of(t * rows, rows), rows)
        dst_ref[r, :] = src_ref[r, :].astype(BF16)
        return carry
    lax.fori_loop(0, src_ref.shape[0] // rows, body, 0)


def _expert_kernel(be_ref, nxt_ref, nused_ref, rtok_ref, h2_hbm, wg_hbm, wu_hbm, wd_hbm, out_ref,
                   xbuf, stg_g, stg_u, stg_d, wg_b, wu_b, wd_b, gsem, wsem):
    tb = xbuf.shape[1]
    i = pl.program_id(0)
    nused = nused_ref[0]
    slot = i % 2
    e = be_ref[i]
    first_of_expert = jnp.logical_or(i == 0, e != be_ref[jnp.maximum(i - 1, 0)])

    def issue(blk, s):
        _issue_row_gather(h2_hbm, xbuf.at[s], gsem.at[s], lambda j: rtok_ref[blk * tb + j], tb)

    def weight_copies(expert):
        return [pltpu.make_async_copy(src.at[expert], dst, wsem.at[k])
                for k, (src, dst) in enumerate(((wg_hbm, stg_g), (wu_hbm, stg_u), (wd_hbm, stg_d)))]

    @pl.when(jnp.logical_and(i == 0, nused > 0))
    def _():
        issue(0, 0)

    @pl.when(i + 1 < nused)
    def _():
        issue(i + 1, 1 - slot)

    @pl.when(i < nused)
    def _():
        @pl.when(first_of_expert)
        def _():
            @pl.when(i == 0)
            def _():
                for c in weight_copies(e):
                    c.start()

            for c in weight_copies(e):
                c.wait()
            for stg, wb in ((stg_g, wg_b), (stg_u, wu_b), (stg_d, wd_b)):
                _round_rows_to_bf16(stg, wb)
            nxt = nxt_ref[i]

            @pl.when(nxt >= 0)
            def _():
                for c in weight_copies(nxt):
                    c.start()

        _wait_row_gather(h2_hbm, xbuf.at[slot], gsem.at[slot])
        x = xbuf[slot].astype(BF16)
        g = jnp.dot(x, wg_b[...], preferred_element_type=F32)
        u = jnp.dot(x, wu_b[...], preferred_element_type=F32)
        hdn = (g * _sigmoid(g) * u).astype(BF16)
        out_ref[...] = jnp.dot(hdn, wd_b[...], preferred_element_type=F32)

    @pl.when(i >= nused)
    def _():
        out_ref[...] = jnp.zeros_like(out_ref)


def expert_ffn(h2, tables, wg, wu, wd, tb=MOE_ROWS):
    block_expert, next_expert, n_used, row_tok = tables
    n, d = h2.shape
    nb = block_expert.shape[0]
    de = wg.shape[2]
    any_spec = pl.BlockSpec(memory_space=pl.ANY)
    grid_spec = pltpu.PrefetchScalarGridSpec(
        num_scalar_prefetch=4,
        grid=(nb,),
        in_specs=[any_spec, any_spec, any_spec, any_spec],
        out_specs=pl.BlockSpec((tb, d), lambda i, *_: (i, 0)),
        scratch_shapes=[pltpu.VMEM((2, tb, d), F32),
                        pltpu.VMEM((d, de), F32), pltpu.VMEM((d, de), F32), pltpu.VMEM((de, d), F32),
                        pltpu.VMEM((d, de), BF16), pltpu.VMEM((d, de), BF16), pltpu.VMEM((de, d), BF16),
                        pltpu.SemaphoreType.DMA((2,)), pltpu.SemaphoreType.DMA((3,))],
    )
    return pl.pallas_call(
        _expert_kernel,
        grid_spec=grid_spec,
        out_shape=jax.ShapeDtypeStruct((nb * tb, d), F32),
        compiler_params=_cparams(("arbitrary",), EXPERT_VMEM_LIMIT),
        name="expert_ffn",
    )(block_expert, next_expert, n_used, row_tok, h2, wg, wu, wd)


def _final_kernel(n_tok, tile0, inv_ref, x1_ref, rw_ref, mod_ref, w_ref, ys_hbm, o_ref, buf, sem):
    tm = x1_ref.shape[0]
    i = pl.program_id(0)
    slot = i % 2

    def issue(tile, s):
        base = (tile0 + tile) * tm
        for k in range(TOP_K):
            _issue_row_gather(ys_hbm, buf.at[s, k], sem.at[s], lambda j, k=k: inv_ref[k * n_tok + base + j], tm)

    @pl.when(i == 0)
    def _():
        issue(0, 0)

    @pl.when(i + 1 < pl.num_programs(0))
    def _():
        issue(i + 1, 1 - slot)

    for k in range(TOP_K):
        _wait_row_gather(ys_hbm, buf.at[slot, k], sem.at[slot])
    rw = rw_ref[...]
    moe = buf[slot, 0] * rw[:, 0:1]
    for k in range(1, TOP_K):
        moe = moe + buf[slot, k] * rw[:, k:k + 1]
    x = x1_ref[...] + mod_ref[0, 5:6, :] * moe
    o_ref[...] = x * lax.rsqrt(jnp.mean(x * x, axis=-1, keepdims=True) + RMS_EPS) * w_ref[...]


def final_norm(x1, ys, inv, route_w, mod3, final_w, row0, rows, seq, tm=512):
    n_tok, d = x1.shape
    off = row0 // tm
    per_seq = seq // tm
    grid_spec = pltpu.PrefetchScalarGridSpec(
        num_scalar_prefetch=1,
        grid=(rows // tm,),
        in_specs=[pl.BlockSpec((tm, d), lambda i, inv: (off + i, 0)),
                  pl.BlockSpec((tm, TOP_K), lambda i, inv: (off + i, 0)),
                  pl.BlockSpec((1, 6, d), lambda i, inv: ((off + i) // per_seq, 0, 0)),
                  pl.BlockSpec((1, d), lambda i, inv: (0, 0)),
                  pl.BlockSpec(memory_space=pl.ANY)],
        out_specs=pl.BlockSpec((tm, d), lambda i, inv: (i, 0)),
        scratch_shapes=[pltpu.VMEM((2, TOP_K, tm, d), F32), pltpu.SemaphoreType.DMA((2,))],
    )
    return pl.pallas_call(
        functools.partial(_final_kernel, n_tok, off),
        grid_spec=grid_spec,
        out_shape=jax.ShapeDtypeStruct((rows, d), F32),
        compiler_params=_cparams(("arbitrary",)),
        name="final_norm",
    )(inv, x1, route_w, mod3, final_w.reshape(1, d), ys)


def kernel(x_prompt, x_sample, c_prompt, c_sample, w_ada, b_ada, norm1_w, w_in, b_in, mlstm_gate_b, conv_w, conv_b, mlstm_norm_w, p_a, p_b, w_out, norm2_w, w_router_group, b_router_group, w_router_expert, b_router_expert, w_expert_gate, w_expert_up, w_expert_down, final_norm_w):
    bp, seq, d = x_prompt.shape
    bs = x_sample.shape[0]
    bt = bp + bs
    n = bt * seq
    layer = 0

    w_in_bf = w_in.astype(BF16)
    w_in_a = slice_cols_bf16(w_in_bf, layer, _SRC_A0, PROJ_A_COLS)
    b_in_a = b_in[layer, _SRC_A0:]
    b_in_m = b_in[layer, :PROJ_M_COLS] + jnp.pad(mlstm_gate_b[layer], (COL_GATE_M, PROJ_M_COLS - _SRC_A0))
    wr = jnp.concatenate([w_router_group[layer], w_router_expert[layer],
                          jnp.zeros((d, ROUTER_COLS - N_GROUPS - N_EXPERTS), F32)], axis=1)
    br = jnp.concatenate([b_router_group[layer], b_router_expert[layer],
                          jnp.zeros((ROUTER_COLS - N_GROUPS - N_EXPERTS,), F32)]).reshape(1, ROUTER_COLS)
    wr_hi = wr.astype(BF16)
    wr_lo = (wr - wr_hi.astype(F32)).astype(BF16)
    slopes = 2.0 ** (-8.0 * jnp.arange(1, A_HEADS + 1, dtype=F32) / A_HEADS)

    c_all = jnp.concatenate([c_prompt, c_sample, jnp.zeros((16 - bt, d), F32)], axis=0)
    mod3 = ada_modulation(c_all, w_ada[layer], b_ada[layer])[:bt].reshape(bt, 6, d)

    h = norm_modulate(x_prompt, x_sample, mod3, norm1_w[layer])
    h2d = h.reshape(n, d)
    proj_m = matmul_bias(h2d, w_in_bf, b_in_m, PROJ_M_COLS, "in_projection_mlstm", layer=layer)
    proj_a = matmul_bias(h2d, w_in_a, b_in_a, PROJ_A_COLS, "in_projection")
    proj_m3 = proj_m.reshape(bt, seq, PROJ_M_COLS)
    proj_a3 = proj_a.reshape(bt, seq, PROJ_A_COLS)

    nc = seq // M_CHUNK
    gates = proj_m3[:, :, COL_GATE_M:COL_GATE_M + GATE_M_W].reshape(bt, nc, M_CHUNK, 4, M_HEADS)
    gates = gates.transpose(0, 4, 3, 1, 2)
    hm = mlstm_branch(proj_m3, gates, conv_w[layer], conv_b[layer], mlstm_norm_w[layer])
    at = attention_branch(proj_a3, slopes)

    x1, h2, logits = merge_project(
        x_prompt.reshape(bp * seq, d), x_sample.reshape(bs * seq, d), hm.reshape(n, -1), at.reshape(n, -1), proj_a,
        mod3, p_a[layer].astype(BF16), p_b[layer].astype(BF16), w_out[layer].astype(BF16), norm2_w[layer],
        wr_hi, wr_lo, br, seq)

    tables, inv, route_w = route(logits, MOE_ROWS)
    ys = expert_ffn(h2, tables, w_expert_gate[layer], w_expert_up[layer], w_expert_down[layer])

    y_p = final_norm(x1, ys, inv, route_w, mod3, final_norm_w, 0, bp * seq, seq)
    y_s = final_norm(x1, ys, inv, route_w, mod3, final_norm_w, bp * seq, bs * seq, seq)
    return (y_p.reshape(bp, seq, d), y_s.reshape(bs, seq, d))
```

```python
import functools

import jax
import jax.numpy as jnp
from jax import lax
from jax.experimental import pallas as pl
from jax.experimental.pallas import tpu as pltpu

F32 = jnp.float32
BF16 = jnp.bfloat16

D_MODEL = 2048
RMS_EPS = 1e-6
M_HEADS = 8
M_QK_DIM = 128
M_V_DIM = 256
M_CHUNK = 128
A_GROUPS = ((128, 1), (512, 4), (2048, 16))
A_SLOTS = 4
A_HEADS = A_SLOTS * len(A_GROUPS)
A_HEAD_DIM = 128
N_GROUPS = 4
EXPERTS_PER_GROUP = 8
N_EXPERTS = N_GROUPS * EXPERTS_PER_GROUP
TOP_K = 2
D_EXPERT = 1024

M_QK_W = M_HEADS * M_QK_DIM
M_V_W = M_HEADS * M_V_DIM
A_W = A_HEADS * A_HEAD_DIM
PROJ_TN = 512
COL_Q_M = 0
COL_K_M = M_QK_W
COL_V_M = 2 * M_QK_W
COL_O_M = 2 * M_QK_W + M_V_W
COL_GATE_M = 2 * M_QK_W + 2 * M_V_W
GATE_M_W = 4 * M_HEADS
PROJ_M_COLS = COL_GATE_M + PROJ_TN
_SRC_A0 = COL_GATE_M + GATE_M_W
COL_Q_A = 0
COL_K_A = A_W
COL_V_A = 2 * A_W
COL_GATE_A = 3 * A_W
COL_GATE_B = 3 * A_W + D_MODEL
PROJ_A_COLS = 3 * A_W + 2 * D_MODEL

LANES = 128
MOE_ROWS = 256
ROUTER_COLS = 128
V7X_VMEM_BYTES = 64 * 1024 * 1024
VMEM_LIMIT = V7X_VMEM_BYTES * 7 // 8
EXPERT_VMEM_LIMIT = V7X_VMEM_BYTES * 15 // 16


def _sigmoid(x):
    return 1.0 / (1.0 + jnp.exp(-x))


def _cparams(sem, vmem=VMEM_LIMIT):
    return pltpu.CompilerParams(dimension_semantics=sem, vmem_limit_bytes=vmem)


def _ada_kernel(c_ref, w_ref, b_ref, o_ref):
    c = c_ref[...]
    a = (c * _sigmoid(c)).astype(BF16)
    o_ref[...] = jnp.dot(a, w_ref[...].astype(BF16), preferred_element_type=F32) + b_ref[...]


def ada_modulation(c, w_ada, b_ada):
    rows, d = c.shape
    n = w_ada.shape[1]
    tn = 1024
    return pl.pallas_call(
        _ada_kernel,
        grid=(n // tn,),
        in_specs=[pl.BlockSpec((rows, d), lambda j: (0, 0)),
                  pl.BlockSpec((d, tn), lambda j: (0, j)),
                  pl.BlockSpec((1, tn), lambda j: (0, j))],
        out_specs=pl.BlockSpec((rows, tn), lambda j: (0, j)),
        out_shape=jax.ShapeDtypeStruct((rows, n), F32),
        compiler_params=_cparams(("parallel",)),
        name="ada_modulation",
    )(c, w_ada, b_ada.reshape(1, n))


def _norm_mod_kernel(nb0, xp_ref, xs_ref, mod_ref, w_ref, o_ref):
    b = pl.program_id(0)
    x = jnp.where(b < nb0, xp_ref[0], xs_ref[0])
    y = x * lax.rsqrt(jnp.mean(x * x, axis=-1, keepdims=True) + RMS_EPS) * w_ref[...]
    o_ref[0] = (y * (1.0 + mod_ref[0, 1:2, :]) + mod_ref[0, 0:1, :]).astype(o_ref.dtype)


def norm_modulate(xp, xs, mod3, norm_w, ts=1024):
    nb0, s, d = xp.shape
    nb1 = xs.shape[0]
    return pl.pallas_call(
        functools.partial(_norm_mod_kernel, nb0),
        grid=(nb0 + nb1, s // ts),
        in_specs=[pl.BlockSpec((1, ts, d), lambda b, t: (jnp.minimum(b, nb0 - 1), jnp.where(b < nb0, t, s // ts - 1), 0)),
                  pl.BlockSpec((1, ts, d), lambda b, t: (jnp.maximum(b - nb0, 0), jnp.where(b < nb0, 0, t), 0)),
                  pl.BlockSpec((1, 6, d), lambda b, t: (b, 0, 0)),
                  pl.BlockSpec((1, d), lambda b, t: (0, 0))],
        out_specs=pl.BlockSpec((1, ts, d), lambda b, t: (b, t, 0)),
        out_shape=jax.ShapeDtypeStruct((nb0 + nb1, s, d), BF16),
        compiler_params=_cparams(("parallel", "parallel")),
        name="norm1_modulate",
    )(xp, xs, mod3, norm_w.reshape(1, d))


def _mm_bias_kernel(a_ref, w_ref, b_ref, o_ref):
    o_ref[...] = jnp.dot(a_ref[...], w_ref[...], preferred_element_type=F32) + b_ref[...]


def _slice_cols_kernel(col0, w_ref, o_ref):
    o_ref[...] = w_ref[:, col0:col0 + o_ref.shape[1]].astype(o_ref.dtype)


def slice_cols_bf16(w, layer, col0, n, tr=256):
    _, k, n_all = w.shape
    return pl.pallas_call(
        functools.partial(_slice_cols_kernel, col0),
        grid=(k // tr,),
        in_specs=[pl.BlockSpec((None, tr, n_all), lambda i: (layer, i, 0))],
        out_specs=pl.BlockSpec((tr, n), lambda i: (i, 0)),
        out_shape=jax.ShapeDtypeStruct((k, n), BF16),
        compiler_params=_cparams(("parallel",)),
        name="slice_cols_bf16",
    )(w)


def matmul_bias(a, w, b, n, name, layer=None, tm=2048, tn=PROJ_TN):
    m, k = a.shape
    if layer is None:
        w_spec = pl.BlockSpec((k, tn), lambda i, j: (0, j))
    else:
        w_spec = pl.BlockSpec((None, k, tn), lambda i, j: (layer, 0, j))
    return pl.pallas_call(
        _mm_bias_kernel,
        grid=(m // tm, n // tn),
        in_specs=[pl.BlockSpec((tm, k), lambda i, j: (i, 0)),
                  w_spec,
                  pl.BlockSpec((1, tn), lambda i, j: (0, j))],
        out_specs=pl.BlockSpec((tm, tn), lambda i, j: (i, j)),
        out_shape=jax.ShapeDtypeStruct((m, n), F32),
        compiler_params=_cparams(("parallel", "parallel")),
        name=name,
    )(a, w, b.reshape(1, -1))


def _lane_scan(x, op, fill, reverse):
    n = x.shape[-1]
    axis = x.ndim - 1
    lane = lax.broadcasted_iota(jnp.int32, x.shape, axis)
    k = 1
    while k < n:
        if reverse:
            x = op(x, jnp.where(lane < n - k, pltpu.roll(x, n - k, axis), fill))
        else:
            x = op(x, jnp.where(lane >= k, pltpu.roll(x, k, axis), fill))
        k *= 2
    return x


def _conv_silu_chunk(x_ref, w_ref, b_ref, c, n_chunks):
    L = M_CHUNK
    s = n_chunks * L
    r0 = pl.multiple_of(c * L, L)
    x = x_ref[0, pl.ds(r0, L), :]
    prev_row = x_ref[0, pl.ds(jnp.maximum(r0 - 1, 0), 1), :]
    next_row = x_ref[0, pl.ds(jnp.minimum(r0 + L, s - 1), 1), :]
    prev_row = jnp.where(c > 0, prev_row, 0.0)
    next_row = jnp.where(c < n_chunks - 1, next_row, 0.0)
    rows = lax.broadcasted_iota(jnp.int32, x.shape, 0)
    x_prev = jnp.where(rows == 0, prev_row, pltpu.roll(x, 1, 0))
    x_next = jnp.where(rows == L - 1, next_row, pltpu.roll(x, L - 1, 0))
    y = b_ref[...] + x_prev * w_ref[0:1, :] + x * w_ref[1:2, :] + x_next * w_ref[2:3, :]
    return y * _sigmoid(y)


def _mlstm_kernel(q_ref, k_ref, v_ref, o_ref, g_ref, cwq_ref, cwk_ref, cbq_ref, cbk_ref, nw_ref, out_ref,
                  qs_ref, kt_ref, hf_ref, hr_ref, c_ref,
                  u_ref, negm_ref, ib_ref, wk_ref, decay_ref, m0_ref, m1_ref):
    L = M_CHUNK
    dk = M_QK_DIM
    dv = M_V_DIM
    hp, nc = kt_ref.shape[0], kt_ref.shape[1]

    def prep(c, carry):
        q = _conv_silu_chunk(q_ref, cwq_ref, cbq_ref, c, nc) * (M_QK_DIM ** -0.5)
        qs_ref[c] = q.astype(BF16)
        k = _conv_silu_chunk(k_ref, cwk_ref, cbk_ref, c, nc)
        for hh in range(hp):
            kt_ref[hh, c] = k[:, hh * dk:(hh + 1) * dk].T
        return carry

    lax.fori_loop(0, nc, prep, 0)

    for hh in range(hp):
        for d in range(2):
            rev = d == 1
            i_pre = g_ref[0, hh, 2 * d]
            f_pre = g_ref[0, hh, 2 * d + 1]
            log_f = -(jnp.maximum(-f_pre, 0.0) + jnp.log1p(jnp.exp(-jnp.abs(f_pre))))
            b = _lane_scan(log_f, jnp.add, 0.0, rev)
            a = jnp.broadcast_to(b[:, 0:1] if rev else b[:, L - 1:L], (nc, L))
            g = a - b + i_pre
            g_max = jnp.broadcast_to(jnp.max(g, axis=1, keepdims=True), (nc, L))
            m = jnp.zeros((1, L), F32)
            for c in (range(nc - 1, -1, -1) if rev else range(nc)):
                m0_ref[hh, d, c:c + 1, :] = m
                m = jnp.maximum(a[c:c + 1, :] + m, g_max[c:c + 1, :])
                m1_ref[hh, d, c:c + 1, :] = m
            m0 = m0_ref[hh, d]
            m1 = m1_ref[hh, d]
            ib = i_pre - b
            m_t = jnp.maximum(b + m0, b + _lane_scan(ib, jnp.maximum, -jnp.inf, rev))
            ib_ref[hh, d] = ib
            u_ref[hh, d] = b - m_t
            negm_ref[hh, d] = -m_t
            wk_ref[hh, d] = jnp.exp(g - m1)
            decay_ref[hh, d] = jnp.exp(a + m0 - m1)

    t_idx = lax.broadcasted_iota(jnp.int32, (L, L), 0)
    s_idx = lax.broadcasted_iota(jnp.int32, (L, L), 1)
    ones_ext = jnp.ones((L, LANES), BF16)

    def chunk_step(hh, d, c):
        r0 = pl.multiple_of(c * L, L)
        q = qs_ref[c, :, hh * dk:(hh + 1) * dk]
        kt = kt_ref[hh, c]
        v_ext = jnp.concatenate([v_ref[0, pl.ds(r0, L), hh * dv:(hh + 1) * dv].astype(BF16), ones_ext], axis=1)

        def row(ref):
            return ref[hh, d, pl.ds(c, 1), :]

        umat = jnp.broadcast_to(row(u_ref), (L, L)).T
        nmat = jnp.broadcast_to(row(negm_ref), (L, L)).T
        causal = (s_idx <= t_idx) if d == 0 else (s_idx >= t_idx)
        w_intra = jnp.where(causal, jnp.exp(umat + row(ib_ref)), 0.0)
        w_inter = jnp.exp(umat + row(m0_ref))
        s_qk = jnp.dot(q, kt.astype(BF16), preferred_element_type=F32)
        c_ext = c_ref[hh, d]
        lhs = jnp.concatenate([(w_intra * s_qk).astype(BF16), (q.astype(F32) * w_inter).astype(BF16)], axis=1)
        rhs = jnp.concatenate([v_ext, c_ext.astype(BF16)], axis=0)
        num = jnp.dot(lhs, rhs, preferred_element_type=F32)
        r = 1.0 / jnp.maximum(jnp.abs(num[:, dv:]), jnp.exp(nmat))
        h = num[:, :dv] * jnp.concatenate([r] * (dv // LANES), axis=1)

        upd = jnp.dot((kt * row(wk_ref)).astype(BF16), v_ext, preferred_element_type=F32)
        decay = jnp.broadcast_to(row(decay_ref), (dk, LANES))
        c_ref[hh, d] = jnp.concatenate([decay] * (c_ext.shape[1] // LANES), axis=1) * c_ext + upd
        return h

    def finish(hh, c, hs):
        rows = pl.ds(pl.multiple_of(c * L, L), L)
        cols = slice(hh * dv, (hh + 1) * dv)
        y = hs * lax.rsqrt(jnp.mean(hs * hs, axis=-1, keepdims=True) + RMS_EPS) * nw_ref[:, cols]
        out_ref[0, rows, cols] = (y * _sigmoid(o_ref[0, rows, cols])).astype(out_ref.dtype)

    def first_half(j, carry):
        cf, cr = j, nc - 1 - j
        for hh in range(hp):
            hf_ref[hh, pl.ds(pl.multiple_of(cf * L, L), L), :] = chunk_step(hh, 0, cf)
            hr_ref[hh, pl.ds(pl.multiple_of((cr - nc // 2) * L, L), L), :] = chunk_step(hh, 1, cr)
        return carry

    def second_half(j, carry):
        cf, cr = j, nc - 1 - j
        for hh in range(hp):
            finish(hh, cf, chunk_step(hh, 0, cf) + hr_ref[hh, pl.ds(pl.multiple_of((cf - nc // 2) * L, L), L), :])
            finish(hh, cr, hf_ref[hh, pl.ds(pl.multiple_of(cr * L, L), L), :] + chunk_step(hh, 1, cr))
        return carry

    c_ref[...] = jnp.zeros_like(c_ref)
    lax.fori_loop(0, nc // 2, first_half, 0, unroll=4)
    lax.fori_loop(nc // 2, nc, second_half, 0, unroll=4)


def mlstm_branch(proj3, gates, conv_w, conv_b, norm_w, hp=2):
    bsz, s, _ = proj3.shape
    L = M_CHUNK
    nc = s // L
    dk, dv = M_QK_DIM, M_V_DIM
    assert dk == L and nc % 2 == 0 and M_HEADS % hp == 0
    wq, wv = hp * dk, hp * dv
    k_off = M_HEADS * dk // wq
    return pl.pallas_call(
        _mlstm_kernel,
        grid=(bsz, M_HEADS // hp),
        in_specs=[pl.BlockSpec((1, s, wq), lambda b, h: (b, 0, COL_Q_M // wq + h)),
                  pl.BlockSpec((1, s, wq), lambda b, h: (b, 0, COL_K_M // wq + h)),
                  pl.BlockSpec((1, s, wv), lambda b, h: (b, 0, COL_V_M // wv + h)),
                  pl.BlockSpec((1, s, wv), lambda b, h: (b, 0, COL_O_M // wv + h)),
                  pl.BlockSpec((1, hp, 4, nc, L), lambda b, h: (b, h, 0, 0, 0)),
                  pl.BlockSpec((3, wq), lambda b, h: (0, h)),
                  pl.BlockSpec((3, wq), lambda b, h: (0, k_off + h)),
                  pl.BlockSpec((1, wq), lambda b, h: (0, h)),
                  pl.BlockSpec((1, wq), lambda b, h: (0, k_off + h)),
                  pl.BlockSpec((1, wv), lambda b, h: (0, h))],
        out_specs=pl.BlockSpec((1, s, wv), lambda b, h: (b, 0, h)),
        out_shape=jax.ShapeDtypeStruct((bsz, s, M_HEADS * dv), BF16),
        scratch_shapes=[pltpu.VMEM((nc, L, wq), BF16),
                        pltpu.VMEM((hp, nc, dk, L), F32),
                        pltpu.VMEM((hp, s // 2, dv), F32),
                        pltpu.VMEM((hp, s // 2, dv), F32),
                        pltpu.VMEM((hp, 2, dk, dv + LANES), F32),
                        *([pltpu.VMEM((hp, 2, nc, L), F32)] * 7)],
        compiler_params=_cparams(("parallel", "parallel")),
        name="mlstm_branch",
    )(proj3, proj3, proj3, proj3, gates, conv_w, conv_w, conv_b.reshape(1, -1), conv_b.reshape(1, -1),
      norm_w.reshape(1, -1))


def _attn_kernel(slopes_ref, q0, q1, q2, k0, k1, k2, v0, v1, v2, out_ref, og_ref, lse_ref, bias_ref,
                 qd_ref, kd_ref, vd_ref, od_ref, ld_ref):
    s_len = out_ref.shape[1]
    dh = A_HEAD_DIM
    T = 128
    slot = pl.program_id(1)
    qs, ks, vs = (q0, q1, q2), (k0, k1, k2), (v0, v1, v2)

    for g, (window, dil) in enumerate(A_GROUPS):
        side = window // (2 * dil)
        u_len = s_len // dil
        nqb = u_len // T
        kw = min(T + 2 * side, u_len)
        slope = slopes_ref[g * A_SLOTS + slot] * float(dil)
        q_ref, k_ref, v_ref = qs[g], ks[g], vs[g]

        regroup = dil >= 8
        paired = regroup and nqb == 1 and dil % 2 == 0
        t_io = lax.broadcasted_iota(jnp.int32, (T, 2 * u_len if paired else kw), 0)
        k_io = lax.broadcasted_iota(jnp.int32, (T, 2 * u_len if paired else kw), 1)
        if paired:
            kw = 2 * u_len
            for case in range(2):
                rel = jnp.abs(t_io - (k_io - case * u_len))
                own = jnp.logical_and(k_io // u_len == case, rel <= side)
                bias_ref[g, case, :, :kw] = jnp.where(own, -slope * rel.astype(F32), -1e30)
        else:
            offsets = (0, side, kw - T) if nqb > 1 else (0,)
            for case, off in enumerate(offsets):
                rel = jnp.abs(t_io + off - k_io)
                bias_ref[g, case, :, :kw] = jnp.where(rel <= side, -slope * rel.astype(F32), -1e30)

        regroup = dil >= 8

        def residue_rows(r, dil=dil, u_len=u_len):
            return pl.ds(r, u_len, stride=dil), pl.ds(pl.multiple_of(r * u_len, u_len), u_len)

        if regroup:
            def gather_residue(r, carry, q_ref=q_ref, k_ref=k_ref, v_ref=v_ref):
                src, dst = residue_rows(r)
                qd_ref[dst, :] = (q_ref[0, src, :] * (dh ** -0.5)).astype(BF16)
                kd_ref[dst, :] = k_ref[0, src, :].astype(BF16)
                vd_ref[dst, :] = v_ref[0, src, :].astype(BF16)
                return carry

            lax.fori_loop(0, dil, gather_residue, 0, unroll=4)

        def block(idx, carry, g=g, dil=dil, side=side, u_len=u_len, nqb=nqb, kw=kw, regroup=regroup, paired=paired,
                  q_ref=q_ref, k_ref=k_ref, v_ref=v_ref):
            r = idx // nqb
            j = idx % nqb
            u0 = j * T
            ku0 = jnp.clip(u0 - side, 0, u_len - kw) if not paired else 0
            case = jnp.where(j == 0, 0, jnp.where(j == nqb - 1, 2, 1)) if nqb > 1 else 0
            if paired:
                case = r % 2
                q_rows = pl.ds(pl.multiple_of(r * u_len, T), T)
                k_rows = pl.ds(pl.multiple_of((r // 2) * kw, kw), kw)
                q, kk, vv = qd_ref[q_rows, :], kd_ref[k_rows, :], vd_ref[k_rows, :]
            elif regroup:
                q_rows = pl.ds(pl.multiple_of(r * u_len + u0, T), T)
                k_rows = pl.ds(pl.multiple_of(r * u_len + ku0, side), kw)
                q, kk, vv = qd_ref[q_rows, :], kd_ref[k_rows, :], vd_ref[k_rows, :]
            else:
                q_rows = pl.ds(r + u0 * dil, T, stride=dil) if dil > 1 else pl.ds(pl.multiple_of(u0, T), T)
                k_rows = pl.ds(r + ku0 * dil, kw, stride=dil) if dil > 1 else pl.ds(pl.multiple_of(ku0, side), kw)
                q = (q_ref[0, q_rows, :] * (dh ** -0.5)).astype(BF16)
                kk = k_ref[0, k_rows, :].astype(BF16)
                vv = v_ref[0, k_rows, :].astype(BF16)
            s = lax.dot_general(q, kk, (((1,), (1,)), ((), ())), preferred_element_type=F32)
            s = s + bias_ref[g, case, :, :kw]
            m = jnp.max(s, axis=1, keepdims=True)
            p = jnp.exp(s - m)
            den = jnp.sum(p, axis=1, keepdims=True)
            o = jnp.dot(p.astype(BF16), vv, preferred_element_type=F32) * (1.0 / den)
            lse = jnp.broadcast_to(m + jnp.log(den), (T, LANES))
            if regroup:
                od_ref[q_rows, :] = o
                ld_ref[q_rows, :] = lse
            else:
                og_ref[g, q_rows, :] = o
                lse_ref[g, q_rows, :] = lse
            return carry

        lax.fori_loop(0, dil * nqb, block, 0, unroll=8)

        if regroup:
            def scatter_residue(r, carry, g=g):
                dst, src = residue_rows(r)
                og_ref[g, dst, :] = od_ref[src, :]
                lse_ref[g, dst, :] = ld_ref[src, :]
                return carry

            lax.fori_loop(0, dil, scatter_residue, 0, unroll=4)

    rows_per = 256

    def merge(i, carry):
        rows = pl.ds(pl.multiple_of(i * rows_per, rows_per), rows_per)
        l0, l1, l2 = lse_ref[0, rows, :], lse_ref[1, rows, :], lse_ref[2, rows, :]
        mx = jnp.maximum(jnp.maximum(l0, l1), l2)
        e0, e1, e2 = jnp.exp(l0 - mx), jnp.exp(l1 - mx), jnp.exp(l2 - mx)
        inv = 1.0 / (e0 + e1 + e2)
        o = (e0 * inv) * og_ref[0, rows, :] + (e1 * inv) * og_ref[1, rows, :] + (e2 * inv) * og_ref[2, rows, :]
        out_ref[0, rows, :] = o.astype(out_ref.dtype)
        return carry

    lax.fori_loop(0, s_len // rows_per, merge, 0)


def attention_branch(proj3, slopes):
    bsz, s, _ = proj3.shape
    dh = A_HEAD_DIM

    def col(base, g):
        return lambda b, t, sl: (b, 0, base // dh + g * A_SLOTS + t)

    grid_spec = pltpu.PrefetchScalarGridSpec(
        num_scalar_prefetch=1,
        grid=(bsz, A_SLOTS),
        in_specs=[pl.BlockSpec((1, s, dh), col(base, g))
                  for base in (COL_Q_A, COL_K_A, COL_V_A) for g in range(len(A_GROUPS))],
        out_specs=pl.BlockSpec((1, s, dh), lambda b, t, sl: (b, 0, t)),
        scratch_shapes=[pltpu.VMEM((3, s, dh), F32), pltpu.VMEM((3, s, LANES), F32),
                        pltpu.VMEM((len(A_GROUPS), 3, 128, 256), F32),
                        pltpu.VMEM((s, dh), BF16), pltpu.VMEM((s, dh), BF16), pltpu.VMEM((s, dh), BF16),
                        pltpu.VMEM((s, dh), F32), pltpu.VMEM((s, LANES), F32)],
    )
    return pl.pallas_call(
        _attn_kernel,
        grid_spec=grid_spec,
        out_shape=jax.ShapeDtypeStruct((bsz, s, A_SLOTS * dh), BF16),
        compiler_params=_cparams(("parallel", "parallel")),
        name="dilated_attention",
    )(slopes, *([proj3] * 9))


def _merge_kernel(n0, n_gate, xp_ref, xs_ref, hm_ref, at_ref, *refs):
    ga_refs, gb_refs = refs[:n_gate], refs[n_gate:2 * n_gate]
    (mod_ref, pa_ref, pb_ref, wo_ref, n2_ref, wrh_ref, wrl_ref, br_ref, x1_ref, h2_ref, lg_ref) = refs[2 * n_gate:]
    i = pl.program_id(0)
    x = jnp.where(i < n0, xp_ref[...], xs_ref[...])
    y_a = jnp.dot(hm_ref[...], pa_ref[...], preferred_element_type=F32)
    y_b = jnp.dot(at_ref[...], pb_ref[...], preferred_element_type=F32)
    gate_a = jnp.concatenate([r[...] for r in ga_refs], axis=1)
    gate_b = jnp.concatenate([r[...] for r in gb_refs], axis=1)
    mixin = _sigmoid(gate_a) * y_a + _sigmoid(gate_b) * y_b
    mix = jnp.dot(mixin.astype(BF16), wo_ref[...], preferred_element_type=F32)
    x1 = x + mod_ref[0, 2:3, :] * mix
    x1_ref[...] = x1
    y = x1 * lax.rsqrt(jnp.mean(x1 * x1, axis=-1, keepdims=True) + RMS_EPS) * n2_ref[...]
    h2 = y * (1.0 + mod_ref[0, 4:5, :]) + mod_ref[0, 3:4, :]
    h2_ref[...] = h2
    hi = h2.astype(BF16)
    lo = (h2 - hi.astype(F32)).astype(BF16)
    lg_ref[...] = (jnp.dot(hi, wrh_ref[...], preferred_element_type=F32)
                   + (jnp.dot(hi, wrl_ref[...], preferred_element_type=F32)
                      + jnp.dot(lo, wrh_ref[...], preferred_element_type=F32))
                   + br_ref[...])


def merge_project(xp2, xs2, hm, at, proj, mod3, p_a, p_b, w_out, norm2_w, wr_hi, wr_lo, br, seq, tm=256):
    n, d = hm.shape
    n0 = xp2.shape[0] // tm
    n1 = xs2.shape[0] // tm
    per_seq = seq // tm
    const = dict(pipeline_mode=pl.Buffered(1))
    gw = PROJ_TN
    n_gate = d // gw

    def gate_specs(col0):
        return [pl.BlockSpec((tm, gw), lambda i, t=t: (i, col0 // gw + t)) for t in range(n_gate)]

    return pl.pallas_call(
        functools.partial(_merge_kernel, n0, n_gate),
        grid=(n0 + n1,),
        in_specs=[pl.BlockSpec((tm, d), lambda i: (jnp.minimum(i, n0 - 1), 0)),
                  pl.BlockSpec((tm, d), lambda i: (jnp.maximum(i - n0, 0), 0)),
                  pl.BlockSpec((tm, d), lambda i: (i, 0)),
                  pl.BlockSpec((tm, at.shape[1]), lambda i: (i, 0)),
                  *gate_specs(COL_GATE_A), *gate_specs(COL_GATE_B),
                  pl.BlockSpec((1, 6, d), lambda i: (i // per_seq, 0, 0)),
                  pl.BlockSpec(p_a.shape, lambda i: (0, 0), **const),
                  pl.BlockSpec(p_b.shape, lambda i: (0, 0), **const),
                  pl.BlockSpec(w_out.shape, lambda i: (0, 0), **const),
                  pl.BlockSpec((1, d), lambda i: (0, 0)),
                  pl.BlockSpec(wr_hi.shape, lambda i: (0, 0), **const),
                  pl.BlockSpec(wr_lo.shape, lambda i: (0, 0), **const),
                  pl.BlockSpec((1, ROUTER_COLS), lambda i: (0, 0))],
        out_specs=[pl.BlockSpec((tm, d), lambda i: (i, 0)),
                   pl.BlockSpec((tm, d), lambda i: (i, 0)),
                   pl.BlockSpec((tm, ROUTER_COLS), lambda i: (i, 0))],
        out_shape=[jax.ShapeDtypeStruct((n, d), F32),
                   jax.ShapeDtypeStruct((n, d), F32),
                   jax.ShapeDtypeStruct((n, ROUTER_COLS), F32)],
        compiler_params=_cparams(("parallel",)),
        name="merge_project",
    )(xp2, xs2, hm, at, *([proj] * (2 * n_gate)), mod3, p_a, p_b, w_out, norm2_w.reshape(1, d), wr_hi, wr_lo, br)


def route(logits, tb):
    n = logits.shape[0]
    g_logits = logits[:, :N_GROUPS]
    e_logits = logits[:, N_GROUPS:N_GROUPS + N_EXPERTS].reshape(n, N_GROUPS, EXPERTS_PER_GROUP)
    g_idx = jnp.argmax(g_logits, axis=-1)
    g_w = jnp.take_along_axis(jax.nn.softmax(g_logits, axis=-1), g_idx[:, None], axis=-1)
    e_sel = jnp.take_along_axis(e_logits, g_idx[:, None, None], axis=1)[:, 0]
    top_v, top_i = lax.top_k(e_sel, TOP_K)
    weights = g_w * jax.nn.softmax(top_v, axis=-1)
    expert = (g_idx[:, None] * EXPERTS_PER_GROUP + top_i).astype(jnp.int32)
    a = n * TOP_K
    flat_e = expert.reshape(a)
    e_ids = jnp.arange(N_EXPERTS, dtype=jnp.int32)
    counts = jnp.sum(flat_e[:, None] == e_ids[None, :], axis=0, dtype=jnp.int32)
    padded = (counts + tb - 1) // tb * tb
    pad_end = jnp.cumsum(padded)
    filler_e = jnp.repeat(e_ids, tb)
    filler_j = jnp.tile(jnp.arange(tb, dtype=jnp.int32), N_EXPERTS)
    filler_key = jnp.where(filler_j < (padded - counts)[filler_e], 2 * filler_e + 1, 2 * N_EXPERTS + 1)
    keys = jnp.concatenate([2 * flat_e, filler_key])
    ids = jnp.arange(a, dtype=jnp.int32)
    filler0 = jnp.zeros((N_EXPERTS * tb,), jnp.int32)
    tok_src = jnp.concatenate([ids // TOP_K, filler0])
    out_src = jnp.concatenate([(ids % TOP_K) * n + ids // TOP_K, filler0])
    sorted_keys, row_tok, row_out = lax.sort((keys, tok_src, out_src), num_keys=1)
    r = a + N_EXPERTS * tb
    n_blocks = r // tb
    block_start = jnp.arange(n_blocks, dtype=jnp.int32) * tb
    block_expert = jnp.minimum(jnp.sum(block_start[:, None] >= pad_end[None, :], axis=1), N_EXPERTS - 1).astype(jnp.int32)
    n_used = (pad_end[-1] // tb).astype(jnp.int32).reshape(1)
    later = jnp.logical_and(e_ids[None, :] > e_ids[:, None], (counts > 0)[None, :])
    next_of = jnp.min(jnp.where(later, e_ids[None, :], N_EXPERTS), axis=1)
    next_expert = jnp.where(next_of < N_EXPERTS, next_of, -1)[block_expert].astype(jnp.int32)
    pos = jnp.arange(r, dtype=jnp.int32)
    inv_key = jnp.where(sorted_keys % 2 == 0, row_out, a + pos)
    _, inv = lax.sort((inv_key, pos), num_keys=1)
    return (block_expert, next_expert, n_used, row_tok), inv[:a], weights


ROW_DMA_UNROLL = 64

def _issue_row_gather(src_hbm, dst_buf, sem, index_of, n_rows):
    def body(j, carry):
        pltpu.make_async_copy(src_hbm.at[pl.ds(index_of(j), 1), :], dst_buf.at[pl.ds(j, 1), :], sem).start()
        return carry
    lax.fori_loop(0, n_rows, body, 0, unroll=ROW_DMA_UNROLL)


def _wait_row_gather(src_hbm, dst_buf, sem):
    pltpu.make_async_copy(src_hbm.at[pl.ds(0, dst_buf.shape[0]), :], dst_buf, sem).wait()


def _round_rows_to_bf16(src_ref, dst_ref, rows=256):
    def body(t, carry):
        r = pl.ds(pl.multiple_of(t * rows, rows), rows)
        dst_ref[r, :] = src_ref[r, :].astype(BF16)
        return carry
    lax.fori_loop(0, src_ref.shape[0] // rows, body, 0)


def _expert_kernel(be_ref, nxt_ref, nused_ref, rtok_ref, h2_hbm, wg_hbm, wu_hbm, wd_hbm, out_ref,
                   xbuf, stg_g, stg_u, stg_d, wg_b, wu_b, wd_b, gsem, wsem):
    tb = xbuf.shape[1]
    i = pl.program_id(0)
    nused = nused_ref[0]
    slot = i % 2
    e = be_ref[i]
    first_of_expert = jnp.logical_or(i == 0, e != be_ref[jnp.maximum(i - 1, 0)])

    def issue(blk, s):
        _issue_row_gather(h2_hbm, xbuf.at[s], gsem.at[s], lambda j: rtok_ref[blk * tb + j], tb)

    def weight_copies(expert):
        return [pltpu.make_async_copy(src.at[expert], dst, wsem.at[k])
                for k, (src, dst) in enumerate(((wg_hbm, stg_g), (wu_hbm, stg_u), (wd_hbm, stg_d)))]

    @pl.when(jnp.logical_and(i == 0, nused > 0))
    def _():
        issue(0, 0)

    @pl.when(i + 1 < nused)
    def _():
        issue(i + 1, 1 - slot)

    @pl.when(i < nused)
    def _():
        @pl.when(first_of_expert)
        def _():
            @pl.when(i == 0)
            def _():
                for c in weight_copies(e):
                    c.start()

            for c in weight_copies(e):
                c.wait()
            for stg, wb in ((stg_g, wg_b), (stg_u, wu_b), (stg_d, wd_b)):
                _round_rows_to_bf16(stg, wb)
            nxt = nxt_ref[i]

            @pl.when(nxt >= 0)
            def _():
                for c in weight_copies(nxt):
                    c.start()

        _wait_row_gather(h2_hbm, xbuf.at[slot], gsem.at[slot])
        x = xbuf[slot].astype(BF16)
        g = jnp.dot(x, wg_b[...], preferred_element_type=F32)
        u = jnp.dot(x, wu_b[...], preferred_element_type=F32)
        hdn = (g * _sigmoid(g) * u).astype(BF16)
        out_ref[...] = jnp.dot(hdn, wd_b[...], preferred_element_type=F32)

    @pl.when(i >= nused)
    def _():
        out_ref[...] = jnp.zeros_like(out_ref)


def expert_ffn(h2, tables, wg, wu, wd, tb=MOE_ROWS):
    block_expert, next_expert, n_used, row_tok = tables
    n, d = h2.shape
    nb = block_expert.shape[0]
    de = wg.shape[2]
    any_spec = pl.BlockSpec(memory_space=pl.ANY)
    grid_spec = pltpu.PrefetchScalarGridSpec(
        num_scalar_prefetch=4,
        grid=(nb,),
        in_specs=[any_spec, any_spec, any_spec, any_spec],
        out_specs=pl.BlockSpec((tb, d), lambda i, *_: (i, 0)),
        scratch_shapes=[pltpu.VMEM((2, tb, d), F32),
                        pltpu.VMEM((d, de), F32), pltpu.VMEM((d, de), F32), pltpu.VMEM((de, d), F32),
                        pltpu.VMEM((d, de), BF16), pltpu.VMEM((d, de), BF16), pltpu.VMEM((de, d), BF16),
                        pltpu.SemaphoreType.DMA((2,)), pltpu.SemaphoreType.DMA((3,))],
    )
    return pl.pallas_call(
        _expert_kernel,
        grid_spec=grid_spec,
        out_shape=jax.ShapeDtypeStruct((nb * tb, d), F32),
        compiler_params=_cparams(("arbitrary",), EXPERT_VMEM_LIMIT),
        name="expert_ffn",
    )(block_expert, next_expert, n_used, row_tok, h2, wg, wu, wd)


def _final_kernel(n_tok, tile0, inv_ref, x1_ref, rw_ref, mod_ref, w_ref, ys_hbm, o_ref, buf, sem):
    tm = x1_ref.shape[0]
    i = pl.program_id(0)
    slot = i % 2

    def issue(tile, s):
        base = (tile0 + tile) * tm
        for k in range(TOP_K):
            _issue_row_gather(ys_hbm, buf.at[s, k], sem.at[s], lambda j, k=k: inv_ref[k * n_tok + base + j], tm)

    @pl.when(i == 0)
    def _():
        issue(0, 0)

    @pl.when(i + 1 < pl.num_programs(0))
    def _():
        issue(i + 1, 1 - slot)

    for k in range(TOP_K):
        _wait_row_gather(ys_hbm, buf.at[slot, k], sem.at[slot])
    rw = rw_ref[...]
    moe = buf[slot, 0] * rw[:, 0:1]
    for k in range(1, TOP_K):
        moe = moe + buf[slot, k] * rw[:, k:k + 1]
    x = x1_ref[...] + mod_ref[0, 5:6, :] * moe
    o_ref[...] = x * lax.rsqrt(jnp.mean(x * x, axis=-1, keepdims=True) + RMS_EPS) * w_ref[...]


def final_norm(x1, ys, inv, route_w, mod3, final_w, row0, rows, seq, tm=512):
    n_tok, d = x1.shape
    off = row0 // tm
    per_seq = seq // tm
    grid_spec = pltpu.PrefetchScalarGridSpec(
        num_scalar_prefetch=1,
        grid=(rows // tm,),
        in_specs=[pl.BlockSpec((tm, d), lambda i, inv: (off + i, 0)),
                  pl.BlockSpec((tm, TOP_K), lambda i, inv: (off + i, 0)),
                  pl.BlockSpec((1, 6, d), lambda i, inv: ((off + i) // per_seq, 0, 0)),
                  pl.BlockSpec((1, d), lambda i, inv: (0, 0)),
                  pl.BlockSpec(memory_space=pl.ANY)],
        out_specs=pl.BlockSpec((tm, d), lambda i, inv: (i, 0)),
        scratch_shapes=[pltpu.VMEM((2, TOP_K, tm, d), F32), pltpu.SemaphoreType.DMA((2,))],
    )
    return pl.pallas_call(
        functools.partial(_final_kernel, n_tok, off),
        grid_spec=grid_spec,
        out_shape=jax.ShapeDtypeStruct((rows, d), F32),
        compiler_params=_cparams(("arbitrary",)),
        name="final_norm",
    )(inv, x1, route_w, mod3, final_w.reshape(1, d), ys)


def kernel(x_prompt, x_sample, c_prompt, c_sample, w_ada, b_ada, norm1_w, w_in, b_in, mlstm_gate_b, conv_w, conv_b, mlstm_norm_w, p_a, p_b, w_out, norm2_w, w_router_group, b_router_group, w_router_expert, b_router_expert, w_expert_gate, w_expert_up, w_expert_down, final_norm_w):
    bp, seq, d = x_prompt.shape
    bs = x_sample.shape[0]
    bt = bp + bs
    n = bt * seq
    layer = 0

    w_in_bf = w_in.astype(BF16)
    w_in_a = slice_cols_bf16(w_in_bf, layer, _SRC_A0, PROJ_A_COLS)
    b_in_a = b_in[layer, _SRC_A0:]
    b_in_m = b_in[layer, :PROJ_M_COLS] + jnp.pad(mlstm_gate_b[layer], (COL_GATE_M, PROJ_M_COLS - _SRC_A0))
    wr = jnp.concatenate([w_router_group[layer], w_router_expert[layer],
                          jnp.zeros((d, ROUTER_COLS - N_GROUPS - N_EXPERTS), F32)], axis=1)
    br = jnp.concatenate([b_router_group[layer], b_router_expert[layer],
                          jnp.zeros((ROUTER_COLS - N_GROUPS - N_EXPERTS,), F32)]).reshape(1, ROUTER_COLS)
    wr_hi = wr.astype(BF16)
    wr_lo = (wr - wr_hi.astype(F32)).astype(BF16)
    slopes = 2.0 ** (-8.0 * jnp.arange(1, A_HEADS + 1, dtype=F32) / A_HEADS)

    c_all = jnp.concatenate([c_prompt, c_sample, jnp.zeros((16 - bt, d), F32)], axis=0)
    mod3 = ada_modulation(c_all, w_ada[layer], b_ada[layer])[:bt].reshape(bt, 6, d)

    h = norm_modulate(x_prompt, x_sample, mod3, norm1_w[layer])
    h2d = h.reshape(n, d)
    proj_m = matmul_bias(h2d, w_in_bf, b_in_m, PROJ_M_COLS, "in_projection_mlstm", layer=layer)
    proj_a = matmul_bias(h2d, w_in_a, b_in_a, PROJ_A_COLS, "in_projection")
    proj_m3 = proj_m.reshape(bt, seq, PROJ_M_COLS)
    proj_a3 = proj_a.reshape(bt, seq, PROJ_A_COLS)

    nc = seq // M_CHUNK
    gates = proj_m3[:, :, COL_GATE_M:COL_GATE_M + GATE_M_W].reshape(bt, nc, M_CHUNK, 4, M_HEADS)
    gates = gates.transpose(0, 4, 3, 1, 2)
    hm = mlstm_branch(proj_m3, gates, conv_w[layer], conv_b[layer], mlstm_norm_w[layer])
    at = attention_branch(proj_a3, slopes)

    x1, h2, logits = merge_project(
        x_prompt.reshape(bp * seq, d), x_sample.reshape(bs * seq, d), hm.reshape(n, -1), at.reshape(n, -1), proj_a,
        mod3, p_a[layer].astype(BF16), p_b[layer].astype(BF16), w_out[layer].astype(BF16), norm2_w[layer],
        wr_hi, wr_lo, br, seq)

    tables, inv, route_w = route(logits, MOE_ROWS)
    ys = expert_ffn(h2, tables, w_expert_gate[layer], w_expert_up[layer], w_expert_down[layer])

    y_p = final_norm(x1, ys, inv, route_w, mod3, final_norm_w, 0, bp * seq, seq)
    y_s = final_norm(x1, ys, inv, route_w, mod3, final_norm_w, bp * seq, bs * seq, seq)
    return (y_p.reshape(bp, seq, d), y_s.reshape(bs, seq, d))
```

```python
import functools

import jax
import jax.numpy as jnp
from jax import lax
from jax.experimental import pallas as pl
from jax.experimental.pallas import tpu as pltpu

F32 = jnp.float32
BF16 = jnp.bfloat16

D_MODEL = 2048
RMS_EPS = 1e-6
M_HEADS = 8
M_QK_DIM = 128
M_V_DIM = 256
M_CHUNK = 128
A_GROUPS = ((128, 1), (512, 4), (2048, 16))
A_SLOTS = 4
A_HEADS = A_SLOTS * len(A_GROUPS)
A_HEAD_DIM = 128
N_GROUPS = 4
EXPERTS_PER_GROUP = 8
N_EXPERTS = N_GROUPS * EXPERTS_PER_GROUP
TOP_K = 2
D_EXPERT = 1024

M_QK_W = M_HEADS * M_QK_DIM
M_V_W = M_HEADS * M_V_DIM
A_W = A_HEADS * A_HEAD_DIM
PROJ_TN = 512
COL_Q_M = 0
COL_K_M = M_QK_W
COL_V_M = 2 * M_QK_W
COL_O_M = 2 * M_QK_W + M_V_W
COL_GATE_M = 2 * M_QK_W + 2 * M_V_W
GATE_M_W = 4 * M_HEADS
PROJ_M_COLS = COL_GATE_M + PROJ_TN
_SRC_A0 = COL_GATE_M + GATE_M_W
COL_Q_A = 0
COL_K_A = A_W
COL_V_A = 2 * A_W
COL_GATE_A = 3 * A_W
COL_GATE_B = 3 * A_W + D_MODEL
PROJ_A_COLS = 3 * A_W + 2 * D_MODEL

LANES = 128
MOE_ROWS = 256
ROUTER_COLS = 128
V7X_VMEM_BYTES = 64 * 1024 * 1024
VMEM_LIMIT = V7X_VMEM_BYTES * 7 // 8
EXPERT_VMEM_LIMIT = V7X_VMEM_BYTES * 15 // 16


def _sigmoid(x):
    return 1.0 / (1.0 + jnp.exp(-x))


def _cparams(sem, vmem=VMEM_LIMIT):
    return pltpu.CompilerParams(dimension_semantics=sem, vmem_limit_bytes=vmem)


def _ada_kernel(c_ref, w_ref, b_ref, o_ref):
    c = c_ref[...]
    a = (c * _sigmoid(c)).astype(BF16)
    o_ref[...] = jnp.dot(a, w_ref[...].astype(BF16), preferred_element_type=F32) + b_ref[...]


def ada_modulation(c, w_ada, b_ada):
    rows, d = c.shape
    n = w_ada.shape[1]
    tn = 1024
    return pl.pallas_call(
        _ada_kernel,
        grid=(n // tn,),
        in_specs=[pl.BlockSpec((rows, d), lambda j: (0, 0)),
                  pl.BlockSpec((d, tn), lambda j: (0, j)),
                  pl.BlockSpec((1, tn), lambda j: (0, j))],
        out_specs=pl.BlockSpec((rows, tn), lambda j: (0, j)),
        out_shape=jax.ShapeDtypeStruct((rows, n), F32),
        compiler_params=_cparams(("parallel",)),
        name="ada_modulation",
    )(c, w_ada, b_ada.reshape(1, n))


def _norm_mod_kernel(nb0, xp_ref, xs_ref, mod_ref, w_ref, o_ref):
    b = pl.program_id(0)
    x = jnp.where(b < nb0, xp_ref[0], xs_ref[0])
    y = x * lax.rsqrt(jnp.mean(x * x, axis=-1, keepdims=True) + RMS_EPS) * w_ref[...]
    o_ref[0] = (y * (1.0 + mod_ref[0, 1:2, :]) + mod_ref[0, 0:1, :]).astype(o_ref.dtype)


def norm_modulate(xp, xs, mod3, norm_w, ts=1024):
    nb0, s, d = xp.shape
    nb1 = xs.shape[0]
    return pl.pallas_call(
        functools.partial(_norm_mod_kernel, nb0),
        grid=(nb0 + nb1, s // ts),
        in_specs=[pl.BlockSpec((1, ts, d), lambda b, t: (jnp.minimum(b, nb0 - 1), jnp.where(b < nb0, t, s // ts - 1), 0)),
                  pl.BlockSpec((1, ts, d), lambda b, t: (jnp.maximum(b - nb0, 0), jnp.where(b < nb0, 0, t), 0)),
                  pl.BlockSpec((1, 6, d), lambda b, t: (b, 0, 0)),
                  pl.BlockSpec((1, d), lambda b, t: (0, 0))],
        out_specs=pl.BlockSpec((1, ts, d), lambda b, t: (b, t, 0)),
        out_shape=jax.ShapeDtypeStruct((nb0 + nb1, s, d), BF16),
        compiler_params=_cparams(("parallel", "parallel")),
        name="norm1_modulate",
    )(xp, xs, mod3, norm_w.reshape(1, d))


def _mm_bias_kernel(a_ref, w_ref, b_ref, o_ref):
    o_ref[...] = jnp.dot(a_ref[...], w_ref[...], preferred_element_type=F32) + b_ref[...]


def _slice_cols_kernel(col0, w_ref, o_ref):
    o_ref[...] = w_ref[:, col0:col0 + o_ref.shape[1]].astype(o_ref.dtype)


def slice_cols_bf16(w, layer, col0, n, tr=256):
    _, k, n_all = w.shape
    return pl.pallas_call(
        functools.partial(_slice_cols_kernel, col0),
        grid=(k // tr,),
        in_specs=[pl.BlockSpec((None, tr, n_all), lambda i: (layer, i, 0))],
        out_specs=pl.BlockSpec((tr, n), lambda i: (i, 0)),
        out_shape=jax.ShapeDtypeStruct((k, n), BF16),
        compiler_params=_cparams(("parallel",)),
        name="slice_cols_bf16",
    )(w)


def matmul_bias(a, w, b, n, name, layer=None, tm=2048, tn=PROJ_TN):
    m, k = a.shape
    if layer is None:
        w_spec = pl.BlockSpec((k, tn), lambda i, j: (0, j))
    else:
        w_spec = pl.BlockSpec((None, k, tn), lambda i, j: (layer, 0, j))
    return pl.pallas_call(
        _mm_bias_kernel,
        grid=(m // tm, n // tn),
        in_specs=[pl.BlockSpec((tm, k), lambda i, j: (i, 0)),
                  w_spec,
                  pl.BlockSpec((1, tn), lambda i, j: (0, j))],
        out_specs=pl.BlockSpec((tm, tn), lambda i, j: (i, j)),
        out_shape=jax.ShapeDtypeStruct((m, n), F32),
        compiler_params=_cparams(("parallel", "parallel")),
        name=name,
    )(a, w, b.reshape(1, -1))


def _lane_scan(x, op, fill, reverse):
    n = x.shape[-1]
    axis = x.ndim - 1
    lane = lax.broadcasted_iota(jnp.int32, x.shape, axis)
    k = 1
    while k < n:
        if reverse:
            x = op(x, jnp.where(lane < n - k, pltpu.roll(x, n - k, axis), fill))
        else:
            x = op(x, jnp.where(lane >= k, pltpu.roll(x, k, axis), fill))
        k *= 2
    return x


def _conv_silu_chunk(x_ref, w_ref, b_ref, c, n_chunks):
    L = M_CHUNK
    s = n_chunks * L
    r0 = pl.multiple_of(c * L, L)
    x = x_ref[0, pl.ds(r0, L), :]
    prev_row = x_ref[0, pl.ds(jnp.maximum(r0 - 1, 0), 1), :]
    next_row = x_ref[0, pl.ds(jnp.minimum(r0 + L, s - 1), 1), :]
    prev_row = jnp.where(c > 0, prev_row, 0.0)
    next_row = jnp.where(c < n_chunks - 1, next_row, 0.0)
    rows = lax.broadcasted_iota(jnp.int32, x.shape, 0)
    x_prev = jnp.where(rows == 0, prev_row, pltpu.roll(x, 1, 0))
    x_next = jnp.where(rows == L - 1, next_row, pltpu.roll(x, L - 1, 0))
    y = b_ref[...] + x_prev * w_ref[0:1, :] + x * w_ref[1:2, :] + x_next * w_ref[2:3, :]
    return y * _sigmoid(y)


def _mlstm_kernel(q_ref, k_ref, v_ref, o_ref, g_ref, cwq_ref, cwk_ref, cbq_ref, cbk_ref, nw_ref, out_ref,
                  qs_ref, kt_ref, hf_ref, hr_ref, c_ref,
                  u_ref, negm_ref, ib_ref, wk_ref, decay_ref, m0_ref, m1_ref):
    L = M_CHUNK
    dk = M_QK_DIM
    dv = M_V_DIM
    hp, nc = kt_ref.shape[0], kt_ref.shape[1]

    def prep(c, carry):
        q = _conv_silu_chunk(q_ref, cwq_ref, cbq_ref, c, nc) * (M_QK_DIM ** -0.5)
        qs_ref[c] = q.astype(BF16)
        k = _conv_silu_chunk(k_ref, cwk_ref, cbk_ref, c, nc)
        for hh in range(hp):
            kt_ref[hh, c] = k[:, hh * dk:(hh + 1) * dk].T
        return carry

    lax.fori_loop(0, nc, prep, 0)

    for hh in range(hp):
        for d in range(2):
            rev = d == 1
            i_pre = g_ref[0, hh, 2 * d]
            f_pre = g_ref[0, hh, 2 * d + 1]
            log_f = -(jnp.maximum(-f_pre, 0.0) + jnp.log1p(jnp.exp(-jnp.abs(f_pre))))
            b = _lane_scan(log_f, jnp.add, 0.0, rev)
            a = jnp.broadcast_to(b[:, 0:1] if rev else b[:, L - 1:L], (nc, L))
            g = a - b + i_pre
            g_max = jnp.broadcast_to(jnp.max(g, axis=1, keepdims=True), (nc, L))
            m = jnp.zeros((1, L), F32)
            for c in (range(nc - 1, -1, -1) if rev else range(nc)):
                m0_ref[hh, d, c:c + 1, :] = m
                m = jnp.maximum(a[c:c + 1, :] + m, g_max[c:c + 1, :])
                m1_ref[hh, d, c:c + 1, :] = m
            m0 = m0_ref[hh, d]
            m1 = m1_ref[hh, d]
            ib = i_pre - b
            m_t = jnp.maximum(b + m0, b + _lane_scan(ib, jnp.maximum, -jnp.inf, rev))
            ib_ref[hh, d] = ib
            u_ref[hh, d] = b - m_t
            negm_ref[hh, d] = -m_t
            wk_ref[hh, d] = jnp.exp(g - m1)
            decay_ref[hh, d] = jnp.exp(a + m0 - m1)

    t_idx = lax.broadcasted_iota(jnp.int32, (L, L), 0)
    s_idx = lax.broadcasted_iota(jnp.int32, (L, L), 1)
    ones_ext = jnp.ones((L, LANES), BF16)

    def chunk_step(hh, d, c):
        r0 = pl.multiple_of(c * L, L)
        q = qs_ref[c, :, hh * dk:(hh + 1) * dk]
        kt = kt_ref[hh, c]
        v_ext = jnp.concatenate([v_ref[0, pl.ds(r0, L), hh * dv:(hh + 1) * dv].astype(BF16), ones_ext], axis=1)

        def row(ref):
            return ref[hh, d, pl.ds(c, 1), :]

        umat = jnp.broadcast_to(row(u_ref), (L, L)).T
        nmat = jnp.broadcast_to(row(negm_ref), (L, L)).T
        causal = (s_idx <= t_idx) if d == 0 else (s_idx >= t_idx)
        w_intra = jnp.where(causal, jnp.exp(umat + row(ib_ref)), 0.0)
        w_inter = jnp.exp(umat + row(m0_ref))
        s_qk = jnp.dot(q, kt.astype(BF16), preferred_element_type=F32)
        c_ext = c_ref[hh, d]
        lhs = jnp.concatenate([(w_intra * s_qk).astype(BF16), (q.astype(F32) * w_inter).astype(BF16)], axis=1)
        rhs = jnp.concatenate([v_ext, c_ext.astype(BF16)], axis=0)
        num = jnp.dot(lhs, rhs, preferred_element_type=F32)
        r = 1.0 / jnp.maximum(jnp.abs(num[:, dv:]), jnp.exp(nmat))
        h = num[:, :dv] * jnp.concatenate([r] * (dv // LANES), axis=1)

        upd = jnp.dot((kt * row(wk_ref)).astype(BF16), v_ext, preferred_element_type=F32)
        decay = jnp.broadcast_to(row(decay_ref), (dk, LANES))
        c_ref[hh, d] = jnp.concatenate([decay] * (c_ext.shape[1] // LANES), axis=1) * c_ext + upd
        return h

    def finish(hh, c, hs):
        rows = pl.ds(pl.multiple_of(c * L, L), L)
        cols = slice(hh * dv, (hh + 1) * dv)
        y = hs * lax.rsqrt(jnp.mean(hs * hs, axis=-1, keepdims=True) + RMS_EPS) * nw_ref[:, cols]
        out_ref[0, rows, cols] = (y * _sigmoid(o_ref[0, rows, cols])).astype(out_ref.dtype)

    def first_half(j, carry):
        cf, cr = j, nc - 1 - j
        for hh in range(hp):
            hf_ref[hh, pl.ds(pl.multiple_of(cf * L, L), L), :] = chunk_step(hh, 0, cf)
            hr_ref[hh, pl.ds(pl.multiple_of((cr - nc // 2) * L, L), L), :] = chunk_step(hh, 1, cr)
        return carry

    def second_half(j, carry):
        cf, cr = j, nc - 1 - j
        for hh in range(hp):
            finish(hh, cf, chunk_step(hh, 0, cf) + hr_ref[hh, pl.ds(pl.multiple_of((cf - nc // 2) * L, L), L), :])
            finish(hh, cr, hf_ref[hh, pl.ds(pl.multiple_of(cr * L, L), L), :] + chunk_step(hh, 1, cr))
        return carry

    c_ref[...] = jnp.zeros_like(c_ref)
    lax.fori_loop(0, nc // 2, first_half, 0, unroll=4)
    lax.fori_loop(nc // 2, nc, second_half, 0, unroll=4)


def mlstm_branch(proj3, gates, conv_w, conv_b, norm_w, hp=2):
    bsz, s, _ = proj3.shape
    L = M_CHUNK
    nc = s // L
    dk, dv = M_QK_DIM, M_V_DIM
    assert dk == L and nc % 2 == 0 and M_HEADS % hp == 0
    wq, wv = hp * dk, hp * dv
    k_off = M_HEADS * dk // wq
    return pl.pallas_call(
        _mlstm_kernel,
        grid=(bsz, M_HEADS // hp),
        in_specs=[pl.BlockSpec((1, s, wq), lambda b, h: (b, 0, COL_Q_M // wq + h)),
                  pl.BlockSpec((1, s, wq), lambda b, h: (b, 0, COL_K_M // wq + h)),
                  pl.BlockSpec((1, s, wv), lambda b, h: (b, 0, COL_V_M // wv + h)),
                  pl.BlockSpec((1, s, wv), lambda b, h: (b, 0, COL_O_M // wv + h)),
                  pl.BlockSpec((1, hp, 4, nc, L), lambda b, h: (b, h, 0, 0, 0)),
                  pl.BlockSpec((3, wq), lambda b, h: (0, h)),
                  pl.BlockSpec((3, wq), lambda b, h: (0, k_off + h)),
                  pl.BlockSpec((1, wq), lambda b, h: (0, h)),
                  pl.BlockSpec((1, wq), lambda b, h: (0, k_off + h)),
                  pl.BlockSpec((1, wv), lambda b, h: (0, h))],
        out_specs=pl.BlockSpec((1, s, wv), lambda b, h: (b, 0, h)),
        out_shape=jax.ShapeDtypeStruct((bsz, s, M_HEADS * dv), BF16),
        scratch_shapes=[pltpu.VMEM((nc, L, wq), BF16),
                        pltpu.VMEM((hp, nc, dk, L), F32),
                        pltpu.VMEM((hp, s // 2, dv), F32),
                        pltpu.VMEM((hp, s // 2, dv), F32),
                        pltpu.VMEM((hp, 2, dk, dv + LANES), F32),
                        *([pltpu.VMEM((hp, 2, nc, L), F32)] * 7)],
        compiler_params=_cparams(("parallel", "parallel")),
        name="mlstm_branch",
    )(proj3, proj3, proj3, proj3, gates, conv_w, conv_w, conv_b.reshape(1, -1), conv_b.reshape(1, -1),
      norm_w.reshape(1, -1))


def _attn_kernel(slopes_ref, q0, q1, q2, k0, k1, k2, v0, v1, v2, out_ref, og_ref, lse_ref, bias_ref,
                 qd_ref, kd_ref, vd_ref, od_ref, ld_ref):
    s_len = out_ref.shape[1]
    dh = A_HEAD_DIM
    T = 128
    slot = pl.program_id(1)
    qs, ks, vs = (q0, q1, q2), (k0, k1, k2), (v0, v1, v2)

    for g, (window, dil) in enumerate(A_GROUPS):
        side = window // (2 * dil)
        u_len = s_len // dil
        nqb = u_len // T
        kw = min(T + 2 * side, u_len)
        slope = slopes_ref[g * A_SLOTS + slot] * float(dil)
        q_ref, k_ref, v_ref = qs[g], ks[g], vs[g]

        regroup = dil >= 8
        paired = regroup and nqb == 1 and dil % 2 == 0
        t_io = lax.broadcasted_iota(jnp.int32, (T, 2 * u_len if paired else kw), 0)
        k_io = lax.broadcasted_iota(jnp.int32, (T, 2 * u_len if paired else kw), 1)
        if paired:
            kw = 2 * u_len
            for case in range(2):
                rel = jnp.abs(t_io - (k_io - case * u_len))
                own = jnp.logical_and(k_io // u_len == case, rel <= side)
                bias_ref[g, case, :, :kw] = jnp.where(own, -slope * rel.astype(F32), -1e30)
        else:
            offsets = (0, side, kw - T) if nqb > 1 else (0,)
            for case, off in enumerate(offsets):
                rel = jnp.abs(t_io + off - k_io)
                bias_ref[g, case, :, :kw] = jnp.where(rel <= side, -slope * rel.astype(F32), -1e30)

        regroup = dil >= 8

        def residue_rows(r, dil=dil, u_len=u_len):
            return pl.ds(r, u_len, stride=dil), pl.ds(pl.multiple_of(r * u_len, u_len), u_len)

        if regroup:
            def gather_residue(r, carry, q_ref=q_ref, k_ref=k_ref, v_ref=v_ref):
                src, dst = residue_rows(r)
                qd_ref[dst, :] = (q_ref[0, src, :] * (dh ** -0.5)).astype(BF16)
                kd_ref[dst, :] = k_ref[0, src, :].astype(BF16)
                vd_ref[dst, :] = v_ref[0, src, :].astype(BF16)
                return carry

            lax.fori_loop(0, dil, gather_residue, 0, unroll=4)

        def block(idx, carry, g=g, dil=dil, side=side, u_len=u_len, nqb=nqb, kw=kw, regroup=regroup, paired=paired,
                  q_ref=q_ref, k_ref=k_ref, v_ref=v_ref):
            r = idx // nqb
            j = idx % nqb
            u0 = j * T
            ku0 = jnp.clip(u0 - side, 0, u_len - kw) if not paired else 0
            case = jnp.where(j == 0, 0, jnp.where(j == nqb - 1, 2, 1)) if nqb > 1 else 0
            if paired:
                case = r % 2
                q_rows = pl.ds(pl.multiple_of(r * u_len, T), T)
                k_rows = pl.ds(pl.multiple_of((r // 2) * kw, kw), kw)
                q, kk, vv = qd_ref[q_rows, :], kd_ref[k_rows, :], vd_ref[k_rows, :]
            elif regroup:
                q_rows = pl.ds(pl.multiple_of(r * u_len + u0, T), T)
                k_rows = pl.ds(pl.multiple_of(r * u_len + ku0, side), kw)
                q, kk, vv = qd_ref[q_rows, :], kd_ref[k_rows, :], vd_ref[k_rows, :]
            else:
                q_rows = pl.ds(r + u0 * dil, T, stride=dil) if dil > 1 else pl.ds(pl.multiple_of(u0, T), T)
                k_rows = pl.ds(r + ku0 * dil, kw, stride=dil) if dil > 1 else pl.ds(pl.multiple_of(ku0, side), kw)
                q = (q_ref[0, q_rows, :] * (dh ** -0.5)).astype(BF16)
                kk = k_ref[0, k_rows, :].astype(BF16)
                vv = v_ref[0, k_rows, :].astype(BF16)
            s = lax.dot_general(q, kk, (((1,), (1,)), ((), ())), preferred_element_type=F32)
            s = s + bias_ref[g, case, :, :kw]
            m = jnp.max(s, axis=1, keepdims=True)
            p = jnp.exp(s - m)
            den = jnp.sum(p, axis=1, keepdims=True)
            o = jnp.dot(p.astype(BF16), vv, preferred_element_type=F32) * (1.0 / den)
            lse = jnp.broadcast_to(m + jnp.log(den), (T, LANES))
            if regroup:
                od_ref[q_rows, :] = o
                ld_ref[q_rows, :] = lse
            else:
                og_ref[g, q_rows, :] = o
                lse_ref[g, q_rows, :] = lse
            return carry

        lax.fori_loop(0, dil * nqb, block, 0, unroll=16)

        if regroup:
            def scatter_residue(r, carry, g=g):
                dst, src = residue_rows(r)
                og_ref[g, dst, :] = od_ref[src, :]
                lse_ref[g, dst, :] = ld_ref[src, :]
                return carry

            lax.fori_loop(0, dil, scatter_residue, 0, unroll=4)

    rows_per = 256

    def merge(i, carry):
        rows = pl.ds(pl.multiple_of(i * rows_per, rows_per), rows_per)
        l0, l1, l2 = lse_ref[0, rows, :], lse_ref[1, rows, :], lse_ref[2, rows, :]
        mx = jnp.maximum(jnp.maximum(l0, l1), l2)
        e0, e1, e2 = jnp.exp(l0 - mx), jnp.exp(l1 - mx), jnp.exp(l2 - mx)
        inv = 1.0 / (e0 + e1 + e2)
        o = (e0 * inv) * og_ref[0, rows, :] + (e1 * inv) * og_ref[1, rows, :] + (e2 * inv) * og_ref[2, rows, :]
        out_ref[0, rows, :] = o.astype(out_ref.dtype)
        return carry

    lax.fori_loop(0, s_len // rows_per, merge, 0)


def attention_branch(proj3, slopes):
    bsz, s, _ = proj3.shape
    dh = A_HEAD_DIM

    def col(base, g):
        return lambda b, t, sl: (b, 0, base // dh + g * A_SLOTS + t)

    grid_spec = pltpu.PrefetchScalarGridSpec(
        num_scalar_prefetch=1,
        grid=(bsz, A_SLOTS),
        in_specs=[pl.BlockSpec((1, s, dh), col(base, g))
                  for base in (COL_Q_A, COL_K_A, COL_V_A) for g in range(len(A_GROUPS))],
        out_specs=pl.BlockSpec((1, s, dh), lambda b, t, sl: (b, 0, t)),
        scratch_shapes=[pltpu.VMEM((3, s, dh), F32), pltpu.VMEM((3, s, LANES), F32),
                        pltpu.VMEM((len(A_GROUPS), 3, 128, 256), F32),
                        pltpu.VMEM((s, dh), BF16), pltpu.VMEM((s, dh), BF16), pltpu.VMEM((s, dh), BF16),
                        pltpu.VMEM((s, dh), F32), pltpu.VMEM((s, LANES), F32)],
    )
    return pl.pallas_call(
        _attn_kernel,
        grid_spec=grid_spec,
        out_shape=jax.ShapeDtypeStruct((bsz, s, A_SLOTS * dh), BF16),
        compiler_params=_cparams(("parallel", "parallel")),
        name="dilated_attention",
    )(slopes, *([proj3] * 9))


def _merge_kernel(n0, n_gate, xp_ref, xs_ref, hm_ref, at_ref, *refs):
    ga_refs, gb_refs = refs[:n_gate], refs[n_gate:2 * n_gate]
    (mod_ref, pa_ref, pb_ref, wo_ref, n2_ref, wrh_ref, wrl_ref, br_ref, x1_ref, h2_ref, lg_ref) = refs[2 * n_gate:]
    i = pl.program_id(0)
    x = jnp.where(i < n0, xp_ref[...], xs_ref[...])
    y_a = jnp.dot(hm_ref[...], pa_ref[...], preferred_element_type=F32)
    y_b = jnp.dot(at_ref[...], pb_ref[...], preferred_element_type=F32)
    gate_a = jnp.concatenate([r[...] for r in ga_refs], axis=1)
    gate_b = jnp.concatenate([r[...] for r in gb_refs], axis=1)
    mixin = _sigmoid(gate_a) * y_a + _sigmoid(gate_b) * y_b
    mix = jnp.dot(mixin.astype(BF16), wo_ref[...], preferred_element_type=F32)
    x1 = x + mod_ref[0, 2:3, :] * mix
    x1_ref[...] = x1
    y = x1 * lax.rsqrt(jnp.mean(x1 * x1, axis=-1, keepdims=True) + RMS_EPS) * n2_ref[...]
    h2 = y * (1.0 + mod_ref[0, 4:5, :]) + mod_ref[0, 3:4, :]
    h2_ref[...] = h2
    hi = h2.astype(BF16)
    lo = (h2 - hi.astype(F32)).astype(BF16)
    lg_ref[...] = (jnp.dot(hi, wrh_ref[...], preferred_element_type=F32)
                   + (jnp.dot(hi, wrl_ref[...], preferred_element_type=F32)
                      + jnp.dot(lo, wrh_ref[...], preferred_element_type=F32))
                   + br_ref[...])


def merge_project(xp2, xs2, hm, at, proj, mod3, p_a, p_b, w_out, norm2_w, wr_hi, wr_lo, br, seq, tm=256):
    n, d = hm.shape
    n0 = xp2.shape[0] // tm
    n1 = xs2.shape[0] // tm
    per_seq = seq // tm
    const = dict(pipeline_mode=pl.Buffered(1))
    gw = PROJ_TN
    n_gate = d // gw

    def gate_specs(col0):
        return [pl.BlockSpec((tm, gw), lambda i, t=t: (i, col0 // gw + t)) for t in range(n_gate)]

    return pl.pallas_call(
        functools.partial(_merge_kernel, n0, n_gate),
        grid=(n0 + n1,),
        in_specs=[pl.BlockSpec((tm, d), lambda i: (jnp.minimum(i, n0 - 1), 0)),
                  pl.BlockSpec((tm, d), lambda i: (jnp.maximum(i - n0, 0), 0)),
                  pl.BlockSpec((tm, d), lambda i: (i, 0)),
                  pl.BlockSpec((tm, at.shape[1]), lambda i: (i, 0)),
                  *gate_specs(COL_GATE_A), *gate_specs(COL_GATE_B),
                  pl.BlockSpec((1, 6, d), lambda i: (i // per_seq, 0, 0)),
                  pl.BlockSpec(p_a.shape, lambda i: (0, 0), **const),
                  pl.BlockSpec(p_b.shape, lambda i: (0, 0), **const),
                  pl.BlockSpec(w_out.shape, lambda i: (0, 0), **const),
                  pl.BlockSpec((1, d), lambda i: (0, 0)),
                  pl.BlockSpec(wr_hi.shape, lambda i: (0, 0), **const),
                  pl.BlockSpec(wr_lo.shape, lambda i: (0, 0), **const),
                  pl.BlockSpec((1, ROUTER_COLS), lambda i: (0, 0))],
        out_specs=[pl.BlockSpec((tm, d), lambda i: (i, 0)),
                   pl.BlockSpec((tm, d), lambda i: (i, 0)),
                   pl.BlockSpec((tm, ROUTER_COLS), lambda i: (i, 0))],
        out_shape=[jax.ShapeDtypeStruct((n, d), F32),
                   jax.ShapeDtypeStruct((n, d), F32),
                   jax.ShapeDtypeStruct((n, ROUTER_COLS), F32)],
        compiler_params=_cparams(("parallel",)),
        name="merge_project",
    )(xp2, xs2, hm, at, *([proj] * (2 * n_gate)), mod3, p_a, p_b, w_out, norm2_w.reshape(1, d), wr_hi, wr_lo, br)


def route(logits, tb):
    n = logits.shape[0]
    g_logits = logits[:, :N_GROUPS]
    e_logits = logits[:, N_GROUPS:N_GROUPS + N_EXPERTS].reshape(n, N_GROUPS, EXPERTS_PER_GROUP)
    g_idx = jnp.argmax(g_logits, axis=-1)
    g_w = jnp.take_along_axis(jax.nn.softmax(g_logits, axis=-1), g_idx[:, None], axis=-1)
    e_sel = jnp.take_along_axis(e_logits, g_idx[:, None, None], axis=1)[:, 0]
    top_v, top_i = lax.top_k(e_sel, TOP_K)
    weights = g_w * jax.nn.softmax(top_v, axis=-1)
    expert = (g_idx[:, None] * EXPERTS_PER_GROUP + top_i).astype(jnp.int32)
    a = n * TOP_K
    flat_e = expert.reshape(a)
    e_ids = jnp.arange(N_EXPERTS, dtype=jnp.int32)
    counts = jnp.sum(flat_e[:, None] == e_ids[None, :], axis=0, dtype=jnp.int32)
    padded = (counts + tb - 1) // tb * tb
    pad_end = jnp.cumsum(padded)
    filler_e = jnp.repeat(e_ids, tb)
    filler_j = jnp.tile(jnp.arange(tb, dtype=jnp.int32), N_EXPERTS)
    filler_key = jnp.where(filler_j < (padded - counts)[filler_e], 2 * filler_e + 1, 2 * N_EXPERTS + 1)
    keys = jnp.concatenate([2 * flat_e, filler_key])
    ids = jnp.arange(a, dtype=jnp.int32)
    filler0 = jnp.zeros((N_EXPERTS * tb,), jnp.int32)
    tok_src = jnp.concatenate([ids // TOP_K, filler0])
    out_src = jnp.concatenate([(ids % TOP_K) * n + ids // TOP_K, filler0])
    sorted_keys, row_tok, row_out = lax.sort((keys, tok_src, out_src), num_keys=1)
    r = a + N_EXPERTS * tb
    n_blocks = r // tb
    block_start = jnp.arange(n_blocks, dtype=jnp.int32) * tb
    block_expert = jnp.minimum(jnp.sum(block_start[:, None] >= pad_end[None, :], axis=1), N_EXPERTS - 1).astype(jnp.int32)
    n_used = (pad_end[-1] // tb).astype(jnp.int32).reshape(1)
    later = jnp.logical_and(e_ids[None, :] > e_ids[:, None], (counts > 0)[None, :])
    next_of = jnp.min(jnp.where(later, e_ids[None, :], N_EXPERTS), axis=1)
    next_expert = jnp.where(next_of < N_EXPERTS, next_of, -1)[block_expert].astype(jnp.int32)
    pos = jnp.arange(r, dtype=jnp.int32)
    inv_key = jnp.where(sorted_keys % 2 == 0, row_out, a + pos)
    _, inv = lax.sort((inv_key, pos), num_keys=1)
    return (block_expert, next_expert, n_used, row_tok), inv[:a], weights


ROW_DMA_UNROLL = 64

def _issue_row_gather(src_hbm, dst_buf, sem, index_of, n_rows):
    def body(j, carry):
        pltpu.make_async_copy(src_hbm.at[pl.ds(index_of(j), 1), :], dst_buf.at[pl.ds(j, 1), :], sem).start()
        return carry
    lax.fori_loop(0, n_rows, body, 0, unroll=ROW_DMA_UNROLL)


def _wait_row_gather(src_hbm, dst_buf, sem):
    pltpu.make_async_copy(src_hbm.at[pl.ds(0, dst_buf.shape[0]), :], dst_buf, sem).wait()


def _round_rows_to_bf16(src_ref, dst_ref, rows=256):
    def body(t, carry):
        r = pl.ds(pl.multiple_of(t * rows, rows), rows)
        dst_ref[r, :] = src_ref[r, :].astype(BF16)
        return carry
    lax.fori_loop(0, src_ref.shape[0] // rows, body, 0)


def _expert_kernel(be_ref, nxt_ref, nused_ref, rtok_ref, h2_hbm, wg_hbm, wu_hbm, wd_hbm, out_ref,
                   xbuf, stg_g, stg_u, stg_d, wg_b, wu_b, wd_b, gsem, wsem):
    tb = xbuf.shape[1]
    i = pl.program_id(0)
    nused = nused_ref[0]
    slot = i % 2
    e = be_ref[i]
    first_of_expert = jnp.logical_or(i == 0, e != be_ref[jnp.maximum(i - 1, 0)])

    def issue(blk, s):
        _issue_row_gather(h2_hbm, xbuf.at[s], gsem.at[s], lambda j: rtok_ref[blk * tb + j], tb)

    def weight_copies(expert):
        return [pltpu.make_async_copy(src.at[expert], dst, wsem.at[k])
                for k, (src, dst) in enumerate(((wg_hbm, stg_g), (wu_hbm, stg_u), (wd_hbm, stg_d)))]

    @pl.when(jnp.logical_and(i == 0, nused > 0))
    def _():
        issue(0, 0)

    @pl.when(i + 1 < nused)
    def _():
        issue(i + 1, 1 - slot)

    @pl.when(i < nused)
    def _():
        @pl.when(first_of_expert)
        def _():
            @pl.when(i == 0)
            def _():
                for c in weight_copies(e):
                    c.start()

            for c in weight_copies(e):
                c.wait()
            for stg, wb in ((stg_g, wg_b), (stg_u, wu_b), (stg_d, wd_b)):
                _round_rows_to_bf16(stg, wb)
            nxt = nxt_ref[i]

            @pl.when(nxt >= 0)
            def _():
                for c in weight_copies(nxt):
                    c.start()

        _wait_row_gather(h2_hbm, xbuf.at[slot], gsem.at[slot])
        x = xbuf[slot].astype(BF16)
        g = jnp.dot(x, wg_b[...], preferred_element_type=F32)
        u = jnp.dot(x, wu_b[...], preferred_element_type=F32)
        hdn = (g * _sigmoid(g) * u).astype(BF16)
        out_ref[...] = jnp.dot(hdn, wd_b[...], preferred_element_type=F32)

    @pl.when(i >= nused)
    def _():
        out_ref[...] = jnp.zeros_like(out_ref)


def expert_ffn(h2, tables, wg, wu, wd, tb=MOE_ROWS):
    block_expert, next_expert, n_used, row_tok = tables
    n, d = h2.shape
    nb = block_expert.shape[0]
    de = wg.shape[2]
    any_spec = pl.BlockSpec(memory_space=pl.ANY)
    grid_spec = pltpu.PrefetchScalarGridSpec(
        num_scalar_prefetch=4,
        grid=(nb,),
        in_specs=[any_spec, any_spec, any_spec, any_spec],
        out_specs=pl.BlockSpec((tb, d), lambda i, *_: (i, 0)),
        scratch_shapes=[pltpu.VMEM((2, tb, d), F32),
                        pltpu.VMEM((d, de), F32), pltpu.VMEM((d, de), F32), pltpu.VMEM((de, d), F32),
                        pltpu.VMEM((d, de), BF16), pltpu.VMEM((d, de), BF16), pltpu.VMEM((de, d), BF16),
                        pltpu.SemaphoreType.DMA((2,)), pltpu.SemaphoreType.DMA((3,))],
    )
    return pl.pallas_call(
        _expert_kernel,
        grid_spec=grid_spec,
        out_shape=jax.ShapeDtypeStruct((nb * tb, d), F32),
        compiler_params=_cparams(("arbitrary",), EXPERT_VMEM_LIMIT),
        name="expert_ffn",
    )(block_expert, next_expert, n_used, row_tok, h2, wg, wu, wd)


def _final_kernel(n_tok, tile0, inv_ref, x1_ref, rw_ref, mod_ref, w_ref, ys_hbm, o_ref, buf, sem):
    tm = x1_ref.shape[0]
    i = pl.program_id(0)
    slot = i % 2

    def issue(tile, s):
        base = (tile0 + tile) * tm
        for k in range(TOP_K):
            _issue_row_gather(ys_hbm, buf.at[s, k], sem.at[s], lambda j, k=k: inv_ref[k * n_tok + base + j], tm)

    @pl.when(i == 0)
    def _():
        issue(0, 0)

    @pl.when(i + 1 < pl.num_programs(0))
    def _():
        issue(i + 1, 1 - slot)

    for k in range(TOP_K):
        _wait_row_gather(ys_hbm, buf.at[slot, k], sem.at[slot])
    rw = rw_ref[...]
    moe = buf[slot, 0] * rw[:, 0:1]
    for k in range(1, TOP_K):
        moe = moe + buf[slot, k] * rw[:, k:k + 1]
    x = x1_ref[...] + mod_ref[0, 5:6, :] * moe
    o_ref[...] = x * lax.rsqrt(jnp.mean(x * x, axis=-1, keepdims=True) + RMS_EPS) * w_ref[...]


def final_norm(x1, ys, inv, route_w, mod3, final_w, row0, rows, seq, tm=512):
    n_tok, d = x1.shape
    off = row0 // tm
    per_seq = seq // tm
    grid_spec = pltpu.PrefetchScalarGridSpec(
        num_scalar_prefetch=1,
        grid=(rows // tm,),
        in_specs=[pl.BlockSpec((tm, d), lambda i, inv: (off + i, 0)),
                  pl.BlockSpec((tm, TOP_K), lambda i, inv: (off + i, 0)),
                  pl.BlockSpec((1, 6, d), lambda i, inv: ((off + i) // per_seq, 0, 0)),
                  pl.BlockSpec((1, d), lambda i, inv: (0, 0)),
                  pl.BlockSpec(memory_space=pl.ANY)],
        out_specs=pl.BlockSpec((tm, d), lambda i, inv: (i, 0)),
        scratch_shapes=[pltpu.VMEM((2, TOP_K, tm, d), F32), pltpu.SemaphoreType.DMA((2,))],
    )
    return pl.pallas_call(
        functools.partial(_final_kernel, n_tok, off),
        grid_spec=grid_spec,
        out_shape=jax.ShapeDtypeStruct((rows, d), F32),
        compiler_params=_cparams(("arbitrary",)),
        name="final_norm",
    )(inv, x1, route_w, mod3, final_w.reshape(1, d), ys)


def kernel(x_prompt, x_sample, c_prompt, c_sample, w_ada, b_ada, norm1_w, w_in, b_in, mlstm_gate_b, conv_w, conv_b, mlstm_norm_w, p_a, p_b, w_out, norm2_w, w_router_group, b_router_group, w_router_expert, b_router_expert, w_expert_gate, w_expert_up, w_expert_down, final_norm_w):
    bp, seq, d = x_prompt.shape
    bs = x_sample.shape[0]
    bt = bp + bs
    n = bt * seq
    layer = 0

    w_in_bf = w_in.astype(BF16)
    w_in_a = slice_cols_bf16(w_in_bf, layer, _SRC_A0, PROJ_A_COLS)
    b_in_a = b_in[layer, _SRC_A0:]
    b_in_m = b_in[layer, :PROJ_M_COLS] + jnp.pad(mlstm_gate_b[layer], (COL_GATE_M, PROJ_M_COLS - _SRC_A0))
    wr = jnp.concatenate([w_router_group[layer], w_router_expert[layer],
                          jnp.zeros((d, ROUTER_COLS - N_GROUPS - N_EXPERTS), F32)], axis=1)
    br = jnp.concatenate([b_router_group[layer], b_router_expert[layer],
                          jnp.zeros((ROUTER_COLS - N_GROUPS - N_EXPERTS,), F32)]).reshape(1, ROUTER_COLS)
    wr_hi = wr.astype(BF16)
    wr_lo = (wr - wr_hi.astype(F32)).astype(BF16)
    slopes = 2.0 ** (-8.0 * jnp.arange(1, A_HEADS + 1, dtype=F32) / A_HEADS)

    c_all = jnp.concatenate([c_prompt, c_sample, jnp.zeros((16 - bt, d), F32)], axis=0)
    mod3 = ada_modulation(c_all, w_ada[layer], b_ada[layer])[:bt].reshape(bt, 6, d)

    h = norm_modulate(x_prompt, x_sample, mod3, norm1_w[layer])
    h2d = h.reshape(n, d)
    proj_m = matmul_bias(h2d, w_in_bf, b_in_m, PROJ_M_COLS, "in_projection_mlstm", layer=layer)
    proj_a = matmul_bias(h2d, w_in_a, b_in_a, PROJ_A_COLS, "in_projection")
    proj_m3 = proj_m.reshape(bt, seq, PROJ_M_COLS)
    proj_a3 = proj_a.reshape(bt, seq, PROJ_A_COLS)

    nc = seq // M_CHUNK
    gates = proj_m3[:, :, COL_GATE_M:COL_GATE_M + GATE_M_W].reshape(bt, nc, M_CHUNK, 4, M_HEADS)
    gates = gates.transpose(0, 4, 3, 1, 2)
    hm = mlstm_branch(proj_m3, gates, conv_w[layer], conv_b[layer], mlstm_norm_w[layer])
    at = attention_branch(proj_a3, slopes)

    x1, h2, logits = merge_project(
        x_prompt.reshape(bp * seq, d), x_sample.reshape(bs * seq, d), hm.reshape(n, -1), at.reshape(n, -1), proj_a,
        mod3, p_a[layer].astype(BF16), p_b[layer].astype(BF16), w_out[layer].astype(BF16), norm2_w[layer],
        wr_hi, wr_lo, br, seq)

    tables, inv, route_w = route(logits, MOE_ROWS)
    ys = expert_ffn(h2, tables, w_expert_gate[layer], w_expert_up[layer], w_expert_down[layer])

    y_p = final_norm(x1, ys, inv, route_w, mod3, final_norm_w, 0, bp * seq, seq)
    y_s = final_norm(x1, ys, inv, route_w, mod3, final_norm_w, bp * seq, bs * seq, seq)
    return (y_p.reshape(bp, seq, d), y_s.reshape(bs, seq, d))
```

```python
import functools

import jax
import jax.numpy as jnp
from jax import lax
from jax.experimental import pallas as pl
from jax.experimental.pallas import tpu as pltpu

F32 = jnp.float32
BF16 = jnp.bfloat16

D_MODEL = 2048
RMS_EPS = 1e-6
M_HEADS = 8
M_QK_DIM = 128
M_V_DIM = 256
M_CHUNK = 128
A_GROUPS = ((128, 1), (512, 4), (2048, 16))
A_SLOTS = 4
A_HEADS = A_SLOTS * len(A_GROUPS)
A_HEAD_DIM = 128
N_GROUPS = 4
EXPERTS_PER_GROUP = 8
N_EXPERTS = N_GROUPS * EXPERTS_PER_GROUP
TOP_K = 2
D_EXPERT = 1024

M_QK_W = M_HEADS * M_QK_DIM
M_V_W = M_HEADS * M_V_DIM
A_W = A_HEADS * A_HEAD_DIM
PROJ_TN = 512
COL_Q_M = 0
COL_K_M = M_QK_W
COL_V_M = 2 * M_QK_W
COL_O_M = 2 * M_QK_W + M_V_W
COL_GATE_M = 2 * M_QK_W + 2 * M_V_W
GATE_M_W = 4 * M_HEADS
PROJ_M_COLS = COL_GATE_M + PROJ_TN
_SRC_A0 = COL_GATE_M + GATE_M_W
COL_Q_A = 0
COL_K_A = A_W
COL_V_A = 2 * A_W
COL_GATE_A = 3 * A_W
COL_GATE_B = 3 * A_W + D_MODEL
PROJ_A_COLS = 3 * A_W + 2 * D_MODEL

LANES = 128
MOE_ROWS = 256
ROUTER_COLS = 128
V7X_VMEM_BYTES = 64 * 1024 * 1024
VMEM_LIMIT = V7X_VMEM_BYTES * 7 // 8
EXPERT_VMEM_LIMIT = V7X_VMEM_BYTES * 15 // 16


def _sigmoid(x):
    return 1.0 / (1.0 + jnp.exp(-x))


def _cparams(sem, vmem=VMEM_LIMIT):
    return pltpu.CompilerParams(dimension_semantics=sem, vmem_limit_bytes=vmem)


def _ada_kernel(c_ref, w_ref, b_ref, o_ref):
    c = c_ref[...]
    a = (c * _sigmoid(c)).astype(BF16)
    o_ref[...] = jnp.dot(a, w_ref[...].astype(BF16), preferred_element_type=F32) + b_ref[...]


def ada_modulation(c, w_ada, b_ada):
    rows, d = c.shape
    n = w_ada.shape[1]
    tn = 1024
    return pl.pallas_call(
        _ada_kernel,
        grid=(n // tn,),
        in_specs=[pl.BlockSpec((rows, d), lambda j: (0, 0)),
                  pl.BlockSpec((d, tn), lambda j: (0, j)),
                  pl.BlockSpec((1, tn), lambda j: (0, j))],
        out_specs=pl.BlockSpec((rows, tn), lambda j: (0, j)),
        out_shape=jax.ShapeDtypeStruct((rows, n), F32),
        compiler_params=_cparams(("parallel",)),
        name="ada_modulation",
    )(c, w_ada, b_ada.reshape(1, n))


def _norm_mod_kernel(nb0, xp_ref, xs_ref, mod_ref, w_ref, o_ref):
    b = pl.program_id(0)
    x = jnp.where(b < nb0, xp_ref[0], xs_ref[0])
    y = x * lax.rsqrt(jnp.mean(x * x, axis=-1, keepdims=True) + RMS_EPS) * w_ref[...]
    o_ref[0] = (y * (1.0 + mod_ref[0, 1:2, :]) + mod_ref[0, 0:1, :]).astype(o_ref.dtype)


def norm_modulate(xp, xs, mod3, norm_w, ts=1024):
    nb0, s, d = xp.shape
    nb1 = xs.shape[0]
    return pl.pallas_call(
        functools.partial(_norm_mod_kernel, nb0),
        grid=(nb0 + nb1, s // ts),
        in_specs=[pl.BlockSpec((1, ts, d), lambda b, t: (jnp.minimum(b, nb0 - 1), jnp.where(b < nb0, t, s // ts - 1), 0)),
                  pl.BlockSpec((1, ts, d), lambda b, t: (jnp.maximum(b - nb0, 0), jnp.where(b < nb0, 0, t), 0)),
                  pl.BlockSpec((1, 6, d), lambda b, t: (b, 0, 0)),
                  pl.BlockSpec((1, d), lambda b, t: (0, 0))],
        out_specs=pl.BlockSpec((1, ts, d), lambda b, t: (b, t, 0)),
        out_shape=jax.ShapeDtypeStruct((nb0 + nb1, s, d), BF16),
        compiler_params=_cparams(("parallel", "parallel")),
        name="norm1_modulate",
    )(xp, xs, mod3, norm_w.reshape(1, d))


def _mm_bias_kernel(a_ref, w_ref, b_ref, o_ref):
    o_ref[...] = jnp.dot(a_ref[...], w_ref[...], preferred_element_type=F32) + b_ref[...]


def _slice_cols_kernel(col0, w_ref, o_ref):
    o_ref[...] = w_ref[:, col0:col0 + o_ref.shape[1]].astype(o_ref.dtype)


def slice_cols_bf16(w, layer, col0, n, tr=256):
    _, k, n_all = w.shape
    return pl.pallas_call(
        functools.partial(_slice_cols_kernel, col0),
        grid=(k // tr,),
        in_specs=[pl.BlockSpec((None, tr, n_all), lambda i: (layer, i, 0))],
        out_specs=pl.BlockSpec((tr, n), lambda i: (i, 0)),
        out_shape=jax.ShapeDtypeStruct((k, n), BF16),
        compiler_params=_cparams(("parallel",)),
        name="slice_cols_bf16",
    )(w)


def matmul_bias(a, w, b, n, name, layer=None, tm=2048, tn=PROJ_TN):
    m, k = a.shape
    if layer is None:
        w_spec = pl.BlockSpec((k, tn), lambda i, j: (0, j))
    else:
        w_spec = pl.BlockSpec((None, k, tn), lambda i, j: (layer, 0, j))
    return pl.pallas_call(
        _mm_bias_kernel,
        grid=(m // tm, n // tn),
        in_specs=[pl.BlockSpec((tm, k), lambda i, j: (i, 0)),
                  w_spec,
                  pl.BlockSpec((1, tn), lambda i, j: (0, j))],
        out_specs=pl.BlockSpec((tm, tn), lambda i, j: (i, j)),
        out_shape=jax.ShapeDtypeStruct((m, n), F32),
        compiler_params=_cparams(("parallel", "parallel")),
        name=name,
    )(a, w, b.reshape(1, -1))


def _lane_scan(x, op, fill, reverse):
    n = x.shape[-1]
    axis = x.ndim - 1
    lane = lax.broadcasted_iota(jnp.int32, x.shape, axis)
    k = 1
    while k < n:
        if reverse:
            x = op(x, jnp.where(lane < n - k, pltpu.roll(x, n - k, axis), fill))
        else:
            x = op(x, jnp.where(lane >= k, pltpu.roll(x, k, axis), fill))
        k *= 2
    return x


def _conv_silu_chunk(x_ref, w_ref, b_ref, c, n_chunks):
    L = M_CHUNK
    s = n_chunks * L
    r0 = pl.multiple_of(c * L, L)
    x = x_ref[0, pl.ds(r0, L), :]
    prev_row = x_ref[0, pl.ds(jnp.maximum(r0 - 1, 0), 1), :]
    next_row = x_ref[0, pl.ds(jnp.minimum(r0 + L, s - 1), 1), :]
    prev_row = jnp.where(c > 0, prev_row, 0.0)
    next_row = jnp.where(c < n_chunks - 1, next_row, 0.0)
    rows = lax.broadcasted_iota(jnp.int32, x.shape, 0)
    x_prev = jnp.where(rows == 0, prev_row, pltpu.roll(x, 1, 0))
    x_next = jnp.where(rows == L - 1, next_row, pltpu.roll(x, L - 1, 0))
    y = b_ref[...] + x_prev * w_ref[0:1, :] + x * w_ref[1:2, :] + x_next * w_ref[2:3, :]
    return y * _sigmoid(y)


def _mlstm_kernel(q_ref, k_ref, v_ref, o_ref, g_ref, cwq_ref, cwk_ref, cbq_ref, cbk_ref, nw_ref, out_ref,
                  qs_ref, kt_ref, hf_ref, hr_ref, c_ref,
                  u_ref, negm_ref, ib_ref, wk_ref, decay_ref, m0_ref, m1_ref):
    L = M_CHUNK
    dk = M_QK_DIM
    dv = M_V_DIM
    hp, nc = kt_ref.shape[0], kt_ref.shape[1]

    def prep(c, carry):
        q = _conv_silu_chunk(q_ref, cwq_ref, cbq_ref, c, nc) * (M_QK_DIM ** -0.5)
        qs_ref[c] = q.astype(BF16)
        k = _conv_silu_chunk(k_ref, cwk_ref, cbk_ref, c, nc)
        for hh in range(hp):
            kt_ref[hh, c] = k[:, hh * dk:(hh + 1) * dk].T
        return carry

    lax.fori_loop(0, nc, prep, 0, unroll=4)

    for hh in range(hp):
        for d in range(2):
            rev = d == 1
            i_pre = g_ref[0, hh, 2 * d]
            f_pre = g_ref[0, hh, 2 * d + 1]
            log_f = -(jnp.maximum(-f_pre, 0.0) + jnp.log1p(jnp.exp(-jnp.abs(f_pre))))
            b = _lane_scan(log_f, jnp.add, 0.0, rev)
            a = jnp.broadcast_to(b[:, 0:1] if rev else b[:, L - 1:L], (nc, L))
            g = a - b + i_pre
            g_max = jnp.broadcast_to(jnp.max(g, axis=1, keepdims=True), (nc, L))
            m = jnp.zeros((1, L), F32)
            for c in (range(nc - 1, -1, -1) if rev else range(nc)):
                m0_ref[hh, d, c:c + 1, :] = m
                m = jnp.maximum(a[c:c + 1, :] + m, g_max[c:c + 1, :])
                m1_ref[hh, d, c:c + 1, :] = m
            m0 = m0_ref[hh, d]
            m1 = m1_ref[hh, d]
            ib = i_pre - b
            m_t = jnp.maximum(b + m0, b + _lane_scan(ib, jnp.maximum, -jnp.inf, rev))
            ib_ref[hh, d] = ib
            u_ref[hh, d] = b - m_t
            negm_ref[hh, d] = -m_t
            wk_ref[hh, d] = jnp.exp(g - m1)
            decay_ref[hh, d] = jnp.exp(a + m0 - m1)

    t_idx = lax.broadcasted_iota(jnp.int32, (L, L), 0)
    s_idx = lax.broadcasted_iota(jnp.int32, (L, L), 1)
    ones_ext = jnp.ones((L, LANES), BF16)

    def chunk_step(hh, d, c):
        r0 = pl.multiple_of(c * L, L)
        q = qs_ref[c, :, hh * dk:(hh + 1) * dk]
        kt = kt_ref[hh, c]
        v_ext = jnp.concatenate([v_ref[0, pl.ds(r0, L), hh * dv:(hh + 1) * dv].astype(BF16), ones_ext], axis=1)

        def row(ref):
            return ref[hh, d, pl.ds(c, 1), :]

        umat = jnp.broadcast_to(row(u_ref), (L, L)).T
        nmat = jnp.broadcast_to(row(negm_ref), (L, L)).T
        causal = (s_idx <= t_idx) if d == 0 else (s_idx >= t_idx)
        w_intra = jnp.where(causal, jnp.exp(umat + row(ib_ref)), 0.0)
        w_inter = jnp.exp(umat + row(m0_ref))
        s_qk = jnp.dot(q, kt.astype(BF16), preferred_element_type=F32)
        c_ext = c_ref[hh, d]
        lhs = jnp.concatenate([(w_intra * s_qk).astype(BF16), (q.astype(F32) * w_inter).astype(BF16)], axis=1)
        rhs = jnp.concatenate([v_ext, c_ext.astype(BF16)], axis=0)
        num = jnp.dot(lhs, rhs, preferred_element_type=F32)
        r = 1.0 / jnp.maximum(jnp.abs(num[:, dv:]), jnp.exp(nmat))
        h = num[:, :dv] * jnp.concatenate([r] * (dv // LANES), axis=1)

        upd = jnp.dot((kt * row(wk_ref)).astype(BF16), v_ext, preferred_element_type=F32)
        decay = jnp.broadcast_to(row(decay_ref), (dk, LANES))
        c_ref[hh, d] = jnp.concatenate([decay] * (c_ext.shape[1] // LANES), axis=1) * c_ext + upd
        return h

    def finish(hh, c, hs):
        rows = pl.ds(pl.multiple_of(c * L, L), L)
        cols = slice(hh * dv, (hh + 1) * dv)
        y = hs * lax.rsqrt(jnp.mean(hs * hs, axis=-1, keepdims=True) + RMS_EPS) * nw_ref[:, cols]
        out_ref[0, rows, cols] = (y * _sigmoid(o_ref[0, rows, cols])).astype(out_ref.dtype)

    def first_half(j, carry):
        cf, cr = j, nc - 1 - j
        for hh in range(hp):
            hf_ref[hh, pl.ds(pl.multiple_of(cf * L, L), L), :] = chunk_step(hh, 0, cf)
            hr_ref[hh, pl.ds(pl.multiple_of((cr - nc // 2) * L, L), L), :] = chunk_step(hh, 1, cr)
        return carry

    def second_half(j, carry):
        cf, cr = j, nc - 1 - j
        for hh in range(hp):
            finish(hh, cf, chunk_step(hh, 0, cf) + hr_ref[hh, pl.ds(pl.multiple_of((cf - nc // 2) * L, L), L), :])
            finish(hh, cr, hf_ref[hh, pl.ds(pl.multiple_of(cr * L, L), L), :] + chunk_step(hh, 1, cr))
        return carry

    c_ref[...] = jnp.zeros_like(c_ref)
    lax.fori_loop(0, nc // 2, first_half, 0, unroll=4)
    lax.fori_loop(nc // 2, nc, second_half, 0, unroll=4)


def mlstm_branch(proj3, gates, conv_w, conv_b, norm_w, hp=2):
    bsz, s, _ = proj3.shape
    L = M_CHUNK
    nc = s // L
    dk, dv = M_QK_DIM, M_V_DIM
    assert dk == L and nc % 2 == 0 and M_HEADS % hp == 0
    wq, wv = hp * dk, hp * dv
    k_off = M_HEADS * dk // wq
    return pl.pallas_call(
        _mlstm_kernel,
        grid=(bsz, M_HEADS // hp),
        in_specs=[pl.BlockSpec((1, s, wq), lambda b, h: (b, 0, COL_Q_M // wq + h)),
                  pl.BlockSpec((1, s, wq), lambda b, h: (b, 0, COL_K_M // wq + h)),
                  pl.BlockSpec((1, s, wv), lambda b, h: (b, 0, COL_V_M // wv + h)),
                  pl.BlockSpec((1, s, wv), lambda b, h: (b, 0, COL_O_M // wv + h)),
                  pl.BlockSpec((1, hp, 4, nc, L), lambda b, h: (b, h, 0, 0, 0)),
                  pl.BlockSpec((3, wq), lambda b, h: (0, h)),
                  pl.BlockSpec((3, wq), lambda b, h: (0, k_off + h)),
                  pl.BlockSpec((1, wq), lambda b, h: (0, h)),
                  pl.BlockSpec((1, wq), lambda b, h: (0, k_off + h)),
                  pl.BlockSpec((1, wv), lambda b, h: (0, h))],
        out_specs=pl.BlockSpec((1, s, wv), lambda b, h: (b, 0, h)),
        out_shape=jax.ShapeDtypeStruct((bsz, s, M_HEADS * dv), BF16),
        scratch_shapes=[pltpu.VMEM((nc, L, wq), BF16),
                        pltpu.VMEM((hp, nc, dk, L), F32),
                        pltpu.VMEM((hp, s // 2, dv), F32),
                        pltpu.VMEM((hp, s // 2, dv), F32),
                        pltpu.VMEM((hp, 2, dk, dv + LANES), F32),
                        *([pltpu.VMEM((hp, 2, nc, L), F32)] * 7)],
        compiler_params=_cparams(("parallel", "parallel")),
        name="mlstm_branch",
    )(proj3, proj3, proj3, proj3, gates, conv_w, conv_w, conv_b.reshape(1, -1), conv_b.reshape(1, -1),
      norm_w.reshape(1, -1))


def _attn_kernel(slopes_ref, q0, q1, q2, k0, k1, k2, v0, v1, v2, out_ref, og_ref, lse_ref, bias_ref,
                 qd_ref, kd_ref, vd_ref, od_ref, ld_ref):
    s_len = out_ref.shape[1]
    dh = A_HEAD_DIM
    T = 128
    slot = pl.program_id(1)
    qs, ks, vs = (q0, q1, q2), (k0, k1, k2), (v0, v1, v2)

    for g, (window, dil) in enumerate(A_GROUPS):
        side = window // (2 * dil)
        u_len = s_len // dil
        nqb = u_len // T
        kw = min(T + 2 * side, u_len)
        slope = slopes_ref[g * A_SLOTS + slot] * float(dil)
        q_ref, k_ref, v_ref = qs[g], ks[g], vs[g]

        regroup = dil >= 8
        paired = regroup and nqb == 1 and dil % 2 == 0
        t_io = lax.broadcasted_iota(jnp.int32, (T, 2 * u_len if paired else kw), 0)
        k_io = lax.broadcasted_iota(jnp.int32, (T, 2 * u_len if paired else kw), 1)
        if paired:
            kw = 2 * u_len
            for case in range(2):
                rel = jnp.abs(t_io - (k_io - case * u_len))
                own = jnp.logical_and(k_io // u_len == case, rel <= side)
                bias_ref[g, case, :, :kw] = jnp.where(own, -slope * rel.astype(F32), -1e30)
        else:
            offsets = (0, side, kw - T) if nqb > 1 else (0,)
            for case, off in enumerate(offsets):
                rel = jnp.abs(t_io + off - k_io)
                bias_ref[g, case, :, :kw] = jnp.where(rel <= side, -slope * rel.astype(F32), -1e30)

        regroup = dil >= 8

        def residue_rows(r, dil=dil, u_len=u_len):
            return pl.ds(r, u_len, stride=dil), pl.ds(pl.multiple_of(r * u_len, u_len), u_len)

        if regroup:
            def gather_residue(r, carry, q_ref=q_ref, k_ref=k_ref, v_ref=v_ref):
                src, dst = residue_rows(r)
                qd_ref[dst, :] = (q_ref[0, src, :] * (dh ** -0.5)).astype(BF16)
                kd_ref[dst, :] = k_ref[0, src, :].astype(BF16)
                vd_ref[dst, :] = v_ref[0, src, :].astype(BF16)
                return carry

            lax.fori_loop(0, dil, gather_residue, 0, unroll=4)

        def block(idx, carry, g=g, dil=dil, side=side, u_len=u_len, nqb=nqb, kw=kw, regroup=regroup, paired=paired,
                  q_ref=q_ref, k_ref=k_ref, v_ref=v_ref):
            r = idx // nqb
            j = idx % nqb
            u0 = j * T
            ku0 = jnp.clip(u0 - side, 0, u_len - kw) if not paired else 0
            case = jnp.where(j == 0, 0, jnp.where(j == nqb - 1, 2, 1)) if nqb > 1 else 0
            if paired:
                case = r % 2
                q_rows = pl.ds(pl.multiple_of(r * u_len, T), T)
                k_rows = pl.ds(pl.multiple_of((r // 2) * kw, kw), kw)
                q, kk, vv = qd_ref[q_rows, :], kd_ref[k_rows, :], vd_ref[k_rows, :]
            elif regroup:
                q_rows = pl.ds(pl.multiple_of(r * u_len + u0, T), T)
                k_rows = pl.ds(pl.multiple_of(r * u_len + ku0, side), kw)
                q, kk, vv = qd_ref[q_rows, :], kd_ref[k_rows, :], vd_ref[k_rows, :]
            else:
                q_rows = pl.ds(r + u0 * dil, T, stride=dil) if dil > 1 else pl.ds(pl.multiple_of(u0, T), T)
                k_rows = pl.ds(r + ku0 * dil, kw, stride=dil) if dil > 1 else pl.ds(pl.multiple_of(ku0, side), kw)
                q = (q_ref[0, q_rows, :] * (dh ** -0.5)).astype(BF16)
                kk = k_ref[0, k_rows, :].astype(BF16)
                vv = v_ref[0, k_rows, :].astype(BF16)
            s = lax.dot_general(q, kk, (((1,), (1,)), ((), ())), preferred_element_type=F32)
            s = s + bias_ref[g, case, :, :kw]
            m = jnp.max(s, axis=1, keepdims=True)
            p = jnp.exp(s - m)
            den = jnp.sum(p, axis=1, keepdims=True)
            o = jnp.dot(p.astype(BF16), vv, preferred_element_type=F32) * (1.0 / den)
            lse = jnp.broadcast_to(m + jnp.log(den), (T, LANES))
            if regroup:
                od_ref[q_rows, :] = o
                ld_ref[q_rows, :] = lse
            else:
                og_ref[g, q_rows, :] = o
                lse_ref[g, q_rows, :] = lse
            return carry

        lax.fori_loop(0, dil * nqb, block, 0, unroll=16)

        if regroup:
            def scatter_residue(r, carry, g=g):
                dst, src = residue_rows(r)
                og_ref[g, dst, :] = od_ref[src, :]
                lse_ref[g, dst, :] = ld_ref[src, :]
                return carry

            lax.fori_loop(0, dil, scatter_residue, 0, unroll=4)

    rows_per = 256

    def merge(i, carry):
        rows = pl.ds(pl.multiple_of(i * rows_per, rows_per), rows_per)
        l0, l1, l2 = lse_ref[0, rows, :], lse_ref[1, rows, :], lse_ref[2, rows, :]
        mx = jnp.maximum(jnp.maximum(l0, l1), l2)
        e0, e1, e2 = jnp.exp(l0 - mx), jnp.exp(l1 - mx), jnp.exp(l2 - mx)
        inv = 1.0 / (e0 + e1 + e2)
        o = (e0 * inv) * og_ref[0, rows, :] + (e1 * inv) * og_ref[1, rows, :] + (e2 * inv) * og_ref[2, rows, :]
        out_ref[0, rows, :] = o.astype(out_ref.dtype)
        return carry

    lax.fori_loop(0, s_len // rows_per, merge, 0)


def attention_branch(proj3, slopes):
    bsz, s, _ = proj3.shape
    dh = A_HEAD_DIM

    def col(base, g):
        return lambda b, t, sl: (b, 0, base // dh + g * A_SLOTS + t)

    grid_spec = pltpu.PrefetchScalarGridSpec(
        num_scalar_prefetch=1,
        grid=(bsz, A_SLOTS),
        in_specs=[pl.BlockSpec((1, s, dh), col(base, g))
                  for base in (COL_Q_A, COL_K_A, COL_V_A) for g in range(len(A_GROUPS))],
        out_specs=pl.BlockSpec((1, s, dh), lambda b, t, sl: (b, 0, t)),
        scratch_shapes=[pltpu.VMEM((3, s, dh), F32), pltpu.VMEM((3, s, LANES), F32),
                        pltpu.VMEM((len(A_GROUPS), 3, 128, 256), F32),
                        pltpu.VMEM((s, dh), BF16), pltpu.VMEM((s, dh), BF16), pltpu.VMEM((s, dh), BF16),
                        pltpu.VMEM((s, dh), F32), pltpu.VMEM((s, LANES), F32)],
    )
    return pl.pallas_call(
        _attn_kernel,
        grid_spec=grid_spec,
        out_shape=jax.ShapeDtypeStruct((bsz, s, A_SLOTS * dh), BF16),
        compiler_params=_cparams(("parallel", "parallel")),
        name="dilated_attention",
    )(slopes, *([proj3] * 9))


def _merge_kernel(n0, n_gate, xp_ref, xs_ref, hm_ref, at_ref, *refs):
    ga_refs, gb_refs = refs[:n_gate], refs[n_gate:2 * n_gate]
    (mod_ref, pa_ref, pb_ref, wo_ref, n2_ref, wrh_ref, wrl_ref, br_ref, x1_ref, h2_ref, lg_ref) = refs[2 * n_gate:]
    i = pl.program_id(0)
    x = jnp.where(i < n0, xp_ref[...], xs_ref[...])
    y_a = jnp.dot(hm_ref[...], pa_ref[...], preferred_element_type=F32)
    y_b = jnp.dot(at_ref[...], pb_ref[...], preferred_element_type=F32)
    gate_a = jnp.concatenate([r[...] for r in ga_refs], axis=1)
    gate_b = jnp.concatenate([r[...] for r in gb_refs], axis=1)
    mixin = _sigmoid(gate_a) * y_a + _sigmoid(gate_b) * y_b
    mix = jnp.dot(mixin.astype(BF16), wo_ref[...], preferred_element_type=F32)
    x1 = x + mod_ref[0, 2:3, :] * mix
    x1_ref[...] = x1
    y = x1 * lax.rsqrt(jnp.mean(x1 * x1, axis=-1, keepdims=True) + RMS_EPS) * n2_ref[...]
    h2 = y * (1.0 + mod_ref[0, 4:5, :]) + mod_ref[0, 3:4, :]
    h2_ref[...] = h2
    hi = h2.astype(BF16)
    lo = (h2 - hi.astype(F32)).astype(BF16)
    lg_ref[...] = (jnp.dot(hi, wrh_ref[...], preferred_element_type=F32)
                   + (jnp.dot(hi, wrl_ref[...], preferred_element_type=F32)
                      + jnp.dot(lo, wrh_ref[...], preferred_element_type=F32))
                   + br_ref[...])


def merge_project(xp2, xs2, hm, at, proj, mod3, p_a, p_b, w_out, norm2_w, wr_hi, wr_lo, br, seq, tm=256):
    n, d = hm.shape
    n0 = xp2.shape[0] // tm
    n1 = xs2.shape[0] // tm
    per_seq = seq // tm
    const = dict(pipeline_mode=pl.Buffered(1))
    gw = PROJ_TN
    n_gate = d // gw

    def gate_specs(col0):
        return [pl.BlockSpec((tm, gw), lambda i, t=t: (i, col0 // gw + t)) for t in range(n_gate)]

    return pl.pallas_call(
        functools.partial(_merge_kernel, n0, n_gate),
        grid=(n0 + n1,),
        in_specs=[pl.BlockSpec((tm, d), lambda i: (jnp.minimum(i, n0 - 1), 0)),
                  pl.BlockSpec((tm, d), lambda i: (jnp.maximum(i - n0, 0), 0)),
                  pl.BlockSpec((tm, d), lambda i: (i, 0)),
                  pl.BlockSpec((tm, at.shape[1]), lambda i: (i, 0)),
                  *gate_specs(COL_GATE_A), *gate_specs(COL_GATE_B),
                  pl.BlockSpec((1, 6, d), lambda i: (i // per_seq, 0, 0)),
                  pl.BlockSpec(p_a.shape, lambda i: (0, 0), **const),
                  pl.BlockSpec(p_b.shape, lambda i: (0, 0), **const),
                  pl.BlockSpec(w_out.shape, lambda i: (0, 0), **const),
                  pl.BlockSpec((1, d), lambda i: (0, 0)),
                  pl.BlockSpec(wr_hi.shape, lambda i: (0, 0), **const),
                  pl.BlockSpec(wr_lo.shape, lambda i: (0, 0), **const),
                  pl.BlockSpec((1, ROUTER_COLS), lambda i: (0, 0))],
        out_specs=[pl.BlockSpec((tm, d), lambda i: (i, 0)),
                   pl.BlockSpec((tm, d), lambda i: (i, 0)),
                   pl.BlockSpec((tm, ROUTER_COLS), lambda i: (i, 0))],
        out_shape=[jax.ShapeDtypeStruct((n, d), F32),
                   jax.ShapeDtypeStruct((n, d), F32),
                   jax.ShapeDtypeStruct((n, ROUTER_COLS), F32)],
        compiler_params=_cparams(("parallel",)),
        name="merge_project",
    )(xp2, xs2, hm, at, *([proj] * (2 * n_gate)), mod3, p_a, p_b, w_out, norm2_w.reshape(1, d), wr_hi, wr_lo, br)


def route(logits, tb):
    n = logits.shape[0]
    g_logits = logits[:, :N_GROUPS]
    e_logits = logits[:, N_GROUPS:N_GROUPS + N_EXPERTS].reshape(n, N_GROUPS, EXPERTS_PER_GROUP)
    g_idx = jnp.argmax(g_logits, axis=-1)
    g_w = jnp.take_along_axis(jax.nn.softmax(g_logits, axis=-1), g_idx[:, None], axis=-1)
    e_sel = jnp.take_along_axis(e_logits, g_idx[:, None, None], axis=1)[:, 0]
    top_v, top_i = lax.top_k(e_sel, TOP_K)
    weights = g_w * jax.nn.softmax(top_v, axis=-1)
    expert = (g_idx[:, None] * EXPERTS_PER_GROUP + top_i).astype(jnp.int32)
    a = n * TOP_K
    flat_e = expert.reshape(a)
    e_ids = jnp.arange(N_EXPERTS, dtype=jnp.int32)
    counts = jnp.sum(flat_e[:, None] == e_ids[None, :], axis=0, dtype=jnp.int32)
    padded = (counts + tb - 1) // tb * tb
    pad_end = jnp.cumsum(padded)
    filler_e = jnp.repeat(e_ids, tb)
    filler_j = jnp.tile(jnp.arange(tb, dtype=jnp.int32), N_EXPERTS)
    filler_key = jnp.where(filler_j < (padded - counts)[filler_e], 2 * filler_e + 1, 2 * N_EXPERTS + 1)
    keys = jnp.concatenate([2 * flat_e, filler_key])
    ids = jnp.arange(a, dtype=jnp.int32)
    filler0 = jnp.zeros((N_EXPERTS * tb,), jnp.int32)
    tok_src = jnp.concatenate([ids // TOP_K, filler0])
    out_src = jnp.concatenate([(ids % TOP_K) * n + ids // TOP_K, filler0])
    sorted_keys, row_tok, row_out = lax.sort((keys, tok_src, out_src), num_keys=1)
    r = a + N_EXPERTS * tb
    n_blocks = r // tb
    block_start = jnp.arange(n_blocks, dtype=jnp.int32) * tb
    block_expert = jnp.minimum(jnp.sum(block_start[:, None] >= pad_end[None, :], axis=1), N_EXPERTS - 1).astype(jnp.int32)
    n_used = (pad_end[-1] // tb).astype(jnp.int32).reshape(1)
    later = jnp.logical_and(e_ids[None, :] > e_ids[:, None], (counts > 0)[None, :])
    next_of = jnp.min(jnp.where(later, e_ids[None, :], N_EXPERTS), axis=1)
    next_expert = jnp.where(next_of < N_EXPERTS, next_of, -1)[block_expert].astype(jnp.int32)
    pos = jnp.arange(r, dtype=jnp.int32)
    inv_key = jnp.where(sorted_keys % 2 == 0, row_out, a + pos)
    _, inv = lax.sort((inv_key, pos), num_keys=1)
    return (block_expert, next_expert, n_used, row_tok), inv[:a], weights


ROW_DMA_UNROLL = 64

def _issue_row_gather(src_hbm, dst_buf, sem, index_of, n_rows):
    def body(j, carry):
        pltpu.make_async_copy(src_hbm.at[pl.ds(index_of(j), 1), :], dst_buf.at[pl.ds(j, 1), :], sem).start()
        return carry
    lax.fori_loop(0, n_rows, body, 0, unroll=ROW_DMA_UNROLL)


def _wait_row_gather(src_hbm, dst_buf, sem):
    pltpu.make_async_copy(src_hbm.at[pl.ds(0, dst_buf.shape[0]), :], dst_buf, sem).wait()


def _round_rows_to_bf16(src_ref, dst_ref, rows=256):
    def body(t, carry):
        r = pl.ds(pl.multiple_of(t * rows, rows), rows)
        dst_ref[r, :] = src_ref[r, :].astype(BF16)
        return carry
    lax.fori_loop(0, src_ref.shape[0] // rows, body, 0)


def _expert_kernel(be_ref, nxt_ref, nused_ref, rtok_ref, h2_hbm, wg_hbm, wu_hbm, wd_hbm, out_ref,
                   xbuf, stg_g, stg_u, stg_d, wg_b, wu_b, wd_b, gsem, wsem):
    tb = xbuf.shape[1]
    i = pl.program_id(0)
    nused = nused_ref[0]
    slot = i % 2
    e = be_ref[i]
    first_of_expert = jnp.logical_or(i == 0, e != be_ref[jnp.maximum(i - 1, 0)])

    def issue(blk, s):
        _issue_row_gather(h2_hbm, xbuf.at[s], gsem.at[s], lambda j: rtok_ref[blk * tb + j], tb)

    def weight_copies(expert):
        return [pltpu.make_async_copy(src.at[expert], dst, wsem.at[k])
                for k, (src, dst) in enumerate(((wg_hbm, stg_g), (wu_hbm, stg_u), (wd_hbm, stg_d)))]

    @pl.when(jnp.logical_and(i == 0, nused > 0))
    def _():
        issue(0, 0)

    @pl.when(i + 1 < nused)
    def _():
        issue(i + 1, 1 - slot)

    @pl.when(i < nused)
    def _():
        @pl.when(first_of_expert)
        def _():
            @pl.when(i == 0)
            def _():
                for c in weight_copies(e):
                    c.start()

            for c in weight_copies(e):
                c.wait()
            for stg, wb in ((stg_g, wg_b), (stg_u, wu_b), (stg_d, wd_b)):
                _round_rows_to_bf16(stg, wb)
            nxt = nxt_ref[i]

            @pl.when(nxt >= 0)
            def _():
                for c in weight_copies(nxt):
                    c.start()

        _wait_row_gather(h2_hbm, xbuf.at[slot], gsem.at[slot])
        x = xbuf[slot].astype(BF16)
        g = jnp.dot(x, wg_b[...], preferred_element_type=F32)
        u = jnp.dot(x, wu_b[...], preferred_element_type=F32)
        hdn = (g * _sigmoid(g) * u).astype(BF16)
        out_ref[...] = jnp.dot(hdn, wd_b[...], preferred_element_type=F32)

    @pl.when(i >= nused)
    def _():
        out_ref[...] = jnp.zeros_like(out_ref)


def expert_ffn(h2, tables, wg, wu, wd, tb=MOE_ROWS):
    block_expert, next_expert, n_used, row_tok = tables
    n, d = h2.shape
    nb = block_expert.shape[0]
    de = wg.shape[2]
    any_spec = pl.BlockSpec(memory_space=pl.ANY)
    grid_spec = pltpu.PrefetchScalarGridSpec(
        num_scalar_prefetch=4,
        grid=(nb,),
        in_specs=[any_spec, any_spec, any_spec, any_spec],
        out_specs=pl.BlockSpec((tb, d), lambda i, *_: (i, 0)),
        scratch_shapes=[pltpu.VMEM((2, tb, d), F32),
                        pltpu.VMEM((d, de), F32), pltpu.VMEM((d, de), F32), pltpu.VMEM((de, d), F32),
                        pltpu.VMEM((d, de), BF16), pltpu.VMEM((d, de), BF16), pltpu.VMEM((de, d), BF16),
                        pltpu.SemaphoreType.DMA((2,)), pltpu.SemaphoreType.DMA((3,))],
    )
    return pl.pallas_call(
        _expert_kernel,
        grid_spec=grid_spec,
        out_shape=jax.ShapeDtypeStruct((nb * tb, d), F32),
        compiler_params=_cparams(("arbitrary",), EXPERT_VMEM_LIMIT),
        name="expert_ffn",
    )(block_expert, next_expert, n_used, row_tok, h2, wg, wu, wd)


def _final_kernel(n_tok, tile0, inv_ref, x1_ref, rw_ref, mod_ref, w_ref, ys_hbm, o_ref, buf, sem):
    tm = x1_ref.shape[0]
    i = pl.program_id(0)
    slot = i % 2

    def issue(tile, s):
        base = (tile0 + tile) * tm
        for k in range(TOP_K):
            _issue_row_gather(ys_hbm, buf.at[s, k], sem.at[s], lambda j, k=k: inv_ref[k * n_tok + base + j], tm)

    @pl.when(i == 0)
    def _():
        issue(0, 0)

    @pl.when(i + 1 < pl.num_programs(0))
    def _():
        issue(i + 1, 1 - slot)

    for k in range(TOP_K):
        _wait_row_gather(ys_hbm, buf.at[slot, k], sem.at[slot])
    rw = rw_ref[...]
    moe = buf[slot, 0] * rw[:, 0:1]
    for k in range(1, TOP_K):
        moe = moe + buf[slot, k] * rw[:, k:k + 1]
    x = x1_ref[...] + mod_ref[0, 5:6, :] * moe
    o_ref[...] = x * lax.rsqrt(jnp.mean(x * x, axis=-1, keepdims=True) + RMS_EPS) * w_ref[...]


def final_norm(x1, ys, inv, route_w, mod3, final_w, row0, rows, seq, tm=512):
    n_tok, d = x1.shape
    off = row0 // tm
    per_seq = seq // tm
    grid_spec = pltpu.PrefetchScalarGridSpec(
        num_scalar_prefetch=1,
        grid=(rows // tm,),
        in_specs=[pl.BlockSpec((tm, d), lambda i, inv: (off + i, 0)),
                  pl.BlockSpec((tm, TOP_K), lambda i, inv: (off + i, 0)),
                  pl.BlockSpec((1, 6, d), lambda i, inv: ((off + i) // per_seq, 0, 0)),
                  pl.BlockSpec((1, d), lambda i, inv: (0, 0)),
                  pl.BlockSpec(memory_space=pl.ANY)],
        out_specs=pl.BlockSpec((tm, d), lambda i, inv: (i, 0)),
        scratch_shapes=[pltpu.VMEM((2, TOP_K, tm, d), F32), pltpu.SemaphoreType.DMA((2,))],
    )
    return pl.pallas_call(
        functools.partial(_final_kernel, n_tok, off),
        grid_spec=grid_spec,
        out_shape=jax.ShapeDtypeStruct((rows, d), F32),
        compiler_params=_cparams(("arbitrary",)),
        name="final_norm",
    )(inv, x1, route_w, mod3, final_w.reshape(1, d), ys)


def kernel(x_prompt, x_sample, c_prompt, c_sample, w_ada, b_ada, norm1_w, w_in, b_in, mlstm_gate_b, conv_w, conv_b, mlstm_norm_w, p_a, p_b, w_out, norm2_w, w_router_group, b_router_group, w_router_expert, b_router_expert, w_expert_gate, w_expert_up, w_expert_down, final_norm_w):
    bp, seq, d = x_prompt.shape
    bs = x_sample.shape[0]
    bt = bp + bs
    n = bt * seq
    layer = 0

    w_in_bf = w_in.astype(BF16)
    w_in_a = slice_cols_bf16(w_in_bf, layer, _SRC_A0, PROJ_A_COLS)
    b_in_a = b_in[layer, _SRC_A0:]
    b_in_m = b_in[layer, :PROJ_M_COLS] + jnp.pad(mlstm_gate_b[layer], (COL_GATE_M, PROJ_M_COLS - _SRC_A0))
    wr = jnp.concatenate([w_router_group[layer], w_router_expert[layer],
                          jnp.zeros((d, ROUTER_COLS - N_GROUPS - N_EXPERTS), F32)], axis=1)
    br = jnp.concatenate([b_router_group[layer], b_router_expert[layer],
                          jnp.zeros((ROUTER_COLS - N_GROUPS - N_EXPERTS,), F32)]).reshape(1, ROUTER_COLS)
    wr_hi = wr.astype(BF16)
    wr_lo = (wr - wr_hi.astype(F32)).astype(BF16)
    slopes = 2.0 ** (-8.0 * jnp.arange(1, A_HEADS + 1, dtype=F32) / A_HEADS)

    c_all = jnp.concatenate([c_prompt, c_sample, jnp.zeros((16 - bt, d), F32)], axis=0)
    mod3 = ada_modulation(c_all, w_ada[layer], b_ada[layer])[:bt].reshape(bt, 6, d)

    h = norm_modulate(x_prompt, x_sample, mod3, norm1_w[layer])
    h2d = h.reshape(n, d)
    proj_m = matmul_bias(h2d, w_in_bf, b_in_m, PROJ_M_COLS, "in_projection_mlstm", layer=layer)
    proj_a = matmul_bias(h2d, w_in_a, b_in_a, PROJ_A_COLS, "in_projection")
    proj_m3 = proj_m.reshape(bt, seq, PROJ_M_COLS)
    proj_a3 = proj_a.reshape(bt, seq, PROJ_A_COLS)

    nc = seq // M_CHUNK
    gates = proj_m3[:, :, COL_GATE_M:COL_GATE_M + GATE_M_W].reshape(bt, nc, M_CHUNK, 4, M_HEADS)
    gates = gates.transpose(0, 4, 3, 1, 2)
    hm = mlstm_branch(proj_m3, gates, conv_w[layer], conv_b[layer], mlstm_norm_w[layer])
    at = attention_branch(proj_a3, slopes)

    x1, h2, logits = merge_project(
        x_prompt.reshape(bp * seq, d), x_sample.reshape(bs * seq, d), hm.reshape(n, -1), at.reshape(n, -1), proj_a,
        mod3, p_a[layer].astype(BF16), p_b[layer].astype(BF16), w_out[layer].astype(BF16), norm2_w[layer],
        wr_hi, wr_lo, br, seq)

    tables, inv, route_w = route(logits, MOE_ROWS)
    ys = expert_ffn(h2, tables, w_expert_gate[layer], w_expert_up[layer], w_expert_down[layer])

    y_p = final_norm(x1, ys, inv, route_w, mod3, final_norm_w, 0, bp * seq, seq)
    y_s = final_norm(x1, ys, inv, route_w, mod3, final_norm_w, bp * seq, bs * seq, seq)
    return (y_p.reshape(bp, seq, d), y_s.reshape(bs, seq, d))
```

```python
import functools

import jax
import jax.numpy as jnp
from jax import lax
from jax.experimental import pallas as pl
from jax.experimental.pallas import tpu as pltpu

F32 = jnp.float32
BF16 = jnp.bfloat16

D_MODEL = 2048
RMS_EPS = 1e-6
M_HEADS = 8
M_QK_DIM = 128
M_V_DIM = 256
M_CHUNK = 128
A_GROUPS = ((128, 1), (512, 4), (2048, 16))
A_SLOTS = 4
A_HEADS = A_SLOTS * len(A_GROUPS)
A_HEAD_DIM = 128
N_GROUPS = 4
EXPERTS_PER_GROUP = 8
N_EXPERTS = N_GROUPS * EXPERTS_PER_GROUP
TOP_K = 2
D_EXPERT = 1024

M_QK_W = M_HEADS * M_QK_DIM
M_V_W = M_HEADS * M_V_DIM
A_W = A_HEADS * A_HEAD_DIM
PROJ_TN = 512
COL_Q_M = 0
COL_K_M = M_QK_W
COL_V_M = 2 * M_QK_W
COL_O_M = 2 * M_QK_W + M_V_W
COL_GATE_M = 2 * M_QK_W + 2 * M_V_W
GATE_M_W = 4 * M_HEADS
PROJ_M_COLS = COL_GATE_M + PROJ_TN
_SRC_A0 = COL_GATE_M + GATE_M_W
COL_Q_A = 0
COL_K_A = A_W
COL_V_A = 2 * A_W
COL_GATE_A = 3 * A_W
COL_GATE_B = 3 * A_W + D_MODEL
PROJ_A_COLS = 3 * A_W + 2 * D_MODEL

LANES = 128
MOE_ROWS = 256
ROUTER_COLS = 128
V7X_VMEM_BYTES = 64 * 1024 * 1024
VMEM_LIMIT = V7X_VMEM_BYTES * 7 // 8
EXPERT_VMEM_LIMIT = V7X_VMEM_BYTES * 15 // 16


def _sigmoid(x):
    return 1.0 / (1.0 + jnp.exp(-x))


def _cparams(sem, vmem=VMEM_LIMIT):
    return pltpu.CompilerParams(dimension_semantics=sem, vmem_limit_bytes=vmem)


def _ada_kernel(c_ref, w_ref, b_ref, o_ref):
    c = c_ref[...]
    a = (c * _sigmoid(c)).astype(BF16)
    o_ref[...] = jnp.dot(a, w_ref[...].astype(BF16), preferred_element_type=F32) + b_ref[...]


def ada_modulation(c, w_ada, b_ada):
    rows, d = c.shape
    n = w_ada.shape[1]
    tn = 1024
    return pl.pallas_call(
        _ada_kernel,
        grid=(n // tn,),
        in_specs=[pl.BlockSpec((rows, d), lambda j: (0, 0)),
                  pl.BlockSpec((d, tn), lambda j: (0, j)),
                  pl.BlockSpec((1, tn), lambda j: (0, j))],
        out_specs=pl.BlockSpec((rows, tn), lambda j: (0, j)),
        out_shape=jax.ShapeDtypeStruct((rows, n), F32),
        compiler_params=_cparams(("parallel",)),
        name="ada_modulation",
    )(c, w_ada, b_ada.reshape(1, n))


def _norm_mod_kernel(nb0, xp_ref, xs_ref, mod_ref, w_ref, o_ref):
    b = pl.program_id(0)
    x = jnp.where(b < nb0, xp_ref[0], xs_ref[0])
    y = x * lax.rsqrt(jnp.mean(x * x, axis=-1, keepdims=True) + RMS_EPS) * w_ref[...]
    o_ref[0] = (y * (1.0 + mod_ref[0, 1:2, :]) + mod_ref[0, 0:1, :]).astype(o_ref.dtype)


def norm_modulate(xp, xs, mod3, norm_w, ts=1024):
    nb0, s, d = xp.shape
    nb1 = xs.shape[0]
    return pl.pallas_call(
        functools.partial(_norm_mod_kernel, nb0),
        grid=(nb0 + nb1, s // ts),
        in_specs=[pl.BlockSpec((1, ts, d), lambda b, t: (jnp.minimum(b, nb0 - 1), jnp.where(b < nb0, t, s // ts - 1), 0)),
                  pl.BlockSpec((1, ts, d), lambda b, t: (jnp.maximum(b - nb0, 0), jnp.where(b < nb0, 0, t), 0)),
                  pl.BlockSpec((1, 6, d), lambda b, t: (b, 0, 0)),
                  pl.BlockSpec((1, d), lambda b, t: (0, 0))],
        out_specs=pl.BlockSpec((1, ts, d), lambda b, t: (b, t, 0)),
        out_shape=jax.ShapeDtypeStruct((nb0 + nb1, s, d), BF16),
        compiler_params=_cparams(("parallel", "parallel")),
        name="norm1_modulate",
    )(xp, xs, mod3, norm_w.reshape(1, d))


def _mm_bias_kernel(a_ref, w_ref, b_ref, o_ref):
    o_ref[...] = jnp.dot(a_ref[...], w_ref[...], preferred_element_type=F32) + b_ref[...]


def _slice_cols_kernel(col0, w_ref, o_ref):
    o_ref[...] = w_ref[:, col0:col0 + o_ref.shape[1]].astype(o_ref.dtype)


def slice_cols_bf16(w, layer, col0, n, tr=256):
    _, k, n_all = w.shape
    return pl.pallas_call(
        functools.partial(_slice_cols_kernel, col0),
        grid=(k // tr,),
        in_specs=[pl.BlockSpec((None, tr, n_all), lambda i: (layer, i, 0))],
        out_specs=pl.BlockSpec((tr, n), lambda i: (i, 0)),
        out_shape=jax.ShapeDtypeStruct((k, n), BF16),
        compiler_params=_cparams(("parallel",)),
        name="slice_cols_bf16",
    )(w)


def matmul_bias(a, w, b, n, name, layer=None, tm=2048, tn=PROJ_TN):
    m, k = a.shape
    if layer is None:
        w_spec = pl.BlockSpec((k, tn), lambda i, j: (0, j))
    else:
        w_spec = pl.BlockSpec((None, k, tn), lambda i, j: (layer, 0, j))
    return pl.pallas_call(
        _mm_bias_kernel,
        grid=(m // tm, n // tn),
        in_specs=[pl.BlockSpec((tm, k), lambda i, j: (i, 0)),
                  w_spec,
                  pl.BlockSpec((1, tn), lambda i, j: (0, j))],
        out_specs=pl.BlockSpec((tm, tn), lambda i, j: (i, j)),
        out_shape=jax.ShapeDtypeStruct((m, n), F32),
        compiler_params=_cparams(("parallel", "parallel")),
        name=name,
    )(a, w, b.reshape(1, -1))


def _lane_scan(x, op, fill, reverse):
    n = x.shape[-1]
    axis = x.ndim - 1
    lane = lax.broadcasted_iota(jnp.int32, x.shape, axis)
    k = 1
    while k < n:
        if reverse:
            x = op(x, jnp.where(lane < n - k, pltpu.roll(x, n - k, axis), fill))
        else:
            x = op(x, jnp.where(lane >= k, pltpu.roll(x, k, axis), fill))
        k *= 2
    return x


def _conv_silu_chunk(x_ref, w_ref, b_ref, c, n_chunks):
    L = M_CHUNK
    s = n_chunks * L
    r0 = pl.multiple_of(c * L, L)
    x = x_ref[0, pl.ds(r0, L), :]
    prev_row = x_ref[0, pl.ds(jnp.maximum(r0 - 1, 0), 1), :]
    next_row = x_ref[0, pl.ds(jnp.minimum(r0 + L, s - 1), 1), :]
    prev_row = jnp.where(c > 0, prev_row, 0.0)
    next_row = jnp.where(c < n_chunks - 1, next_row, 0.0)
    rows = lax.broadcasted_iota(jnp.int32, x.shape, 0)
    x_prev = jnp.where(rows == 0, prev_row, pltpu.roll(x, 1, 0))
    x_next = jnp.where(rows == L - 1, next_row, pltpu.roll(x, L - 1, 0))
    y = b_ref[...] + x_prev * w_ref[0:1, :] + x * w_ref[1:2, :] + x_next * w_ref[2:3, :]
    return y * _sigmoid(y)


def _mlstm_kernel(q_ref, k_ref, v_ref, o_ref, g_ref, cwq_ref, cwk_ref, cbq_ref, cbk_ref, nw_ref, out_ref,
                  qs_ref, kt_ref, hf_ref, hr_ref, c_ref,
                  u_ref, negm_ref, ib_ref, wk_ref, decay_ref, m0_ref, m1_ref):
    L = M_CHUNK
    dk = M_QK_DIM
    dv = M_V_DIM
    hp, nc = kt_ref.shape[0], kt_ref.shape[1]

    def prep(c, carry):
        q = _conv_silu_chunk(q_ref, cwq_ref, cbq_ref, c, nc) * (M_QK_DIM ** -0.5)
        qs_ref[c] = q.astype(BF16)
        k = _conv_silu_chunk(k_ref, cwk_ref, cbk_ref, c, nc)
        for hh in range(hp):
            kt_ref[hh, c] = k[:, hh * dk:(hh + 1) * dk].T
        return carry

    lax.fori_loop(0, nc, prep, 0, unroll=4)

    for hh in range(hp):
        for d in range(2):
            rev = d == 1
            i_pre = g_ref[0, hh, 2 * d]
            f_pre = g_ref[0, hh, 2 * d + 1]
            log_f = -(jnp.maximum(-f_pre, 0.0) + jnp.log1p(jnp.exp(-jnp.abs(f_pre))))
            b = _lane_scan(log_f, jnp.add, 0.0, rev)
            a = jnp.broadcast_to(b[:, 0:1] if rev else b[:, L - 1:L], (nc, L))
            g = a - b + i_pre
            g_max = jnp.broadcast_to(jnp.max(g, axis=1, keepdims=True), (nc, L))
            m = jnp.zeros((1, L), F32)
            for c in (range(nc - 1, -1, -1) if rev else range(nc)):
                m0_ref[hh, d, c:c + 1, :] = m
                m = jnp.maximum(a[c:c + 1, :] + m, g_max[c:c + 1, :])
                m1_ref[hh, d, c:c + 1, :] = m
            m0 = m0_ref[hh, d]
            m1 = m1_ref[hh, d]
            ib = i_pre - b
            m_t = jnp.maximum(b + m0, b + _lane_scan(ib, jnp.maximum, -jnp.inf, rev))
            ib_ref[hh, d] = ib
            u_ref[hh, d] = b - m_t
            negm_ref[hh, d] = -m_t
            wk_ref[hh, d] = jnp.exp(g - m1)
            decay_ref[hh, d] = jnp.exp(a + m0 - m1)

    t_idx = lax.broadcasted_iota(jnp.int32, (L, L), 0)
    s_idx = lax.broadcasted_iota(jnp.int32, (L, L), 1)
    ones_ext = jnp.ones((L, LANES), BF16)

    def chunk_step(hh, d, c):
        r0 = pl.multiple_of(c * L, L)
        q = qs_ref[c, :, hh * dk:(hh + 1) * dk]
        kt = kt_ref[hh, c]
        v_ext = jnp.concatenate([v_ref[0, pl.ds(r0, L), hh * dv:(hh + 1) * dv].astype(BF16), ones_ext], axis=1)

        def row(ref):
            return ref[hh, d, pl.ds(c, 1), :]

        umat = jnp.broadcast_to(row(u_ref), (L, L)).T
        nmat = jnp.broadcast_to(row(negm_ref), (L, L)).T
        causal = (s_idx <= t_idx) if d == 0 else (s_idx >= t_idx)
        w_intra = jnp.where(causal, jnp.exp(umat + row(ib_ref)), 0.0)
        w_inter = jnp.exp(umat + row(m0_ref))
        s_qk = jnp.dot(q, kt.astype(BF16), preferred_element_type=F32)
        c_ext = c_ref[hh, d]
        lhs = jnp.concatenate([(w_intra * s_qk).astype(BF16), (q.astype(F32) * w_inter).astype(BF16)], axis=1)
        rhs = jnp.concatenate([v_ext, c_ext.astype(BF16)], axis=0)
        num = jnp.dot(lhs, rhs, preferred_element_type=F32)
        r = 1.0 / jnp.maximum(jnp.abs(num[:, dv:]), jnp.exp(nmat))
        h = num[:, :dv] * jnp.concatenate([r] * (dv // LANES), axis=1)

        upd = jnp.dot((kt * row(wk_ref)).astype(BF16), v_ext, preferred_element_type=F32)
        decay = jnp.broadcast_to(row(decay_ref), (dk, LANES))
        c_ref[hh, d] = jnp.concatenate([decay] * (c_ext.shape[1] // LANES), axis=1) * c_ext + upd
        return h

    def finish(hh, c, hs):
        rows = pl.ds(pl.multiple_of(c * L, L), L)
        cols = slice(hh * dv, (hh + 1) * dv)
        y = hs * lax.rsqrt(jnp.mean(hs * hs, axis=-1, keepdims=True) + RMS_EPS) * nw_ref[:, cols]
        out_ref[0, rows, cols] = (y * _sigmoid(o_ref[0, rows, cols])).astype(out_ref.dtype)

    def first_half(j, carry):
        cf, cr = j, nc - 1 - j
        for hh in range(hp):
            hf_ref[hh, pl.ds(pl.multiple_of(cf * L, L), L), :] = chunk_step(hh, 0, cf)
            hr_ref[hh, pl.ds(pl.multiple_of((cr - nc // 2) * L, L), L), :] = chunk_step(hh, 1, cr)
        return carry

    def second_half(j, carry):
        cf, cr = j, nc - 1 - j
        for hh in range(hp):
            finish(hh, cf, chunk_step(hh, 0, cf) + hr_ref[hh, pl.ds(pl.multiple_of((cf - nc // 2) * L, L), L), :])
            finish(hh, cr, hf_ref[hh, pl.ds(pl.multiple_of(cr * L, L), L), :] + chunk_step(hh, 1, cr))
        return carry

    c_ref[...] = jnp.zeros_like(c_ref)
    lax.fori_loop(0, nc // 2, first_half, 0, unroll=4)
    lax.fori_loop(nc // 2, nc, second_half, 0, unroll=4)


def mlstm_branch(proj3, gates, conv_w, conv_b, norm_w, hp=2):
    bsz, s, _ = proj3.shape
    L = M_CHUNK
    nc = s // L
    dk, dv = M_QK_DIM, M_V_DIM
    assert dk == L and nc % 2 == 0 and M_HEADS % hp == 0
    wq, wv = hp * dk, hp * dv
    k_off = M_HEADS * dk // wq
    return pl.pallas_call(
        _mlstm_kernel,
        grid=(bsz, M_HEADS // hp),
        in_specs=[pl.BlockSpec((1, s, wq), lambda b, h: (b, 0, COL_Q_M // wq + h)),
                  pl.BlockSpec((1, s, wq), lambda b, h: (b, 0, COL_K_M // wq + h)),
                  pl.BlockSpec((1, s, wv), lambda b, h: (b, 0, COL_V_M // wv + h)),
                  pl.BlockSpec((1, s, wv), lambda b, h: (b, 0, COL_O_M // wv + h)),
                  pl.BlockSpec((1, hp, 4, nc, L), lambda b, h: (b, h, 0, 0, 0)),
                  pl.BlockSpec((3, wq), lambda b, h: (0, h)),
                  pl.BlockSpec((3, wq), lambda b, h: (0, k_off + h)),
                  pl.BlockSpec((1, wq), lambda b, h: (0, h)),
                  pl.BlockSpec((1, wq), lambda b, h: (0, k_off + h)),
                  pl.BlockSpec((1, wv), lambda b, h: (0, h))],
        out_specs=pl.BlockSpec((1, s, wv), lambda b, h: (b, 0, h)),
        out_shape=jax.ShapeDtypeStruct((bsz, s, M_HEADS * dv), BF16),
        scratch_shapes=[pltpu.VMEM((nc, L, wq), BF16),
                        pltpu.VMEM((hp, nc, dk, L), F32),
                        pltpu.VMEM((hp, s // 2, dv), F32),
                        pltpu.VMEM((hp, s // 2, dv), F32),
                        pltpu.VMEM((hp, 2, dk, dv + LANES), F32),
                        *([pltpu.VMEM((hp, 2, nc, L), F32)] * 7)],
        compiler_params=_cparams(("parallel", "parallel")),
        name="mlstm_branch",
    )(proj3, proj3, proj3, proj3, gates, conv_w, conv_w, conv_b.reshape(1, -1), conv_b.reshape(1, -1),
      norm_w.reshape(1, -1))


def _attn_kernel(slopes_ref, q0, q1, q2, k0, k1, k2, v0, v1, v2, out_ref, og_ref, lse_ref, bias_ref,
                 qd_ref, kd_ref, vd_ref, od_ref, ld_ref):
    s_len = out_ref.shape[1]
    dh = A_HEAD_DIM
    T = 128
    slot = pl.program_id(1)
    qs, ks, vs = (q0, q1, q2), (k0, k1, k2), (v0, v1, v2)

    for g, (window, dil) in enumerate(A_GROUPS):
        side = window // (2 * dil)
        u_len = s_len // dil
        nqb = u_len // T
        kw = min(T + 2 * side, u_len)
        slope = slopes_ref[g * A_SLOTS + slot] * float(dil)
        q_ref, k_ref, v_ref = qs[g], ks[g], vs[g]

        regroup = dil >= 8
        paired = regroup and nqb == 1 and dil % 2 == 0
        t_io = lax.broadcasted_iota(jnp.int32, (T, 2 * u_len if paired else kw), 0)
        k_io = lax.broadcasted_iota(jnp.int32, (T, 2 * u_len if paired else kw), 1)
        if paired:
            kw = 2 * u_len
            for case in range(2):
                rel = jnp.abs(t_io - (k_io - case * u_len))
                own = jnp.logical_and(k_io // u_len == case, rel <= side)
                bias_ref[g, case, :, :kw] = jnp.where(own, -slope * rel.astype(F32), -1e30)
        else:
            offsets = (0, side, kw - T) if nqb > 1 else (0,)
            for case, off in enumerate(offsets):
                rel = jnp.abs(t_io + off - k_io)
                bias_ref[g, case, :, :kw] = jnp.where(rel <= side, -slope * rel.astype(F32), -1e30)

        regroup = dil >= 8

        def residue_rows(r, dil=dil, u_len=u_len):
            return pl.ds(r, u_len, stride=dil), pl.ds(pl.multiple_of(r * u_len, u_len), u_len)

        if regroup:
            def gather_residue(r, carry, q_ref=q_ref, k_ref=k_ref, v_ref=v_ref):
                src, dst = residue_rows(r)
                qd_ref[dst, :] = (q_ref[0, src, :] * (dh ** -0.5)).astype(BF16)
                kd_ref[dst, :] = k_ref[0, src, :].astype(BF16)
                vd_ref[dst, :] = v_ref[0, src, :].astype(BF16)
                return carry

            lax.fori_loop(0, dil, gather_residue, 0, unroll=4)

        def block(idx, carry, g=g, dil=dil, side=side, u_len=u_len, nqb=nqb, kw=kw, regroup=regroup, paired=paired,
                  q_ref=q_ref, k_ref=k_ref, v_ref=v_ref):
            r = idx // nqb
            j = idx % nqb
            u0 = j * T
            ku0 = jnp.clip(u0 - side, 0, u_len - kw) if not paired else 0
            case = jnp.where(j == 0, 0, jnp.where(j == nqb - 1, 2, 1)) if nqb > 1 else 0
            if paired:
                case = r % 2
                q_rows = pl.ds(pl.multiple_of(r * u_len, T), T)
                k_rows = pl.ds(pl.multiple_of((r // 2) * kw, kw), kw)
                q, kk, vv = qd_ref[q_rows, :], kd_ref[k_rows, :], vd_ref[k_rows, :]
            elif regroup:
                q_rows = pl.ds(pl.multiple_of(r * u_len + u0, T), T)
                k_rows = pl.ds(pl.multiple_of(r * u_len + ku0, side), kw)
                q, kk, vv = qd_ref[q_rows, :], kd_ref[k_rows, :], vd_ref[k_rows, :]
            else:
                q_rows = pl.ds(r + u0 * dil, T, stride=dil) if dil > 1 else pl.ds(pl.multiple_of(u0, T), T)
                k_rows = pl.ds(r + ku0 * dil, kw, stride=dil) if dil > 1 else pl.ds(pl.multiple_of(ku0, side), kw)
                q = (q_ref[0, q_rows, :] * (dh ** -0.5)).astype(BF16)
                kk = k_ref[0, k_rows, :].astype(BF16)
                vv = v_ref[0, k_rows, :].astype(BF16)
            s = lax.dot_general(q, kk, (((1,), (1,)), ((), ())), preferred_element_type=F32)
            s = s + bias_ref[g, case, :, :kw]
            m = jnp.max(s, axis=1, keepdims=True)
            p = jnp.exp(s - m)
            den = jnp.sum(p, axis=1, keepdims=True)
            o = jnp.dot(p.astype(BF16), vv, preferred_element_type=F32) * (1.0 / den)
            lse = jnp.broadcast_to(m + jnp.log(den), (T, LANES))
            if regroup:
                od_ref[q_rows, :] = o
                ld_ref[q_rows, :] = lse
            else:
                og_ref[g, q_rows, :] = o
                lse_ref[g, q_rows, :] = lse
            return carry

        lax.fori_loop(0, dil * nqb, block, 0, unroll=16)

        if regroup:
            def scatter_residue(r, carry, g=g):
                dst, src = residue_rows(r)
                og_ref[g, dst, :] = od_ref[src, :]
                lse_ref[g, dst, :] = ld_ref[src, :]
                return carry

            lax.fori_loop(0, dil, scatter_residue, 0, unroll=4)

    rows_per = 256

    def merge(i, carry):
        rows = pl.ds(pl.multiple_of(i * rows_per, rows_per), rows_per)
        l0, l1, l2 = lse_ref[0, rows, :], lse_ref[1, rows, :], lse_ref[2, rows, :]
        mx = jnp.maximum(jnp.maximum(l0, l1), l2)
        e0, e1, e2 = jnp.exp(l0 - mx), jnp.exp(l1 - mx), jnp.exp(l2 - mx)
        inv = 1.0 / (e0 + e1 + e2)
        o = (e0 * inv) * og_ref[0, rows, :] + (e1 * inv) * og_ref[1, rows, :] + (e2 * inv) * og_ref[2, rows, :]
        out_ref[0, rows, :] = o.astype(out_ref.dtype)
        return carry

    lax.fori_loop(0, s_len // rows_per, merge, 0)


def attention_branch(proj3, slopes):
    bsz, s, _ = proj3.shape
    dh = A_HEAD_DIM

    def col(base, g):
        return lambda b, t, sl: (b, 0, base // dh + g * A_SLOTS + t)

    grid_spec = pltpu.PrefetchScalarGridSpec(
        num_scalar_prefetch=1,
        grid=(bsz, A_SLOTS),
        in_specs=[pl.BlockSpec((1, s, dh), col(base, g))
                  for base in (COL_Q_A, COL_K_A, COL_V_A) for g in range(len(A_GROUPS))],
        out_specs=pl.BlockSpec((1, s, dh), lambda b, t, sl: (b, 0, t)),
        scratch_shapes=[pltpu.VMEM((3, s, dh), F32), pltpu.VMEM((3, s, LANES), F32),
                        pltpu.VMEM((len(A_GROUPS), 3, 128, 256), F32),
                        pltpu.VMEM((s, dh), BF16), pltpu.VMEM((s, dh), BF16), pltpu.VMEM((s, dh), BF16),
                        pltpu.VMEM((s, dh), F32), pltpu.VMEM((s, LANES), F32)],
    )
    return pl.pallas_call(
        _attn_kernel,
        grid_spec=grid_spec,
        out_shape=jax.ShapeDtypeStruct((bsz, s, A_SLOTS * dh), BF16),
        compiler_params=_cparams(("parallel", "parallel")),
        name="dilated_attention",
    )(slopes, *([proj3] * 9))


def _merge_kernel(n0, n_gate, xp_ref, xs_ref, hm_ref, at_ref, *refs):
    ga_refs, gb_refs = refs[:n_gate], refs[n_gate:2 * n_gate]
    (mod_ref, pa_ref, pb_ref, wo_ref, n2_ref, wrh_ref, wrl_ref, br_ref, x1_ref, h2_ref, lg_ref) = refs[2 * n_gate:]
    i = pl.program_id(0)
    x = jnp.where(i < n0, xp_ref[...], xs_ref[...])
    y_a = jnp.dot(hm_ref[...], pa_ref[...], preferred_element_type=F32)
    y_b = jnp.dot(at_ref[...], pb_ref[...], preferred_element_type=F32)
    gate_a = jnp.concatenate([r[...] for r in ga_refs], axis=1)
    gate_b = jnp.concatenate([r[...] for r in gb_refs], axis=1)
    mixin = _sigmoid(gate_a) * y_a + _sigmoid(gate_b) * y_b
    mix = jnp.dot(mixin.astype(BF16), wo_ref[...], preferred_element_type=F32)
    x1 = x + mod_ref[0, 2:3, :] * mix
    x1_ref[...] = x1
    y = x1 * lax.rsqrt(jnp.mean(x1 * x1, axis=-1, keepdims=True) + RMS_EPS) * n2_ref[...]
    h2 = y * (1.0 + mod_ref[0, 4:5, :]) + mod_ref[0, 3:4, :]
    h2_ref[...] = h2
    hi = h2.astype(BF16)
    lo = (h2 - hi.astype(F32)).astype(BF16)
    lg_ref[...] = (jnp.dot(hi, wrh_ref[...], preferred_element_type=F32)
                   + (jnp.dot(hi, wrl_ref[...], preferred_element_type=F32)
                      + jnp.dot(lo, wrh_ref[...], preferred_element_type=F32))
                   + br_ref[...])


def merge_project(xp2, xs2, hm, at, proj, mod3, p_a, p_b, w_out, norm2_w, wr_hi, wr_lo, br, seq, tm=256):
    n, d = hm.shape
    n0 = xp2.shape[0] // tm
    n1 = xs2.shape[0] // tm
    per_seq = seq // tm
    const = dict(pipeline_mode=pl.Buffered(1))
    gw = PROJ_TN
    n_gate = d // gw

    def gate_specs(col0):
        return [pl.BlockSpec((tm, gw), lambda i, t=t: (i, col0 // gw + t)) for t in range(n_gate)]

    return pl.pallas_call(
        functools.partial(_merge_kernel, n0, n_gate),
        grid=(n0 + n1,),
        in_specs=[pl.BlockSpec((tm, d), lambda i: (jnp.minimum(i, n0 - 1), 0)),
                  pl.BlockSpec((tm, d), lambda i: (jnp.maximum(i - n0, 0), 0)),
                  pl.BlockSpec((tm, d), lambda i: (i, 0)),
                  pl.BlockSpec((tm, at.shape[1]), lambda i: (i, 0)),
                  *gate_specs(COL_GATE_A), *gate_specs(COL_GATE_B),
                  pl.BlockSpec((1, 6, d), lambda i: (i // per_seq, 0, 0)),
                  pl.BlockSpec(p_a.shape, lambda i: (0, 0), **const),
                  pl.BlockSpec(p_b.shape, lambda i: (0, 0), **const),
                  pl.BlockSpec(w_out.shape, lambda i: (0, 0), **const),
                  pl.BlockSpec((1, d), lambda i: (0, 0)),
                  pl.BlockSpec(wr_hi.shape, lambda i: (0, 0), **const),
                  pl.BlockSpec(wr_lo.shape, lambda i: (0, 0), **const),
                  pl.BlockSpec((1, ROUTER_COLS), lambda i: (0, 0))],
        out_specs=[pl.BlockSpec((tm, d), lambda i: (i, 0)),
                   pl.BlockSpec((tm, d), lambda i: (i, 0)),
                   pl.BlockSpec((tm, ROUTER_COLS), lambda i: (i, 0))],
        out_shape=[jax.ShapeDtypeStruct((n, d), F32),
                   jax.ShapeDtypeStruct((n, d), F32),
                   jax.ShapeDtypeStruct((n, ROUTER_COLS), F32)],
        compiler_params=_cparams(("parallel",)),
        name="merge_project",
    )(xp2, xs2, hm, at, *([proj] * (2 * n_gate)), mod3, p_a, p_b, w_out, norm2_w.reshape(1, d), wr_hi, wr_lo, br)


def route(logits, tb):
    n = logits.shape[0]
    g_logits = logits[:, :N_GROUPS]
    e_logits = logits[:, N_GROUPS:N_GROUPS + N_EXPERTS].reshape(n, N_GROUPS, EXPERTS_PER_GROUP)
    g_idx = jnp.argmax(g_logits, axis=-1)
    g_w = jnp.take_along_axis(jax.nn.softmax(g_logits, axis=-1), g_idx[:, None], axis=-1)
    e_sel = jnp.take_along_axis(e_logits, g_idx[:, None, None], axis=1)[:, 0]
    top_v, top_i = lax.top_k(e_sel, TOP_K)
    weights = g_w * jax.nn.softmax(top_v, axis=-1)
    expert = (g_idx[:, None] * EXPERTS_PER_GROUP + top_i).astype(jnp.int32)
    a = n * TOP_K
    flat_e = expert.reshape(a)
    e_ids = jnp.arange(N_EXPERTS, dtype=jnp.int32)
    counts = jnp.sum(flat_e[:, None] == e_ids[None, :], axis=0, dtype=jnp.int32)
    padded = (counts + tb - 1) // tb * tb
    pad_end = jnp.cumsum(padded)
    filler_e = jnp.repeat(e_ids, tb)
    filler_j = jnp.tile(jnp.arange(tb, dtype=jnp.int32), N_EXPERTS)
    filler_key = jnp.where(filler_j < (padded - counts)[filler_e], 2 * filler_e + 1, 2 * N_EXPERTS + 1)
    keys = jnp.concatenate([2 * flat_e, filler_key])
    ids = jnp.arange(a, dtype=jnp.int32)
    filler0 = jnp.zeros((N_EXPERTS * tb,), jnp.int32)
    tok_src = jnp.concatenate([ids // TOP_K, filler0])
    out_src = jnp.concatenate([(ids % TOP_K) * n + ids // TOP_K, filler0])
    sorted_keys, row_tok, row_out = lax.sort((keys, tok_src, out_src), num_keys=1)
    r = a + N_EXPERTS * tb
    n_blocks = r // tb
    block_start = jnp.arange(n_blocks, dtype=jnp.int32) * tb
    block_expert = jnp.minimum(jnp.sum(block_start[:, None] >= pad_end[None, :], axis=1), N_EXPERTS - 1).astype(jnp.int32)
    n_used = (pad_end[-1] // tb).astype(jnp.int32).reshape(1)
    later = jnp.logical_and(e_ids[None, :] > e_ids[:, None], (counts > 0)[None, :])
    next_of = jnp.min(jnp.where(later, e_ids[None, :], N_EXPERTS), axis=1)
    next_expert = jnp.where(next_of < N_EXPERTS, next_of, -1)[block_expert].astype(jnp.int32)
    pos = jnp.arange(r, dtype=jnp.int32)
    inv_key = jnp.where(sorted_keys % 2 == 0, row_out, a + pos)
    _, inv = lax.sort((inv_key, pos), num_keys=1)
    return (block_expert, next_expert, n_used, row_tok), inv[:a], weights


ROW_DMA_UNROLL = 64

def _issue_row_gather(src_hbm, dst_buf, sem, index_of, n_rows, alternate):
    group = ROW_DMA_UNROLL

    def body(i, carry):
        for t in range(group):
            j = i * group + t
            pltpu.make_async_copy(src_hbm.at[pl.ds(index_of(j), 1), :], dst_buf.at[pl.ds(j, 1), :], sem).start(
                priority=t % 2 if alternate else 0)
        return carry
    lax.fori_loop(0, n_rows // group, body, 0)


def _wait_row_gather(src_hbm, dst_buf, sem):
    pltpu.make_async_copy(src_hbm.at[pl.ds(0, dst_buf.shape[0]), :], dst_buf, sem).wait()


def _round_rows_to_bf16(src_ref, dst_ref, rows=256):
    def body(t, carry):
        r = pl.ds(pl.multiple_of(t * rows, rows), rows)
        dst_ref[r, :] = src_ref[r, :].astype(BF16)
        return carry
    lax.fori_loop(0, src_ref.shape[0] // rows, body, 0)


def _expert_kernel(be_ref, nxt_ref, nused_ref, rtok_ref, h2_hbm, wg_hbm, wu_hbm, wd_hbm, out_ref,
                   xbuf, stg_g, stg_u, stg_d, wg_b, wu_b, wd_b, gsem, wsem):
    tb = xbuf.shape[1]
    i = pl.program_id(0)
    nused = nused_ref[0]
    slot = i % 2
    e = be_ref[i]
    first_of_expert = jnp.logical_or(i == 0, e != be_ref[jnp.maximum(i - 1, 0)])

    def issue(blk, s):
        _issue_row_gather(h2_hbm, xbuf.at[s], gsem.at[s], lambda j: rtok_ref[blk * tb + j], tb, alternate=False)

    def weight_copies(expert):
        return [pltpu.make_async_copy(src.at[expert], dst, wsem.at[k])
                for k, (src, dst) in enumerate(((wg_hbm, stg_g), (wu_hbm, stg_u), (wd_hbm, stg_d)))]

    @pl.when(jnp.logical_and(i == 0, nused > 0))
    def _():
        issue(0, 0)

    @pl.when(i + 1 < nused)
    def _():
        issue(i + 1, 1 - slot)

    @pl.when(i < nused)
    def _():
        @pl.when(first_of_expert)
        def _():
            @pl.when(i == 0)
            def _():
                for c in weight_copies(e):
                    c.start(priority=1)

            for c in weight_copies(e):
                c.wait()
            for stg, wb in ((stg_g, wg_b), (stg_u, wu_b), (stg_d, wd_b)):
                _round_rows_to_bf16(stg, wb)
            nxt = nxt_ref[i]

            @pl.when(nxt >= 0)
            def _():
                for c in weight_copies(nxt):
                    c.start(priority=1)

        _wait_row_gather(h2_hbm, xbuf.at[slot], gsem.at[slot])
        x = xbuf[slot].astype(BF16)
        g = jnp.dot(x, wg_b[...], preferred_element_type=F32)
        u = jnp.dot(x, wu_b[...], preferred_element_type=F32)
        hdn = (g * _sigmoid(g) * u).astype(BF16)
        out_ref[...] = jnp.dot(hdn, wd_b[...], preferred_element_type=F32)

    @pl.when(i >= nused)
    def _():
        out_ref[...] = jnp.zeros_like(out_ref)


def expert_ffn(h2, tables, wg, wu, wd, tb=MOE_ROWS):
    block_expert, next_expert, n_used, row_tok = tables
    n, d = h2.shape
    nb = block_expert.shape[0]
    de = wg.shape[2]
    any_spec = pl.BlockSpec(memory_space=pl.ANY)
    grid_spec = pltpu.PrefetchScalarGridSpec(
        num_scalar_prefetch=4,
        grid=(nb,),
        in_specs=[any_spec, any_spec, any_spec, any_spec],
        out_specs=pl.BlockSpec((tb, d), lambda i, *_: (i, 0)),
        scratch_shapes=[pltpu.VMEM((2, tb, d), F32),
                        pltpu.VMEM((d, de), F32), pltpu.VMEM((d, de), F32), pltpu.VMEM((de, d), F32),
                        pltpu.VMEM((d, de), BF16), pltpu.VMEM((d, de), BF16), pltpu.VMEM((de, d), BF16),
                        pltpu.SemaphoreType.DMA((2,)), pltpu.SemaphoreType.DMA((3,))],
    )
    return pl.pallas_call(
        _expert_kernel,
        grid_spec=grid_spec,
        out_shape=jax.ShapeDtypeStruct((nb * tb, d), F32),
        compiler_params=_cparams(("arbitrary",), EXPERT_VMEM_LIMIT),
        name="expert_ffn",
    )(block_expert, next_expert, n_used, row_tok, h2, wg, wu, wd)


def _final_kernel(n_tok, tile0, inv_ref, x1_ref, rw_ref, mod_ref, w_ref, ys_hbm, o_ref, buf, sem):
    tm = x1_ref.shape[0]
    i = pl.program_id(0)
    slot = i % 2

    def issue(tile, s):
        base = (tile0 + tile) * tm
        for k in range(TOP_K):
            _issue_row_gather(ys_hbm, buf.at[s, k], sem.at[s], lambda j, k=k: inv_ref[k * n_tok + base + j], tm,
                              alternate=True)

    @pl.when(i == 0)
    def _():
        issue(0, 0)

    @pl.when(i + 1 < pl.num_programs(0))
    def _():
        issue(i + 1, 1 - slot)

    for k in range(TOP_K):
        _wait_row_gather(ys_hbm, buf.at[slot, k], sem.at[slot])
    rw = rw_ref[...]
    moe = buf[slot, 0] * rw[:, 0:1]
    for k in range(1, TOP_K):
        moe = moe + buf[slot, k] * rw[:, k:k + 1]
    x = x1_ref[...] + mod_ref[0, 5:6, :] * moe
    o_ref[...] = x * lax.rsqrt(jnp.mean(x * x, axis=-1, keepdims=True) + RMS_EPS) * w_ref[...]


def final_norm(x1, ys, inv, route_w, mod3, final_w, row0, rows, seq, tm=512):
    n_tok, d = x1.shape
    off = row0 // tm
    per_seq = seq // tm
    grid_spec = pltpu.PrefetchScalarGridSpec(
        num_scalar_prefetch=1,
        grid=(rows // tm,),
        in_specs=[pl.BlockSpec((tm, d), lambda i, inv: (off + i, 0)),
                  pl.BlockSpec((tm, TOP_K), lambda i, inv: (off + i, 0)),
                  pl.BlockSpec((1, 6, d), lambda i, inv: ((off + i) // per_seq, 0, 0)),
                  pl.BlockSpec((1, d), lambda i, inv: (0, 0)),
                  pl.BlockSpec(memory_space=pl.ANY)],
        out_specs=pl.BlockSpec((tm, d), lambda i, inv: (i, 0)),
        scratch_shapes=[pltpu.VMEM((2, TOP_K, tm, d), F32), pltpu.SemaphoreType.DMA((2,))],
    )
    return pl.pallas_call(
        functools.partial(_final_kernel, n_tok, off),
        grid_spec=grid_spec,
        out_shape=jax.ShapeDtypeStruct((rows, d), F32),
        compiler_params=_cparams(("arbitrary",)),
        name="final_norm",
    )(inv, x1, route_w, mod3, final_w.reshape(1, d), ys)


def kernel(x_prompt, x_sample, c_prompt, c_sample, w_ada, b_ada, norm1_w, w_in, b_in, mlstm_gate_b, conv_w, conv_b, mlstm_norm_w, p_a, p_b, w_out, norm2_w, w_router_group, b_router_group, w_router_expert, b_router_expert, w_expert_gate, w_expert_up, w_expert_down, final_norm_w):
    bp, seq, d = x_prompt.shape
    bs = x_sample.shape[0]
    bt = bp + bs
    n = bt * seq
    layer = 0

    w_in_bf = w_in.astype(BF16)
    w_in_a = slice_cols_bf16(w_in_bf, layer, _SRC_A0, PROJ_A_COLS)
    b_in_a = b_in[layer, _SRC_A0:]
    b_in_m = b_in[layer, :PROJ_M_COLS] + jnp.pad(mlstm_gate_b[layer], (COL_GATE_M, PROJ_M_COLS - _SRC_A0))
    wr = jnp.concatenate([w_router_group[layer], w_router_expert[layer],
                          jnp.zeros((d, ROUTER_COLS - N_GROUPS - N_EXPERTS), F32)], axis=1)
    br = jnp.concatenate([b_router_group[layer], b_router_expert[layer],
                          jnp.zeros((ROUTER_COLS - N_GROUPS - N_EXPERTS,), F32)]).reshape(1, ROUTER_COLS)
    wr_hi = wr.astype(BF16)
    wr_lo = (wr - wr_hi.astype(F32)).astype(BF16)
    slopes = 2.0 ** (-8.0 * jnp.arange(1, A_HEADS + 1, dtype=F32) / A_HEADS)

    c_all = jnp.concatenate([c_prompt, c_sample, jnp.zeros((16 - bt, d), F32)], axis=0)
    mod3 = ada_modulation(c_all, w_ada[layer], b_ada[layer])[:bt].reshape(bt, 6, d)

    h = norm_modulate(x_prompt, x_sample, mod3, norm1_w[layer])
    h2d = h.reshape(n, d)
    proj_m = matmul_bias(h2d, w_in_bf, b_in_m, PROJ_M_COLS, "in_projection_mlstm", layer=layer)
    proj_a = matmul_bias(h2d, w_in_a, b_in_a, PROJ_A_COLS, "in_projection")
    proj_m3 = proj_m.reshape(bt, seq, PROJ_M_COLS)
    proj_a3 = proj_a.reshape(bt, seq, PROJ_A_COLS)

    nc = seq // M_CHUNK
    gates = proj_m3[:, :, COL_GATE_M:COL_GATE_M + GATE_M_W].reshape(bt, nc, M_CHUNK, 4, M_HEADS)
    gates = gates.transpose(0, 4, 3, 1, 2)
    hm = mlstm_branch(proj_m3, gates, conv_w[layer], conv_b[layer], mlstm_norm_w[layer])
    at = attention_branch(proj_a3, slopes)

    x1, h2, logits = merge_project(
        x_prompt.reshape(bp * seq, d), x_sample.reshape(bs * seq, d), hm.reshape(n, -1), at.reshape(n, -1), proj_a,
        mod3, p_a[layer].astype(BF16), p_b[layer].astype(BF16), w_out[layer].astype(BF16), norm2_w[layer],
        wr_hi, wr_lo, br, seq)

    tables, inv, route_w = route(logits, MOE_ROWS)
    ys = expert_ffn(h2, tables, w_expert_gate[layer], w_expert_up[layer], w_expert_down[layer])

    y_p = final_norm(x1, ys, inv, route_w, mod3, final_norm_w, 0, bp * seq, seq)
    y_s = final_norm(x1, ys, inv, route_w, mod3, final_norm_w, bp * seq, bs * seq, seq)
    return (y_p.reshape(bp, seq, d), y_s.reshape(bs, seq, d))
```
